```python
import math
import jax, jax.numpy as jnp
from jax import lax
import numpy as np

D_MODEL = 2048
BATCH = 1
SEQ = 8192
DEPTH = 1

D_MIX = D_MODEL
D_S5 = D_MIX // 2
D_RG = D_MIX - D_S5
S5_GROUP = 16
S5_GROUPS = D_S5 // S5_GROUP
S5_STATE = 64
RG_HEADS = 16
RG_HEAD_DIM = D_RG // RG_HEADS
CONV_W = 4
RG_C = 8.0
D_IN = D_S5 + 2 * D_RG
D_FF = 4 * D_MODEL
PLE_DIM = 256
N_DIR = 2
LN_EPS = 1e-5
DN_ALPHA = (2.0 * DEPTH) ** 0.25
DN_BETA = (8.0 * DEPTH) ** -0.25

kernel_name = "hybrid_s5_rglru_postnorm_encoder"


def _layer_norm(x, g, b):
    xf = x.astype(jnp.float32)
    mu = jnp.mean(xf, axis=-1, keepdims=True)
    var = jnp.mean(jnp.square(xf - mu), axis=-1, keepdims=True)
    y = (xf - mu) * lax.rsqrt(var + LN_EPS) * g.astype(jnp.float32) + b.astype(jnp.float32)
    return y.astype(x.dtype)


def _rms_norm(x, g):
    xf = x.astype(jnp.float32)
    y = xf * lax.rsqrt(jnp.mean(jnp.square(xf), axis=-1, keepdims=True) + LN_EPS)
    return y * g.astype(jnp.float32)


def _linear_scan(a, b, reverse):
    def combine(left, right):
        a_l, b_l = left
        a_r, b_r = right
        return a_r * a_l, a_r * b_l + b_r
    _, h = lax.associative_scan(combine, (a, b), axis=1, reverse=reverse)
    return h


def _s5_mixer(u, lam_re, lam_im, log_step, b_re, b_im, c_re, c_im, d, glu_w, glu_b):
    bsz, seq, _ = u.shape
    f32 = jnp.float32
    uf = u.astype(f32).reshape(bsz, seq, S5_GROUPS, S5_GROUP)
    uc = uf.astype(jnp.complex64)
    lam = lax.complex(jnp.minimum(lam_re.astype(f32), -1e-4), lam_im.astype(f32))
    step = jnp.exp(log_step.astype(f32))[..., None]
    lam_bar = jnp.exp(lam * step)
    b_mat = lax.complex(b_re.astype(f32), b_im.astype(f32))
    b_bar = ((lam_bar - 1.0) / lam)[..., None] * b_mat
    c_mat = lax.complex(c_re.astype(f32), c_im.astype(f32))

    def direction(n, rev):
        bu = jnp.einsum('gpk,blgk->blgp', b_bar[n], uc)
        a = jnp.broadcast_to(lam_bar[n], bu.shape)
        hs = _linear_scan(a, bu, rev)
        return jnp.einsum('gkp,blgp->blgk', c_mat[n], hs).real

    y = direction(0, False) + direction(1, True) + d.astype(f32) * uf
    y = jax.nn.gelu(y.reshape(bsz, seq, D_S5))
    return y * jax.nn.sigmoid(y @ glu_w.astype(f32) + glu_b.astype(f32))


def _rglru_mixer(xb, zb, conv_w, conv_b, wa, ba, wx, bx, lam):
    bsz, seq, _ = xb.shape
    f32 = jnp.float32
    pad_l = CONV_W // 2
    xc = lax.conv_general_dilated(
        xb.astype(f32), conv_w.astype(f32)[:, None, :], window_strides=(1,),
        padding=[(pad_l, CONV_W - 1 - pad_l)], dimension_numbers=('NWC', 'WIO', 'NWC'),
        feature_group_count=D_RG) + conv_b.astype(f32)
    xh = xc.reshape(bsz, seq, RG_HEADS, RG_HEAD_DIM)

    def direction(n, rev):
        r = jax.nn.sigmoid(jnp.einsum('blhi,hij->blhj', xh, wa[n].astype(f32)) + ba[n].astype(f32))
        ig = jax.nn.sigmoid(jnp.einsum('blhi,hij->blhj', xh, wx[n].astype(f32)) + bx[n].astype(f32))
        log_a = -RG_C * r * jax.nn.softplus(-lam[n].astype(f32)).reshape(RG_HEADS, RG_HEAD_DIM)
        a = jnp.exp(log_a)
        bterm = jnp.sqrt(-jnp.expm1(2.0 * log_a)) * (ig * xh)
        return _linear_scan(a, bterm, rev)

    y = (direction(0, False) + direction(1, True)).reshape(bsz, seq, D_RG)
    return y * jax.nn.gelu(zb.astype(f32))


def setup_inputs(seed: int = 0) -> dict:
    key = jax.random.key(seed)
    k = jax.random.split(key, 40)
    f32 = jnp.float32

    def nrm(kk, shape, scale):
        return jax.random.normal(kk, shape, f32) * scale

    G, P, K = S5_GROUPS, S5_STATE, S5_GROUP
    H, hd = RG_HEADS, RG_HEAD_DIM
    u_dec = jax.random.uniform(k[22], (DEPTH, N_DIR, D_RG), f32, minval=0.9, maxval=0.999)
    s_dec = u_dec ** (1.0 / RG_C)
    return {
        "x": nrm(k[0], (BATCH, SEQ, D_MODEL), 1.0),
        "p": nrm(k[1], (DEPTH, BATCH, SEQ, PLE_DIM), 1.0),
        "ln_in_g": 1.0 + nrm(k[2], (D_MODEL,), 0.02),
        "ln_in_b": nrm(k[3], (D_MODEL,), 0.02),
        "w_in": nrm(k[4], (DEPTH, D_MODEL, D_IN), D_MODEL ** -0.5),
        "b_in": nrm(k[5], (DEPTH, D_IN), 0.02),
        "s5_lambda_re": -0.5 + nrm(k[6], (DEPTH, N_DIR, G, P), 0.01),
        "s5_lambda_im": math.pi * jnp.arange(P, dtype=f32) + nrm(k[7], (DEPTH, N_DIR, G, P), 0.01),
        "s5_log_step": jax.random.uniform(k[8], (DEPTH, N_DIR, G), f32,
                                          minval=math.log(1e-3), maxval=math.log(1e-1)),
        "s5_b_re": nrm(k[9], (DEPTH, N_DIR, G, P, K), (2.0 * K) ** -0.5),
        "s5_b_im": nrm(k[10], (DEPTH, N_DIR, G, P, K), (2.0 * K) ** -0.5),
        "s5_c_re": nrm(k[11], (DEPTH, N_DIR, G, K, P), (2.0 * P) ** -0.5),
        "s5_c_im": nrm(k[12], (DEPTH, N_DIR, G, K, P), (2.0 * P) ** -0.5),
        "s5_d": nrm(k[13], (DEPTH, G, K), 1.0),
        "s5_glu_w": nrm(k[14], (DEPTH, D_S5, D_S5), D_S5 ** -0.5),
        "s5_glu_b": nrm(k[15], (DEPTH, D_S5), 0.02),
        "rg_conv_w": nrm(k[16], (DEPTH, CONV_W, D_RG), CONV_W ** -0.5),
        "rg_conv_b": nrm(k[17], (DEPTH, D_RG), 0.02),
        "rg_wa": nrm(k[18], (DEPTH, N_DIR, H, hd, hd), hd ** -0.5),
        "rg_ba": nrm(k[19], (DEPTH, N_DIR, H, hd), 0.02),
        "rg_wx": nrm(k[20], (DEPTH, N_DIR, H, hd, hd), hd ** -0.5),
        "rg_bx": nrm(k[21], (DEPTH, N_DIR, H, hd), 0.02),
        "rg_lambda": jnp.log(s_dec) - jnp.log1p(-s_dec),
        "gn_s5": 1.0 + nrm(k[23], (DEPTH, D_S5), 0.02),
        "gn_rg": 1.0 + nrm(k[24], (DEPTH, D_RG), 0.02),
        "w_out": nrm(k[25], (DEPTH, D_MIX, D_MODEL), D_MIX ** -0.5 * DN_BETA),
        "b_out": nrm(k[26], (DEPTH, D_MODEL), 0.02),
        "ln1_g": 1.0 + nrm(k[27], (DEPTH, D_MODEL), 0.02),
        "ln1_b": nrm(k[28], (DEPTH, D_MODEL), 0.02),
        "w_ff1": nrm(k[29], (DEPTH, D_MODEL, D_FF), D_MODEL ** -0.5),
        "b_ff1": nrm(k[30], (DEPTH, D_FF), 0.02),
        "w_ff2": nrm(k[31], (DEPTH, D_FF, D_MODEL), D_FF ** -0.5 * DN_BETA),
        "b_ff2": nrm(k[32], (DEPTH, D_MODEL), 0.02),
        "ple_w": nrm(k[33], (DEPTH, PLE_DIM, D_MODEL), PLE_DIM ** -0.5 * DN_BETA),
        "ple_gate_w": nrm(k[34], (DEPTH, D_MODEL, D_MODEL), D_MODEL ** -0.5),
        "ple_gate_b": nrm(k[35], (DEPTH, D_MODEL), 0.02),
        "ln2_g": 1.0 + nrm(k[36], (DEPTH, D_MODEL), 0.02),
        "ln2_b": nrm(k[37], (DEPTH, D_MODEL), 0.02),
    }


def reference(x, p, ln_in_g, ln_in_b, w_in, b_in, s5_lambda_re, s5_lambda_im, s5_log_step,
              s5_b_re, s5_b_im, s5_c_re, s5_c_im, s5_d, s5_glu_w, s5_glu_b,
              rg_conv_w, rg_conv_b, rg_wa, rg_ba, rg_wx, rg_bx, rg_lambda,
              gn_s5, gn_rg, w_out, b_out, ln1_g, ln1_b,
              w_ff1, b_ff1, w_ff2, b_ff2, ple_w, ple_gate_w, ple_gate_b, ln2_g, ln2_b):
    dtype = x.dtype
    h = _layer_norm(x, ln_in_g, ln_in_b)
    for i in range(DEPTH):
        proj = h @ w_in[i] + b_in[i]
        u_s5 = proj[..., :D_S5]
        x_rg = proj[..., D_S5:D_S5 + D_RG]
        z_rg = proj[..., D_S5 + D_RG:]
        y_s5 = _s5_mixer(u_s5, s5_lambda_re[i], s5_lambda_im[i], s5_log_step[i],
                         s5_b_re[i], s5_b_im[i], s5_c_re[i], s5_c_im[i], s5_d[i],
                         s5_glu_w[i], s5_glu_b[i])
        y_rg = _rglru_mixer(x_rg, z_rg, rg_conv_w[i], rg_conv_b[i], rg_wa[i], rg_ba[i],
                            rg_wx[i], rg_bx[i], rg_lambda[i])
        y_cat = jnp.concatenate([_rms_norm(y_s5, gn_s5[i]), _rms_norm(y_rg, gn_rg[i])],
                                axis=-1).astype(dtype)
        mix = y_cat @ w_out[i] + b_out[i]
        h = _layer_norm(DN_ALPHA * h + mix, ln1_g[i], ln1_b[i])
        ff = jnp.square(jax.nn.relu(h @ w_ff1[i] + b_ff1[i])) @ w_ff2[i] + b_ff2[i]
        ple = jax.nn.sigmoid(h @ ple_gate_w[i] + ple_gate_b[i]) * (p[i] @ ple_w[i])
        h = _layer_norm(DN_ALPHA * h + ff + ple, ln2_g[i], ln2_b[i])
    return h.astype(dtype)
```

```python
import functools
import math

import jax
import jax.numpy as jnp
from jax import lax
from jax.experimental import pallas as pl
from jax.experimental.pallas import tpu as pltpu

F32 = jnp.float32
BF16 = jnp.bfloat16

NPH = 16
S5_K = 16
S5_P = 64
RG_HD = 64
RG_CB = 128
CONV_W = 4
RG_C = 8.0
LN_EPS = 1e-5
VMEM_LIMIT_V7X = 56 * 1024 * 1024


def _cparams(n_axes):
    return pltpu.CompilerParams(
        dimension_semantics=("arbitrary",) * n_axes,
        vmem_limit_bytes=VMEM_LIMIT_V7X)


def _resident(shape):
    return pl.BlockSpec(shape, lambda *_: (0,) * len(shape), pipeline_mode=pl.Buffered(1))


def _layer_norm(x, g, b):
    mu = jnp.mean(x, axis=-1, keepdims=True)
    xc = x - mu
    var = jnp.mean(xc * xc, axis=-1, keepdims=True)
    return xc * lax.rsqrt(var + LN_EPS) * g + b


def _gelu(x):
    c = math.sqrt(2.0 / math.pi)
    return 0.5 * x * (1.0 + jnp.tanh(c * (x + 0.044715 * (x * x * x))))


def _sigmoid(x):
    return 1.0 / (1.0 + jnp.exp(-x))


def _inproj_kernel(x_ref, g_ref, b_ref, ws5t_ref, bs5_ref, wrg_ref, brg_ref,
                   h_ref, ut_ref, xrg_ref, zrg_ref):
    h = _layer_norm(x_ref[...], g_ref[...], b_ref[...])
    h_ref[...] = h
    hb = h.astype(BF16)
    ut = lax.dot_general(ws5t_ref[...], hb, (((1,), (1,)), ((), ())),
                         preferred_element_type=F32) + bs5_ref[...]
    ut_ref[...] = ut.astype(BF16)
    xz = jnp.dot(hb, wrg_ref[...], preferred_element_type=F32) + brg_ref[...]
    d_rg = xrg_ref.shape[-1]
    xrg_ref[...] = xz[:, :d_rg]
    zrg_ref[...] = xz[:, d_rg:]


def _inproj(x2, ln_g, ln_b, ws5t, bs5, wrg, brg, n_chunks):
    d_model = ln_g.shape[-1]
    d_s5 = ws5t.shape[0]
    d_rg = wrg.shape[1] // 2
    rows = NPH * n_chunks
    return pl.pallas_call(
        _inproj_kernel,
        grid=(NPH,),
        in_specs=[
            pl.BlockSpec((n_chunks, d_model), lambda s: (0, s)),
            _resident((1, d_model)), _resident((1, d_model)),
            _resident((d_s5, d_model)), _resident((d_s5, 1)),
            _resident((d_model, 2 * d_rg)), _resident((1, 2 * d_rg)),
        ],
        out_specs=[
            pl.BlockSpec((n_chunks, d_model), lambda s: (s, 0)),
            pl.BlockSpec((None, d_s5, n_chunks), lambda s: (s, 0, 0)),
            pl.BlockSpec((n_chunks, d_rg), lambda s: (s, 0)),
            pl.BlockSpec((n_chunks, d_rg), lambda s: (s, 0)),
        ],
        out_shape=[
            jax.ShapeDtypeStruct((rows, d_model), F32),
            jax.ShapeDtypeStruct((NPH, d_s5, n_chunks), BF16),
            jax.ShapeDtypeStruct((rows, d_rg), F32),
            jax.ShapeDtypeStruct((rows, d_rg), F32),
        ],
        compiler_params=_cparams(1),
        name="inproj",
    )(x2, ln_g, ln_b, ws5t, bs5, wrg, brg)


def _lane_scan_exclusive(sre, sim, qre, qim, reverse):
    rows, n = sre.shape
    lane = lax.broadcasted_iota(jnp.int32, (rows, n), 1)

    def shift(v, k):
        if k % 128 == 0:
            z = jnp.zeros((rows, k), v.dtype)
            if reverse:
                return jnp.concatenate([v[:, k:], z], axis=1)
            return jnp.concatenate([z, v[:, :n - k]], axis=1)
        if reverse:
            return jnp.where(lane < n - k, pltpu.roll(v, n - k, axis=1), 0.0)
        return jnp.where(lane >= k, pltpu.roll(v, k, axis=1), 0.0)

    xre, xim = shift(sre, 1), shift(sim, 1)
    k, i = 1, 0
    while k < n:
        pr, pi = qre[:, i:i + 1], qim[:, i:i + 1]
        sr, si = shift(xre, k), shift(xim, k)
        xre, xim = xre + pr * sr - pi * si, xim + pr * si + pi * sr
        k *= 2
        i += 1
    return xre, xim


def _s5_kernel(ut_ref, a_ref, bc_ref, cc_ref, q_ref, y_ref):
    n = ut_ref.shape[-1]
    p = S5_P
    x = ut_ref[...].reshape(NPH * S5_K, n)
    y = jnp.dot(a_ref[0], x, preferred_element_type=F32)
    s = jnp.dot(bc_ref[0], x, preferred_element_type=F32)
    q = q_ref[0]
    hf_re, hf_im = _lane_scan_exclusive(s[0:p], s[p:2 * p], q[:, 0:16], q[:, 16:32], False)
    hb_re, hb_im = _lane_scan_exclusive(s[2 * p:3 * p], s[3 * p:4 * p], q[:, 32:48], q[:, 48:64], True)
    hin = jnp.concatenate([hf_re, hf_im, hb_re, hb_im], axis=0).astype(BF16)
    y = y + jnp.dot(cc_ref[0], hin, preferred_element_type=F32)
    y_ref[...] = _gelu(y).reshape(NPH, S5_K, n)


def _s5(ut, a_mat, bc_mat, cc_mat, q_pow):
    _, d_s5, n_chunks = ut.shape
    groups = d_s5 // S5_K
    tk = NPH * S5_K
    return pl.pallas_call(
        _s5_kernel,
        grid=(groups,),
        in_specs=[
            pl.BlockSpec((NPH, S5_K, n_chunks), lambda g: (0, g, 0)),
            pl.BlockSpec((1, tk, tk), lambda g: (g, 0, 0)),
            pl.BlockSpec((1, 4 * S5_P, tk), lambda g: (g, 0, 0)),
            pl.BlockSpec((1, tk, 4 * S5_P), lambda g: (g, 0, 0)),
            pl.BlockSpec((1, S5_P, 64), lambda g: (g, 0, 0)),
        ],
        out_specs=pl.BlockSpec((NPH, S5_K, n_chunks), lambda g: (0, g, 0)),
        out_shape=jax.ShapeDtypeStruct((NPH, d_s5, n_chunks), F32),
        compiler_params=_cparams(1),
        name="s5",
    )(ut, a_mat, bc_mat, cc_mat, q_pow)


def _s5_operators(lam_re, lam_im, log_step, b_re, b_im, c_re, c_im, d, n_chunks):
    n_dir, groups, p = lam_re.shape
    k = S5_K
    lam = lax.complex(jnp.minimum(lam_re.astype(F32), -1e-4), lam_im.astype(F32))
    step = jnp.exp(log_step.astype(F32))[..., None]
    lam_dt = lam * step
    lam_bar = jnp.exp(lam_dt)
    b_bar = ((lam_bar - 1.0) / lam)[..., None] * lax.complex(b_re.astype(F32), b_im.astype(F32))
    c_mat = lax.complex(c_re.astype(F32), c_im.astype(F32))
    j = jnp.arange(NPH + 1, dtype=F32)
    pw = jnp.exp(lam_dt[:, None] * j[None, :, None, None].astype(jnp.complex64))

    kern = jnp.einsum('ngkp,ntgp,ngpj->ntgkj', c_mat, pw[:, :NPH], b_bar).real
    t = jnp.arange(NPH)
    lag = t[:, None] - t[None, :]
    fwd = jnp.where((lag >= 0)[:, :, None, None, None], kern[0][jnp.clip(lag, 0, NPH - 1)], 0.0)
    bwd = jnp.where((lag <= 0)[:, :, None, None, None], kern[1][jnp.clip(-lag, 0, NPH - 1)], 0.0)
    skip = (jnp.eye(NPH, dtype=F32)[:, :, None, None, None]
            * (jnp.eye(k, dtype=F32)[None, None, None] * d.astype(F32)[None, None, :, :, None]))
    a_full = fwd + bwd + skip
    a_mat = a_full.transpose(2, 0, 3, 1, 4).reshape(groups, NPH * k, NPH * k)

    bf = pw[0, ::-1][1:, :, :, None] * b_bar[0][None]
    bb = pw[1, :NPH][:, :, :, None] * b_bar[1][None]
    def _b_rows(bx):
        bx = bx.transpose(1, 2, 0, 3).reshape(groups, p, NPH * k)
        return [bx.real, bx.imag]
    bc_mat = jnp.concatenate(_b_rows(bf) + _b_rows(bb), axis=1)

    cf = c_mat[0][None] * pw[0, 1:][:, :, None, :]
    cb = c_mat[1][None] * pw[1, ::-1][:NPH][:, :, None, :]
    def _c_cols(cx):
        cx = cx.transpose(1, 0, 2, 3).reshape(groups, NPH * k, p)
        return [cx.real, -cx.imag]
    cc_mat = jnp.concatenate(_c_cols(cf) + _c_cols(cb), axis=2)

    n_steps = max(1, int(math.ceil(math.log2(n_chunks))))
    assert n_steps <= 16
    e = (NPH * (2.0 ** jnp.arange(16, dtype=F32))).astype(jnp.complex64)
    qp = jnp.exp(lam_dt[..., None] * e)
    q_pow = jnp.concatenate([qp[0].real, qp[0].imag, qp[1].real, qp[1].imag], axis=-1)
    return a_mat.astype(BF16), bc_mat.astype(BF16), cc_mat.astype(BF16), q_pow.astype(F32)


def _glu_kernel(y_ref, w_ref, b_ref, gn_ref, o_ref):
    y = y_ref[...]
    g = jnp.dot(w_ref[...], y.astype(BF16), preferred_element_type=F32) + b_ref[...]
    o = y * _sigmoid(g)
    ms = jnp.mean(o * o, axis=0, keepdims=True)
    on = o * lax.rsqrt(ms + LN_EPS) * gn_ref[...]
    o_ref[...] = on.T.astype(BF16)


def _glu(yt, glu_wt, glu_b, gn):
    _, d_s5, n_chunks = yt.shape
    return pl.pallas_call(
        _glu_kernel,
        grid=(NPH,),
        in_specs=[
            pl.BlockSpec((None, d_s5, n_chunks), lambda s: (s, 0, 0)),
            _resident((d_s5, d_s5)), _resident((d_s5, 1)), _resident((d_s5, 1)),
        ],
        out_specs=pl.BlockSpec((n_chunks, d_s5), lambda s: (s, 0)),
        out_shape=jax.ShapeDtypeStruct((NPH * n_chunks, d_s5), BF16),
        compiler_params=_cparams(1),
        name="s5_glu",
    )(yt, glu_wt, glu_b, gn)


def _row_scan_carry(a, h, reverse):
    n, w = a.shape
    row = lax.broadcasted_iota(jnp.int32, (n, w), 0)

    def shift(v, k, fill):
        if k % 8 == 0:
            z = jnp.full((k, w), fill, v.dtype)
            if reverse:
                return jnp.concatenate([v[k:], z], axis=0)
            return jnp.concatenate([z, v[:n - k]], axis=0)
        if reverse:
            return jnp.where(row < n - k, pltpu.roll(v, n - k, axis=0), fill)
        return jnp.where(row >= k, pltpu.roll(v, k, axis=0), fill)

    k = 1
    while k < n:
        h = h + a * shift(h, k, 0.0)
        a = a * shift(a, k, 1.0)
        k *= 2
    return shift(h, 1, 0.0)


def _rglru_kernel(x_ref, cw_ref, cb_ref, wg_ref, bg_ref, lam_ref, y_ref, xe_ref, al_ref, hl_ref):
    n, w = x_ref.shape[1], x_ref.shape[2]
    row = lax.broadcasted_iota(jnp.int32, (n, w), 0)

    def from_prev_chunk(v):
        return jnp.where(row >= 1, pltpu.roll(v, 1, axis=0), 0.0)

    def from_next_chunk(v):
        return jnp.where(row < n - 1, pltpu.roll(v, n - 1, axis=0), 0.0)

    xe_ref[0] = from_prev_chunk(x_ref[NPH - 2])
    xe_ref[1] = from_prev_chunk(x_ref[NPH - 1])

    def copy_body(s, _):
        xe_ref[s + 2] = x_ref[s]
        return 0
    lax.fori_loop(0, NPH, copy_body, 0)
    xe_ref[NPH + 2] = from_next_chunk(x_ref[0])

    cw = cw_ref[...]
    cb = cb_ref[...]
    lam = lam_ref[...]
    neg = -lam
    softplus = jnp.maximum(neg, 0.0) + jnp.log(1.0 + jnp.exp(-jnp.abs(neg)))

    for d, reverse in ((0, False), (1, True)):
        sp = softplus[d:d + 1]
        wg = wg_ref[0][:, 2 * w * d:2 * w * (d + 1)]
        bg = bg_ref[0][:, 2 * w * d:2 * w * (d + 1)]
        init = NPH if reverse else 0
        al_ref[init] = jnp.ones((n, w), F32)
        hl_ref[init] = jnp.zeros((n, w), F32)

        def local_body(i, _):
            s = (NPH - 1 - i) if reverse else i
            xc = (cw[0:1] * xe_ref[s] + cw[1:2] * xe_ref[s + 1]
                  + cw[2:3] * xe_ref[s + 2] + cw[3:4] * xe_ref[s + 3] + cb)
            g = jnp.dot(xc.astype(BF16), wg, preferred_element_type=F32) + bg
            r = _sigmoid(g[:, :w])
            ig = _sigmoid(g[:, w:])
            log_a = (-RG_C) * r * sp
            a = jnp.exp(log_a)
            bt = jnp.sqrt(1.0 - jnp.exp(2.0 * log_a)) * (ig * xc)
            src = (s + 1) if reverse else s
            dst = s if reverse else (s + 1)
            hl_ref[dst] = a * hl_ref[src] + bt
            al_ref[dst] = a * al_ref[src]
            return 0
        lax.fori_loop(0, NPH, local_body, 0)

        last = 0 if reverse else NPH
        carry = _row_scan_carry(al_ref[last], hl_ref[last], reverse)

        def fix_body(s, _):
            slot = s if reverse else (s + 1)
            v = hl_ref[slot] + al_ref[slot] * carry
            if reverse:
                y_ref[s] = y_ref[s] + v
            else:
                y_ref[s] = v
            return 0
        lax.fori_loop(0, NPH, fix_body, 0)


def _rglru(xrg3, conv_w, conv_b, wg, bg, lam):
    _, n_chunks, d_rg = xrg3.shape
    nb = d_rg // RG_CB
    return pl.pallas_call(
        _rglru_kernel,
        grid=(nb,),
        in_specs=[
            pl.BlockSpec((NPH, n_chunks, RG_CB), lambda j: (0, 0, j)),
            pl.BlockSpec((CONV_W, RG_CB), lambda j: (0, j)),
            pl.BlockSpec((1, RG_CB), lambda j: (0, j)),
            pl.BlockSpec((1, RG_CB, 4 * RG_CB), lambda j: (j, 0, 0)),
            pl.BlockSpec((1, 1, 4 * RG_CB), lambda j: (j, 0, 0)),
            pl.BlockSpec((2, RG_CB), lambda j: (0, j)),
        ],
        out_specs=pl.BlockSpec((NPH, n_chunks, RG_CB), lambda j: (0, 0, j)),
        out_shape=jax.ShapeDtypeStruct((NPH, n_chunks, d_rg), F32),
        scratch_shapes=[
            pltpu.VMEM((NPH + 3, n_chunks, RG_CB), F32),
            pltpu.VMEM((NPH + 1, n_chunks, RG_CB), F32),
            pltpu.VMEM((NPH + 1, n_chunks, RG_CB), F32),
        ],
        compiler_params=_cparams(1),
        name="rglru",
    )(xrg3, conv_w, conv_b, wg, bg, lam)


def _rglru_gate_weights(wa, ba, wx, bx):
    n_dir, heads, hd, _ = wa.shape
    per = RG_CB // hd
    nb = heads // per
    eye = jnp.eye(per, dtype=F32)

    def blockdiag(w):
        w = w.astype(F32).reshape(nb, per, hd, hd)
        return jnp.einsum('bpij,pq->bpiqj', w, eye).reshape(nb, RG_CB, RG_CB)

    cols, bias = [], []
    for d in range(n_dir):
        for w, b in ((wa, ba), (wx, bx)):
            cols.append(blockdiag(w[d]))
            bias.append(b[d].astype(F32).reshape(nb, 1, RG_CB))
    return jnp.concatenate(cols, axis=-1).astype(BF16), jnp.concatenate(bias, axis=-1)


def _outproj_kernel(alpha, ys5_ref, yrg_ref, z_ref, h_ref, gn_ref, w_ref, b_ref, g_ref, be_ref, o_ref):
    d_s5 = ys5_ref.shape[-1]
    yrg = yrg_ref[...] * _gelu(z_ref[...])
    ms = jnp.mean(yrg * yrg, axis=-1, keepdims=True)
    yrgn = (yrg * lax.rsqrt(ms + LN_EPS) * gn_ref[...]).astype(BF16)
    mix = (jnp.dot(ys5_ref[...], w_ref[:d_s5, :], preferred_element_type=F32)
           + jnp.dot(yrgn, w_ref[d_s5:, :], preferred_element_type=F32) + b_ref[...])
    o_ref[...] = _layer_norm(alpha * h_ref[...] + mix, g_ref[...], be_ref[...])


def _outproj(alpha, ys5n, yrg, zrg, h, gn_rg, w_out, b_out, ln_g, ln_b, n_chunks):
    rows, d_model = h.shape
    d_s5, d_rg = ys5n.shape[-1], yrg.shape[-1]
    row_block = lambda width: pl.BlockSpec((n_chunks, width), lambda s: (s, 0))
    return pl.pallas_call(
        functools.partial(_outproj_kernel, alpha),
        grid=(rows // n_chunks,),
        in_specs=[
            row_block(d_s5), row_block(d_rg), row_block(d_rg), row_block(d_model),
            _resident((1, d_rg)), _resident((d_s5 + d_rg, d_model)), _resident((1, d_model)),
            _resident((1, d_model)), _resident((1, d_model)),
        ],
        out_specs=row_block(d_model),
        out_shape=jax.ShapeDtypeStruct((rows, d_model), F32),
        compiler_params=_cparams(1),
        name="outproj",
    )(ys5n, yrg, zrg, h, gn_rg, w_out, b_out, ln_g, ln_b)


def _ffn_kernel(alpha, h_ref, p_ref, w1_ref, b1_ref, w2_ref, b2_ref, pw_ref, gw_ref, gb_ref,
                g_ref, be_ref, o_ref, hb_ref, acc_ref):
    j = pl.program_id(1)

    @pl.when(j == 0)
    def _():
        hb = h_ref[...].astype(BF16)
        hb_ref[...] = hb
        gate = _sigmoid(jnp.dot(hb, gw_ref[...], preferred_element_type=F32) + gb_ref[...])
        ple = jnp.dot(p_ref[...].astype(BF16), pw_ref[...], preferred_element_type=F32)
        acc_ref[...] = gate * ple + b2_ref[...]

    a = jnp.dot(hb_ref[...], w1_ref[...], preferred_element_type=F32) + b1_ref[...]
    a = jnp.maximum(a, 0.0)
    acc_ref[...] += jnp.dot((a * a).astype(BF16), w2_ref[...], preferred_element_type=F32)

    @pl.when(j == pl.num_programs(1) - 1)
    def _():
        o_ref[...] = _layer_norm(alpha * h_ref[...] + acc_ref[...], g_ref[...], be_ref[...])


def _ffn(alpha, h1, p2, w1, b1, w2, b2, ple_w, gate_w, gate_b, ln_g, ln_b, n_chunks, ff_tile):
    rows, d_model = h1.shape
    d_ff = w1.shape[1]
    ple_dim = ple_w.shape[0]
    return pl.pallas_call(
        functools.partial(_ffn_kernel, alpha),
        grid=(rows // n_chunks, d_ff // ff_tile),
        in_specs=[
            pl.BlockSpec((n_chunks, d_model), lambda s, j: (s, 0)),
            pl.BlockSpec((n_chunks, ple_dim), lambda s, j: (0, s)),
            pl.BlockSpec((d_model, ff_tile), lambda s, j: (0, j)),
            pl.BlockSpec((1, ff_tile), lambda s, j: (0, j)),
            pl.BlockSpec((ff_tile, d_model), lambda s, j: (j, 0)),
            _resident((1, d_model)),
            _resident((ple_dim, d_model)), _resident((d_model, d_model)), _resident((1, d_model)),
            _resident((1, d_model)), _resident((1, d_model)),
        ],
        out_specs=pl.BlockSpec((n_chunks, d_model), lambda s, j: (0, s)),
        out_shape=jax.ShapeDtypeStruct((n_chunks, NPH * d_model), F32),
        scratch_shapes=[pltpu.VMEM((n_chunks, d_model), BF16), pltpu.VMEM((n_chunks, d_model), F32)],
        compiler_params=_cparams(2),
        name="ffn",
    )(h1, p2, w1, b1, w2, b2, ple_w, gate_w, gate_b, ln_g, ln_b)


def kernel(x, p, ln_in_g, ln_in_b, w_in, b_in, s5_lambda_re, s5_lambda_im, s5_log_step, s5_b_re, s5_b_im, s5_c_re, s5_c_im, s5_d, s5_glu_w, s5_glu_b, rg_conv_w, rg_conv_b, rg_wa, rg_ba, rg_wx, rg_bx, rg_lambda, gn_s5, gn_rg, w_out, b_out, ln1_g, ln1_b, w_ff1, b_ff1, w_ff2, b_ff2, ple_w, ple_gate_w, ple_gate_b, ln2_g, ln2_b):
    batch, seq, d_model = x.shape
    depth = w_in.shape[0]
    assert batch == 1 and depth == 1 and seq % NPH == 0
    n_chunks = seq // NPH
    d_s5 = s5_glu_w.shape[-1]
    d_rg = rg_conv_w.shape[-1]
    alpha = (2.0 * depth) ** 0.25
    row = lambda v: v.reshape(1, -1).astype(F32)
    col = lambda v: v.reshape(-1, 1).astype(F32)

    x2 = x.reshape(n_chunks, NPH * d_model)
    p2 = p.reshape(n_chunks, NPH * p.shape[-1])

    w_in0 = w_in[0]
    h, ut, xrg, zrg = _inproj(
        x2, row(ln_in_g), row(ln_in_b),
        w_in0[:, :d_s5].T.astype(BF16), col(b_in[0, :d_s5]),
        w_in0[:, d_s5:].astype(BF16), row(b_in[0, d_s5:]), n_chunks)

    a_mat, bc_mat, cc_mat, q_pow = _s5_operators(
        s5_lambda_re[0], s5_lambda_im[0], s5_log_step[0], s5_b_re[0], s5_b_im[0],
        s5_c_re[0], s5_c_im[0], s5_d[0], n_chunks)
    yt = _s5(ut, a_mat, bc_mat, cc_mat, q_pow)
    ys5n = _glu(yt, s5_glu_w[0].T.astype(BF16), col(s5_glu_b[0]), col(gn_s5[0]))

    wg, bg = _rglru_gate_weights(rg_wa[0], rg_ba[0], rg_wx[0], rg_bx[0])
    yrg = _rglru(xrg.reshape(NPH, n_chunks, d_rg), rg_conv_w[0].astype(F32), row(rg_conv_b[0]),
                 wg, bg, rg_lambda[0].astype(F32))

    h1 = _outproj(alpha, ys5n, yrg.reshape(NPH * n_chunks, d_rg), zrg, h, row(gn_rg[0]),
                  w_out[0].astype(BF16), row(b_out[0]), row(ln1_g[0]), row(ln1_b[0]), n_chunks)

    out2 = _ffn(alpha, h1, p2, w_ff1[0].astype(BF16), row(b_ff1[0]), w_ff2[0].astype(BF16), row(b_ff2[0]),
                ple_w[0].astype(BF16), ple_gate_w[0].astype(BF16), row(ple_gate_b[0]),
                row(ln2_g[0]), row(ln2_b[0]), n_chunks, ff_tile=512)
    return out2.reshape(batch, seq, d_model).astype(x.dtype)
```

```python
import functools
import math

import jax
import jax.numpy as jnp
from jax import lax
from jax.experimental import pallas as pl
from jax.experimental.pallas import tpu as pltpu

F32 = jnp.float32
BF16 = jnp.bfloat16

NPH = 16
CT = 32
S5_K = 16
S5_P = 64
S5_LB = 128
RG_CB = 128
CONV_W = 4
RG_C = 8.0
LN_EPS = 1e-5
VMEM_LIMIT_V7X = 56 * 1024 * 1024


def _cparams(n_axes):
    return pltpu.CompilerParams(
        dimension_semantics=("arbitrary",) * n_axes,
        vmem_limit_bytes=VMEM_LIMIT_V7X)


def _resident(shape):
    return pl.BlockSpec(shape, lambda *_: (0,) * len(shape), pipeline_mode=pl.Buffered(1))


def _tile3(width):
    return pl.BlockSpec((NPH, CT, width), lambda i, *_: (0, i, 0))


def _to_phase_major(x):
    w = x.shape[-1]
    return jnp.swapaxes(x.reshape(CT, NPH, w), 0, 1).reshape(NPH * CT, w)


def _to_time_major(x):
    w = x.shape[-1]
    return jnp.swapaxes(x.reshape(NPH, CT, w), 0, 1).reshape(CT * NPH, w)


def _layer_norm(x, g, b):
    mu = jnp.mean(x, axis=-1, keepdims=True)
    xc = x - mu
    var = jnp.mean(xc * xc, axis=-1, keepdims=True)
    return xc * lax.rsqrt(var + LN_EPS) * g + b


def _gelu(x):
    c = math.sqrt(2.0 / math.pi)
    return 0.5 * x * (1.0 + jnp.tanh(c * (x + 0.044715 * (x * x * x))))


def _sigmoid(x):
    return 1.0 / (1.0 + jnp.exp(-x))


def _inproj_kernel(x_ref, g_ref, b_ref, w_ref, bw_ref, h_ref, u_ref, xrg_ref, zrg_ref):
    d_s5, d_rg = u_ref.shape[-1], xrg_ref.shape[-1]
    h = _layer_norm(_to_phase_major(x_ref[...]), g_ref[...], b_ref[...])
    h_ref[...] = h.reshape(h_ref.shape)
    proj = jnp.dot(h.astype(BF16), w_ref[...], preferred_element_type=F32) + bw_ref[...]
    u_ref[...] = proj[:, :d_s5].astype(BF16).reshape(u_ref.shape)
    xrg_ref[...] = proj[:, d_s5:d_s5 + d_rg].reshape(xrg_ref.shape)
    zrg_ref[...] = proj[:, d_s5 + d_rg:].reshape(zrg_ref.shape)


def _inproj(x2, ln_g, ln_b, w_in, b_in, d_s5, d_rg):
    seq, d_model = x2.shape
    n_chunks = seq // NPH
    return pl.pallas_call(
        _inproj_kernel,
        grid=(n_chunks // CT,),
        in_specs=[
            pl.BlockSpec((CT * NPH, d_model), lambda i: (i, 0)),
            _resident((1, d_model)), _resident((1, d_model)),
            _resident(w_in.shape), _resident(b_in.shape),
        ],
        out_specs=[_tile3(d_model), _tile3(d_s5), _tile3(d_rg), _tile3(d_rg)],
        out_shape=[
            jax.ShapeDtypeStruct((NPH, n_chunks, d_model), F32),
            jax.ShapeDtypeStruct((NPH, n_chunks, d_s5), BF16),
            jax.ShapeDtypeStruct((NPH, n_chunks, d_rg), F32),
            jax.ShapeDtypeStruct((NPH, n_chunks, d_rg), F32),
        ],
        compiler_params=_cparams(1),
        name="inproj",
    )(x2, ln_g, ln_b, w_in, b_in)


def _lane_scan_exclusive(sre, sim, qre, qim, reverse):
    rows, n = sre.shape
    lane = lax.broadcasted_iota(jnp.int32, (rows, n), 1)

    def shift(v, k):
        if k % 128 == 0:
            z = jnp.zeros((rows, k), v.dtype)
            if reverse:
                return jnp.concatenate([v[:, k:], z], axis=1)
            return jnp.concatenate([z, v[:, :n - k]], axis=1)
        if reverse:
            return jnp.where(lane < n - k, pltpu.roll(v, n - k, axis=1), 0.0)
        return jnp.where(lane >= k, pltpu.roll(v, k, axis=1), 0.0)

    xre, xim = shift(sre, 1), shift(sim, 1)
    k, i = 1, 0
    while k < n:
        pr, pi = qre[:, i:i + 1], qim[:, i:i + 1]
        sr, si = shift(xre, k), shift(xim, k)
        xre, xim = xre + pr * sr - pi * si, xim + pr * si + pi * sr
        k *= 2
        i += 1
    return xre, xim


def _hdot(a, b):
    return jnp.dot(a, b, precision=lax.Precision.HIGHEST, preferred_element_type=F32)


def _s5_chunk_operators(pa, pb, dvec):
    tk = NPH * S5_K
    lane16 = lax.broadcasted_iota(jnp.int32, (16, tk), 1)
    row16 = lax.broadcasted_iota(jnp.int32, (16, tk), 0)
    rep = (lax.shift_right_logical(lane16, 4) == row16).astype(F32)
    til = ((lane16 & 15) == row16).astype(F32)
    lane_t = lax.broadcasted_iota(jnp.int32, (tk, 16), 0)
    col_t = lax.broadcasted_iota(jnp.int32, (tk, 16), 1)
    rep_t = (lax.shift_right_logical(lane_t, 4) == col_t).astype(F32)
    til_t = ((lane_t & 15) == col_t).astype(F32)

    def cmul(ar, ai, br, bi):
        return ar * br - ai * bi, ar * bi + ai * br

    def w_of(c0):
        return cmul(_hdot(pa[:, c0:c0 + 16], rep), _hdot(pa[:, c0 + 16:c0 + 32], rep),
                    _hdot(pa[:, c0 + 64:c0 + 80], til), _hdot(pa[:, c0 + 80:c0 + 96], til))

    wf_re, wf_im = w_of(0)
    wb_re, wb_im = w_of(32)
    bc = jnp.concatenate([wf_re, wf_im, wb_re, wb_im], axis=0)

    def c_of(r0):
        cr, ci = cmul(_hdot(til_t, pb[r0 + 64:r0 + 80]), _hdot(til_t, pb[r0 + 80:r0 + 96]),
                      _hdot(rep_t, pb[r0:r0 + 16]), _hdot(rep_t, pb[r0 + 16:r0 + 32]))
        return cr, -ci

    cc = jnp.concatenate(list(c_of(0)) + list(c_of(32)), axis=1)

    kf = _hdot(pb[64:80], wf_re) - _hdot(pb[80:96], wf_im)
    kb = _hdot(pb[96:112], wb_re) - _hdot(pb[112:128], wb_im)
    blocks = []
    for t in range(NPH):
        left = S5_K * (NPH - 1 - t)
        right = S5_K * t
        f = kf if left == 0 else jnp.where(lane16 < tk - left, pltpu.roll(kf, tk - left, axis=1), 0.0)
        b = kb if right == 0 else jnp.where(lane16 >= right, pltpu.roll(kb, right, axis=1), 0.0)
        blocks.append(f + b)
    a = jnp.concatenate(blocks, axis=0)
    ri = lax.broadcasted_iota(jnp.int32, (tk, tk), 0)
    ci = lax.broadcasted_iota(jnp.int32, (tk, tk), 1)
    a = a + jnp.where(ri == ci, dvec, 0.0)
    return a, bc, cc


def _s5_kernel(u_ref, pa_ref, pb_ref, d_ref, q_ref, y_ref, xt_ref, yt_ref):
    n = u_ref.shape[1]
    p = S5_P

    def to_rows(s, _):
        xt_ref[s] = u_ref[s].T
        return 0
    lax.fori_loop(0, NPH, to_rows, 0)

    def group(g, _):
        rows = pl.ds(pl.multiple_of(g * S5_K, S5_K), S5_K)
        a, bc, cc = _s5_chunk_operators(pa_ref[g], pb_ref[g], d_ref[g])
        x = xt_ref[:, rows, :].reshape(NPH * S5_K, n)
        y = jnp.dot(a.astype(BF16), x, preferred_element_type=F32)
        s = jnp.dot(bc.astype(BF16), x, preferred_element_type=F32)
        q = q_ref[g]
        hf_re, hf_im = _lane_scan_exclusive(s[0:p], s[p:2 * p], q[:, 0:16], q[:, 16:32], False)
        hb_re, hb_im = _lane_scan_exclusive(s[2 * p:3 * p], s[3 * p:4 * p], q[:, 32:48], q[:, 48:64], True)
        hin = jnp.concatenate([hf_re, hf_im, hb_re, hb_im], axis=0).astype(BF16)
        y = y + jnp.dot(cc.astype(BF16), hin, preferred_element_type=F32)
        yt_ref[:, rows, :] = _gelu(y).reshape(NPH, S5_K, n)
        return 0
    lax.fori_loop(0, S5_LB // S5_K, group, 0)

    def to_lanes(t, _):
        y_ref[t] = yt_ref[t].T
        return 0
    lax.fori_loop(0, NPH, to_lanes, 0)


def _s5(u3, pa, pb, dvec, q_pow):
    _, n_chunks, d_s5 = u3.shape
    gpb = S5_LB // S5_K
    tk = NPH * S5_K
    per_block = lambda *tail: pl.BlockSpec((gpb,) + tail, lambda b: (b,) + (0,) * len(tail))
    return pl.pallas_call(
        _s5_kernel,
        grid=(d_s5 // S5_LB,),
        in_specs=[
            pl.BlockSpec((NPH, n_chunks, S5_LB), lambda b: (0, 0, b)),
            per_block(S5_P, 128), per_block(128, S5_P), per_block(1, tk), per_block(S5_P, 64),
        ],
        out_specs=pl.BlockSpec((NPH, n_chunks, S5_LB), lambda b: (0, 0, b)),
        out_shape=jax.ShapeDtypeStruct((NPH, n_chunks, d_s5), F32),
        scratch_shapes=[
            pltpu.VMEM((NPH, S5_LB, n_chunks), BF16),
            pltpu.VMEM((NPH, S5_LB, n_chunks), F32),
        ],
        compiler_params=_cparams(1),
        name="s5",
    )(u3, pa, pb, dvec, q_pow)


def _s5_discretise(lam_re, lam_im, log_step, b_re, b_im, c_re, c_im, d, n_chunks):
    assert NPH == 16 and S5_K == 16 and n_chunks <= 2 ** 16
    lre = jnp.minimum(lam_re.astype(F32), -1e-4)
    lim = lam_im.astype(F32)
    step = jnp.exp(log_step.astype(F32))[..., None]
    are, aim = lre * step, lim * step

    def cexp(e):
        mag = jnp.exp(are[..., None] * e)
        ang = aim[..., None] * e
        return mag * jnp.cos(ang), mag * jnp.sin(ang)

    j = jnp.arange(NPH + 1, dtype=F32)
    pw_re, pw_im = cexp(j)
    nr, ni = pw_re[..., 1] - 1.0, pw_im[..., 1]
    den = lre * lre + lim * lim
    zr, zi = (nr * lre + ni * lim) / den, (ni * lre - nr * lim) / den
    br, bi = b_re.astype(F32), b_im.astype(F32)
    bbar_re = zr[..., None] * br - zi[..., None] * bi
    bbar_im = zr[..., None] * bi + zi[..., None] * br

    desc = lambda v: v[..., NPH - 1::-1]
    asc = lambda v: v[..., :NPH]
    pa = jnp.concatenate([desc(pw_re[0]), desc(pw_im[0]), asc(pw_re[1]), asc(pw_im[1]),
                          bbar_re[0], bbar_im[0], bbar_re[1], bbar_im[1]], axis=-1)
    rows = lambda v: jnp.swapaxes(v, -1, -2)
    up = lambda v: v[..., 1:]
    down = lambda v: v[..., NPH:0:-1]
    pb = jnp.concatenate([rows(up(pw_re[0])), rows(up(pw_im[0])), rows(down(pw_re[1])), rows(down(pw_im[1])),
                          c_re[0].astype(F32), c_im[0].astype(F32), c_re[1].astype(F32), c_im[1].astype(F32)],
                         axis=1)
    dvec = jnp.tile(d.astype(F32), (1, NPH))[:, None, :]
    qr, qi = cexp(NPH * (2.0 ** jnp.arange(16, dtype=F32)))
    q_pow = jnp.concatenate([qr[0], qi[0], qr[1], qi[1]], axis=-1)
    return pa, pb, dvec, q_pow


def _row_scan_carry(a, h, reverse):
    n, w = a.shape
    row = lax.broadcasted_iota(jnp.int32, (n, w), 0)

    def shift(v, k, fill):
        if k % 8 == 0:
            z = jnp.full((k, w), fill, v.dtype)
            if reverse:
                return jnp.concatenate([v[k:], z], axis=0)
            return jnp.concatenate([z, v[:n - k]], axis=0)
        if reverse:
            return jnp.where(row < n - k, pltpu.roll(v, n - k, axis=0), fill)
        return jnp.where(row >= k, pltpu.roll(v, k, axis=0), fill)

    k = 1
    while k < n:
        h = h + a * shift(h, k, 0.0)
        a = a * shift(a, k, 1.0)
        k *= 2
    return shift(h, 1, 0.0)


def _rglru_kernel(x_ref, cw_ref, cb_ref, wg_ref, bg_ref, lam_ref, y_ref, xe_ref, al_ref, hl_ref):
    n, w = x_ref.shape[1], x_ref.shape[2]
    row = lax.broadcasted_iota(jnp.int32, (n, w), 0)

    def from_prev_chunk(v):
        return jnp.where(row >= 1, pltpu.roll(v, 1, axis=0), 0.0)

    def from_next_chunk(v):
        return jnp.where(row < n - 1, pltpu.roll(v, n - 1, axis=0), 0.0)

    xe_ref[0] = from_prev_chunk(x_ref[NPH - 2])
    xe_ref[1] = from_prev_chunk(x_ref[NPH - 1])

    def copy_body(s, _):
        xe_ref[s + 2] = x_ref[s]
        return 0
    lax.fori_loop(0, NPH, copy_body, 0)
    xe_ref[NPH + 2] = from_next_chunk(x_ref[0])

    cw = cw_ref[...]
    cb = cb_ref[...]
    lam = lam_ref[...]
    neg = -lam
    softplus = jnp.maximum(neg, 0.0) + jnp.log(1.0 + jnp.exp(-jnp.abs(neg)))

    for d, reverse in ((0, False), (1, True)):
        sp = softplus[d:d + 1]
        wg = wg_ref[0][:, 2 * w * d:2 * w * (d + 1)]
        bg = bg_ref[0][:, 2 * w * d:2 * w * (d + 1)]
        init = NPH if reverse else 0
        al_ref[init] = jnp.ones((n, w), F32)
        hl_ref[init] = jnp.zeros((n, w), F32)

        def local_body(i, _):
            s = (NPH - 1 - i) if reverse else i
            xc = (cw[0:1] * xe_ref[s] + cw[1:2] * xe_ref[s + 1]
                  + cw[2:3] * xe_ref[s + 2] + cw[3:4] * xe_ref[s + 3] + cb)
            g = jnp.dot(xc.astype(BF16), wg, preferred_element_type=F32) + bg
            r = _sigmoid(g[:, :w])
            ig = _sigmoid(g[:, w:])
            log_a = (-RG_C) * r * sp
            a = jnp.exp(log_a)
            bt = jnp.sqrt(1.0 - jnp.exp(2.0 * log_a)) * (ig * xc)
            src = (s + 1) if reverse else s
            dst = s if reverse else (s + 1)
            hl_ref[dst] = a * hl_ref[src] + bt
            al_ref[dst] = a * al_ref[src]
            return 0
        lax.fori_loop(0, NPH, local_body, 0)

        last = 0 if reverse else NPH
        carry = _row_scan_carry(al_ref[last], hl_ref[last], reverse)

        def fix_body(s, _):
            slot = s if reverse else (s + 1)
            v = hl_ref[slot] + al_ref[slot] * carry
            if reverse:
                y_ref[s] = y_ref[s] + v
            else:
                y_ref[s] = v
            return 0
        lax.fori_loop(0, NPH, fix_body, 0)


def _rglru(xrg3, conv_w, conv_b, wg, bg, lam):
    _, n_chunks, d_rg = xrg3.shape
    nb = d_rg // RG_CB
    return pl.pallas_call(
        _rglru_kernel,
        grid=(nb,),
        in_specs=[
            pl.BlockSpec((NPH, n_chunks, RG_CB), lambda j: (0, 0, j)),
            pl.BlockSpec((CONV_W, RG_CB), lambda j: (0, j)),
            pl.BlockSpec((1, RG_CB), lambda j: (0, j)),
            pl.BlockSpec((1, RG_CB, 4 * RG_CB), lambda j: (j, 0, 0)),
            pl.BlockSpec((1, 1, 4 * RG_CB), lambda j: (j, 0, 0)),
            pl.BlockSpec((2, RG_CB), lambda j: (0, j)),
        ],
        out_specs=pl.BlockSpec((NPH, n_chunks, RG_CB), lambda j: (0, 0, j)),
        out_shape=jax.ShapeDtypeStruct((NPH, n_chunks, d_rg), F32),
        scratch_shapes=[
            pltpu.VMEM((NPH + 3, n_chunks, RG_CB), F32),
            pltpu.VMEM((NPH + 1, n_chunks, RG_CB), F32),
            pltpu.VMEM((NPH + 1, n_chunks, RG_CB), F32),
        ],
        compiler_params=_cparams(1),
        name="rglru",
    )(xrg3, conv_w, conv_b, wg, bg, lam)


def _rglru_gate_weights(wa, ba, wx, bx):
    n_dir, heads, hd, _ = wa.shape
    per = RG_CB // hd
    nb = heads // per
    eye = jnp.eye(per, dtype=F32)

    def blockdiag(w):
        w = w.astype(F32).reshape(nb, per, hd, hd)
        return jnp.einsum('bpij,pq->bpiqj', w, eye).reshape(nb, RG_CB, RG_CB)

    cols, bias = [], []
    for d in range(n_dir):
        for w, b in ((wa, ba), (wx, bx)):
            cols.append(blockdiag(w[d]))
            bias.append(b[d].astype(F32).reshape(nb, 1, RG_CB))
    return jnp.concatenate(cols, axis=-1).astype(BF16), jnp.concatenate(bias, axis=-1)


def _rms_norm(x, g):
    return x * lax.rsqrt(jnp.mean(x * x, axis=-1, keepdims=True) + LN_EPS) * g


def _mixout_kernel(alpha, ys5_ref, yrg_ref, z_ref, h_ref, gw_ref, gb_ref, gn5_ref, gnr_ref,
                   w_ref, b_ref, g_ref, be_ref, o_ref):
    rows = NPH * CT
    d_s5 = ys5_ref.shape[-1]
    y = ys5_ref[...].reshape(rows, d_s5)
    glu = y * _sigmoid(jnp.dot(y.astype(BF16), gw_ref[...], preferred_element_type=F32) + gb_ref[...])
    ys5n = _rms_norm(glu, gn5_ref[...]).astype(BF16)
    yrg = yrg_ref[...].reshape(rows, -1) * _gelu(z_ref[...].reshape(rows, -1))
    yrgn = _rms_norm(yrg, gnr_ref[...]).astype(BF16)
    mix = (jnp.dot(ys5n, w_ref[:d_s5, :], preferred_element_type=F32)
           + jnp.dot(yrgn, w_ref[d_s5:, :], preferred_element_type=F32) + b_ref[...])
    h1 = _layer_norm(alpha * h_ref[...].reshape(rows, -1) + mix, g_ref[...], be_ref[...])
    o_ref[...] = h1.reshape(o_ref.shape)


def _mixout(alpha, ys5, yrg, zrg, h, glu_w, glu_b, gn_s5, gn_rg, w_out, b_out, ln_g, ln_b):
    _, n_chunks, d_model = h.shape
    d_s5, d_rg = ys5.shape[-1], yrg.shape[-1]
    return pl.pallas_call(
        functools.partial(_mixout_kernel, alpha),
        grid=(n_chunks // CT,),
        in_specs=[
            _tile3(d_s5), _tile3(d_rg), _tile3(d_rg), _tile3(d_model),
            _resident((d_s5, d_s5)), _resident((1, d_s5)), _resident((1, d_s5)), _resident((1, d_rg)),
            _resident((d_s5 + d_rg, d_model)), _resident((1, d_model)),
            _resident((1, d_model)), _resident((1, d_model)),
        ],
        out_specs=_tile3(d_model),
        out_shape=jax.ShapeDtypeStruct((NPH, n_chunks, d_model), F32),
        compiler_params=_cparams(1),
        name="mixout",
    )(ys5, yrg, zrg, h, glu_w, glu_b, gn_s5, gn_rg, w_out, b_out, ln_g, ln_b)


def _ffn_kernel(alpha, h_ref, p_ref, w1_ref, b1_ref, w2_ref, b2_ref, pw_ref, gw_ref, gb_ref,
                g_ref, be_ref, o_ref, hb_ref, acc_ref):
    j = pl.program_id(1)
    rows = NPH * CT

    @pl.when(j == 0)
    def _():
        hb = h_ref[...].reshape(rows, -1).astype(BF16)
        hb_ref[...] = hb
        gate = _sigmoid(jnp.dot(hb, gw_ref[...], preferred_element_type=F32) + gb_ref[...])
        pe = _to_phase_major(p_ref[...]).astype(BF16)
        acc_ref[...] = gate * jnp.dot(pe, pw_ref[...], preferred_element_type=F32) + b2_ref[...]

    a = jnp.dot(hb_ref[...], w1_ref[...], preferred_element_type=F32) + b1_ref[...]
    a = jnp.maximum(a, 0.0)
    acc_ref[...] += jnp.dot((a * a).astype(BF16), w2_ref[...], preferred_element_type=F32)

    @pl.when(j == pl.num_programs(1) - 1)
    def _():
        out = _layer_norm(alpha * h_ref[...].reshape(rows, -1) + acc_ref[...], g_ref[...], be_ref[...])
        o_ref[...] = _to_time_major(out)


def _ffn(alpha, h1, p2, w1, b1, w2, b2, ple_w, gate_w, gate_b, ln_g, ln_b, ff_tile):
    _, n_chunks, d_model = h1.shape
    d_ff = w1.shape[1]
    ple_dim = ple_w.shape[0]
    rows = NPH * CT
    return pl.pallas_call(
        functools.partial(_ffn_kernel, alpha),
        grid=(n_chunks // CT, d_ff // ff_tile),
        in_specs=[
            _tile3(d_model),
            pl.BlockSpec((rows, ple_dim), lambda i, j: (i, 0)),
            pl.BlockSpec((d_model, ff_tile), lambda i, j: (0, j)),
            pl.BlockSpec((1, ff_tile), lambda i, j: (0, j)),
            pl.BlockSpec((ff_tile, d_model), lambda i, j: (j, 0)),
            _resident((1, d_model)),
            _resident((ple_dim, d_model)), _resident((d_model, d_model)), _resident((1, d_model)),
            _resident((1, d_model)), _resident((1, d_model)),
        ],
        out_specs=pl.BlockSpec((rows, d_model), lambda i, j: (i, 0)),
        out_shape=jax.ShapeDtypeStruct((NPH * n_chunks, d_model), F32),
        scratch_shapes=[pltpu.VMEM((rows, d_model), BF16), pltpu.VMEM((rows, d_model), F32)],
        compiler_params=_cparams(2),
        name="ffn",
    )(h1, p2, w1, b1, w2, b2, ple_w, gate_w, gate_b, ln_g, ln_b)


def kernel(x, p, ln_in_g, ln_in_b, w_in, b_in, s5_lambda_re, s5_lambda_im, s5_log_step, s5_b_re, s5_b_im, s5_c_re, s5_c_im, s5_d, s5_glu_w, s5_glu_b, rg_conv_w, rg_conv_b, rg_wa, rg_ba, rg_wx, rg_bx, rg_lambda, gn_s5, gn_rg, w_out, b_out, ln1_g, ln1_b, w_ff1, b_ff1, w_ff2, b_ff2, ple_w, ple_gate_w, ple_gate_b, ln2_g, ln2_b):
    batch, seq, d_model = x.shape
    depth = w_in.shape[0]
    assert batch == 1 and depth == 1 and seq % (NPH * CT) == 0
    n_chunks = seq // NPH
    d_s5 = s5_glu_w.shape[-1]
    d_rg = rg_conv_w.shape[-1]
    alpha = (2.0 * depth) ** 0.25
    row = lambda v: v.reshape(1, -1).astype(F32)

    h, u3, xrg, zrg = _inproj(x.reshape(seq, d_model), row(ln_in_g), row(ln_in_b),
                              w_in[0].astype(BF16), row(b_in[0]), d_s5, d_rg)

    pa, pb, dvec, q_pow = _s5_discretise(
        s5_lambda_re[0], s5_lambda_im[0], s5_log_step[0], s5_b_re[0], s5_b_im[0],
        s5_c_re[0], s5_c_im[0], s5_d[0], n_chunks)
    ys5 = _s5(u3, pa, pb, dvec, q_pow)

    wg, bg = _rglru_gate_weights(rg_wa[0], rg_ba[0], rg_wx[0], rg_bx[0])
    yrg = _rglru(xrg, rg_conv_w[0].astype(F32), row(rg_conv_b[0]), wg, bg, rg_lambda[0].astype(F32))

    h1 = _mixout(alpha, ys5, yrg, zrg, h, s5_glu_w[0].astype(BF16), row(s5_glu_b[0]), row(gn_s5[0]),
                 row(gn_rg[0]), w_out[0].astype(BF16), row(b_out[0]), row(ln1_g[0]), row(ln1_b[0]))

    out = _ffn(alpha, h1, p.reshape(seq, p.shape[-1]), w_ff1[0].astype(BF16), row(b_ff1[0]),
               w_ff2[0].astype(BF16), row(b_ff2[0]), ple_w[0].astype(BF16), ple_gate_w[0].astype(BF16),
               row(ple_gate_b[0]), row(ln2_g[0]), row(ln2_b[0]), ff_tile=512)
    return out.reshape(batch, seq, d_model).astype(x.dtype)
```

```python
import functools
import math

import jax
import jax.numpy as jnp
from jax import lax
from jax.experimental import pallas as pl
from jax.experimental.pallas import tpu as pltpu

F32 = jnp.float32
BF16 = jnp.bfloat16

NPH = 16
CT = 32
S5_K = 16
S5_P = 64
S5_LB = 128
RG_CB = 128
CONV_W = 4
RG_C = 8.0
LN_EPS = 1e-5
VMEM_LIMIT_V7X = 56 * 1024 * 1024


def _cparams(n_axes):
    return pltpu.CompilerParams(
        dimension_semantics=("arbitrary",) * n_axes,
        vmem_limit_bytes=VMEM_LIMIT_V7X)


def _resident(shape):
    return pl.BlockSpec(shape, lambda *_: (0,) * len(shape), pipeline_mode=pl.Buffered(1))


def _tile3(width):
    return pl.BlockSpec((NPH, CT, width), lambda i, *_: (0, i, 0))


def _to_phase_major(x):
    w = x.shape[-1]
    return jnp.swapaxes(x.reshape(CT, NPH, w), 0, 1).reshape(NPH * CT, w)


def _to_time_major(x):
    w = x.shape[-1]
    return jnp.swapaxes(x.reshape(NPH, CT, w), 0, 1).reshape(CT * NPH, w)


def _layer_norm(x, g, b):
    mu = jnp.mean(x, axis=-1, keepdims=True)
    xc = x - mu
    var = jnp.mean(xc * xc, axis=-1, keepdims=True)
    return xc * lax.rsqrt(var + LN_EPS) * g + b


def _gelu(x):
    c = math.sqrt(2.0 / math.pi)
    return 0.5 * x * (1.0 + jnp.tanh(c * (x + 0.044715 * (x * x * x))))


def _sigmoid(x):
    return 1.0 / (1.0 + jnp.exp(-x))


def _inproj_kernel(x_ref, g_ref, b_ref, w_ref, bw_ref, h_ref, u_ref, xrg_ref, zrg_ref):
    d_s5, d_rg = u_ref.shape[-1], xrg_ref.shape[-1]
    h = _layer_norm(_to_phase_major(x_ref[...]), g_ref[...], b_ref[...])
    h_ref[...] = h.reshape(h_ref.shape)
    proj = jnp.dot(h.astype(BF16), w_ref[...], preferred_element_type=F32) + bw_ref[...]
    u_ref[...] = proj[:, :d_s5].astype(BF16).reshape(u_ref.shape)
    xrg_ref[...] = proj[:, d_s5:d_s5 + d_rg].reshape(xrg_ref.shape)
    zrg_ref[...] = proj[:, d_s5 + d_rg:].reshape(zrg_ref.shape)


def _inproj(x2, ln_g, ln_b, w_in, b_in, d_s5, d_rg):
    seq, d_model = x2.shape
    n_chunks = seq // NPH
    return pl.pallas_call(
        _inproj_kernel,
        grid=(n_chunks // CT,),
        in_specs=[
            pl.BlockSpec((CT * NPH, d_model), lambda i: (i, 0)),
            _resident((1, d_model)), _resident((1, d_model)),
            _resident(w_in.shape), _resident(b_in.shape),
        ],
        out_specs=[_tile3(d_model), _tile3(d_s5), _tile3(d_rg), _tile3(d_rg)],
        out_shape=[
            jax.ShapeDtypeStruct((NPH, n_chunks, d_model), F32),
            jax.ShapeDtypeStruct((NPH, n_chunks, d_s5), BF16),
            jax.ShapeDtypeStruct((NPH, n_chunks, d_rg), F32),
            jax.ShapeDtypeStruct((NPH, n_chunks, d_rg), F32),
        ],
        compiler_params=_cparams(1),
        name="inproj",
    )(x2, ln_g, ln_b, w_in, b_in)


def _row_scan_exclusive(sre, sim, qre, qim, reverse):
    n, w = sre.shape
    row = lax.broadcasted_iota(jnp.int32, (n, w), 0)

    def shift(v, k):
        if k % 8 == 0:
            z = jnp.zeros((k, w), v.dtype)
            return jnp.concatenate([v[k:], z] if reverse else [z, v[:n - k]], axis=0)
        if reverse:
            return jnp.where(row < n - k, pltpu.roll(v, n - k, axis=0), 0.0)
        return jnp.where(row >= k, pltpu.roll(v, k, axis=0), 0.0)

    xre, xim = shift(sre, 1), shift(sim, 1)
    k, i = 1, 0
    while k < n:
        pr, pi = qre[i:i + 1], qim[i:i + 1]
        if k % 8 == 0:
            keep = slice(n - k, n) if reverse else slice(0, k)
            dst = slice(0, n - k) if reverse else slice(k, n)
            src = slice(k, n) if reverse else slice(0, n - k)
            sr, si = xre[src], xim[src]
            nre = xre[dst] + pr * sr - pi * si
            nim = xim[dst] + pr * si + pi * sr
            order = (lambda new, old: [new, old]) if reverse else (lambda new, old: [old, new])
            xre = jnp.concatenate(order(nre, xre[keep]), axis=0)
            xim = jnp.concatenate(order(nim, xim[keep]), axis=0)
        else:
            sr, si = shift(xre, k), shift(xim, k)
            xre, xim = xre + pr * sr - pi * si, xim + pr * si + pi * sr
        k *= 2
        i += 1
    return xre, xim


def _dot(a, b):
    return jnp.dot(a.astype(BF16), b.astype(BF16), preferred_element_type=F32)


def _s5_chunk_operators(pa, pb, dvec):
    tk = NPH * S5_K
    lane16 = lax.broadcasted_iota(jnp.int32, (16, tk), 1)
    row16 = lax.broadcasted_iota(jnp.int32, (16, tk), 0)
    rep = (lax.shift_right_logical(lane16, 4) == row16).astype(F32)
    til = ((lane16 & 15) == row16).astype(F32)
    lane_t = lax.broadcasted_iota(jnp.int32, (tk, 16), 0)
    col_t = lax.broadcasted_iota(jnp.int32, (tk, 16), 1)
    rep_t = (lax.shift_right_logical(lane_t, 4) == col_t).astype(F32)
    til_t = ((lane_t & 15) == col_t).astype(F32)

    def cmul(ar, ai, br, bi):
        return ar * br - ai * bi, ar * bi + ai * br

    def w_of(c0):
        return cmul(_dot(pa[:, c0:c0 + 16], rep), _dot(pa[:, c0 + 16:c0 + 32], rep),
                    _dot(pa[:, c0 + 64:c0 + 80], til), _dot(pa[:, c0 + 80:c0 + 96], til))

    wf_re, wf_im = w_of(0)
    wb_re, wb_im = w_of(32)
    bc = jnp.concatenate([wf_re, wf_im, wb_re, wb_im], axis=0)

    def c_of(r0):
        cr, ci = cmul(_dot(til_t, pb[r0 + 64:r0 + 80]), _dot(til_t, pb[r0 + 80:r0 + 96]),
                      _dot(rep_t, pb[r0:r0 + 16]), _dot(rep_t, pb[r0 + 16:r0 + 32]))
        return cr, -ci

    cc = jnp.concatenate(list(c_of(0)) + list(c_of(32)), axis=1)

    kf = _dot(pb[64:80], wf_re) - _dot(pb[80:96], wf_im)
    kb = _dot(pb[96:112], wb_re) - _dot(pb[112:128], wb_im)
    blocks = []
    for t in range(NPH):
        left = S5_K * (NPH - 1 - t)
        right = S5_K * t
        f = kf if left == 0 else jnp.where(lane16 < tk - left, pltpu.roll(kf, tk - left, axis=1), 0.0)
        b = kb if right == 0 else jnp.where(lane16 >= right, pltpu.roll(kb, right, axis=1), 0.0)
        blocks.append(f + b)
    a = jnp.concatenate(blocks, axis=0)
    ri = lax.broadcasted_iota(jnp.int32, (tk, tk), 0)
    ci = lax.broadcasted_iota(jnp.int32, (tk, tk), 1)
    a = a + jnp.where(ri == ci, dvec, 0.0)
    return a.astype(BF16), bc.astype(BF16), cc.astype(BF16)


def _s5_kernel(u_ref, pa_ref, pb_ref, d_ref, q_ref, y_ref, xt_ref, yt_ref):
    n = u_ref.shape[1]
    p2 = 2 * S5_P
    tk = NPH * S5_K

    def to_rows(s, _):
        xt_ref[s] = u_ref[s].T
        return 0
    lax.fori_loop(0, NPH, to_rows, 0)

    def pair(gp, _):
        xs, ys, bcs, ccs = [], [], [], []
        for h in range(2):
            g = 2 * gp + h
            rows = pl.ds(pl.multiple_of(g * S5_K, S5_K), S5_K)
            a, bc, cc = _s5_chunk_operators(pa_ref[g], pb_ref[g], d_ref[g])
            x = xt_ref[:, rows, :].reshape(tk, n)
            xs.append(x)
            ys.append(jnp.dot(a, x, preferred_element_type=F32))
            bcs.append(bc)
            ccs.append(cc)
        s = jnp.dot(jnp.concatenate(bcs, axis=1), jnp.concatenate(xs, axis=0), preferred_element_type=F32)
        st = [s[i * p2:(i + 1) * p2].T for i in range(4)]
        q = q_ref[gp]
        hf_re, hf_im = _row_scan_exclusive(st[0], st[1], q[0:16], q[16:32], False)
        hb_re, hb_im = _row_scan_exclusive(st[2], st[3], q[32:48], q[48:64], True)
        hin = jnp.concatenate([hf_re, hf_im, hb_re, hb_im], axis=1).astype(BF16)
        for h in range(2):
            g = 2 * gp + h
            rows = pl.ds(pl.multiple_of(g * S5_K, S5_K), S5_K)
            y = ys[h] + lax.dot_general(ccs[h], hin, (((1,), (1,)), ((), ())), preferred_element_type=F32)
            yt_ref[:, rows, :] = _gelu(y).reshape(NPH, S5_K, n)
        return 0
    lax.fori_loop(0, S5_LB // (2 * S5_K), pair, 0)

    def to_lanes(t, _):
        y_ref[t] = yt_ref[t].T
        return 0
    lax.fori_loop(0, NPH, to_lanes, 0)


def _s5(u3, pa, pb, dvec, q_rows):
    _, n_chunks, d_s5 = u3.shape
    gpb = S5_LB // S5_K
    tk = NPH * S5_K
    per_block = lambda n, *tail: pl.BlockSpec((n,) + tail, lambda b: (b,) + (0,) * len(tail))
    return pl.pallas_call(
        _s5_kernel,
        grid=(d_s5 // S5_LB,),
        in_specs=[
            pl.BlockSpec((NPH, n_chunks, S5_LB), lambda b: (0, 0, b)),
            per_block(gpb, 2 * S5_P, 128), per_block(gpb, 128, 2 * S5_P), per_block(gpb, 1, tk),
            per_block(gpb // 2, 64, 2 * S5_P),
        ],
        out_specs=pl.BlockSpec((NPH, n_chunks, S5_LB), lambda b: (0, 0, b)),
        out_shape=jax.ShapeDtypeStruct((NPH, n_chunks, d_s5), F32),
        scratch_shapes=[
            pltpu.VMEM((NPH, S5_LB, n_chunks), BF16),
            pltpu.VMEM((NPH, S5_LB, n_chunks), F32),
        ],
        compiler_params=_cparams(1),
        name="s5",
    )(u3, pa, pb, dvec, q_rows)


def _s5_discretise(lam_re, lam_im, log_step, b_re, b_im, c_re, c_im, d, n_chunks):
    assert NPH == 16 and S5_K == 16 and n_chunks <= 2 ** 16
    lre = jnp.minimum(lam_re.astype(F32), -1e-4)
    lim = lam_im.astype(F32)
    step = jnp.exp(log_step.astype(F32))[..., None]
    are, aim = lre * step, lim * step

    def cexp(e):
        mag = jnp.exp(are[..., None] * e)
        ang = aim[..., None] * e
        return mag * jnp.cos(ang), mag * jnp.sin(ang)

    j = jnp.arange(NPH + 1, dtype=F32)
    pw_re, pw_im = cexp(j)
    nr, ni = pw_re[..., 1] - 1.0, pw_im[..., 1]
    den = lre * lre + lim * lim
    zr, zi = (nr * lre + ni * lim) / den, (ni * lre - nr * lim) / den
    br, bi = b_re.astype(F32), b_im.astype(F32)
    bbar_re = zr[..., None] * br - zi[..., None] * bi
    bbar_im = zr[..., None] * bi + zi[..., None] * br

    desc = lambda v: v[..., NPH - 1::-1]
    asc = lambda v: v[..., :NPH]
    pa = jnp.concatenate([desc(pw_re[0]), desc(pw_im[0]), asc(pw_re[1]), asc(pw_im[1]),
                          bbar_re[0], bbar_im[0], bbar_re[1], bbar_im[1]], axis=-1)
    rows = lambda v: jnp.swapaxes(v, -1, -2)
    up = lambda v: v[..., 1:]
    down = lambda v: v[..., NPH:0:-1]
    pb = jnp.concatenate([rows(up(pw_re[0])), rows(up(pw_im[0])), rows(down(pw_re[1])), rows(down(pw_im[1])),
                          c_re[0].astype(F32), c_im[0].astype(F32), c_re[1].astype(F32), c_im[1].astype(F32)],
                         axis=1)
    groups, p = lam_re.shape[1:]
    half = (jnp.arange(groups) % 2)[:, None, None, None] == jnp.arange(2)[None, :, None, None]
    pa = jnp.where(half, pa[:, None], 0.0).reshape(groups, 2 * p, 128)
    pb = jnp.where(half, pb[:, None], 0.0).transpose(0, 2, 1, 3).reshape(groups, 128, 2 * p)
    dvec = jnp.tile(d.astype(F32), (1, NPH))[:, None, :]
    qr, qi = cexp(NPH * (2.0 ** jnp.arange(16, dtype=F32)))
    lanes = lambda v: v.reshape(groups // 2, 2, p, 16).transpose(0, 3, 1, 2).reshape(groups // 2, 16, 2 * p)
    q_rows = jnp.concatenate([lanes(qr[0]), lanes(qi[0]), lanes(qr[1]), lanes(qi[1])], axis=1)
    return pa, pb, dvec, q_rows


def _row_scan_carry(a, h, reverse):
    n, w = a.shape
    row = lax.broadcasted_iota(jnp.int32, (n, w), 0)

    def shift(v, k, fill):
        if k % 8 == 0:
            z = jnp.full((k, w), fill, v.dtype)
            if reverse:
                return jnp.concatenate([v[k:], z], axis=0)
            return jnp.concatenate([z, v[:n - k]], axis=0)
        if reverse:
            return jnp.where(row < n - k, pltpu.roll(v, n - k, axis=0), fill)
        return jnp.where(row >= k, pltpu.roll(v, k, axis=0), fill)

    k = 1
    while k < n:
        h = h + a * shift(h, k, 0.0)
        a = a * shift(a, k, 1.0)
        k *= 2
    return shift(h, 1, 0.0)


def _rglru_kernel(x_ref, cw_ref, cb_ref, wg_ref, bg_ref, lam_ref, y_ref, xe_ref, al_ref, hl_ref):
    n, w = x_ref.shape[1], x_ref.shape[2]
    row = lax.broadcasted_iota(jnp.int32, (n, w), 0)

    def from_prev_chunk(v):
        return jnp.where(row >= 1, pltpu.roll(v, 1, axis=0), 0.0)

    def from_next_chunk(v):
        return jnp.where(row < n - 1, pltpu.roll(v, n - 1, axis=0), 0.0)

    xe_ref[0] = from_prev_chunk(x_ref[NPH - 2])
    xe_ref[1] = from_prev_chunk(x_ref[NPH - 1])

    def copy_body(s, _):
        xe_ref[s + 2] = x_ref[s]
        return 0
    lax.fori_loop(0, NPH, copy_body, 0)
    xe_ref[NPH + 2] = from_next_chunk(x_ref[0])

    cw = cw_ref[...]
    cb = cb_ref[...]
    lam = lam_ref[...]
    neg = -lam
    softplus = jnp.maximum(neg, 0.0) + jnp.log(1.0 + jnp.exp(-jnp.abs(neg)))

    for d, reverse in ((0, False), (1, True)):
        sp = softplus[d:d + 1]
        wg = wg_ref[0][:, 2 * w * d:2 * w * (d + 1)]
        bg = bg_ref[0][:, 2 * w * d:2 * w * (d + 1)]
        init = NPH if reverse else 0
        al_ref[init] = jnp.ones((n, w), F32)
        hl_ref[init] = jnp.zeros((n, w), F32)

        def local_body(i, _):
            s = (NPH - 1 - i) if reverse else i
            xc = (cw[0:1] * xe_ref[s] + cw[1:2] * xe_ref[s + 1]
                  + cw[2:3] * xe_ref[s + 2] + cw[3:4] * xe_ref[s + 3] + cb)
            g = jnp.dot(xc.astype(BF16), wg, preferred_element_type=F32) + bg
            r = _sigmoid(g[:, :w])
            ig = _sigmoid(g[:, w:])
            log_a = (-RG_C) * r * sp
            a = jnp.exp(log_a)
            bt = jnp.sqrt(1.0 - jnp.exp(2.0 * log_a)) * (ig * xc)
            src = (s + 1) if reverse else s
            dst = s if reverse else (s + 1)
            hl_ref[dst] = a * hl_ref[src] + bt
            al_ref[dst] = a * al_ref[src]
            return 0
        lax.fori_loop(0, NPH, local_body, 0)

        last = 0 if reverse else NPH
        carry = _row_scan_carry(al_ref[last], hl_ref[last], reverse)

        def fix_body(s, _):
            slot = s if reverse else (s + 1)
            v = hl_ref[slot] + al_ref[slot] * carry
            if reverse:
                y_ref[s] = y_ref[s] + v
            else:
                y_ref[s] = v
            return 0
        lax.fori_loop(0, NPH, fix_body, 0)


def _rglru(xrg3, conv_w, conv_b, wg, bg, lam):
    _, n_chunks, d_rg = xrg3.shape
    nb = d_rg // RG_CB
    return pl.pallas_call(
        _rglru_kernel,
        grid=(nb,),
        in_specs=[
            pl.BlockSpec((NPH, n_chunks, RG_CB), lambda j: (0, 0, j)),
            pl.BlockSpec((CONV_W, RG_CB), lambda j: (0, j)),
            pl.BlockSpec((1, RG_CB), lambda j: (0, j)),
            pl.BlockSpec((1, RG_CB, 4 * RG_CB), lambda j: (j, 0, 0)),
            pl.BlockSpec((1, 1, 4 * RG_CB), lambda j: (j, 0, 0)),
            pl.BlockSpec((2, RG_CB), lambda j: (0, j)),
        ],
        out_specs=pl.BlockSpec((NPH, n_chunks, RG_CB), lambda j: (0, 0, j)),
        out_shape=jax.ShapeDtypeStruct((NPH, n_chunks, d_rg), F32),
        scratch_shapes=[
            pltpu.VMEM((NPH + 3, n_chunks, RG_CB), F32),
            pltpu.VMEM((NPH + 1, n_chunks, RG_CB), F32),
            pltpu.VMEM((NPH + 1, n_chunks, RG_CB), F32),
        ],
        compiler_params=_cparams(1),
        name="rglru",
    )(xrg3, conv_w, conv_b, wg, bg, lam)


def _rglru_gate_weights(wa, ba, wx, bx):
    n_dir, heads, hd, _ = wa.shape
    per = RG_CB // hd
    nb = heads // per
    eye = jnp.eye(per, dtype=F32)

    def blockdiag(w):
        w = w.astype(F32).reshape(nb, per, hd, hd)
        return jnp.einsum('bpij,pq->bpiqj', w, eye).reshape(nb, RG_CB, RG_CB)

    cols, bias = [], []
    for d in range(n_dir):
        for w, b in ((wa, ba), (wx, bx)):
            cols.append(blockdiag(w[d]))
            bias.append(b[d].astype(F32).reshape(nb, 1, RG_CB))
    return jnp.concatenate(cols, axis=-1).astype(BF16), jnp.concatenate(bias, axis=-1)


def _rms_norm(x, g):
    return x * lax.rsqrt(jnp.mean(x * x, axis=-1, keepdims=True) + LN_EPS) * g


def _mixout_kernel(alpha, ys5_ref, yrg_ref, z_ref, h_ref, gw_ref, gb_ref, gn5_ref, gnr_ref,
                   w_ref, b_ref, g_ref, be_ref, o_ref):
    rows = NPH * CT
    d_s5 = ys5_ref.shape[-1]
    y = ys5_ref[...].reshape(rows, d_s5)
    glu = y * _sigmoid(jnp.dot(y.astype(BF16), gw_ref[...], preferred_element_type=F32) + gb_ref[...])
    ys5n = _rms_norm(glu, gn5_ref[...]).astype(BF16)
    yrg = yrg_ref[...].reshape(rows, -1) * _gelu(z_ref[...].reshape(rows, -1))
    yrgn = _rms_norm(yrg, gnr_ref[...]).astype(BF16)
    mix = (jnp.dot(ys5n, w_ref[:d_s5, :], preferred_element_type=F32)
           + jnp.dot(yrgn, w_ref[d_s5:, :], preferred_element_type=F32) + b_ref[...])
    h1 = _layer_norm(alpha * h_ref[...].reshape(rows, -1) + mix, g_ref[...], be_ref[...])
    o_ref[...] = h1.reshape(o_ref.shape)


def _mixout(alpha, ys5, yrg, zrg, h, glu_w, glu_b, gn_s5, gn_rg, w_out, b_out, ln_g, ln_b):
    _, n_chunks, d_model = h.shape
    d_s5, d_rg = ys5.shape[-1], yrg.shape[-1]
    return pl.pallas_call(
        functools.partial(_mixout_kernel, alpha),
        grid=(n_chunks // CT,),
        in_specs=[
            _tile3(d_s5), _tile3(d_rg), _tile3(d_rg), _tile3(d_model),
            _resident((d_s5, d_s5)), _resident((1, d_s5)), _resident((1, d_s5)), _resident((1, d_rg)),
            _resident((d_s5 + d_rg, d_model)), _resident((1, d_model)),
            _resident((1, d_model)), _resident((1, d_model)),
        ],
        out_specs=_tile3(d_model),
        out_shape=jax.ShapeDtypeStruct((NPH, n_chunks, d_model), F32),
        compiler_params=_cparams(1),
        name="mixout",
    )(ys5, yrg, zrg, h, glu_w, glu_b, gn_s5, gn_rg, w_out, b_out, ln_g, ln_b)


def _ffn_kernel(alpha, h_ref, p_ref, w1_ref, b1_ref, w2_ref, b2_ref, pw_ref, gw_ref, gb_ref,
                g_ref, be_ref, o_ref, hb_ref, acc_ref):
    j = pl.program_id(1)
    rows = NPH * CT

    @pl.when(j == 0)
    def _():
        hb = h_ref[...].reshape(rows, -1).astype(BF16)
        hb_ref[...] = hb
        gate = _sigmoid(jnp.dot(hb, gw_ref[...], preferred_element_type=F32) + gb_ref[...])
        pe = _to_phase_major(p_ref[...]).astype(BF16)
        acc_ref[...] = gate * jnp.dot(pe, pw_ref[...], preferred_element_type=F32) + b2_ref[...]

    a = jnp.dot(hb_ref[...], w1_ref[...], preferred_element_type=F32) + b1_ref[...]
    a = jnp.maximum(a, 0.0)
    acc_ref[...] += jnp.dot((a * a).astype(BF16), w2_ref[...], preferred_element_type=F32)

    @pl.when(j == pl.num_programs(1) - 1)
    def _():
        out = _layer_norm(alpha * h_ref[...].reshape(rows, -1) + acc_ref[...], g_ref[...], be_ref[...])
        o_ref[...] = _to_time_major(out)


def _ffn(alpha, h1, p2, w1, b1, w2, b2, ple_w, gate_w, gate_b, ln_g, ln_b, ff_tile):
    _, n_chunks, d_model = h1.shape
    d_ff = w1.shape[1]
    ple_dim = ple_w.shape[0]
    rows = NPH * CT
    return pl.pallas_call(
        functools.partial(_ffn_kernel, alpha),
        grid=(n_chunks // CT, d_ff // ff_tile),
        in_specs=[
            _tile3(d_model),
            pl.BlockSpec((rows, ple_dim), lambda i, j: (i, 0)),
            pl.BlockSpec((d_model, ff_tile), lambda i, j: (0, j)),
            pl.BlockSpec((1, ff_tile), lambda i, j: (0, j)),
            pl.BlockSpec((ff_tile, d_model), lambda i, j: (j, 0)),
            _resident((1, d_model)),
            _resident((ple_dim, d_model)), _resident((d_model, d_model)), _resident((1, d_model)),
            _resident((1, d_model)), _resident((1, d_model)),
        ],
        out_specs=pl.BlockSpec((rows, d_model), lambda i, j: (i, 0)),
        out_shape=jax.ShapeDtypeStruct((NPH * n_chunks, d_model), F32),
        scratch_shapes=[pltpu.VMEM((rows, d_model), BF16), pltpu.VMEM((rows, d_model), F32)],
        compiler_params=_cparams(2),
        name="ffn",
    )(h1, p2, w1, b1, w2, b2, ple_w, gate_w, gate_b, ln_g, ln_b)


def kernel(x, p, ln_in_g, ln_in_b, w_in, b_in, s5_lambda_re, s5_lambda_im, s5_log_step, s5_b_re, s5_b_im, s5_c_re, s5_c_im, s5_d, s5_glu_w, s5_glu_b, rg_conv_w, rg_conv_b, rg_wa, rg_ba, rg_wx, rg_bx, rg_lambda, gn_s5, gn_rg, w_out, b_out, ln1_g, ln1_b, w_ff1, b_ff1, w_ff2, b_ff2, ple_w, ple_gate_w, ple_gate_b, ln2_g, ln2_b):
    batch, seq, d_model = x.shape
    depth = w_in.shape[0]
    assert batch == 1 and depth == 1 and seq % (NPH * CT) == 0
    n_chunks = seq // NPH
    d_s5 = s5_glu_w.shape[-1]
    d_rg = rg_conv_w.shape[-1]
    alpha = (2.0 * depth) ** 0.25
    row = lambda v: v.reshape(1, -1).astype(F32)

    h, u3, xrg, zrg = _inproj(x.reshape(seq, d_model), row(ln_in_g), row(ln_in_b),
                              w_in[0].astype(BF16), row(b_in[0]), d_s5, d_rg)

    pa, pb, dvec, q_rows = _s5_discretise(
        s5_lambda_re[0], s5_lambda_im[0], s5_log_step[0], s5_b_re[0], s5_b_im[0],
        s5_c_re[0], s5_c_im[0], s5_d[0], n_chunks)
    ys5 = _s5(u3, pa, pb, dvec, q_rows)

    wg, bg = _rglru_gate_weights(rg_wa[0], rg_ba[0], rg_wx[0], rg_bx[0])
    yrg = _rglru(xrg, rg_conv_w[0].astype(F32), row(rg_conv_b[0]), wg, bg, rg_lambda[0].astype(F32))

    h1 = _mixout(alpha, ys5, yrg, zrg, h, s5_glu_w[0].astype(BF16), row(s5_glu_b[0]), row(gn_s5[0]),
                 row(gn_rg[0]), w_out[0].astype(BF16), row(b_out[0]), row(ln1_g[0]), row(ln1_b[0]))

    out = _ffn(alpha, h1, p.reshape(seq, p.shape[-1]), w_ff1[0].astype(BF16), row(b_ff1[0]),
               w_ff2[0].astype(BF16), row(b_ff2[0]), ple_w[0].astype(BF16), ple_gate_w[0].astype(BF16),
               row(ple_gate_b[0]), row(ln2_g[0]), row(ln2_b[0]), ff_tile=512)
    return out.reshape(batch, seq, d_model).astype(x.dtype)
```

```python
import functools
import math

import jax
import jax.numpy as jnp
from jax import lax
from jax.experimental import pallas as pl
from jax.experimental.pallas import tpu as pltpu

F32 = jnp.float32
BF16 = jnp.bfloat16

NPH = 16
CT = 32
S5_K = 16
S5_P = 64
S5_LB = 128
RG_CB = 128
CONV_W = 4
RG_C = 8.0
LN_EPS = 1e-5
VMEM_LIMIT_V7X = 56 * 1024 * 1024


def _cparams(n_axes):
    return pltpu.CompilerParams(
        dimension_semantics=("arbitrary",) * n_axes,
        vmem_limit_bytes=VMEM_LIMIT_V7X)


def _resident(shape):
    return pl.BlockSpec(shape, lambda *_: (0,) * len(shape), pipeline_mode=pl.Buffered(1))


def _tile3(width):
    return pl.BlockSpec((NPH, CT, width), lambda i, *_: (0, i, 0))


def _to_phase_major(x):
    w = x.shape[-1]
    return jnp.swapaxes(x.reshape(CT, NPH, w), 0, 1).reshape(NPH * CT, w)


def _to_time_major(x):
    w = x.shape[-1]
    return jnp.swapaxes(x.reshape(NPH, CT, w), 0, 1).reshape(CT * NPH, w)


def _layer_norm(x, g, b):
    mu = jnp.mean(x, axis=-1, keepdims=True)
    xc = x - mu
    var = jnp.mean(xc * xc, axis=-1, keepdims=True)
    return xc * lax.rsqrt(var + LN_EPS) * g + b


def _gelu(x):
    c = math.sqrt(2.0 / math.pi)
    return 0.5 * x * (1.0 + jnp.tanh(c * (x + 0.044715 * (x * x * x))))


def _sigmoid(x):
    return 1.0 / (1.0 + jnp.exp(-x))


def _inproj_kernel(x_ref, g_ref, b_ref, w_ref, bw_ref, h_ref, u_ref, xrg_ref, zrg_ref):
    d_s5, d_rg = u_ref.shape[-1], xrg_ref.shape[-1]
    h = _layer_norm(_to_phase_major(x_ref[...]), g_ref[...], b_ref[...])
    h_ref[...] = h.reshape(h_ref.shape)
    proj = jnp.dot(h.astype(BF16), w_ref[...], preferred_element_type=F32) + bw_ref[...]
    u_ref[...] = proj[:, :d_s5].astype(BF16).reshape(u_ref.shape)
    xrg_ref[...] = proj[:, d_s5:d_s5 + d_rg].reshape(xrg_ref.shape)
    zrg_ref[...] = proj[:, d_s5 + d_rg:].reshape(zrg_ref.shape)


def _inproj(x2, ln_g, ln_b, w_in, b_in, d_s5, d_rg):
    seq, d_model = x2.shape
    n_chunks = seq // NPH
    return pl.pallas_call(
        _inproj_kernel,
        grid=(n_chunks // CT,),
        in_specs=[
            pl.BlockSpec((CT * NPH, d_model), lambda i: (i, 0)),
            _resident((1, d_model)), _resident((1, d_model)),
            _resident(w_in.shape), _resident(b_in.shape),
        ],
        out_specs=[_tile3(d_model), _tile3(d_s5), _tile3(d_rg), _tile3(d_rg)],
        out_shape=[
            jax.ShapeDtypeStruct((NPH, n_chunks, d_model), F32),
            jax.ShapeDtypeStruct((NPH, n_chunks, d_s5), BF16),
            jax.ShapeDtypeStruct((NPH, n_chunks, d_rg), F32),
            jax.ShapeDtypeStruct((NPH, n_chunks, d_rg), F32),
        ],
        compiler_params=_cparams(1),
        name="inproj",
    )(x2, ln_g, ln_b, w_in, b_in)


def _row_scan_exclusive(sre, sim, qre, qim, reverse):
    n, w = sre.shape
    row = lax.broadcasted_iota(jnp.int32, (n, w), 0)

    def shift(v, k):
        if k % 8 == 0:
            z = jnp.zeros((k, w), v.dtype)
            return jnp.concatenate([v[k:], z] if reverse else [z, v[:n - k]], axis=0)
        if reverse:
            return jnp.where(row < n - k, pltpu.roll(v, n - k, axis=0), 0.0)
        return jnp.where(row >= k, pltpu.roll(v, k, axis=0), 0.0)

    xre, xim = shift(sre, 1), shift(sim, 1)
    k, i = 1, 0
    while k < n:
        pr, pi = qre[i:i + 1], qim[i:i + 1]
        if k % 8 == 0:
            keep = slice(n - k, n) if reverse else slice(0, k)
            dst = slice(0, n - k) if reverse else slice(k, n)
            src = slice(k, n) if reverse else slice(0, n - k)
            sr, si = xre[src], xim[src]
            nre = xre[dst] + pr * sr - pi * si
            nim = xim[dst] + pr * si + pi * sr
            order = (lambda new, old: [new, old]) if reverse else (lambda new, old: [old, new])
            xre = jnp.concatenate(order(nre, xre[keep]), axis=0)
            xim = jnp.concatenate(order(nim, xim[keep]), axis=0)
        else:
            sr, si = shift(xre, k), shift(xim, k)
            xre, xim = xre + pr * sr - pi * si, xim + pr * si + pi * sr
        k *= 2
        i += 1
    return xre, xim


def _dot(a, b):
    return jnp.dot(a.astype(BF16), b.astype(BF16), preferred_element_type=F32)


def _s5_chunk_operators(pa, pb, dvec):
    tk = NPH * S5_K
    lane16 = lax.broadcasted_iota(jnp.int32, (16, tk), 1)
    row16 = lax.broadcasted_iota(jnp.int32, (16, tk), 0)
    rep = (lax.shift_right_logical(lane16, 4) == row16).astype(F32)
    til = ((lane16 & 15) == row16).astype(F32)
    lane_t = lax.broadcasted_iota(jnp.int32, (tk, 16), 0)
    col_t = lax.broadcasted_iota(jnp.int32, (tk, 16), 1)
    rep_t = (lax.shift_right_logical(lane_t, 4) == col_t).astype(F32)
    til_t = ((lane_t & 15) == col_t).astype(F32)

    def cmul(ar, ai, br, bi):
        return ar * br - ai * bi, ar * bi + ai * br

    def w_of(c0):
        return cmul(_dot(pa[:, c0:c0 + 16], rep), _dot(pa[:, c0 + 16:c0 + 32], rep),
                    _dot(pa[:, c0 + 64:c0 + 80], til), _dot(pa[:, c0 + 80:c0 + 96], til))

    wf_re, wf_im = w_of(0)
    wb_re, wb_im = w_of(32)
    bc = jnp.concatenate([wf_re, wf_im, wb_re, wb_im], axis=0)

    def c_of(r0):
        cr, ci = cmul(_dot(til_t, pb[r0 + 64:r0 + 80]), _dot(til_t, pb[r0 + 80:r0 + 96]),
                      _dot(rep_t, pb[r0:r0 + 16]), _dot(rep_t, pb[r0 + 16:r0 + 32]))
        return cr, -ci

    cc = jnp.concatenate(list(c_of(0)) + list(c_of(32)), axis=1)

    kf = _dot(pb[64:80], wf_re) - _dot(pb[80:96], wf_im)
    kb = _dot(pb[96:112], wb_re) - _dot(pb[112:128], wb_im)
    blocks = []
    for t in range(NPH):
        left = S5_K * (NPH - 1 - t)
        right = S5_K * t
        f = kf if left == 0 else jnp.where(lane16 < tk - left, pltpu.roll(kf, tk - left, axis=1), 0.0)
        b = kb if right == 0 else jnp.where(lane16 >= right, pltpu.roll(kb, right, axis=1), 0.0)
        blocks.append(f + b)
    a = jnp.concatenate(blocks, axis=0)
    ri = lax.broadcasted_iota(jnp.int32, (tk, tk), 0)
    ci = lax.broadcasted_iota(jnp.int32, (tk, tk), 1)
    a = a + jnp.where(ri == ci, dvec, 0.0)
    return a.astype(BF16), bc.astype(BF16), cc.astype(BF16)


def _s5_kernel(u_ref, pa_ref, pb_ref, d_ref, q_ref, y_ref, xt_ref, yt_ref):
    n = u_ref.shape[1]
    p2 = 2 * S5_P
    tk = NPH * S5_K

    def to_rows(s, _):
        xt_ref[s] = u_ref[s].T
        return 0
    lax.fori_loop(0, NPH, to_rows, 0)

    def pair(gp, _):
        xs, ys, bcs, ccs = [], [], [], []
        for h in range(2):
            g = 2 * gp + h
            rows = pl.ds(pl.multiple_of(g * S5_K, S5_K), S5_K)
            a, bc, cc = _s5_chunk_operators(pa_ref[g], pb_ref[g], d_ref[g])
            x = xt_ref[:, rows, :].reshape(tk, n)
            xs.append(x)
            ys.append(jnp.dot(a, x, preferred_element_type=F32))
            bcs.append(bc)
            ccs.append(cc)
        s = jnp.dot(jnp.concatenate(bcs, axis=1), jnp.concatenate(xs, axis=0), preferred_element_type=F32)
        st = [s[i * p2:(i + 1) * p2].T for i in range(4)]
        q = q_ref[gp]
        hf_re, hf_im = _row_scan_exclusive(st[0], st[1], q[0:16], q[16:32], False)
        hb_re, hb_im = _row_scan_exclusive(st[2], st[3], q[32:48], q[48:64], True)
        hin = jnp.concatenate([hf_re, hf_im, hb_re, hb_im], axis=1).astype(BF16)
        for h in range(2):
            g = 2 * gp + h
            rows = pl.ds(pl.multiple_of(g * S5_K, S5_K), S5_K)
            y = ys[h] + lax.dot_general(ccs[h], hin, (((1,), (1,)), ((), ())), preferred_element_type=F32)
            yt_ref[:, rows, :] = _gelu(y).reshape(NPH, S5_K, n)
        return 0
    lax.fori_loop(0, S5_LB // (2 * S5_K), pair, 0)

    def to_lanes(t, _):
        y_ref[t] = yt_ref[t].T
        return 0
    lax.fori_loop(0, NPH, to_lanes, 0)


def _s5(u3, pa, pb, dvec, q_rows):
    _, n_chunks, d_s5 = u3.shape
    gpb = S5_LB // S5_K
    tk = NPH * S5_K
    per_block = lambda n, *tail: pl.BlockSpec((n,) + tail, lambda b: (b,) + (0,) * len(tail))
    return pl.pallas_call(
        _s5_kernel,
        grid=(d_s5 // S5_LB,),
        in_specs=[
            pl.BlockSpec((NPH, n_chunks, S5_LB), lambda b: (0, 0, b)),
            per_block(gpb, 2 * S5_P, 128), per_block(gpb, 128, 2 * S5_P), per_block(gpb, 1, tk),
            per_block(gpb // 2, 64, 2 * S5_P),
        ],
        out_specs=pl.BlockSpec((NPH, n_chunks, S5_LB), lambda b: (0, 0, b)),
        out_shape=jax.ShapeDtypeStruct((NPH, n_chunks, d_s5), F32),
        scratch_shapes=[
            pltpu.VMEM((NPH, S5_LB, n_chunks), BF16),
            pltpu.VMEM((NPH, S5_LB, n_chunks), F32),
        ],
        compiler_params=_cparams(1),
        name="s5",
    )(u3, pa, pb, dvec, q_rows)


def _s5_discretise(lam_re, lam_im, log_step, b_re, b_im, c_re, c_im, d, n_chunks):
    assert NPH == 16 and S5_K == 16 and n_chunks <= 2 ** 16
    lre = jnp.minimum(lam_re.astype(F32), -1e-4)
    lim = lam_im.astype(F32)
    step = jnp.exp(log_step.astype(F32))[..., None]
    are, aim = lre * step, lim * step

    def cexp(e):
        mag = jnp.exp(are[..., None] * e)
        ang = aim[..., None] * e
        return mag * jnp.cos(ang), mag * jnp.sin(ang)

    j = jnp.arange(NPH + 1, dtype=F32)
    pw_re, pw_im = cexp(j)
    nr, ni = pw_re[..., 1] - 1.0, pw_im[..., 1]
    den = lre * lre + lim * lim
    zr, zi = (nr * lre + ni * lim) / den, (ni * lre - nr * lim) / den
    br, bi = b_re.astype(F32), b_im.astype(F32)
    bbar_re = zr[..., None] * br - zi[..., None] * bi
    bbar_im = zr[..., None] * bi + zi[..., None] * br

    desc = lambda v: v[..., NPH - 1::-1]
    asc = lambda v: v[..., :NPH]
    pa = jnp.concatenate([desc(pw_re[0]), desc(pw_im[0]), asc(pw_re[1]), asc(pw_im[1]),
                          bbar_re[0], bbar_im[0], bbar_re[1], bbar_im[1]], axis=-1)
    rows = lambda v: jnp.swapaxes(v, -1, -2)
    up = lambda v: v[..., 1:]
    down = lambda v: v[..., NPH:0:-1]
    pb = jnp.concatenate([rows(up(pw_re[0])), rows(up(pw_im[0])), rows(down(pw_re[1])), rows(down(pw_im[1])),
                          c_re[0].astype(F32), c_im[0].astype(F32), c_re[1].astype(F32), c_im[1].astype(F32)],
                         axis=1)
    groups, p = lam_re.shape[1:]
    half = (jnp.arange(groups) % 2)[:, None, None, None] == jnp.arange(2)[None, :, None, None]
    pa = jnp.where(half, pa[:, None], 0.0).reshape(groups, 2 * p, 128)
    pb = jnp.where(half, pb[:, None], 0.0).transpose(0, 2, 1, 3).reshape(groups, 128, 2 * p)
    dvec = jnp.tile(d.astype(F32), (1, NPH))[:, None, :]
    qr, qi = cexp(NPH * (2.0 ** jnp.arange(16, dtype=F32)))
    lanes = lambda v: v.reshape(groups // 2, 2, p, 16).transpose(0, 3, 1, 2).reshape(groups // 2, 16, 2 * p)
    q_rows = jnp.concatenate([lanes(qr[0]), lanes(qi[0]), lanes(qr[1]), lanes(qi[1])], axis=1)
    return pa, pb, dvec, q_rows


def _row_scan_carry(a, h, reverse):
    n, w = a.shape
    row = lax.broadcasted_iota(jnp.int32, (n, w), 0)

    def shift(v, k, fill):
        if k % 8 == 0:
            z = jnp.full((k, w), fill, v.dtype)
            if reverse:
                return jnp.concatenate([v[k:], z], axis=0)
            return jnp.concatenate([z, v[:n - k]], axis=0)
        if reverse:
            return jnp.where(row < n - k, pltpu.roll(v, n - k, axis=0), fill)
        return jnp.where(row >= k, pltpu.roll(v, k, axis=0), fill)

    k = 1
    while k < n:
        h = h + a * shift(h, k, 0.0)
        a = a * shift(a, k, 1.0)
        k *= 2
    return shift(h, 1, 0.0)


def _rglru_kernel(x_ref, cw_ref, cb_ref, wg_ref, bg_ref, lam_ref, y_ref, xe_ref, al_ref, hl_ref):
    n, w = x_ref.shape[1], x_ref.shape[2]
    row = lax.broadcasted_iota(jnp.int32, (n, w), 0)

    def from_prev_chunk(v):
        return jnp.where(row >= 1, pltpu.roll(v, 1, axis=0), 0.0)

    def from_next_chunk(v):
        return jnp.where(row < n - 1, pltpu.roll(v, n - 1, axis=0), 0.0)

    xe_ref[0] = from_prev_chunk(x_ref[NPH - 2])
    xe_ref[1] = from_prev_chunk(x_ref[NPH - 1])

    def copy_body(s, _):
        xe_ref[s + 2] = x_ref[s]
        return 0
    lax.fori_loop(0, NPH, copy_body, 0)
    xe_ref[NPH + 2] = from_next_chunk(x_ref[0])

    cw = cw_ref[...]
    cb = cb_ref[...]
    lam = lam_ref[...]
    neg = -lam
    softplus = jnp.maximum(neg, 0.0) + jnp.log(1.0 + jnp.exp(-jnp.abs(neg)))

    for d, reverse in ((0, False), (1, True)):
        rate = (-RG_C) * softplus[d:d + 1]
        wg = wg_ref[0][:, 2 * w * d:2 * w * (d + 1)]
        bg = bg_ref[0][:, 2 * w * d:2 * w * (d + 1)]
        init = NPH if reverse else 0
        al_ref[init] = jnp.ones((n, w), F32)
        hl_ref[init] = jnp.zeros((n, w), F32)

        def local_body(i, _):
            s = (NPH - 1 - i) if reverse else i
            xc = (cw[0:1] * xe_ref[s] + cw[1:2] * xe_ref[s + 1]
                  + cw[2:3] * xe_ref[s + 2] + cw[3:4] * xe_ref[s + 3] + cb)
            g = jnp.dot(xc.astype(BF16), wg, preferred_element_type=F32) + bg
            r = _sigmoid(g[:, :w])
            ig = _sigmoid(g[:, w:])
            a = jnp.exp(rate * r)
            bt = jnp.sqrt(1.0 - a * a) * (ig * xc)
            src = (s + 1) if reverse else s
            dst = s if reverse else (s + 1)
            hl_ref[dst] = a * hl_ref[src] + bt
            al_ref[dst] = a * al_ref[src]
            return 0
        lax.fori_loop(0, NPH, local_body, 0)

        last = 0 if reverse else NPH
        carry = _row_scan_carry(al_ref[last], hl_ref[last], reverse)

        def fix_body(s, _):
            slot = s if reverse else (s + 1)
            v = hl_ref[slot] + al_ref[slot] * carry
            if reverse:
                y_ref[s] = y_ref[s] + v
            else:
                y_ref[s] = v
            return 0
        lax.fori_loop(0, NPH, fix_body, 0)


def _rglru(xrg3, conv_w, conv_b, wg, bg, lam):
    _, n_chunks, d_rg = xrg3.shape
    nb = d_rg // RG_CB
    return pl.pallas_call(
        _rglru_kernel,
        grid=(nb,),
        in_specs=[
            pl.BlockSpec((NPH, n_chunks, RG_CB), lambda j: (0, 0, j)),
            pl.BlockSpec((CONV_W, RG_CB), lambda j: (0, j)),
            pl.BlockSpec((1, RG_CB), lambda j: (0, j)),
            pl.BlockSpec((1, RG_CB, 4 * RG_CB), lambda j: (j, 0, 0)),
            pl.BlockSpec((1, 1, 4 * RG_CB), lambda j: (j, 0, 0)),
            pl.BlockSpec((2, RG_CB), lambda j: (0, j)),
        ],
        out_specs=pl.BlockSpec((NPH, n_chunks, RG_CB), lambda j: (0, 0, j)),
        out_shape=jax.ShapeDtypeStruct((NPH, n_chunks, d_rg), F32),
        scratch_shapes=[
            pltpu.VMEM((NPH + 3, n_chunks, RG_CB), F32),
            pltpu.VMEM((NPH + 1, n_chunks, RG_CB), F32),
            pltpu.VMEM((NPH + 1, n_chunks, RG_CB), F32),
        ],
        compiler_params=_cparams(1),
        name="rglru",
    )(xrg3, conv_w, conv_b, wg, bg, lam)


def _rglru_gate_weights(wa, ba, wx, bx):
    n_dir, heads, hd, _ = wa.shape
    per = RG_CB // hd
    nb = heads // per
    eye = jnp.eye(per, dtype=F32)

    def blockdiag(w):
        w = w.astype(F32).reshape(nb, per, hd, hd)
        return jnp.einsum('bpij,pq->bpiqj', w, eye).reshape(nb, RG_CB, RG_CB)

    cols, bias = [], []
    for d in range(n_dir):
        for w, b in ((wa, ba), (wx, bx)):
            cols.append(blockdiag(w[d]))
            bias.append(b[d].astype(F32).reshape(nb, 1, RG_CB))
    return jnp.concatenate(cols, axis=-1).astype(BF16), jnp.concatenate(bias, axis=-1)


def _rms_norm(x, g):
    return x * lax.rsqrt(jnp.mean(x * x, axis=-1, keepdims=True) + LN_EPS) * g


def _mixout_kernel(alpha, ys5_ref, yrg_ref, z_ref, h_ref, gw_ref, gb_ref, gn5_ref, gnr_ref,
                   w_ref, b_ref, g_ref, be_ref, o_ref):
    rows = NPH * CT
    d_s5 = ys5_ref.shape[-1]
    y = ys5_ref[...].reshape(rows, d_s5)
    glu = y * _sigmoid(jnp.dot(y.astype(BF16), gw_ref[...], preferred_element_type=F32) + gb_ref[...])
    ys5n = _rms_norm(glu, gn5_ref[...]).astype(BF16)
    yrg = yrg_ref[...].reshape(rows, -1) * _gelu(z_ref[...].reshape(rows, -1))
    yrgn = _rms_norm(yrg, gnr_ref[...]).astype(BF16)
    mix = (jnp.dot(ys5n, w_ref[:d_s5, :], preferred_element_type=F32)
           + jnp.dot(yrgn, w_ref[d_s5:, :], preferred_element_type=F32) + b_ref[...])
    h1 = _layer_norm(alpha * h_ref[...].reshape(rows, -1) + mix, g_ref[...], be_ref[...])
    o_ref[...] = h1.reshape(o_ref.shape)


def _mixout(alpha, ys5, yrg, zrg, h, glu_w, glu_b, gn_s5, gn_rg, w_out, b_out, ln_g, ln_b):
    _, n_chunks, d_model = h.shape
    d_s5, d_rg = ys5.shape[-1], yrg.shape[-1]
    return pl.pallas_call(
        functools.partial(_mixout_kernel, alpha),
        grid=(n_chunks // CT,),
        in_specs=[
            _tile3(d_s5), _tile3(d_rg), _tile3(d_rg), _tile3(d_model),
            _resident((d_s5, d_s5)), _resident((1, d_s5)), _resident((1, d_s5)), _resident((1, d_rg)),
            _resident((d_s5 + d_rg, d_model)), _resident((1, d_model)),
            _resident((1, d_model)), _resident((1, d_model)),
        ],
        out_specs=_tile3(d_model),
        out_shape=jax.ShapeDtypeStruct((NPH, n_chunks, d_model), F32),
        compiler_params=_cparams(1),
        name="mixout",
    )(ys5, yrg, zrg, h, glu_w, glu_b, gn_s5, gn_rg, w_out, b_out, ln_g, ln_b)


def _ffn_kernel(alpha, h_ref, p_ref, w1_ref, b1_ref, w2_ref, b2_ref, pw_ref, gw_ref, gb_ref,
                g_ref, be_ref, o_ref, hb_ref, acc_ref):
    j = pl.program_id(1)
    rows = NPH * CT

    @pl.when(j == 0)
    def _():
        hb = h_ref[...].reshape(rows, -1).astype(BF16)
        hb_ref[...] = hb
        gate = _sigmoid(jnp.dot(hb, gw_ref[...], preferred_element_type=F32) + gb_ref[...])
        pe = _to_phase_major(p_ref[...]).astype(BF16)
        acc_ref[...] = gate * jnp.dot(pe, pw_ref[...], preferred_element_type=F32) + b2_ref[...]

    a = jnp.dot(hb_ref[...], w1_ref[...], preferred_element_type=F32) + b1_ref[...]
    a = jnp.maximum(a, 0.0)
    acc_ref[...] += jnp.dot((a * a).astype(BF16), w2_ref[...], preferred_element_type=F32)

    @pl.when(j == pl.num_programs(1) - 1)
    def _():
        out = _layer_norm(alpha * h_ref[...].reshape(rows, -1) + acc_ref[...], g_ref[...], be_ref[...])
        o_ref[...] = _to_time_major(out)


def _ffn(alpha, h1, p2, w1, b1, w2, b2, ple_w, gate_w, gate_b, ln_g, ln_b, ff_tile):
    _, n_chunks, d_model = h1.shape
    d_ff = w1.shape[1]
    ple_dim = ple_w.shape[0]
    rows = NPH * CT
    return pl.pallas_call(
        functools.partial(_ffn_kernel, alpha),
        grid=(n_chunks // CT, d_ff // ff_tile),
        in_specs=[
            _tile3(d_model),
            pl.BlockSpec((rows, ple_dim), lambda i, j: (i, 0)),
            pl.BlockSpec((d_model, ff_tile), lambda i, j: (0, j)),
            pl.BlockSpec((1, ff_tile), lambda i, j: (0, j)),
            pl.BlockSpec((ff_tile, d_model), lambda i, j: (j, 0)),
            _resident((1, d_model)),
            _resident((ple_dim, d_model)), _resident((d_model, d_model)), _resident((1, d_model)),
            _resident((1, d_model)), _resident((1, d_model)),
        ],
        out_specs=pl.BlockSpec((rows, d_model), lambda i, j: (i, 0)),
        out_shape=jax.ShapeDtypeStruct((NPH * n_chunks, d_model), F32),
        scratch_shapes=[pltpu.VMEM((rows, d_model), BF16), pltpu.VMEM((rows, d_model), F32)],
        compiler_params=_cparams(2),
        name="ffn",
    )(h1, p2, w1, b1, w2, b2, ple_w, gate_w, gate_b, ln_g, ln_b)


def kernel(x, p, ln_in_g, ln_in_b, w_in, b_in, s5_lambda_re, s5_lambda_im, s5_log_step, s5_b_re, s5_b_im, s5_c_re, s5_c_im, s5_d, s5_glu_w, s5_glu_b, rg_conv_w, rg_conv_b, rg_wa, rg_ba, rg_wx, rg_bx, rg_lambda, gn_s5, gn_rg, w_out, b_out, ln1_g, ln1_b, w_ff1, b_ff1, w_ff2, b_ff2, ple_w, ple_gate_w, ple_gate_b, ln2_g, ln2_b):
    batch, seq, d_model = x.shape
    depth = w_in.shape[0]
    assert batch == 1 and depth == 1 and seq % (NPH * CT) == 0
    n_chunks = seq // NPH
    d_s5 = s5_glu_w.shape[-1]
    d_rg = rg_conv_w.shape[-1]
    alpha = (2.0 * depth) ** 0.25
    row = lambda v: v.reshape(1, -1).astype(F32)

    h, u3, xrg, zrg = _inproj(x.reshape(seq, d_model), row(ln_in_g), row(ln_in_b),
                              w_in[0].astype(BF16), row(b_in[0]), d_s5, d_rg)

    pa, pb, dvec, q_rows = _s5_discretise(
        s5_lambda_re[0], s5_lambda_im[0], s5_log_step[0], s5_b_re[0], s5_b_im[0],
        s5_c_re[0], s5_c_im[0], s5_d[0], n_chunks)
    ys5 = _s5(u3, pa, pb, dvec, q_rows)

    wg, bg = _rglru_gate_weights(rg_wa[0], rg_ba[0], rg_wx[0], rg_bx[0])
    yrg = _rglru(xrg, rg_conv_w[0].astype(F32), row(rg_conv_b[0]), wg, bg, rg_lambda[0].astype(F32))

    h1 = _mixout(alpha, ys5, yrg, zrg, h, s5_glu_w[0].astype(BF16), row(s5_glu_b[0]), row(gn_s5[0]),
                 row(gn_rg[0]), w_out[0].astype(BF16), row(b_out[0]), row(ln1_g[0]), row(ln1_b[0]))

    out = _ffn(alpha, h1, p.reshape(seq, p.shape[-1]), w_ff1[0].astype(BF16), row(b_ff1[0]),
               w_ff2[0].astype(BF16), row(b_ff2[0]), ple_w[0].astype(BF16), ple_gate_w[0].astype(BF16),
               row(ple_gate_b[0]), row(ln2_g[0]), row(ln2_b[0]), ff_tile=1024)
    return out.reshape(batch, seq, d_model).astype(x.dtype)
```

```python
import functools
import math

import jax
import jax.numpy as jnp
from jax import lax
from jax.experimental import pallas as pl
from jax.experimental.pallas import tpu as pltpu

F32 = jnp.float32
BF16 = jnp.bfloat16

NPH = 16
CT = 32
S5_K = 16
S5_P = 64
S5_LB = 128
RG_CB = 128
CONV_W = 4
RG_C = 8.0
LN_EPS = 1e-5
VMEM_LIMIT_V7X = 56 * 1024 * 1024


def _cparams(n_axes):
    return pltpu.CompilerParams(
        dimension_semantics=("arbitrary",) * n_axes,
        vmem_limit_bytes=VMEM_LIMIT_V7X)


def _resident(shape):
    return pl.BlockSpec(shape, lambda *_: (0,) * len(shape), pipeline_mode=pl.Buffered(1))


def _tile3(width):
    return pl.BlockSpec((NPH, CT, width), lambda i, *_: (0, i, 0))


def _to_phase_major(x):
    w = x.shape[-1]
    return jnp.swapaxes(x.reshape(CT, NPH, w), 0, 1).reshape(NPH * CT, w)


def _to_time_major(x):
    w = x.shape[-1]
    return jnp.swapaxes(x.reshape(NPH, CT, w), 0, 1).reshape(CT * NPH, w)


def _layer_norm(x, g, b):
    mu = jnp.mean(x, axis=-1, keepdims=True)
    xc = x - mu
    var = jnp.mean(xc * xc, axis=-1, keepdims=True)
    return xc * lax.rsqrt(var + LN_EPS) * g + b


def _gelu(x):
    c = math.sqrt(2.0 / math.pi)
    return 0.5 * x * (1.0 + jnp.tanh(c * (x + 0.044715 * (x * x * x))))


def _sigmoid(x):
    return 1.0 / (1.0 + jnp.exp(-x))


def _inproj_kernel(x_ref, g_ref, b_ref, w_ref, bw_ref, h_ref, u_ref, xrg_ref, zrg_ref):
    d_s5, d_rg = u_ref.shape[-1], xrg_ref.shape[-1]
    h = _layer_norm(_to_phase_major(x_ref[...]), g_ref[...], b_ref[...])
    h_ref[...] = h.reshape(h_ref.shape)
    proj = jnp.dot(h.astype(BF16), w_ref[...], preferred_element_type=F32) + bw_ref[...]
    u_ref[...] = proj[:, :d_s5].astype(BF16).reshape(u_ref.shape)
    xrg_ref[...] = proj[:, d_s5:d_s5 + d_rg].reshape(xrg_ref.shape)
    zrg_ref[...] = proj[:, d_s5 + d_rg:].reshape(zrg_ref.shape)


def _inproj(x2, ln_g, ln_b, w_in, b_in, d_s5, d_rg):
    seq, d_model = x2.shape
    n_chunks = seq // NPH
    return pl.pallas_call(
        _inproj_kernel,
        grid=(n_chunks // CT,),
        in_specs=[
            pl.BlockSpec((CT * NPH, d_model), lambda i: (i, 0)),
            _resident((1, d_model)), _resident((1, d_model)),
            _resident(w_in.shape), _resident(b_in.shape),
        ],
        out_specs=[_tile3(d_model), _tile3(d_s5), _tile3(d_rg), _tile3(d_rg)],
        out_shape=[
            jax.ShapeDtypeStruct((NPH, n_chunks, d_model), F32),
            jax.ShapeDtypeStruct((NPH, n_chunks, d_s5), BF16),
            jax.ShapeDtypeStruct((NPH, n_chunks, d_rg), F32),
            jax.ShapeDtypeStruct((NPH, n_chunks, d_rg), F32),
        ],
        compiler_params=_cparams(1),
        name="inproj",
    )(x2, ln_g, ln_b, w_in, b_in)


def _row_scan_exclusive(sre, sim, qre, qim, reverse):
    n, w = sre.shape
    row = lax.broadcasted_iota(jnp.int32, (n, w), 0)

    def shift(v, k):
        if k % 8 == 0:
            z = jnp.zeros((k, w), v.dtype)
            return jnp.concatenate([v[k:], z] if reverse else [z, v[:n - k]], axis=0)
        if reverse:
            return jnp.where(row < n - k, pltpu.roll(v, n - k, axis=0), 0.0)
        return jnp.where(row >= k, pltpu.roll(v, k, axis=0), 0.0)

    xre, xim = shift(sre, 1), shift(sim, 1)
    k, i = 1, 0
    while k < n:
        pr, pi = qre[i:i + 1], qim[i:i + 1]
        if k % 8 == 0:
            keep = slice(n - k, n) if reverse else slice(0, k)
            dst = slice(0, n - k) if reverse else slice(k, n)
            src = slice(k, n) if reverse else slice(0, n - k)
            sr, si = xre[src], xim[src]
            nre = xre[dst] + pr * sr - pi * si
            nim = xim[dst] + pr * si + pi * sr
            order = (lambda new, old: [new, old]) if reverse else (lambda new, old: [old, new])
            xre = jnp.concatenate(order(nre, xre[keep]), axis=0)
            xim = jnp.concatenate(order(nim, xim[keep]), axis=0)
        else:
            sr, si = shift(xre, k), shift(xim, k)
            xre, xim = xre + pr * sr - pi * si, xim + pr * si + pi * sr
        k *= 2
        i += 1
    return xre, xim


def _dot(a, b):
    return jnp.dot(a.astype(BF16), b.astype(BF16), preferred_element_type=F32)


def _s5_chunk_operators(pa, pb, dvec):
    tk = NPH * S5_K
    lane16 = lax.broadcasted_iota(jnp.int32, (16, tk), 1)
    row16 = lax.broadcasted_iota(jnp.int32, (16, tk), 0)
    rep = (lax.shift_right_logical(lane16, 4) == row16).astype(F32)
    til = ((lane16 & 15) == row16).astype(F32)
    lane_t = lax.broadcasted_iota(jnp.int32, (tk, 16), 0)
    col_t = lax.broadcasted_iota(jnp.int32, (tk, 16), 1)
    rep_t = (lax.shift_right_logical(lane_t, 4) == col_t).astype(F32)
    til_t = ((lane_t & 15) == col_t).astype(F32)

    def cmul(ar, ai, br, bi):
        return ar * br - ai * bi, ar * bi + ai * br

    def w_of(c0):
        return cmul(_dot(pa[:, c0:c0 + 16], rep), _dot(pa[:, c0 + 16:c0 + 32], rep),
                    _dot(pa[:, c0 + 64:c0 + 80], til), _dot(pa[:, c0 + 80:c0 + 96], til))

    wf_re, wf_im = w_of(0)
    wb_re, wb_im = w_of(32)
    bc = jnp.concatenate([wf_re, wf_im, wb_re, wb_im], axis=0)

    def c_of(r0):
        cr, ci = cmul(_dot(til_t, pb[r0 + 64:r0 + 80]), _dot(til_t, pb[r0 + 80:r0 + 96]),
                      _dot(rep_t, pb[r0:r0 + 16]), _dot(rep_t, pb[r0 + 16:r0 + 32]))
        return cr, -ci

    cc = jnp.concatenate(list(c_of(0)) + list(c_of(32)), axis=1)

    kf = _dot(pb[64:80], wf_re) - _dot(pb[80:96], wf_im)
    kb = _dot(pb[96:112], wb_re) - _dot(pb[112:128], wb_im)
    blocks = []
    for t in range(NPH):
        left = S5_K * (NPH - 1 - t)
        right = S5_K * t
        f = kf if left == 0 else jnp.where(lane16 < tk - left, pltpu.roll(kf, tk - left, axis=1), 0.0)
        b = kb if right == 0 else jnp.where(lane16 >= right, pltpu.roll(kb, right, axis=1), 0.0)
        blocks.append(f + b)
    a = jnp.concatenate(blocks, axis=0)
    ri = lax.broadcasted_iota(jnp.int32, (tk, tk), 0)
    ci = lax.broadcasted_iota(jnp.int32, (tk, tk), 1)
    a = a + jnp.where(ri == ci, dvec, 0.0)
    return a.astype(BF16), bc.astype(BF16), cc.astype(BF16)


def _cast_stream_step(k, n_steps, srcs, dsts, inbufs, outbufs, sems):
    assert n_steps >= 2
    slot = lax.rem(k, 2)
    for i, (src, dst, inbuf, outbuf) in enumerate(zip(srcs, dsts, inbufs, outbufs)):
        rows = src.shape[0] // n_steps

        def read(step, sl, src=src, inbuf=inbuf, rows=rows, i=i):
            return pltpu.make_async_copy(src.at[pl.ds(step * rows, rows), :], inbuf.at[sl], sems.at[i, 0, sl])

        def write(step, sl, dst=dst, outbuf=outbuf, rows=rows, i=i):
            return pltpu.make_async_copy(outbuf.at[sl], dst.at[pl.ds(step * rows, rows), :], sems.at[i, 1, sl])

        @pl.when(k == 0)
        def _(read=read):
            read(0, 0).start()

        @pl.when(k + 1 < n_steps)
        def _(read=read):
            read(k + 1, 1 - slot).start()

        read(k, slot).wait()

        @pl.when(k >= 2)
        def _(write=write):
            write(k - 2, slot).wait()

        outbuf[slot] = inbuf[slot].astype(BF16)
        write(k, slot).start()

        @pl.when(k == n_steps - 1)
        def _(write=write):
            write(k - 1, 1 - slot).wait()
            write(k, slot).wait()


def _s5_kernel(n_w, n_grid, u_ref, pa_ref, pb_ref, d_ref, q_ref, *rest):
    w_src, rest = rest[:n_w], rest[n_w:]
    y_ref, rest = rest[0], rest[1:]
    w_dst, rest = rest[:n_w], rest[n_w:]
    xt_ref, yt_ref, rest = rest[0], rest[1], rest[2:]
    w_in_buf, w_out_buf, w_sems = rest[:n_w], rest[n_w:2 * n_w], rest[2 * n_w]
    n = u_ref.shape[1]
    p2 = 2 * S5_P
    tk = NPH * S5_K
    n_pairs = S5_LB // (2 * S5_K)

    def to_rows(s, _):
        xt_ref[s] = u_ref[s].T
        return 0
    lax.fori_loop(0, NPH, to_rows, 0)

    def pair(gp, _):
        _cast_stream_step(pl.program_id(0) * n_pairs + gp, n_grid * n_pairs,
                          w_src, w_dst, w_in_buf, w_out_buf, w_sems)
        xs, ys, bcs, ccs = [], [], [], []
        for h in range(2):
            g = 2 * gp + h
            rows = pl.ds(pl.multiple_of(g * S5_K, S5_K), S5_K)
            a, bc, cc = _s5_chunk_operators(pa_ref[g], pb_ref[g], d_ref[g])
            x = xt_ref[:, rows, :].reshape(tk, n)
            xs.append(x)
            ys.append(jnp.dot(a, x, preferred_element_type=F32))
            bcs.append(bc)
            ccs.append(cc)
        s = jnp.dot(jnp.concatenate(bcs, axis=1), jnp.concatenate(xs, axis=0), preferred_element_type=F32)
        st = [s[i * p2:(i + 1) * p2].T for i in range(4)]
        q = q_ref[gp]
        hf_re, hf_im = _row_scan_exclusive(st[0], st[1], q[0:16], q[16:32], False)
        hb_re, hb_im = _row_scan_exclusive(st[2], st[3], q[32:48], q[48:64], True)
        hin = jnp.concatenate([hf_re, hf_im, hb_re, hb_im], axis=1).astype(BF16)
        for h in range(2):
            g = 2 * gp + h
            rows = pl.ds(pl.multiple_of(g * S5_K, S5_K), S5_K)
            y = ys[h] + lax.dot_general(ccs[h], hin, (((1,), (1,)), ((), ())), preferred_element_type=F32)
            yt_ref[:, rows, :] = _gelu(y).reshape(NPH, S5_K, n)
        return 0
    lax.fori_loop(0, n_pairs, pair, 0)

    def to_lanes(t, _):
        y_ref[t] = yt_ref[t].T
        return 0
    lax.fori_loop(0, NPH, to_lanes, 0)


def _s5(u3, pa, pb, dvec, q_rows, weights):
    _, n_chunks, d_s5 = u3.shape
    gpb = S5_LB // S5_K
    tk = NPH * S5_K
    n_grid = d_s5 // S5_LB
    n_steps = n_grid * (gpb // 2)
    chunk = lambda w: (w.shape[0] // n_steps, w.shape[1])
    assert all(w.shape[0] % (16 * n_steps) == 0 for w in weights)
    per_block = lambda n, *tail: pl.BlockSpec((n,) + tail, lambda b: (b,) + (0,) * len(tail))
    hbm = pl.BlockSpec(memory_space=pl.ANY)
    outs = pl.pallas_call(
        functools.partial(_s5_kernel, len(weights), n_grid),
        grid=(n_grid,),
        in_specs=[
            pl.BlockSpec((NPH, n_chunks, S5_LB), lambda b: (0, 0, b)),
            per_block(gpb, 2 * S5_P, 128), per_block(gpb, 128, 2 * S5_P), per_block(gpb, 1, tk),
            per_block(gpb // 2, 64, 2 * S5_P),
        ] + [hbm] * len(weights),
        out_specs=[pl.BlockSpec((NPH, n_chunks, S5_LB), lambda b: (0, 0, b))] + [hbm] * len(weights),
        out_shape=[jax.ShapeDtypeStruct((NPH, n_chunks, d_s5), F32)]
        + [jax.ShapeDtypeStruct(w.shape, BF16) for w in weights],
        scratch_shapes=[
            pltpu.VMEM((NPH, S5_LB, n_chunks), BF16),
            pltpu.VMEM((NPH, S5_LB, n_chunks), F32),
        ] + [pltpu.VMEM((2,) + chunk(w), F32) for w in weights]
        + [pltpu.VMEM((2,) + chunk(w), BF16) for w in weights]
        + [pltpu.SemaphoreType.DMA((len(weights), 2, 2))],
        compiler_params=_cparams(1),
        name="s5",
    )(u3, pa, pb, dvec, q_rows, *weights)
    return outs[0], outs[1:]


def _s5_discretise(lam_re, lam_im, log_step, b_re, b_im, c_re, c_im, d, n_chunks):
    assert NPH == 16 and S5_K == 16 and n_chunks <= 2 ** 16
    lre = jnp.minimum(lam_re.astype(F32), -1e-4)
    lim = lam_im.astype(F32)
    step = jnp.exp(log_step.astype(F32))[..., None]
    are, aim = lre * step, lim * step

    def cexp(e):
        mag = jnp.exp(are[..., None] * e)
        ang = aim[..., None] * e
        return mag * jnp.cos(ang), mag * jnp.sin(ang)

    j = jnp.arange(NPH + 1, dtype=F32)
    pw_re, pw_im = cexp(j)
    nr, ni = pw_re[..., 1] - 1.0, pw_im[..., 1]
    den = lre * lre + lim * lim
    zr, zi = (nr * lre + ni * lim) / den, (ni * lre - nr * lim) / den
    br, bi = b_re.astype(F32), b_im.astype(F32)
    bbar_re = zr[..., None] * br - zi[..., None] * bi
    bbar_im = zr[..., None] * bi + zi[..., None] * br

    desc = lambda v: v[..., NPH - 1::-1]
    asc = lambda v: v[..., :NPH]
    pa = jnp.concatenate([desc(pw_re[0]), desc(pw_im[0]), asc(pw_re[1]), asc(pw_im[1]),
                          bbar_re[0], bbar_im[0], bbar_re[1], bbar_im[1]], axis=-1)
    rows = lambda v: jnp.swapaxes(v, -1, -2)
    up = lambda v: v[..., 1:]
    down = lambda v: v[..., NPH:0:-1]
    pb = jnp.concatenate([rows(up(pw_re[0])), rows(up(pw_im[0])), rows(down(pw_re[1])), rows(down(pw_im[1])),
                          c_re[0].astype(F32), c_im[0].astype(F32), c_re[1].astype(F32), c_im[1].astype(F32)],
                         axis=1)
    groups, p = lam_re.shape[1:]
    half = (jnp.arange(groups) % 2)[:, None, None, None] == jnp.arange(2)[None, :, None, None]
    pa = jnp.where(half, pa[:, None], 0.0).reshape(groups, 2 * p, 128)
    pb = jnp.where(half, pb[:, None], 0.0).transpose(0, 2, 1, 3).reshape(groups, 128, 2 * p)
    dvec = jnp.tile(d.astype(F32), (1, NPH))[:, None, :]
    qr, qi = cexp(NPH * (2.0 ** jnp.arange(16, dtype=F32)))
    lanes = lambda v: v.reshape(groups // 2, 2, p, 16).transpose(0, 3, 1, 2).reshape(groups // 2, 16, 2 * p)
    q_rows = jnp.concatenate([lanes(qr[0]), lanes(qi[0]), lanes(qr[1]), lanes(qi[1])], axis=1)
    return pa, pb, dvec, q_rows


def _row_scan_carry(a, h, reverse):
    n, w = a.shape
    row = lax.broadcasted_iota(jnp.int32, (n, w), 0)

    def shift(v, k, fill):
        if k % 8 == 0:
            z = jnp.full((k, w), fill, v.dtype)
            if reverse:
                return jnp.concatenate([v[k:], z], axis=0)
            return jnp.concatenate([z, v[:n - k]], axis=0)
        if reverse:
            return jnp.where(row < n - k, pltpu.roll(v, n - k, axis=0), fill)
        return jnp.where(row >= k, pltpu.roll(v, k, axis=0), fill)

    k = 1
    while k < n:
        h = h + a * shift(h, k, 0.0)
        a = a * shift(a, k, 1.0)
        k *= 2
    return shift(h, 1, 0.0)


def _rglru_kernel(x_ref, cw_ref, cb_ref, wg_ref, bg_ref, lam_ref, y_ref, xe_ref, al_ref, hl_ref):
    n, w = x_ref.shape[1], x_ref.shape[2]
    row = lax.broadcasted_iota(jnp.int32, (n, w), 0)

    def from_prev_chunk(v):
        return jnp.where(row >= 1, pltpu.roll(v, 1, axis=0), 0.0)

    def from_next_chunk(v):
        return jnp.where(row < n - 1, pltpu.roll(v, n - 1, axis=0), 0.0)

    xe_ref[0] = from_prev_chunk(x_ref[NPH - 2])
    xe_ref[1] = from_prev_chunk(x_ref[NPH - 1])

    def copy_body(s, _):
        xe_ref[s + 2] = x_ref[s]
        return 0
    lax.fori_loop(0, NPH, copy_body, 0)
    xe_ref[NPH + 2] = from_next_chunk(x_ref[0])

    cw = cw_ref[...]
    cb = cb_ref[...]
    lam = lam_ref[...]
    neg = -lam
    softplus = jnp.maximum(neg, 0.0) + jnp.log(1.0 + jnp.exp(-jnp.abs(neg)))

    for d, reverse in ((0, False), (1, True)):
        rate = (-RG_C) * softplus[d:d + 1]
        wg = wg_ref[0][:, 2 * w * d:2 * w * (d + 1)]
        bg = bg_ref[0][:, 2 * w * d:2 * w * (d + 1)]
        init = NPH if reverse else 0
        al_ref[init] = jnp.ones((n, w), F32)
        hl_ref[init] = jnp.zeros((n, w), F32)

        def local_body(i, _):
            s = (NPH - 1 - i) if reverse else i
            xc = (cw[0:1] * xe_ref[s] + cw[1:2] * xe_ref[s + 1]
                  + cw[2:3] * xe_ref[s + 2] + cw[3:4] * xe_ref[s + 3] + cb)
            g = jnp.dot(xc.astype(BF16), wg, preferred_element_type=F32) + bg
            r = _sigmoid(g[:, :w])
            ig = _sigmoid(g[:, w:])
            a = jnp.exp(rate * r)
            bt = jnp.sqrt(1.0 - a * a) * (ig * xc)
            src = (s + 1) if reverse else s
            dst = s if reverse else (s + 1)
            hl_ref[dst] = a * hl_ref[src] + bt
            al_ref[dst] = a * al_ref[src]
            return 0
        lax.fori_loop(0, NPH, local_body, 0)

        last = 0 if reverse else NPH
        carry = _row_scan_carry(al_ref[last], hl_ref[last], reverse)

        def fix_body(s, _):
            slot = s if reverse else (s + 1)
            v = hl_ref[slot] + al_ref[slot] * carry
            if reverse:
                y_ref[s] = y_ref[s] + v
            else:
                y_ref[s] = v
            return 0
        lax.fori_loop(0, NPH, fix_body, 0)


def _rglru(xrg3, conv_w, conv_b, wg, bg, lam):
    _, n_chunks, d_rg = xrg3.shape
    nb = d_rg // RG_CB
    return pl.pallas_call(
        _rglru_kernel,
        grid=(nb,),
        in_specs=[
            pl.BlockSpec((NPH, n_chunks, RG_CB), lambda j: (0, 0, j)),
            pl.BlockSpec((CONV_W, RG_CB), lambda j: (0, j)),
            pl.BlockSpec((1, RG_CB), lambda j: (0, j)),
            pl.BlockSpec((1, RG_CB, 4 * RG_CB), lambda j: (j, 0, 0)),
            pl.BlockSpec((1, 1, 4 * RG_CB), lambda j: (j, 0, 0)),
            pl.BlockSpec((2, RG_CB), lambda j: (0, j)),
        ],
        out_specs=pl.BlockSpec((NPH, n_chunks, RG_CB), lambda j: (0, 0, j)),
        out_shape=jax.ShapeDtypeStruct((NPH, n_chunks, d_rg), F32),
        scratch_shapes=[
            pltpu.VMEM((NPH + 3, n_chunks, RG_CB), F32),
            pltpu.VMEM((NPH + 1, n_chunks, RG_CB), F32),
            pltpu.VMEM((NPH + 1, n_chunks, RG_CB), F32),
        ],
        compiler_params=_cparams(1),
        name="rglru",
    )(xrg3, conv_w, conv_b, wg, bg, lam)


def _rglru_gate_weights(wa, ba, wx, bx):
    n_dir, heads, hd, _ = wa.shape
    per = RG_CB // hd
    nb = heads // per
    eye = jnp.eye(per, dtype=F32)

    def blockdiag(w):
        w = w.astype(F32).reshape(nb, per, hd, hd)
        return jnp.einsum('bpij,pq->bpiqj', w, eye).reshape(nb, RG_CB, RG_CB)

    cols, bias = [], []
    for d in range(n_dir):
        for w, b in ((wa, ba), (wx, bx)):
            cols.append(blockdiag(w[d]))
            bias.append(b[d].astype(F32).reshape(nb, 1, RG_CB))
    return jnp.concatenate(cols, axis=-1).astype(BF16), jnp.concatenate(bias, axis=-1)


def _rms_norm(x, g):
    return x * lax.rsqrt(jnp.mean(x * x, axis=-1, keepdims=True) + LN_EPS) * g


def _mixout_kernel(alpha, ys5_ref, yrg_ref, z_ref, h_ref, gw_ref, gb_ref, gn5_ref, gnr_ref,
                   w_ref, b_ref, g_ref, be_ref, o_ref):
    rows = NPH * CT
    d_s5 = ys5_ref.shape[-1]
    y = ys5_ref[...].reshape(rows, d_s5)
    glu = y * _sigmoid(jnp.dot(y.astype(BF16), gw_ref[...], preferred_element_type=F32) + gb_ref[...])
    ys5n = _rms_norm(glu, gn5_ref[...]).astype(BF16)
    yrg = yrg_ref[...].reshape(rows, -1) * _gelu(z_ref[...].reshape(rows, -1))
    yrgn = _rms_norm(yrg, gnr_ref[...]).astype(BF16)
    mix = (jnp.dot(ys5n, w_ref[:d_s5, :], preferred_element_type=F32)
           + jnp.dot(yrgn, w_ref[d_s5:, :], preferred_element_type=F32) + b_ref[...])
    h1 = _layer_norm(alpha * h_ref[...].reshape(rows, -1) + mix, g_ref[...], be_ref[...])
    o_ref[...] = h1.reshape(o_ref.shape)


def _mixout(alpha, ys5, yrg, zrg, h, glu_w, glu_b, gn_s5, gn_rg, w_out, b_out, ln_g, ln_b):
    _, n_chunks, d_model = h.shape
    d_s5, d_rg = ys5.shape[-1], yrg.shape[-1]
    return pl.pallas_call(
        functools.partial(_mixout_kernel, alpha),
        grid=(n_chunks // CT,),
        in_specs=[
            _tile3(d_s5), _tile3(d_rg), _tile3(d_rg), _tile3(d_model),
            _resident((d_s5, d_s5)), _resident((1, d_s5)), _resident((1, d_s5)), _resident((1, d_rg)),
            _resident((d_s5 + d_rg, d_model)), _resident((1, d_model)),
            _resident((1, d_model)), _resident((1, d_model)),
        ],
        out_specs=_tile3(d_model),
        out_shape=jax.ShapeDtypeStruct((NPH, n_chunks, d_model), F32),
        compiler_params=_cparams(1),
        name="mixout",
    )(ys5, yrg, zrg, h, glu_w, glu_b, gn_s5, gn_rg, w_out, b_out, ln_g, ln_b)


def _ffn_kernel(alpha, h_ref, p_ref, w1_ref, b1_ref, w2_ref, b2_ref, pw_ref, gw_ref, gb_ref,
                g_ref, be_ref, o_ref, hb_ref, acc_ref):
    j = pl.program_id(1)
    rows = NPH * CT

    @pl.when(j == 0)
    def _():
        hb = h_ref[...].reshape(rows, -1).astype(BF16)
        hb_ref[...] = hb
        gate = _sigmoid(jnp.dot(hb, gw_ref[...], preferred_element_type=F32) + gb_ref[...])
        pe = _to_phase_major(p_ref[...]).astype(BF16)
        acc_ref[...] = gate * jnp.dot(pe, pw_ref[...], preferred_element_type=F32) + b2_ref[...]

    a = jnp.dot(hb_ref[...], w1_ref[...], preferred_element_type=F32) + b1_ref[...]
    a = jnp.maximum(a, 0.0)
    acc_ref[...] += jnp.dot((a * a).astype(BF16), w2_ref[...], preferred_element_type=F32)

    @pl.when(j == pl.num_programs(1) - 1)
    def _():
        out = _layer_norm(alpha * h_ref[...].reshape(rows, -1) + acc_ref[...], g_ref[...], be_ref[...])
        o_ref[...] = _to_time_major(out)


def _ffn(alpha, h1, p2, w1, b1, w2, b2, ple_w, gate_w, gate_b, ln_g, ln_b, ff_tile):
    _, n_chunks, d_model = h1.shape
    d_ff = w1.shape[1]
    ple_dim = ple_w.shape[0]
    rows = NPH * CT
    return pl.pallas_call(
        functools.partial(_ffn_kernel, alpha),
        grid=(n_chunks // CT, d_ff // ff_tile),
        in_specs=[
            _tile3(d_model),
            pl.BlockSpec((rows, ple_dim), lambda i, j: (i, 0)),
            pl.BlockSpec((d_model, ff_tile), lambda i, j: (0, j)),
            pl.BlockSpec((1, ff_tile), lambda i, j: (0, j)),
            pl.BlockSpec((ff_tile, d_model), lambda i, j: (j, 0)),
            _resident((1, d_model)),
            _resident((ple_dim, d_model)), _resident((d_model, d_model)), _resident((1, d_model)),
            _resident((1, d_model)), _resident((1, d_model)),
        ],
        out_specs=pl.BlockSpec((rows, d_model), lambda i, j: (i, 0)),
        out_shape=jax.ShapeDtypeStruct((NPH * n_chunks, d_model), F32),
        scratch_shapes=[pltpu.VMEM((rows, d_model), BF16), pltpu.VMEM((rows, d_model), F32)],
        compiler_params=_cparams(2),
        name="ffn",
    )(h1, p2, w1, b1, w2, b2, ple_w, gate_w, gate_b, ln_g, ln_b)


def kernel(x, p, ln_in_g, ln_in_b, w_in, b_in, s5_lambda_re, s5_lambda_im, s5_log_step, s5_b_re, s5_b_im, s5_c_re, s5_c_im, s5_d, s5_glu_w, s5_glu_b, rg_conv_w, rg_conv_b, rg_wa, rg_ba, rg_wx, rg_bx, rg_lambda, gn_s5, gn_rg, w_out, b_out, ln1_g, ln1_b, w_ff1, b_ff1, w_ff2, b_ff2, ple_w, ple_gate_w, ple_gate_b, ln2_g, ln2_b):
    batch, seq, d_model = x.shape
    depth = w_in.shape[0]
    assert batch == 1 and depth == 1 and seq % (NPH * CT) == 0
    n_chunks = seq // NPH
    d_s5 = s5_glu_w.shape[-1]
    d_rg = rg_conv_w.shape[-1]
    alpha = (2.0 * depth) ** 0.25
    row = lambda v: v.reshape(1, -1).astype(F32)

    h, u3, xrg, zrg = _inproj(x.reshape(seq, d_model), row(ln_in_g), row(ln_in_b),
                              w_in[0].astype(BF16), row(b_in[0]), d_s5, d_rg)

    pa, pb, dvec, q_rows = _s5_discretise(
        s5_lambda_re[0], s5_lambda_im[0], s5_log_step[0], s5_b_re[0], s5_b_im[0],
        s5_c_re[0], s5_c_im[0], s5_d[0], n_chunks)
    ys5, (w_out_b, gate_w_b, w_ff1_b, w_ff2_b) = _s5(
        u3, pa, pb, dvec, q_rows, [w_out[0], ple_gate_w[0], w_ff1[0], w_ff2[0]])

    wg, bg = _rglru_gate_weights(rg_wa[0], rg_ba[0], rg_wx[0], rg_bx[0])
    yrg = _rglru(xrg, rg_conv_w[0].astype(F32), row(rg_conv_b[0]), wg, bg, rg_lambda[0].astype(F32))

    h1 = _mixout(alpha, ys5, yrg, zrg, h, s5_glu_w[0].astype(BF16), row(s5_glu_b[0]), row(gn_s5[0]),
                 row(gn_rg[0]), w_out_b, row(b_out[0]), row(ln1_g[0]), row(ln1_b[0]))

    out = _ffn(alpha, h1, p.reshape(seq, p.shape[-1]), w_ff1_b, row(b_ff1[0]),
               w_ff2_b, row(b_ff2[0]), ple_w[0].astype(BF16), gate_w_b,
               row(ple_gate_b[0]), row(ln2_g[0]), row(ln2_b[0]), ff_tile=1024)
    return out.reshape(batch, seq, d_model).astype(x.dtype)
```

```python
import functools
import math

import jax
import jax.numpy as jnp
from jax import lax
from jax.experimental import pallas as pl
from jax.experimental.pallas import tpu as pltpu

F32 = jnp.float32
BF16 = jnp.bfloat16

NPH = 16
CT = 32
S5_K = 16
S5_P = 64
S5_LB = 128
RG_CB = 128
CONV_W = 4
RG_C = 8.0
LN_EPS = 1e-5
VMEM_LIMIT_V7X = 56 * 1024 * 1024


def _cparams(n_axes):
    return pltpu.CompilerParams(
        dimension_semantics=("arbitrary",) * n_axes,
        vmem_limit_bytes=VMEM_LIMIT_V7X)


def _resident(shape):
    return pl.BlockSpec(shape, lambda *_: (0,) * len(shape), pipeline_mode=pl.Buffered(1))


def _tile3(width):
    return pl.BlockSpec((NPH, CT, width), lambda i, *_: (0, i, 0))


def _to_phase_major(x):
    w = x.shape[-1]
    return jnp.swapaxes(x.reshape(CT, NPH, w), 0, 1).reshape(NPH * CT, w)


def _to_time_major(x):
    w = x.shape[-1]
    return jnp.swapaxes(x.reshape(NPH, CT, w), 0, 1).reshape(CT * NPH, w)


def _layer_norm(x, g, b):
    mu = jnp.mean(x, axis=-1, keepdims=True)
    xc = x - mu
    var = jnp.mean(xc * xc, axis=-1, keepdims=True)
    return xc * lax.rsqrt(var + LN_EPS) * g + b


def _gelu(x):
    c = math.sqrt(2.0 / math.pi)
    return 0.5 * x * (1.0 + jnp.tanh(c * (x + 0.044715 * (x * x * x))))


def _sigmoid(x):
    return 1.0 / (1.0 + jnp.exp(-x))


def _inproj_kernel(x_ref, g_ref, b_ref, w_ref, bw_ref, h_ref, u_ref, xrg_ref, zrg_ref):
    d_s5, d_rg = u_ref.shape[-1], xrg_ref.shape[-1]
    h = _layer_norm(_to_phase_major(x_ref[...]), g_ref[...], b_ref[...])
    h_ref[...] = h.reshape(h_ref.shape)
    proj = jnp.dot(h.astype(BF16), w_ref[...], preferred_element_type=F32) + bw_ref[...]
    u_ref[...] = proj[:, :d_s5].astype(BF16).reshape(u_ref.shape)
    xrg_ref[...] = proj[:, d_s5:d_s5 + d_rg].reshape(xrg_ref.shape)
    zrg_ref[...] = proj[:, d_s5 + d_rg:].reshape(zrg_ref.shape)


def _inproj(x2, ln_g, ln_b, w_in, b_in, d_s5, d_rg):
    seq, d_model = x2.shape
    n_chunks = seq // NPH
    return pl.pallas_call(
        _inproj_kernel,
        grid=(n_chunks // CT,),
        in_specs=[
            pl.BlockSpec((CT * NPH, d_model), lambda i: (i, 0)),
            _resident((1, d_model)), _resident((1, d_model)),
            _resident(w_in.shape), _resident(b_in.shape),
        ],
        out_specs=[_tile3(d_model), _tile3(d_s5), _tile3(d_rg), _tile3(d_rg)],
        out_shape=[
            jax.ShapeDtypeStruct((NPH, n_chunks, d_model), F32),
            jax.ShapeDtypeStruct((NPH, n_chunks, d_s5), BF16),
            jax.ShapeDtypeStruct((NPH, n_chunks, d_rg), F32),
            jax.ShapeDtypeStruct((NPH, n_chunks, d_rg), F32),
        ],
        compiler_params=_cparams(1),
        name="inproj",
    )(x2, ln_g, ln_b, w_in, b_in)


def _row_scan_exclusive(sre, sim, qre, qim, reverse):
    n, w = sre.shape
    row = lax.broadcasted_iota(jnp.int32, (n, w), 0)

    def shift(v, k):
        if k % 8 == 0:
            z = jnp.zeros((k, w), v.dtype)
            return jnp.concatenate([v[k:], z] if reverse else [z, v[:n - k]], axis=0)
        if reverse:
            return jnp.where(row < n - k, pltpu.roll(v, n - k, axis=0), 0.0)
        return jnp.where(row >= k, pltpu.roll(v, k, axis=0), 0.0)

    xre, xim = shift(sre, 1), shift(sim, 1)
    k, i = 1, 0
    while k < n:
        pr, pi = qre[i:i + 1], qim[i:i + 1]
        if k % 8 == 0:
            keep = slice(n - k, n) if reverse else slice(0, k)
            dst = slice(0, n - k) if reverse else slice(k, n)
            src = slice(k, n) if reverse else slice(0, n - k)
            sr, si = xre[src], xim[src]
            nre = xre[dst] + pr * sr - pi * si
            nim = xim[dst] + pr * si + pi * sr
            order = (lambda new, old: [new, old]) if reverse else (lambda new, old: [old, new])
            xre = jnp.concatenate(order(nre, xre[keep]), axis=0)
            xim = jnp.concatenate(order(nim, xim[keep]), axis=0)
        else:
            sr, si = shift(xre, k), shift(xim, k)
            xre, xim = xre + pr * sr - pi * si, xim + pr * si + pi * sr
        k *= 2
        i += 1
    return xre, xim


def _dot(a, b):
    return jnp.dot(a.astype(BF16), b.astype(BF16), preferred_element_type=F32)


def _s5_chunk_operators(par, dvec, half):
    tk = NPH * S5_K
    lane = lax.broadcasted_iota(jnp.int32, par.shape, 1)
    par = jnp.where(lax.shift_right_logical(lane, 6) == half, par, 0.0)
    pa = par[0:128].T
    pb = par[128:256]
    lane16 = lax.broadcasted_iota(jnp.int32, (16, tk), 1)
    row16 = lax.broadcasted_iota(jnp.int32, (16, tk), 0)
    rep = (lax.shift_right_logical(lane16, 4) == row16).astype(F32)
    til = ((lane16 & 15) == row16).astype(F32)

    def cmul(ar, ai, br, bi):
        return ar * br - ai * bi, ar * bi + ai * br

    def w_of(c0):
        return cmul(_dot(pa[:, c0:c0 + 16], rep), _dot(pa[:, c0 + 16:c0 + 32], rep),
                    _dot(pa[:, c0 + 64:c0 + 80], til), _dot(pa[:, c0 + 80:c0 + 96], til))

    wf_re, wf_im = w_of(0)
    wb_re, wb_im = w_of(32)
    bc = jnp.concatenate([wf_re, wf_im, wb_re, wb_im], axis=0)

    def c_of(r0):
        c_re, c_im = pb[r0 + 64:r0 + 80], pb[r0 + 80:r0 + 96]
        blocks = [cmul(c_re, c_im, pb[r0 + t:r0 + t + 1], pb[r0 + 16 + t:r0 + 17 + t]) for t in range(NPH)]
        return (jnp.concatenate([b[0] for b in blocks], axis=0),
                -jnp.concatenate([b[1] for b in blocks], axis=0))

    cc = jnp.concatenate(list(c_of(0)) + list(c_of(32)), axis=1)

    kf = _dot(pb[64:80], wf_re) - _dot(pb[80:96], wf_im)
    kb = _dot(pb[96:112], wb_re) - _dot(pb[112:128], wb_im)
    blocks = []
    for t in range(NPH):
        left = S5_K * (NPH - 1 - t)
        right = S5_K * t
        f = kf if left == 0 else jnp.where(lane16 < tk - left, pltpu.roll(kf, tk - left, axis=1), 0.0)
        b = kb if right == 0 else jnp.where(lane16 >= right, pltpu.roll(kb, right, axis=1), 0.0)
        blocks.append(f + b)
    a = jnp.concatenate(blocks, axis=0)
    ri = lax.broadcasted_iota(jnp.int32, (tk, tk), 0)
    ci = lax.broadcasted_iota(jnp.int32, (tk, tk), 1)
    a = a + jnp.where(ri == ci, dvec, 0.0)
    return a.astype(BF16), bc.astype(BF16), cc.astype(BF16)


def _cast_stream(k, n_steps, srcs, dsts, inbufs, outbufs, sems):
    assert n_steps >= 2
    slot = lax.rem(k, 2)
    n_streams = len(srcs)
    rows = [src.shape[0] // n_steps for src in srcs]

    def read(i, step, sl):
        return pltpu.make_async_copy(srcs[i].at[pl.ds(step * rows[i], rows[i]), :], inbufs[i].at[sl],
                                     sems.at[i, 0, sl])

    def write(i, step, sl):
        return pltpu.make_async_copy(outbufs[i].at[sl], dsts[i].at[pl.ds(step * rows[i], rows[i]), :],
                                     sems.at[i, 1, sl])

    def begin():
        @pl.when(k == 0)
        def _():
            for i in range(n_streams):
                read(i, 0, 0).start()

        @pl.when(k + 1 < n_steps)
        def _():
            for i in range(n_streams):
                read(i, k + 1, 1 - slot).start()

        @pl.when(k >= 2)
        def _():
            for i in range(n_streams):
                write(i, k - 2, slot).wait()

        for i in range(n_streams):
            read(i, k, slot).wait()

    def cast():
        for i in range(n_streams):
            outbufs[i][slot] = inbufs[i][slot].astype(BF16)

    def end():
        for i in range(n_streams):
            write(i, k, slot).start()

        @pl.when(k == n_steps - 1)
        def _():
            for i in range(n_streams):
                write(i, k - 1, 1 - slot).wait()
                write(i, k, slot).wait()

    return begin, cast, end


def _s5_kernel(n_w, n_grid, u_ref, par_ref, d_ref, q_ref, *rest):
    w_src, rest = rest[:n_w], rest[n_w:]
    y_ref, rest = rest[0], rest[1:]
    w_dst, rest = rest[:n_w], rest[n_w:]
    xt_ref, yt_ref, a_ref, bc_ref, cc_ref, rest = rest[0], rest[1], rest[2], rest[3], rest[4], rest[5:]
    w_in_buf, w_out_buf, w_sems = rest[:n_w], rest[n_w:2 * n_w], rest[2 * n_w]
    n = u_ref.shape[1]
    p2 = 2 * S5_P
    tk = NPH * S5_K
    n_groups = S5_LB // S5_K
    n_pairs = n_groups // 2

    def to_rows(s, _):
        xt_ref[s] = u_ref[s].T
        return 0
    lax.fori_loop(0, NPH, to_rows, 0)

    for g in range(n_groups):
        a_ref[g], bc_ref[g], cc_ref[g] = _s5_chunk_operators(par_ref[g], d_ref[g], g % 2)

    def pair(gp, _):
        stream_begin, stream_cast, stream_end = _cast_stream(
            pl.program_id(0) * n_pairs + gp, n_grid * n_pairs, w_src, w_dst, w_in_buf, w_out_buf, w_sems)
        stream_begin()
        stream_cast()
        xs, ys = [], []
        for h in range(2):
            g = 2 * gp + h
            rows = pl.ds(pl.multiple_of(g * S5_K, S5_K), S5_K)
            x = xt_ref[:, rows, :].reshape(tk, n)
            xs.append(x)
            ys.append(jnp.dot(a_ref[g], x, preferred_element_type=F32))
        bc = jnp.concatenate([bc_ref[2 * gp], bc_ref[2 * gp + 1]], axis=1)
        s = jnp.dot(bc, jnp.concatenate(xs, axis=0), preferred_element_type=F32)
        st = [s[i * p2:(i + 1) * p2].T for i in range(4)]
        q = q_ref[gp]
        hf_re, hf_im = _row_scan_exclusive(st[0], st[1], q[0:16], q[16:32], False)
        hb_re, hb_im = _row_scan_exclusive(st[2], st[3], q[32:48], q[48:64], True)
        hin = jnp.concatenate([hf_re, hf_im, hb_re, hb_im], axis=1).astype(BF16)
        for h in range(2):
            g = 2 * gp + h
            rows = pl.ds(pl.multiple_of(g * S5_K, S5_K), S5_K)
            y = ys[h] + lax.dot_general(cc_ref[g], hin, (((1,), (1,)), ((), ())), preferred_element_type=F32)
            yt_ref[:, rows, :] = _gelu(y).reshape(NPH, S5_K, n)
        stream_end()
        return 0
    lax.fori_loop(0, n_pairs, pair, 0)

    def to_lanes(t, _):
        y_ref[t] = yt_ref[t].T
        return 0
    lax.fori_loop(0, NPH, to_lanes, 0)


def _s5(u3, par, dvec, q_rows, weights):
    _, n_chunks, d_s5 = u3.shape
    gpb = S5_LB // S5_K
    tk = NPH * S5_K
    n_grid = d_s5 // S5_LB
    n_steps = n_grid * (gpb // 2)
    chunk = lambda w: (w.shape[0] // n_steps, w.shape[1])
    assert all(w.shape[0] % (16 * n_steps) == 0 for w in weights)
    per_block = lambda n, *tail: pl.BlockSpec((n,) + tail, lambda b: (b,) + (0,) * len(tail))
    hbm = pl.BlockSpec(memory_space=pl.ANY)
    outs = pl.pallas_call(
        functools.partial(_s5_kernel, len(weights), n_grid),
        grid=(n_grid,),
        in_specs=[
            pl.BlockSpec((NPH, n_chunks, S5_LB), lambda b: (0, 0, b)),
            per_block(gpb, 256, 2 * S5_P), per_block(gpb, 1, tk),
            per_block(gpb // 2, 64, 2 * S5_P),
        ] + [hbm] * len(weights),
        out_specs=[pl.BlockSpec((NPH, n_chunks, S5_LB), lambda b: (0, 0, b))] + [hbm] * len(weights),
        out_shape=[jax.ShapeDtypeStruct((NPH, n_chunks, d_s5), F32)]
        + [jax.ShapeDtypeStruct(w.shape, BF16) for w in weights],
        scratch_shapes=[
            pltpu.VMEM((NPH, S5_LB, n_chunks), BF16),
            pltpu.VMEM((NPH, S5_LB, n_chunks), F32),
            pltpu.VMEM((gpb, tk, tk), BF16),
            pltpu.VMEM((gpb, 8 * S5_P, tk), BF16),
            pltpu.VMEM((gpb, tk, 8 * S5_P), BF16),
        ] + [pltpu.VMEM((2,) + chunk(w), F32) for w in weights]
        + [pltpu.VMEM((2,) + chunk(w), BF16) for w in weights]
        + [pltpu.SemaphoreType.DMA((len(weights), 2, 2))],
        compiler_params=_cparams(1),
        name="s5",
    )(u3, par, dvec, q_rows, *weights)
    return outs[0], outs[1:]


def _s5_discretise(lam_re, lam_im, log_step, b_re, b_im, c_re, c_im, d, n_chunks):
    assert NPH == 16 and S5_K == 16 and n_chunks <= 2 ** 16
    groups, p = lam_re.shape[1:]
    lre = jnp.minimum(lam_re.astype(F32), -1e-4)[:, :, None, :]
    lim = lam_im.astype(F32)[:, :, None, :]
    step = jnp.exp(log_step.astype(F32))[:, :, None, None]
    are, aim = lre * step, lim * step

    def cexp(e):
        mag = jnp.exp(are * e[:, None])
        ang = aim * e[:, None]
        return mag * jnp.cos(ang), mag * jnp.sin(ang)

    pw_re, pw_im = cexp(jnp.arange(NPH + 1, dtype=F32))
    nr, ni = pw_re[:, :, 1:2] - 1.0, pw_im[:, :, 1:2]
    den = lre * lre + lim * lim
    zr, zi = (nr * lre + ni * lim) / den, (ni * lre - nr * lim) / den
    br, bi = jnp.swapaxes(b_re.astype(F32), -1, -2), jnp.swapaxes(b_im.astype(F32), -1, -2)
    bbar_re, bbar_im = zr * br - zi * bi, zr * bi + zi * br

    desc = lambda v: v[:, NPH - 1::-1]
    asc = lambda v: v[:, :NPH]
    up = lambda v: v[:, 1:]
    down = lambda v: v[:, NPH:0:-1]
    par = jnp.concatenate(
        [desc(pw_re[0]), desc(pw_im[0]), asc(pw_re[1]), asc(pw_im[1]),
         bbar_re[0], bbar_im[0], bbar_re[1], bbar_im[1],
         up(pw_re[0]), up(pw_im[0]), down(pw_re[1]), down(pw_im[1]),
         c_re[0].astype(F32), c_im[0].astype(F32), c_re[1].astype(F32), c_im[1].astype(F32)], axis=1)
    par = jnp.concatenate([par, par], axis=-1)
    dvec = jnp.tile(d.astype(F32), (1, NPH))[:, None, :]
    qr, qi = cexp(NPH * (2.0 ** jnp.arange(16, dtype=F32)))
    lanes = lambda v: v.reshape(groups // 2, 2, 16, p).transpose(0, 2, 1, 3).reshape(groups // 2, 16, 2 * p)
    q_rows = jnp.concatenate([lanes(qr[0]), lanes(qi[0]), lanes(qr[1]), lanes(qi[1])], axis=1)
    return par, dvec, q_rows


def _row_scan_carry(a, h, reverse):
    n, w = a.shape
    row = lax.broadcasted_iota(jnp.int32, (n, w), 0)

    def shift(v, k, fill):
        if k % 8 == 0:
            z = jnp.full((k, w), fill, v.dtype)
            if reverse:
                return jnp.concatenate([v[k:], z], axis=0)
            return jnp.concatenate([z, v[:n - k]], axis=0)
        if reverse:
            return jnp.where(row < n - k, pltpu.roll(v, n - k, axis=0), fill)
        return jnp.where(row >= k, pltpu.roll(v, k, axis=0), fill)

    k = 1
    while k < n:
        h = h + a * shift(h, k, 0.0)
        a = a * shift(a, k, 1.0)
        k *= 2
    return shift(h, 1, 0.0)


def _rglru_kernel(x_ref, cw_ref, cb_ref, wg_ref, bg_ref, lam_ref, y_ref, xe_ref, al_ref, hl_ref):
    n, w = x_ref.shape[1], x_ref.shape[2]
    row = lax.broadcasted_iota(jnp.int32, (n, w), 0)

    def from_prev_chunk(v):
        return jnp.where(row >= 1, pltpu.roll(v, 1, axis=0), 0.0)

    def from_next_chunk(v):
        return jnp.where(row < n - 1, pltpu.roll(v, n - 1, axis=0), 0.0)

    xe_ref[0] = from_prev_chunk(x_ref[NPH - 2])
    xe_ref[1] = from_prev_chunk(x_ref[NPH - 1])

    def copy_body(s, _):
        xe_ref[s + 2] = x_ref[s]
        return 0
    lax.fori_loop(0, NPH, copy_body, 0)
    xe_ref[NPH + 2] = from_next_chunk(x_ref[0])

    cw = cw_ref[...]
    cb = cb_ref[...]
    lam = lam_ref[...]
    neg = -lam
    softplus = jnp.maximum(neg, 0.0) + jnp.log(1.0 + jnp.exp(-jnp.abs(neg)))

    for d, reverse in ((0, False), (1, True)):
        rate = (-RG_C) * softplus[d:d + 1]
        wg = wg_ref[0][:, 2 * w * d:2 * w * (d + 1)]
        bg = bg_ref[0][:, 2 * w * d:2 * w * (d + 1)]
        init = NPH if reverse else 0
        al_ref[init] = jnp.ones((n, w), F32)
        hl_ref[init] = jnp.zeros((n, w), F32)

        def local_body(i, _):
            s = (NPH - 1 - i) if reverse else i
            xc = (cw[0:1] * xe_ref[s] + cw[1:2] * xe_ref[s + 1]
                  + cw[2:3] * xe_ref[s + 2] + cw[3:4] * xe_ref[s + 3] + cb)
            g = jnp.dot(xc.astype(BF16), wg, preferred_element_type=F32) + bg
            r = _sigmoid(g[:, :w])
            ig = _sigmoid(g[:, w:])
            a = jnp.exp(rate * r)
            bt = jnp.sqrt(1.0 - a * a) * (ig * xc)
            src = (s + 1) if reverse else s
            dst = s if reverse else (s + 1)
            hl_ref[dst] = a * hl_ref[src] + bt
            al_ref[dst] = a * al_ref[src]
            return 0
        lax.fori_loop(0, NPH, local_body, 0)

        last = 0 if reverse else NPH
        carry = _row_scan_carry(al_ref[last], hl_ref[last], reverse)

        def fix_body(s, _):
            slot = s if reverse else (s + 1)
            v = hl_ref[slot] + al_ref[slot] * carry
            if reverse:
                y_ref[s] = y_ref[s] + v
            else:
                y_ref[s] = v
            return 0
        lax.fori_loop(0, NPH, fix_body, 0)


def _rglru(xrg3, conv_w, conv_b, wg, bg, lam):
    _, n_chunks, d_rg = xrg3.shape
    nb = d_rg // RG_CB
    return pl.pallas_call(
        _rglru_kernel,
        grid=(nb,),
        in_specs=[
            pl.BlockSpec((NPH, n_chunks, RG_CB), lambda j: (0, 0, j)),
            pl.BlockSpec((CONV_W, RG_CB), lambda j: (0, j)),
            pl.BlockSpec((1, RG_CB), lambda j: (0, j)),
            pl.BlockSpec((1, RG_CB, 4 * RG_CB), lambda j: (j, 0, 0)),
            pl.BlockSpec((1, 1, 4 * RG_CB), lambda j: (j, 0, 0)),
            pl.BlockSpec((2, RG_CB), lambda j: (0, j)),
        ],
        out_specs=pl.BlockSpec((NPH, n_chunks, RG_CB), lambda j: (0, 0, j)),
        out_shape=jax.ShapeDtypeStruct((NPH, n_chunks, d_rg), F32),
        scratch_shapes=[
            pltpu.VMEM((NPH + 3, n_chunks, RG_CB), F32),
            pltpu.VMEM((NPH + 1, n_chunks, RG_CB), F32),
            pltpu.VMEM((NPH + 1, n_chunks, RG_CB), F32),
        ],
        compiler_params=_cparams(1),
        name="rglru",
    )(xrg3, conv_w, conv_b, wg, bg, lam)


def _rglru_gate_weights(wa, ba, wx, bx):
    n_dir, heads, hd, _ = wa.shape
    per = RG_CB // hd
    nb = heads // per
    eye = jnp.eye(per, dtype=F32)

    def blockdiag(w):
        w = w.astype(F32).reshape(nb, per, hd, hd)
        return jnp.einsum('bpij,pq->bpiqj', w, eye).reshape(nb, RG_CB, RG_CB)

    cols, bias = [], []
    for d in range(n_dir):
        for w, b in ((wa, ba), (wx, bx)):
            cols.append(blockdiag(w[d]))
            bias.append(b[d].astype(F32).reshape(nb, 1, RG_CB))
    return jnp.concatenate(cols, axis=-1).astype(BF16), jnp.concatenate(bias, axis=-1)


def _rms_norm(x, g):
    return x * lax.rsqrt(jnp.mean(x * x, axis=-1, keepdims=True) + LN_EPS) * g


def _mixout_kernel(alpha, ys5_ref, yrg_ref, z_ref, h_ref, gw_ref, gb_ref, gn5_ref, gnr_ref,
                   w_ref, b_ref, g_ref, be_ref, o_ref):
    rows = NPH * CT
    d_s5 = ys5_ref.shape[-1]
    y = ys5_ref[...].reshape(rows, d_s5)
    glu = y * _sigmoid(jnp.dot(y.astype(BF16), gw_ref[...], preferred_element_type=F32) + gb_ref[...])
    ys5n = _rms_norm(glu, gn5_ref[...]).astype(BF16)
    yrg = yrg_ref[...].reshape(rows, -1) * _gelu(z_ref[...].reshape(rows, -1))
    yrgn = _rms_norm(yrg, gnr_ref[...]).astype(BF16)
    mix = (jnp.dot(ys5n, w_ref[:d_s5, :], preferred_element_type=F32)
           + jnp.dot(yrgn, w_ref[d_s5:, :], preferred_element_type=F32) + b_ref[...])
    h1 = _layer_norm(alpha * h_ref[...].reshape(rows, -1) + mix, g_ref[...], be_ref[...])
    o_ref[...] = h1.reshape(o_ref.shape)


def _mixout(alpha, ys5, yrg, zrg, h, glu_w, glu_b, gn_s5, gn_rg, w_out, b_out, ln_g, ln_b):
    _, n_chunks, d_model = h.shape
    d_s5, d_rg = ys5.shape[-1], yrg.shape[-1]
    return pl.pallas_call(
        functools.partial(_mixout_kernel, alpha),
        grid=(n_chunks // CT,),
        in_specs=[
            _tile3(d_s5), _tile3(d_rg), _tile3(d_rg), _tile3(d_model),
            _resident((d_s5, d_s5)), _resident((1, d_s5)), _resident((1, d_s5)), _resident((1, d_rg)),
            _resident((d_s5 + d_rg, d_model)), _resident((1, d_model)),
            _resident((1, d_model)), _resident((1, d_model)),
        ],
        out_specs=_tile3(d_model),
        out_shape=jax.ShapeDtypeStruct((NPH, n_chunks, d_model), F32),
        compiler_params=_cparams(1),
        name="mixout",
    )(ys5, yrg, zrg, h, glu_w, glu_b, gn_s5, gn_rg, w_out, b_out, ln_g, ln_b)


def _ffn_kernel(alpha, h_ref, p_ref, w1_ref, b1_ref, w2_ref, b2_ref, pw_ref, gw_ref, gb_ref,
                g_ref, be_ref, o_ref, hb_ref, acc_ref):
    j = pl.program_id(1)
    rows = NPH * CT

    @pl.when(j == 0)
    def _():
        hb = h_ref[...].reshape(rows, -1).astype(BF16)
        hb_ref[...] = hb
        gate = _sigmoid(jnp.dot(hb, gw_ref[...], preferred_element_type=F32) + gb_ref[...])
        pe = _to_phase_major(p_ref[...]).astype(BF16)
        acc_ref[...] = gate * jnp.dot(pe, pw_ref[...], preferred_element_type=F32) + b2_ref[...]

    a = jnp.dot(hb_ref[...], w1_ref[...], preferred_element_type=F32) + b1_ref[...]
    a = jnp.maximum(a, 0.0)
    acc_ref[...] += jnp.dot((a * a).astype(BF16), w2_ref[...], preferred_element_type=F32)

    @pl.when(j == pl.num_programs(1) - 1)
    def _():
        out = _layer_norm(alpha * h_ref[...].reshape(rows, -1) + acc_ref[...], g_ref[...], be_ref[...])
        o_ref[...] = _to_time_major(out)


def _ffn(alpha, h1, p2, w1, b1, w2, b2, ple_w, gate_w, gate_b, ln_g, ln_b, ff_tile):
    _, n_chunks, d_model = h1.shape
    d_ff = w1.shape[1]
    ple_dim = ple_w.shape[0]
    rows = NPH * CT
    return pl.pallas_call(
        functools.partial(_ffn_kernel, alpha),
        grid=(n_chunks // CT, d_ff // ff_tile),
        in_specs=[
            _tile3(d_model),
            pl.BlockSpec((rows, ple_dim), lambda i, j: (i, 0)),
            pl.BlockSpec((d_model, ff_tile), lambda i, j: (0, j)),
            pl.BlockSpec((1, ff_tile), lambda i, j: (0, j)),
            pl.BlockSpec((ff_tile, d_model), lambda i, j: (j, 0)),
            _resident((1, d_model)),
            _resident((ple_dim, d_model)), _resident((d_model, d_model)), _resident((1, d_model)),
            _resident((1, d_model)), _resident((1, d_model)),
        ],
        out_specs=pl.BlockSpec((rows, d_model), lambda i, j: (i, 0)),
        out_shape=jax.ShapeDtypeStruct((NPH * n_chunks, d_model), F32),
        scratch_shapes=[pltpu.VMEM((rows, d_model), BF16), pltpu.VMEM((rows, d_model), F32)],
        compiler_params=_cparams(2),
        name="ffn",
    )(h1, p2, w1, b1, w2, b2, ple_w, gate_w, gate_b, ln_g, ln_b)


def kernel(x, p, ln_in_g, ln_in_b, w_in, b_in, s5_lambda_re, s5_lambda_im, s5_log_step, s5_b_re, s5_b_im, s5_c_re, s5_c_im, s5_d, s5_glu_w, s5_glu_b, rg_conv_w, rg_conv_b, rg_wa, rg_ba, rg_wx, rg_bx, rg_lambda, gn_s5, gn_rg, w_out, b_out, ln1_g, ln1_b, w_ff1, b_ff1, w_ff2, b_ff2, ple_w, ple_gate_w, ple_gate_b, ln2_g, ln2_b):
    batch, seq, d_model = x.shape
    depth = w_in.shape[0]
    assert batch == 1 and depth == 1 and seq % (NPH * CT) == 0
    n_chunks = seq // NPH
    d_s5 = s5_glu_w.shape[-1]
    d_rg = rg_conv_w.shape[-1]
    alpha = (2.0 * depth) ** 0.25
    row = lambda v: v.reshape(1, -1).astype(F32)

    h, u3, xrg, zrg = _inproj(x.reshape(seq, d_model), row(ln_in_g), row(ln_in_b),
                              w_in[0].astype(BF16), row(b_in[0]), d_s5, d_rg)

    par, dvec, q_rows = _s5_discretise(
        s5_lambda_re[0], s5_lambda_im[0], s5_log_step[0], s5_b_re[0], s5_b_im[0],
        s5_c_re[0], s5_c_im[0], s5_d[0], n_chunks)
    ys5, (w_out_b, gate_w_b, w_ff1_b, w_ff2_b) = _s5(
        u3, par, dvec, q_rows, [w_out[0], ple_gate_w[0], w_ff1[0], w_ff2[0]])

    wg, bg = _rglru_gate_weights(rg_wa[0], rg_ba[0], rg_wx[0], rg_bx[0])
    yrg = _rglru(xrg, rg_conv_w[0].astype(F32), row(rg_conv_b[0]), wg, bg, rg_lambda[0].astype(F32))

    h1 = _mixout(alpha, ys5, yrg, zrg, h, s5_glu_w[0].astype(BF16), row(s5_glu_b[0]), row(gn_s5[0]),
                 row(gn_rg[0]), w_out_b, row(b_out[0]), row(ln1_g[0]), row(ln1_b[0]))

    out = _ffn(alpha, h1, p.reshape(seq, p.shape[-1]), w_ff1_b, row(b_ff1[0]),
               w_ff2_b, row(b_ff2[0]), ple_w[0].astype(BF16), gate_w_b,
               row(ple_gate_b[0]), row(ln2_g[0]), row(ln2_b[0]), ff_tile=1024)
    return out.reshape(batch, seq, d_model).astype(x.dtype)
```

```python
import functools
import math

import jax
import jax.numpy as jnp
from jax import lax
from jax.experimental import pallas as pl
from jax.experimental.pallas import tpu as pltpu

F32 = jnp.float32
BF16 = jnp.bfloat16

NPH = 16
CT = 32
S5_K = 16
S5_P = 64
S5_LB = 128
RG_CB = 128
CONV_W = 4
RG_C = 8.0
LN_EPS = 1e-5
VMEM_LIMIT_V7X = 56 * 1024 * 1024


def _cparams(n_axes):
    return pltpu.CompilerParams(
        dimension_semantics=("arbitrary",) * n_axes,
        vmem_limit_bytes=VMEM_LIMIT_V7X)


def _resident(shape):
    return pl.BlockSpec(shape, lambda *_: (0,) * len(shape), pipeline_mode=pl.Buffered(1))


def _tile3(width):
    return pl.BlockSpec((NPH, CT, width), lambda i, *_: (0, i, 0))


def _to_phase_major(x):
    w = x.shape[-1]
    return jnp.swapaxes(x.reshape(CT, NPH, w), 0, 1).reshape(NPH * CT, w)


def _to_time_major(x):
    w = x.shape[-1]
    return jnp.swapaxes(x.reshape(NPH, CT, w), 0, 1).reshape(CT * NPH, w)


def _layer_norm(x, g, b):
    mu = jnp.mean(x, axis=-1, keepdims=True)
    xc = x - mu
    var = jnp.mean(xc * xc, axis=-1, keepdims=True)
    return xc * lax.rsqrt(var + LN_EPS) * g + b


def _gelu(x):
    c = math.sqrt(2.0 / math.pi)
    return 0.5 * x * (1.0 + jnp.tanh(c * (x + 0.044715 * (x * x * x))))


def _sigmoid(x):
    return 1.0 / (1.0 + jnp.exp(-x))


def _inproj_kernel(x_ref, g_ref, b_ref, w_ref, bw_ref, h_ref, u_ref, xrg_ref, zrg_ref):
    d_s5, d_rg = u_ref.shape[-1], xrg_ref.shape[-1]
    h = _layer_norm(_to_phase_major(x_ref[...]), g_ref[...], b_ref[...])
    h_ref[...] = h.reshape(h_ref.shape)
    proj = jnp.dot(h.astype(BF16), w_ref[...], preferred_element_type=F32) + bw_ref[...]
    u_ref[...] = proj[:, :d_s5].astype(BF16).reshape(u_ref.shape)
    xrg_ref[...] = proj[:, d_s5:d_s5 + d_rg].reshape(xrg_ref.shape)
    zrg_ref[...] = proj[:, d_s5 + d_rg:].reshape(zrg_ref.shape)


def _inproj(x2, ln_g, ln_b, w_in, b_in, d_s5, d_rg):
    seq, d_model = x2.shape
    n_chunks = seq // NPH
    return pl.pallas_call(
        _inproj_kernel,
        grid=(n_chunks // CT,),
        in_specs=[
            pl.BlockSpec((CT * NPH, d_model), lambda i: (i, 0)),
            _resident((1, d_model)), _resident((1, d_model)),
            _resident(w_in.shape), _resident(b_in.shape),
        ],
        out_specs=[_tile3(d_model), _tile3(d_s5), _tile3(d_rg), _tile3(d_rg)],
        out_shape=[
            jax.ShapeDtypeStruct((NPH, n_chunks, d_model), F32),
            jax.ShapeDtypeStruct((NPH, n_chunks, d_s5), BF16),
            jax.ShapeDtypeStruct((NPH, n_chunks, d_rg), F32),
            jax.ShapeDtypeStruct((NPH, n_chunks, d_rg), F32),
        ],
        compiler_params=_cparams(1),
        name="inproj",
    )(x2, ln_g, ln_b, w_in, b_in)


def _row_scan_exclusive(sre, sim, qre, qim, reverse):
    n, w = sre.shape
    row = lax.broadcasted_iota(jnp.int32, (n, w), 0)

    def shift(v, k):
        if k % 8 == 0:
            z = jnp.zeros((k, w), v.dtype)
            return jnp.concatenate([v[k:], z] if reverse else [z, v[:n - k]], axis=0)
        if reverse:
            return jnp.where(row < n - k, pltpu.roll(v, n - k, axis=0), 0.0)
        return jnp.where(row >= k, pltpu.roll(v, k, axis=0), 0.0)

    xre, xim = shift(sre, 1), shift(sim, 1)
    k, i = 1, 0
    while k < n:
        pr, pi = qre[i:i + 1], qim[i:i + 1]
        if k % 8 == 0:
            keep = slice(n - k, n) if reverse else slice(0, k)
            dst = slice(0, n - k) if reverse else slice(k, n)
            src = slice(k, n) if reverse else slice(0, n - k)
            sr, si = xre[src], xim[src]
            nre = xre[dst] + pr * sr - pi * si
            nim = xim[dst] + pr * si + pi * sr
            order = (lambda new, old: [new, old]) if reverse else (lambda new, old: [old, new])
            xre = jnp.concatenate(order(nre, xre[keep]), axis=0)
            xim = jnp.concatenate(order(nim, xim[keep]), axis=0)
        else:
            sr, si = shift(xre, k), shift(xim, k)
            xre, xim = xre + pr * sr - pi * si, xim + pr * si + pi * sr
        k *= 2
        i += 1
    return xre, xim


def _dot(a, b):
    return jnp.dot(a.astype(BF16), b.astype(BF16), preferred_element_type=F32)


def _s5_chunk_operators(par, dvec, half):
    tk = NPH * S5_K
    lane = lax.broadcasted_iota(jnp.int32, par.shape, 1)
    par = jnp.where(lax.shift_right_logical(lane, 6) == half, par, 0.0)
    pa = par[0:128].T
    pb = par[128:256]
    lane16 = lax.broadcasted_iota(jnp.int32, (16, tk), 1)
    row16 = lax.broadcasted_iota(jnp.int32, (16, tk), 0)
    rep = (lax.shift_right_logical(lane16, 4) == row16).astype(F32)
    til = ((lane16 & 15) == row16).astype(F32)

    def cmul(ar, ai, br, bi):
        return ar * br - ai * bi, ar * bi + ai * br

    def w_of(c0):
        return cmul(_dot(pa[:, c0:c0 + 16], rep), _dot(pa[:, c0 + 16:c0 + 32], rep),
                    _dot(pa[:, c0 + 64:c0 + 80], til), _dot(pa[:, c0 + 80:c0 + 96], til))

    wf_re, wf_im = w_of(0)
    wb_re, wb_im = w_of(32)
    bc = jnp.concatenate([wf_re, wf_im, wb_re, wb_im], axis=0)

    def c_of(r0):
        c_re, c_im = pb[r0 + 64:r0 + 80], pb[r0 + 80:r0 + 96]
        blocks = [cmul(c_re, c_im, pb[r0 + t:r0 + t + 1], pb[r0 + 16 + t:r0 + 17 + t]) for t in range(NPH)]
        return (jnp.concatenate([b[0] for b in blocks], axis=0),
                -jnp.concatenate([b[1] for b in blocks], axis=0))

    cc = jnp.concatenate(list(c_of(0)) + list(c_of(32)), axis=1)

    kf = _dot(pb[64:80], wf_re) - _dot(pb[80:96], wf_im)
    kb = _dot(pb[96:112], wb_re) - _dot(pb[112:128], wb_im)
    blocks = []
    for t in range(NPH):
        left = S5_K * (NPH - 1 - t)
        right = S5_K * t
        f = kf if left == 0 else jnp.where(lane16 < tk - left, pltpu.roll(kf, tk - left, axis=1), 0.0)
        b = kb if right == 0 else jnp.where(lane16 >= right, pltpu.roll(kb, right, axis=1), 0.0)
        blocks.append(f + b)
    a = jnp.concatenate(blocks, axis=0)
    ri = lax.broadcasted_iota(jnp.int32, (tk, tk), 0)
    ci = lax.broadcasted_iota(jnp.int32, (tk, tk), 1)
    a = a + jnp.where(ri == ci, dvec, 0.0)
    return a.astype(BF16), bc.astype(BF16), cc.astype(BF16)


def _cast_stream(k, n_steps, srcs, dsts, inbufs, outbufs, sems):
    assert n_steps >= 2
    slot = lax.rem(k, 2)
    n_streams = len(srcs)
    rows = [src.shape[0] // n_steps for src in srcs]

    def read(i, step, sl):
        return pltpu.make_async_copy(srcs[i].at[pl.ds(step * rows[i], rows[i]), :], inbufs[i].at[sl],
                                     sems.at[i, 0, sl])

    def write(i, step, sl):
        return pltpu.make_async_copy(outbufs[i].at[sl], dsts[i].at[pl.ds(step * rows[i], rows[i]), :],
                                     sems.at[i, 1, sl])

    def begin():
        @pl.when(k == 0)
        def _():
            for i in range(n_streams):
                read(i, 0, 0).start()

        @pl.when(k + 1 < n_steps)
        def _():
            for i in range(n_streams):
                read(i, k + 1, 1 - slot).start()

        @pl.when(k >= 2)
        def _():
            for i in range(n_streams):
                write(i, k - 2, slot).wait()

        for i in range(n_streams):
            read(i, k, slot).wait()

    def cast():
        for i in range(n_streams):
            outbufs[i][slot] = inbufs[i][slot].astype(BF16)

    def end():
        for i in range(n_streams):
            write(i, k, slot).start()

        @pl.when(k == n_steps - 1)
        def _():
            for i in range(n_streams):
                write(i, k - 1, 1 - slot).wait()
                write(i, k, slot).wait()

    return begin, cast, end


def _s5_kernel(n_w, n_grid, u_ref, par_ref, d_ref, q_ref, *rest):
    w_src, rest = rest[:n_w], rest[n_w:]
    y_ref, rest = rest[0], rest[1:]
    w_dst, rest = rest[:n_w], rest[n_w:]
    xt_ref, yt_ref, a_ref, bc_ref, cc_ref, rest = rest[0], rest[1], rest[2], rest[3], rest[4], rest[5:]
    w_in_buf, w_out_buf, w_sems = rest[:n_w], rest[n_w:2 * n_w], rest[2 * n_w]
    n = u_ref.shape[1]
    p2 = 2 * S5_P
    tk = NPH * S5_K
    n_groups = S5_LB // S5_K
    n_pairs = n_groups // 2

    def to_rows(s, _):
        xt_ref[s] = u_ref[s].T
        return 0
    lax.fori_loop(0, NPH, to_rows, 0)

    for g in range(n_groups):
        a_ref[g], bc_ref[g], cc_ref[g] = _s5_chunk_operators(par_ref[g], d_ref[g], g % 2)

    def pair(gp, _):
        stream_begin, stream_cast, stream_end = _cast_stream(
            pl.program_id(0) * n_pairs + gp, n_grid * n_pairs, w_src, w_dst, w_in_buf, w_out_buf, w_sems)
        stream_begin()
        stream_cast()
        xs, ys = [], []
        for h in range(2):
            g = 2 * gp + h
            rows = pl.ds(pl.multiple_of(g * S5_K, S5_K), S5_K)
            x = xt_ref[:, rows, :].reshape(tk, n)
            xs.append(x)
            ys.append(jnp.dot(a_ref[g], x, preferred_element_type=F32))
        bc = jnp.concatenate([bc_ref[2 * gp], bc_ref[2 * gp + 1]], axis=1)
        s = jnp.dot(bc, jnp.concatenate(xs, axis=0), preferred_element_type=F32)
        st = [s[i * p2:(i + 1) * p2].T for i in range(4)]
        q = q_ref[gp]
        hf_re, hf_im = _row_scan_exclusive(st[0], st[1], q[0:16], q[16:32], False)
        hb_re, hb_im = _row_scan_exclusive(st[2], st[3], q[32:48], q[48:64], True)
        hin = jnp.concatenate([hf_re, hf_im, hb_re, hb_im], axis=1).astype(BF16)
        for h in range(2):
            g = 2 * gp + h
            rows = pl.ds(pl.multiple_of(g * S5_K, S5_K), S5_K)
            y = ys[h] + lax.dot_general(cc_ref[g], hin, (((1,), (1,)), ((), ())), preferred_element_type=F32)
            yt_ref[:, rows, :] = _gelu(y).reshape(NPH, S5_K, n)
        stream_end()
        return 0
    lax.fori_loop(0, n_pairs, pair, 0)

    def to_lanes(t, _):
        y_ref[t] = yt_ref[t].T
        return 0
    lax.fori_loop(0, NPH, to_lanes, 0)


def _s5(u3, par, dvec, q_rows, weights):
    _, n_chunks, d_s5 = u3.shape
    gpb = S5_LB // S5_K
    tk = NPH * S5_K
    n_grid = d_s5 // S5_LB
    n_steps = n_grid * (gpb // 2)
    chunk = lambda w: (w.shape[0] // n_steps, w.shape[1])
    assert all(w.shape[0] % (16 * n_steps) == 0 for w in weights)
    per_block = lambda n, *tail: pl.BlockSpec((n,) + tail, lambda b: (b,) + (0,) * len(tail))
    hbm = pl.BlockSpec(memory_space=pl.ANY)
    outs = pl.pallas_call(
        functools.partial(_s5_kernel, len(weights), n_grid),
        grid=(n_grid,),
        in_specs=[
            pl.BlockSpec((NPH, n_chunks, S5_LB), lambda b: (0, 0, b)),
            per_block(gpb, 256, 2 * S5_P), per_block(gpb, 1, tk),
            per_block(gpb // 2, 64, 2 * S5_P),
        ] + [hbm] * len(weights),
        out_specs=[pl.BlockSpec((NPH, n_chunks, S5_LB), lambda b: (0, 0, b))] + [hbm] * len(weights),
        out_shape=[jax.ShapeDtypeStruct((NPH, n_chunks, d_s5), F32)]
        + [jax.ShapeDtypeStruct(w.shape, BF16) for w in weights],
        scratch_shapes=[
            pltpu.VMEM((NPH, S5_LB, n_chunks), BF16),
            pltpu.VMEM((NPH, S5_LB, n_chunks), F32),
            pltpu.VMEM((gpb, tk, tk), BF16),
            pltpu.VMEM((gpb, 8 * S5_P, tk), BF16),
            pltpu.VMEM((gpb, tk, 8 * S5_P), BF16),
        ] + [pltpu.VMEM((2,) + chunk(w), F32) for w in weights]
        + [pltpu.VMEM((2,) + chunk(w), BF16) for w in weights]
        + [pltpu.SemaphoreType.DMA((len(weights), 2, 2))],
        compiler_params=_cparams(1),
        name="s5",
    )(u3, par, dvec, q_rows, *weights)
    return outs[0], outs[1:]


def _s5_discretise(lam_re, lam_im, log_step, b_re, b_im, c_re, c_im, d, n_chunks):
    assert NPH == 16 and S5_K == 16 and n_chunks <= 2 ** 16
    groups, p = lam_re.shape[1:]
    lre = jnp.minimum(lam_re.astype(F32), -1e-4)[:, :, None, :]
    lim = lam_im.astype(F32)[:, :, None, :]
    step = jnp.exp(log_step.astype(F32))[:, :, None, None]
    are, aim = lre * step, lim * step

    def cmul(x, y):
        return x[0] * y[0] - x[1] * y[1], x[0] * y[1] + x[1] * y[0]

    def stack(zs):
        return jnp.concatenate([z[0] for z in zs], axis=2), jnp.concatenate([z[1] for z in zs], axis=2)

    lam_bar = (jnp.exp(are) * jnp.cos(aim), jnp.exp(are) * jnp.sin(aim))
    powers = [(jnp.ones_like(are), jnp.zeros_like(are))]
    for _ in range(NPH):
        powers.append(cmul(powers[-1], lam_bar))
    squares = [powers[NPH]]
    for _ in range(15):
        squares.append(cmul(squares[-1], squares[-1]))
    (pw_re, pw_im), (qr, qi) = lax.optimization_barrier((stack(powers), stack(squares)))
    nr, ni = pw_re[:, :, 1:2] - 1.0, pw_im[:, :, 1:2]
    den = lre * lre + lim * lim
    zr, zi = (nr * lre + ni * lim) / den, (ni * lre - nr * lim) / den
    br, bi = jnp.swapaxes(b_re.astype(F32), -1, -2), jnp.swapaxes(b_im.astype(F32), -1, -2)
    bbar_re, bbar_im = zr * br - zi * bi, zr * bi + zi * br

    desc = lambda v: v[:, NPH - 1::-1]
    asc = lambda v: v[:, :NPH]
    up = lambda v: v[:, 1:]
    down = lambda v: v[:, NPH:0:-1]
    par = jnp.concatenate(
        [desc(pw_re[0]), desc(pw_im[0]), asc(pw_re[1]), asc(pw_im[1]),
         bbar_re[0], bbar_im[0], bbar_re[1], bbar_im[1],
         up(pw_re[0]), up(pw_im[0]), down(pw_re[1]), down(pw_im[1]),
         c_re[0].astype(F32), c_im[0].astype(F32), c_re[1].astype(F32), c_im[1].astype(F32)], axis=1)
    par = jnp.concatenate([par, par], axis=-1)
    dvec = jnp.tile(d.astype(F32), (1, NPH))[:, None, :]
    lanes = lambda v: v.reshape(groups // 2, 2, 16, p).transpose(0, 2, 1, 3).reshape(groups // 2, 16, 2 * p)
    q_rows = jnp.concatenate([lanes(qr[0]), lanes(qi[0]), lanes(qr[1]), lanes(qi[1])], axis=1)
    return par, dvec, q_rows


def _row_scan_carry(a, h, reverse):
    n, w = a.shape
    row = lax.broadcasted_iota(jnp.int32, (n, w), 0)

    def shift(v, k, fill):
        if k % 8 == 0:
            z = jnp.full((k, w), fill, v.dtype)
            if reverse:
                return jnp.concatenate([v[k:], z], axis=0)
            return jnp.concatenate([z, v[:n - k]], axis=0)
        if reverse:
            return jnp.where(row < n - k, pltpu.roll(v, n - k, axis=0), fill)
        return jnp.where(row >= k, pltpu.roll(v, k, axis=0), fill)

    k = 1
    while k < n:
        h = h + a * shift(h, k, 0.0)
        a = a * shift(a, k, 1.0)
        k *= 2
    return shift(h, 1, 0.0)


def _rglru_kernel(x_ref, cw_ref, cb_ref, wg_ref, bg_ref, lam_ref, y_ref, xe_ref, al_ref, hl_ref):
    n, w = x_ref.shape[1], x_ref.shape[2]
    row = lax.broadcasted_iota(jnp.int32, (n, w), 0)

    def from_prev_chunk(v):
        return jnp.where(row >= 1, pltpu.roll(v, 1, axis=0), 0.0)

    def from_next_chunk(v):
        return jnp.where(row < n - 1, pltpu.roll(v, n - 1, axis=0), 0.0)

    xe_ref[0] = from_prev_chunk(x_ref[NPH - 2])
    xe_ref[1] = from_prev_chunk(x_ref[NPH - 1])

    def copy_body(s, _):
        xe_ref[s + 2] = x_ref[s]
        return 0
    lax.fori_loop(0, NPH, copy_body, 0)
    xe_ref[NPH + 2] = from_next_chunk(x_ref[0])

    cw = cw_ref[...]
    cb = cb_ref[...]
    lam = lam_ref[...]
    neg = -lam
    softplus = jnp.maximum(neg, 0.0) + jnp.log(1.0 + jnp.exp(-jnp.abs(neg)))

    for d, reverse in ((0, False), (1, True)):
        rate = (-RG_C) * softplus[d:d + 1]
        wg = wg_ref[0][:, 2 * w * d:2 * w * (d + 1)]
        bg = bg_ref[0][:, 2 * w * d:2 * w * (d + 1)]
        init = NPH if reverse else 0
        al_ref[init] = jnp.ones((n, w), F32)
        hl_ref[init] = jnp.zeros((n, w), F32)

        def local_body(i, _):
            s = (NPH - 1 - i) if reverse else i
            xc = (cw[0:1] * xe_ref[s] + cw[1:2] * xe_ref[s + 1]
                  + cw[2:3] * xe_ref[s + 2] + cw[3:4] * xe_ref[s + 3] + cb)
            g = jnp.dot(xc.astype(BF16), wg, preferred_element_type=F32) + bg
            r = _sigmoid(g[:, :w])
            ig = _sigmoid(g[:, w:])
            a = jnp.exp(rate * r)
            bt = jnp.sqrt(1.0 - a * a) * (ig * xc)
            src = (s + 1) if reverse else s
            dst = s if reverse else (s + 1)
            hl_ref[dst] = a * hl_ref[src] + bt
            al_ref[dst] = a * al_ref[src]
            return 0
        lax.fori_loop(0, NPH, local_body, 0)

        last = 0 if reverse else NPH
        carry = _row_scan_carry(al_ref[last], hl_ref[last], reverse)

        def fix_body(s, _):
            slot = s if reverse else (s + 1)
            v = hl_ref[slot] + al_ref[slot] * carry
            if reverse:
                y_ref[s] = y_ref[s] + v
            else:
                y_ref[s] = v
            return 0
        lax.fori_loop(0, NPH, fix_body, 0)


def _rglru(xrg3, conv_w, conv_b, wg, bg, lam):
    _, n_chunks, d_rg = xrg3.shape
    nb = d_rg // RG_CB
    return pl.pallas_call(
        _rglru_kernel,
        grid=(nb,),
        in_specs=[
            pl.BlockSpec((NPH, n_chunks, RG_CB), lambda j: (0, 0, j)),
            pl.BlockSpec((CONV_W, RG_CB), lambda j: (0, j)),
            pl.BlockSpec((1, RG_CB), lambda j: (0, j)),
            pl.BlockSpec((1, RG_CB, 4 * RG_CB), lambda j: (j, 0, 0)),
            pl.BlockSpec((1, 1, 4 * RG_CB), lambda j: (j, 0, 0)),
            pl.BlockSpec((2, RG_CB), lambda j: (0, j)),
        ],
        out_specs=pl.BlockSpec((NPH, n_chunks, RG_CB), lambda j: (0, 0, j)),
        out_shape=jax.ShapeDtypeStruct((NPH, n_chunks, d_rg), F32),
        scratch_shapes=[
            pltpu.VMEM((NPH + 3, n_chunks, RG_CB), F32),
            pltpu.VMEM((NPH + 1, n_chunks, RG_CB), F32),
            pltpu.VMEM((NPH + 1, n_chunks, RG_CB), F32),
        ],
        compiler_params=_cparams(1),
        name="rglru",
    )(xrg3, conv_w, conv_b, wg, bg, lam)


def _rglru_gate_weights(wa, ba, wx, bx):
    n_dir, heads, hd, _ = wa.shape
    per = RG_CB // hd
    nb = heads // per
    eye = jnp.eye(per, dtype=F32)

    def blockdiag(w):
        w = w.astype(F32).reshape(nb, per, hd, hd)
        return jnp.einsum('bpij,pq->bpiqj', w, eye).reshape(nb, RG_CB, RG_CB)

    cols, bias = [], []
    for d in range(n_dir):
        for w, b in ((wa, ba), (wx, bx)):
            cols.append(blockdiag(w[d]))
            bias.append(b[d].astype(F32).reshape(nb, 1, RG_CB))
    return jnp.concatenate(cols, axis=-1).astype(BF16), jnp.concatenate(bias, axis=-1)


def _rms_norm(x, g):
    return x * lax.rsqrt(jnp.mean(x * x, axis=-1, keepdims=True) + LN_EPS) * g


def _mixout_kernel(alpha, ys5_ref, yrg_ref, z_ref, h_ref, gw_ref, gb_ref, gn5_ref, gnr_ref,
                   w_ref, b_ref, g_ref, be_ref, o_ref):
    rows = NPH * CT
    d_s5 = ys5_ref.shape[-1]
    y = ys5_ref[...].reshape(rows, d_s5)
    glu = y * _sigmoid(jnp.dot(y.astype(BF16), gw_ref[...], preferred_element_type=F32) + gb_ref[...])
    ys5n = _rms_norm(glu, gn5_ref[...]).astype(BF16)
    yrg = yrg_ref[...].reshape(rows, -1) * _gelu(z_ref[...].reshape(rows, -1))
    yrgn = _rms_norm(yrg, gnr_ref[...]).astype(BF16)
    mix = (jnp.dot(ys5n, w_ref[:d_s5, :], preferred_element_type=F32)
           + jnp.dot(yrgn, w_ref[d_s5:, :], preferred_element_type=F32) + b_ref[...])
    h1 = _layer_norm(alpha * h_ref[...].reshape(rows, -1) + mix, g_ref[...], be_ref[...])
    o_ref[...] = h1.reshape(o_ref.shape)


def _mixout(alpha, ys5, yrg, zrg, h, glu_w, glu_b, gn_s5, gn_rg, w_out, b_out, ln_g, ln_b):
    _, n_chunks, d_model = h.shape
    d_s5, d_rg = ys5.shape[-1], yrg.shape[-1]
    return pl.pallas_call(
        functools.partial(_mixout_kernel, alpha),
        grid=(n_chunks // CT,),
        in_specs=[
            _tile3(d_s5), _tile3(d_rg), _tile3(d_rg), _tile3(d_model),
            _resident((d_s5, d_s5)), _resident((1, d_s5)), _resident((1, d_s5)), _resident((1, d_rg)),
            _resident((d_s5 + d_rg, d_model)), _resident((1, d_model)),
            _resident((1, d_model)), _resident((1, d_model)),
        ],
        out_specs=_tile3(d_model),
        out_shape=jax.ShapeDtypeStruct((NPH, n_chunks, d_model), F32),
        compiler_params=_cparams(1),
        name="mixout",
    )(ys5, yrg, zrg, h, glu_w, glu_b, gn_s5, gn_rg, w_out, b_out, ln_g, ln_b)


def _ffn_kernel(alpha, h_ref, p_ref, w1_ref, b1_ref, w2_ref, b2_ref, pw_ref, gw_ref, gb_ref,
                g_ref, be_ref, o_ref, hb_ref, acc_ref):
    j = pl.program_id(1)
    rows = NPH * CT

    @pl.when(j == 0)
    def _():
        hb = h_ref[...].reshape(rows, -1).astype(BF16)
        hb_ref[...] = hb
        gate = _sigmoid(jnp.dot(hb, gw_ref[...], preferred_element_type=F32) + gb_ref[...])
        pe = _to_phase_major(p_ref[...]).astype(BF16)
        acc_ref[...] = gate * jnp.dot(pe, pw_ref[...], preferred_element_type=F32) + b2_ref[...]

    a = jnp.dot(hb_ref[...], w1_ref[...], preferred_element_type=F32) + b1_ref[...]
    a = jnp.maximum(a, 0.0)
    acc_ref[...] += jnp.dot((a * a).astype(BF16), w2_ref[...], preferred_element_type=F32)

    @pl.when(j == pl.num_programs(1) - 1)
    def _():
        out = _layer_norm(alpha * h_ref[...].reshape(rows, -1) + acc_ref[...], g_ref[...], be_ref[...])
        o_ref[...] = _to_time_major(out)


def _ffn(alpha, h1, p2, w1, b1, w2, b2, ple_w, gate_w, gate_b, ln_g, ln_b, ff_tile):
    _, n_chunks, d_model = h1.shape
    d_ff = w1.shape[1]
    ple_dim = ple_w.shape[0]
    rows = NPH * CT
    return pl.pallas_call(
        functools.partial(_ffn_kernel, alpha),
        grid=(n_chunks // CT, d_ff // ff_tile),
        in_specs=[
            _tile3(d_model),
            pl.BlockSpec((rows, ple_dim), lambda i, j: (i, 0)),
            pl.BlockSpec((d_model, ff_tile), lambda i, j: (0, j)),
            pl.BlockSpec((1, ff_tile), lambda i, j: (0, j)),
            pl.BlockSpec((ff_tile, d_model), lambda i, j: (j, 0)),
            _resident((1, d_model)),
            _resident((ple_dim, d_model)), _resident((d_model, d_model)), _resident((1, d_model)),
            _resident((1, d_model)), _resident((1, d_model)),
        ],
        out_specs=pl.BlockSpec((rows, d_model), lambda i, j: (i, 0)),
        out_shape=jax.ShapeDtypeStruct((NPH * n_chunks, d_model), F32),
        scratch_shapes=[pltpu.VMEM((rows, d_model), BF16), pltpu.VMEM((rows, d_model), F32)],
        compiler_params=_cparams(2),
        name="ffn",
    )(h1, p2, w1, b1, w2, b2, ple_w, gate_w, gate_b, ln_g, ln_b)


def kernel(x, p, ln_in_g, ln_in_b, w_in, b_in, s5_lambda_re, s5_lambda_im, s5_log_step, s5_b_re, s5_b_im, s5_c_re, s5_c_im, s5_d, s5_glu_w, s5_glu_b, rg_conv_w, rg_conv_b, rg_wa, rg_ba, rg_wx, rg_bx, rg_lambda, gn_s5, gn_rg, w_out, b_out, ln1_g, ln1_b, w_ff1, b_ff1, w_ff2, b_ff2, ple_w, ple_gate_w, ple_gate_b, ln2_g, ln2_b):
    batch, seq, d_model = x.shape
    depth = w_in.shape[0]
    assert batch == 1 and depth == 1 and seq % (NPH * CT) == 0
    n_chunks = seq // NPH
    d_s5 = s5_glu_w.shape[-1]
    d_rg = rg_conv_w.shape[-1]
    alpha = (2.0 * depth) ** 0.25
    row = lambda v: v.reshape(1, -1).astype(F32)

    h, u3, xrg, zrg = _inproj(x.reshape(seq, d_model), row(ln_in_g), row(ln_in_b),
                              w_in[0].astype(BF16), row(b_in[0]), d_s5, d_rg)

    par, dvec, q_rows = _s5_discretise(
        s5_lambda_re[0], s5_lambda_im[0], s5_log_step[0], s5_b_re[0], s5_b_im[0],
        s5_c_re[0], s5_c_im[0], s5_d[0], n_chunks)
    ys5, (w_out_b, gate_w_b, w_ff1_b, w_ff2_b) = _s5(
        u3, par, dvec, q_rows, [w_out[0], ple_gate_w[0], w_ff1[0], w_ff2[0]])

    wg, bg = _rglru_gate_weights(rg_wa[0], rg_ba[0], rg_wx[0], rg_bx[0])
    yrg = _rglru(xrg, rg_conv_w[0].astype(F32), row(rg_conv_b[0]), wg, bg, rg_lambda[0].astype(F32))

    h1 = _mixout(alpha, ys5, yrg, zrg, h, s5_glu_w[0].astype(BF16), row(s5_glu_b[0]), row(gn_s5[0]),
                 row(gn_rg[0]), w_out_b, row(b_out[0]), row(ln1_g[0]), row(ln1_b[0]))

    out = _ffn(alpha, h1, p.reshape(seq, p.shape[-1]), w_ff1_b, row(b_ff1[0]),
               w_ff2_b, row(b_ff2[0]), ple_w[0].astype(BF16), gate_w_b,
               row(ple_gate_b[0]), row(ln2_g[0]), row(ln2_b[0]), ff_tile=1024)
    return out.reshape(batch, seq, d_model).astype(x.dtype)
```

```python
import functools
import math

import jax
import jax.numpy as jnp
from jax import lax
from jax.experimental import pallas as pl
from jax.experimental.pallas import tpu as pltpu

F32 = jnp.float32
BF16 = jnp.bfloat16

NPH = 16
CT = 32
S5_K = 16
S5_P = 64
S5_LB = 128
RG_CB = 128
CONV_W = 4
RG_C = 8.0
LN_EPS = 1e-5
VMEM_LIMIT_V7X = 56 * 1024 * 1024


def _cparams(n_axes):
    return pltpu.CompilerParams(
        dimension_semantics=("arbitrary",) * n_axes,
        vmem_limit_bytes=VMEM_LIMIT_V7X)


def _resident(shape):
    return pl.BlockSpec(shape, lambda *_: (0,) * len(shape), pipeline_mode=pl.Buffered(1))


def _tile3(width):
    return pl.BlockSpec((NPH, CT, width), lambda i, *_: (0, i, 0))


def _to_phase_major(x):
    w = x.shape[-1]
    return jnp.swapaxes(x.reshape(CT, NPH, w), 0, 1).reshape(NPH * CT, w)


def _to_time_major(x):
    w = x.shape[-1]
    return jnp.swapaxes(x.reshape(NPH, CT, w), 0, 1).reshape(CT * NPH, w)


def _layer_norm(x, g, b):
    mu = jnp.mean(x, axis=-1, keepdims=True)
    xc = x - mu
    var = jnp.mean(xc * xc, axis=-1, keepdims=True)
    return xc * lax.rsqrt(var + LN_EPS) * g + b


def _gelu(x):
    c = math.sqrt(2.0 / math.pi)
    return 0.5 * x * (1.0 + jnp.tanh(c * (x + 0.044715 * (x * x * x))))


def _sigmoid(x):
    return 1.0 / (1.0 + jnp.exp(-x))


def _inproj_kernel(x_ref, g_ref, b_ref, w_ref, bw_ref, h_ref, u_ref, xrg_ref, zrg_ref):
    d_s5, d_rg = u_ref.shape[-1], xrg_ref.shape[-1]
    h = _layer_norm(_to_phase_major(x_ref[...]), g_ref[...], b_ref[...])
    h_ref[...] = h.reshape(h_ref.shape)
    proj = jnp.dot(h.astype(BF16), w_ref[...], preferred_element_type=F32) + bw_ref[...]
    u_ref[...] = proj[:, :d_s5].astype(BF16).reshape(u_ref.shape)
    xrg_ref[...] = proj[:, d_s5:d_s5 + d_rg].reshape(xrg_ref.shape)
    zrg_ref[...] = proj[:, d_s5 + d_rg:].reshape(zrg_ref.shape)


def _inproj(x2, ln_g, ln_b, w_in, b_in, d_s5, d_rg):
    seq, d_model = x2.shape
    n_chunks = seq // NPH
    return pl.pallas_call(
        _inproj_kernel,
        grid=(n_chunks // CT,),
        in_specs=[
            pl.BlockSpec((CT * NPH, d_model), lambda i: (i, 0)),
            _resident((1, d_model)), _resident((1, d_model)),
            _resident(w_in.shape), _resident(b_in.shape),
        ],
        out_specs=[_tile3(d_model), _tile3(d_s5), _tile3(d_rg), _tile3(d_rg)],
        out_shape=[
            jax.ShapeDtypeStruct((NPH, n_chunks, d_model), F32),
            jax.ShapeDtypeStruct((NPH, n_chunks, d_s5), BF16),
            jax.ShapeDtypeStruct((NPH, n_chunks, d_rg), F32),
            jax.ShapeDtypeStruct((NPH, n_chunks, d_rg), F32),
        ],
        compiler_params=_cparams(1),
        name="inproj",
    )(x2, ln_g, ln_b, w_in, b_in)


def _row_scan_exclusive(sre, sim, qre, qim, reverse):
    n, w = sre.shape
    row = lax.broadcasted_iota(jnp.int32, (n, w), 0)

    def shift(v, k):
        if k % 8 == 0:
            z = jnp.zeros((k, w), v.dtype)
            return jnp.concatenate([v[k:], z] if reverse else [z, v[:n - k]], axis=0)
        if reverse:
            return jnp.where(row < n - k, pltpu.roll(v, n - k, axis=0), 0.0)
        return jnp.where(row >= k, pltpu.roll(v, k, axis=0), 0.0)

    xre, xim = shift(sre, 1), shift(sim, 1)
    k, i = 1, 0
    while k < n:
        pr, pi = qre[i:i + 1], qim[i:i + 1]
        if k % 8 == 0:
            keep = slice(n - k, n) if reverse else slice(0, k)
            dst = slice(0, n - k) if reverse else slice(k, n)
            src = slice(k, n) if reverse else slice(0, n - k)
            sr, si = xre[src], xim[src]
            nre = xre[dst] + pr * sr - pi * si
            nim = xim[dst] + pr * si + pi * sr
            order = (lambda new, old: [new, old]) if reverse else (lambda new, old: [old, new])
            xre = jnp.concatenate(order(nre, xre[keep]), axis=0)
            xim = jnp.concatenate(order(nim, xim[keep]), axis=0)
        else:
            sr, si = shift(xre, k), shift(xim, k)
            xre, xim = xre + pr * sr - pi * si, xim + pr * si + pi * sr
        k *= 2
        i += 1
    return xre, xim


def _dot(a, b):
    return jnp.dot(a.astype(BF16), b.astype(BF16), preferred_element_type=F32)


def _s5_chunk_operators(par, dvec, half):
    tk = NPH * S5_K
    lane = lax.broadcasted_iota(jnp.int32, par.shape, 1)
    par = jnp.where(lax.shift_right_logical(lane, 6) == half, par, 0.0)
    pa = par[0:128].T
    pb = par[128:256]
    lane16 = lax.broadcasted_iota(jnp.int32, (16, tk), 1)
    row16 = lax.broadcasted_iota(jnp.int32, (16, tk), 0)
    rep = (lax.shift_right_logical(lane16, 4) == row16).astype(F32)
    til = ((lane16 & 15) == row16).astype(F32)

    def cmul(ar, ai, br, bi):
        return ar * br - ai * bi, ar * bi + ai * br

    def w_of(c0):
        return cmul(_dot(pa[:, c0:c0 + 16], rep), _dot(pa[:, c0 + 16:c0 + 32], rep),
                    _dot(pa[:, c0 + 64:c0 + 80], til), _dot(pa[:, c0 + 80:c0 + 96], til))

    wf_re, wf_im = w_of(0)
    wb_re, wb_im = w_of(32)
    bc = jnp.concatenate([wf_re, wf_im, wb_re, wb_im], axis=0)

    def c_of(r0):
        c_re, c_im = pb[r0 + 64:r0 + 80], pb[r0 + 80:r0 + 96]
        blocks = [cmul(c_re, c_im, pb[r0 + t:r0 + t + 1], pb[r0 + 16 + t:r0 + 17 + t]) for t in range(NPH)]
        return (jnp.concatenate([b[0] for b in blocks], axis=0),
                -jnp.concatenate([b[1] for b in blocks], axis=0))

    cc = jnp.concatenate(list(c_of(0)) + list(c_of(32)), axis=1)

    kf = _dot(pb[64:80], wf_re) - _dot(pb[80:96], wf_im)
    kb = _dot(pb[96:112], wb_re) - _dot(pb[112:128], wb_im)
    blocks = []
    for t in range(NPH):
        left = S5_K * (NPH - 1 - t)
        right = S5_K * t
        f = kf if left == 0 else jnp.where(lane16 < tk - left, pltpu.roll(kf, tk - left, axis=1), 0.0)
        b = kb if right == 0 else jnp.where(lane16 >= right, pltpu.roll(kb, right, axis=1), 0.0)
        blocks.append(f + b)
    a = jnp.concatenate(blocks, axis=0)
    ri = lax.broadcasted_iota(jnp.int32, (tk, tk), 0)
    ci = lax.broadcasted_iota(jnp.int32, (tk, tk), 1)
    a = a + jnp.where(ri == ci, dvec, 0.0)
    return a.astype(BF16), bc.astype(BF16), cc.astype(BF16)


def _cast_stream(k, n_steps, srcs, dsts, inbufs, outbufs, sems):
    assert n_steps >= 2
    slot = lax.rem(k, 2)
    n_streams = len(srcs)
    rows = [src.shape[0] // n_steps for src in srcs]

    def read(i, step, sl):
        return pltpu.make_async_copy(srcs[i].at[pl.ds(step * rows[i], rows[i]), :], inbufs[i].at[sl],
                                     sems.at[i, 0, sl])

    def write(i, step, sl):
        return pltpu.make_async_copy(outbufs[i].at[sl], dsts[i].at[pl.ds(step * rows[i], rows[i]), :],
                                     sems.at[i, 1, sl])

    def begin():
        @pl.when(k == 0)
        def _():
            for i in range(n_streams):
                read(i, 0, 0).start()

        @pl.when(k + 1 < n_steps)
        def _():
            for i in range(n_streams):
                read(i, k + 1, 1 - slot).start()

        @pl.when(k >= 2)
        def _():
            for i in range(n_streams):
                write(i, k - 2, slot).wait()

        for i in range(n_streams):
            read(i, k, slot).wait()

    def cast():
        for i in range(n_streams):
            outbufs[i][slot] = inbufs[i][slot].astype(BF16)

    def end():
        for i in range(n_streams):
            write(i, k, slot).start()

        @pl.when(k == n_steps - 1)
        def _():
            for i in range(n_streams):
                write(i, k - 1, 1 - slot).wait()
                write(i, k, slot).wait()

    return begin, cast, end


def _s5_kernel(n_w, n_grid, u_ref, raw_ref, d_ref, *rest):
    w_src, rest = rest[:n_w], rest[n_w:]
    y_ref, rest = rest[0], rest[1:]
    w_dst, rest = rest[:n_w], rest[n_w:]
    xt_ref, yt_ref, a_ref, bc_ref, cc_ref, q_ref, rest = rest[:6] + (rest[6:],)
    w_in_buf, w_out_buf, w_sems = rest[:n_w], rest[n_w:2 * n_w], rest[2 * n_w]
    n = u_ref.shape[1]
    p2 = 2 * S5_P
    tk = NPH * S5_K
    n_groups = S5_LB // S5_K
    n_pairs = n_groups // 2

    def to_rows(s, _):
        xt_ref[s] = u_ref[s].T
        return 0
    lax.fori_loop(0, NPH, to_rows, 0)

    for g in range(n_groups):
        par, q_ref[g] = _s5_discretise(raw_ref[0, g], raw_ref[1, g])
        a_ref[g], bc_ref[g], cc_ref[g] = _s5_chunk_operators(par, d_ref[g], g % 2)

    def pair(gp, _):
        stream_begin, stream_cast, stream_end = _cast_stream(
            pl.program_id(0) * n_pairs + gp, n_grid * n_pairs, w_src, w_dst, w_in_buf, w_out_buf, w_sems)
        stream_begin()
        stream_cast()
        xs, ys = [], []
        for h in range(2):
            g = 2 * gp + h
            rows = pl.ds(pl.multiple_of(g * S5_K, S5_K), S5_K)
            x = xt_ref[:, rows, :].reshape(tk, n)
            xs.append(x)
            ys.append(jnp.dot(a_ref[g], x, preferred_element_type=F32))
        bc = jnp.concatenate([bc_ref[2 * gp], bc_ref[2 * gp + 1]], axis=1)
        s = jnp.dot(bc, jnp.concatenate(xs, axis=0), preferred_element_type=F32)
        st = [s[i * p2:(i + 1) * p2].T for i in range(4)]
        lane = lax.broadcasted_iota(jnp.int32, (4 * 16, p2), 1)
        q = jnp.where(lane < S5_P, q_ref[2 * gp], q_ref[2 * gp + 1])
        hf_re, hf_im = _row_scan_exclusive(st[0], st[1], q[0:16], q[16:32], False)
        hb_re, hb_im = _row_scan_exclusive(st[2], st[3], q[32:48], q[48:64], True)
        hin = jnp.concatenate([hf_re, hf_im, hb_re, hb_im], axis=1).astype(BF16)
        for h in range(2):
            g = 2 * gp + h
            rows = pl.ds(pl.multiple_of(g * S5_K, S5_K), S5_K)
            y = ys[h] + lax.dot_general(cc_ref[g], hin, (((1,), (1,)), ((), ())), preferred_element_type=F32)
            yt_ref[:, rows, :] = _gelu(y).reshape(NPH, S5_K, n)
        stream_end()
        return 0
    lax.fori_loop(0, n_pairs, pair, 0)

    def to_lanes(t, _):
        y_ref[t] = yt_ref[t].T
        return 0
    lax.fori_loop(0, NPH, to_lanes, 0)


def _s5(u3, raw, dvec, weights):
    _, n_chunks, d_s5 = u3.shape
    gpb = S5_LB // S5_K
    tk = NPH * S5_K
    n_grid = d_s5 // S5_LB
    n_steps = n_grid * (gpb // 2)
    chunk = lambda w: (w.shape[0] // n_steps, w.shape[1])
    assert all(w.shape[0] % (16 * n_steps) == 0 for w in weights)
    per_block = lambda n, *tail: pl.BlockSpec((n,) + tail, lambda b: (b,) + (0,) * len(tail))
    hbm = pl.BlockSpec(memory_space=pl.ANY)
    outs = pl.pallas_call(
        functools.partial(_s5_kernel, len(weights), n_grid),
        grid=(n_grid,),
        in_specs=[
            pl.BlockSpec((NPH, n_chunks, S5_LB), lambda b: (0, 0, b)),
            pl.BlockSpec((2, gpb) + raw.shape[2:], lambda b: (0, b, 0, 0)), per_block(gpb, 1, tk),
        ] + [hbm] * len(weights),
        out_specs=[pl.BlockSpec((NPH, n_chunks, S5_LB), lambda b: (0, 0, b))] + [hbm] * len(weights),
        out_shape=[jax.ShapeDtypeStruct((NPH, n_chunks, d_s5), F32)]
        + [jax.ShapeDtypeStruct(w.shape, BF16) for w in weights],
        scratch_shapes=[
            pltpu.VMEM((NPH, S5_LB, n_chunks), BF16),
            pltpu.VMEM((NPH, S5_LB, n_chunks), F32),
            pltpu.VMEM((gpb, tk, tk), BF16),
            pltpu.VMEM((gpb, 8 * S5_P, tk), BF16),
            pltpu.VMEM((gpb, tk, 8 * S5_P), BF16),
            pltpu.VMEM((gpb, 4 * 16, 2 * S5_P), F32),
        ] + [pltpu.VMEM((2,) + chunk(w), F32) for w in weights]
        + [pltpu.VMEM((2,) + chunk(w), BF16) for w in weights]
        + [pltpu.SemaphoreType.DMA((len(weights), 2, 2))],
        compiler_params=_cparams(1),
        name="s5",
    )(u3, raw, dvec, *weights)
    return outs[0], outs[1:]


def _s5_pack_params(lam_re, lam_im, log_step, b_re, b_im, c_re, c_im, d):
    assert NPH == 16 and S5_K == 16 and 2 * lam_re.shape[-1] == 128
    f = lambda v: v.astype(F32)
    head = jnp.stack([f(lam_re), f(lam_im), jnp.broadcast_to(f(log_step)[..., None], lam_re.shape)], axis=2)
    raw = jnp.concatenate(
        [head, jnp.zeros(lam_re.shape[:2] + (5, lam_re.shape[2]), F32), f(c_re), f(c_im),
         jnp.swapaxes(f(b_re), -1, -2), jnp.swapaxes(f(b_im), -1, -2)], axis=2)
    raw = jnp.concatenate([raw, raw], axis=-1)
    dvec = jnp.tile(f(d), (1, NPH))[:, None, :]
    return raw, dvec


def _s5_discretise(raw_f, raw_b):
    k = S5_K

    def cmul(x, y):
        return x[0] * y[0] - x[1] * y[1], x[0] * y[1] + x[1] * y[0]

    def rows(zs):
        return jnp.concatenate([z[0] for z in zs], axis=0), jnp.concatenate([z[1] for z in zs], axis=0)

    def one_direction(raw):
        lre, lim = jnp.minimum(raw[0:1], -1e-4), raw[1:2]
        step = jnp.exp(raw[2:3])
        are, aim = lre * step, lim * step
        mag = jnp.exp(are)
        lam_bar = (mag * jnp.cos(aim), mag * jnp.sin(aim))
        powers = [(jnp.ones_like(are), jnp.zeros_like(are))]
        for _ in range(NPH):
            powers.append(cmul(powers[-1], lam_bar))
        squares = [powers[NPH]]
        for _ in range(15):
            squares.append(cmul(squares[-1], squares[-1]))
        nr, ni = lam_bar[0] - 1.0, lam_bar[1]
        den = lre * lre + lim * lim
        z = ((nr * lre + ni * lim) / den, (ni * lre - nr * lim) / den)
        bbar = cmul(z, (raw[8 + 2 * k:8 + 3 * k], raw[8 + 3 * k:8 + 4 * k]))
        c = (raw[8:8 + k], raw[8 + k:8 + 2 * k])
        return powers, squares, bbar, c

    pf, qf, bbar_f, c_f = one_direction(raw_f)
    pb, qb, bbar_b, c_b = one_direction(raw_b)
    par = jnp.concatenate(
        list(rows(pf[NPH - 1::-1])) + list(rows(pb[:NPH])) + list(bbar_f) + list(bbar_b)
        + list(rows(pf[1:])) + list(rows(pb[NPH:0:-1])) + list(c_f) + list(c_b), axis=0)
    q_tab = jnp.concatenate(list(rows(qf)) + list(rows(qb)), axis=0)
    return par, q_tab


def _row_scan_carry(a, h, reverse):
    n, w = a.shape
    row = lax.broadcasted_iota(jnp.int32, (n, w), 0)

    def shift(v, k, fill):
        if k % 8 == 0:
            z = jnp.full((k, w), fill, v.dtype)
            if reverse:
                return jnp.concatenate([v[k:], z], axis=0)
            return jnp.concatenate([z, v[:n - k]], axis=0)
        if reverse:
            return jnp.where(row < n - k, pltpu.roll(v, n - k, axis=0), fill)
        return jnp.where(row >= k, pltpu.roll(v, k, axis=0), fill)

    k = 1
    while k < n:
        h = h + a * shift(h, k, 0.0)
        a = a * shift(a, k, 1.0)
        k *= 2
    return shift(h, 1, 0.0)


def _rglru_kernel(x_ref, cw_ref, cb_ref, wg_ref, bg_ref, lam_ref, y_ref, xe_ref, al_ref, hl_ref):
    n, w = x_ref.shape[1], x_ref.shape[2]
    row = lax.broadcasted_iota(jnp.int32, (n, w), 0)

    def from_prev_chunk(v):
        return jnp.where(row >= 1, pltpu.roll(v, 1, axis=0), 0.0)

    def from_next_chunk(v):
        return jnp.where(row < n - 1, pltpu.roll(v, n - 1, axis=0), 0.0)

    xe_ref[0] = from_prev_chunk(x_ref[NPH - 2])
    xe_ref[1] = from_prev_chunk(x_ref[NPH - 1])

    def copy_body(s, _):
        xe_ref[s + 2] = x_ref[s]
        return 0
    lax.fori_loop(0, NPH, copy_body, 0)
    xe_ref[NPH + 2] = from_next_chunk(x_ref[0])

    cw = cw_ref[...]
    cb = cb_ref[...]
    lam = lam_ref[...]
    neg = -lam
    softplus = jnp.maximum(neg, 0.0) + jnp.log(1.0 + jnp.exp(-jnp.abs(neg)))

    for d, reverse in ((0, False), (1, True)):
        rate = (-RG_C) * softplus[d:d + 1]
        wg = wg_ref[0][:, 2 * w * d:2 * w * (d + 1)]
        bg = bg_ref[0][:, 2 * w * d:2 * w * (d + 1)]
        init = NPH if reverse else 0
        al_ref[init] = jnp.ones((n, w), F32)
        hl_ref[init] = jnp.zeros((n, w), F32)

        def local_body(i, _):
            s = (NPH - 1 - i) if reverse else i
            xc = (cw[0:1] * xe_ref[s] + cw[1:2] * xe_ref[s + 1]
                  + cw[2:3] * xe_ref[s + 2] + cw[3:4] * xe_ref[s + 3] + cb)
            g = jnp.dot(xc.astype(BF16), wg, preferred_element_type=F32) + bg
            r = _sigmoid(g[:, :w])
            ig = _sigmoid(g[:, w:])
            a = jnp.exp(rate * r)
            bt = jnp.sqrt(1.0 - a * a) * (ig * xc)
            src = (s + 1) if reverse else s
            dst = s if reverse else (s + 1)
            hl_ref[dst] = a * hl_ref[src] + bt
            al_ref[dst] = a * al_ref[src]
            return 0
        lax.fori_loop(0, NPH, local_body, 0)

        last = 0 if reverse else NPH
        carry = _row_scan_carry(al_ref[last], hl_ref[last], reverse)

        def fix_body(s, _):
            slot = s if reverse else (s + 1)
            v = hl_ref[slot] + al_ref[slot] * carry
            if reverse:
                y_ref[s] = y_ref[s] + v
            else:
                y_ref[s] = v
            return 0
        lax.fori_loop(0, NPH, fix_body, 0)


def _rglru(xrg3, conv_w, conv_b, wg, bg, lam):
    _, n_chunks, d_rg = xrg3.shape
    nb = d_rg // RG_CB
    return pl.pallas_call(
        _rglru_kernel,
        grid=(nb,),
        in_specs=[
            pl.BlockSpec((NPH, n_chunks, RG_CB), lambda j: (0, 0, j)),
            pl.BlockSpec((CONV_W, RG_CB), lambda j: (0, j)),
            pl.BlockSpec((1, RG_CB), lambda j: (0, j)),
            pl.BlockSpec((1, RG_CB, 4 * RG_CB), lambda j: (j, 0, 0)),
            pl.BlockSpec((1, 1, 4 * RG_CB), lambda j: (j, 0, 0)),
            pl.BlockSpec((2, RG_CB), lambda j: (0, j)),
        ],
        out_specs=pl.BlockSpec((NPH, n_chunks, RG_CB), lambda j: (0, 0, j)),
        out_shape=jax.ShapeDtypeStruct((NPH, n_chunks, d_rg), F32),
        scratch_shapes=[
            pltpu.VMEM((NPH + 3, n_chunks, RG_CB), F32),
            pltpu.VMEM((NPH + 1, n_chunks, RG_CB), F32),
            pltpu.VMEM((NPH + 1, n_chunks, RG_CB), F32),
        ],
        compiler_params=_cparams(1),
        name="rglru",
    )(xrg3, conv_w, conv_b, wg, bg, lam)


def _rglru_gate_weights(wa, ba, wx, bx):
    n_dir, heads, hd, _ = wa.shape
    per = RG_CB // hd
    nb = heads // per
    eye = jnp.eye(per, dtype=F32)

    def blockdiag(w):
        w = w.astype(F32).reshape(nb, per, hd, hd)
        return jnp.einsum('bpij,pq->bpiqj', w, eye).reshape(nb, RG_CB, RG_CB)

    cols, bias = [], []
    for d in range(n_dir):
        for w, b in ((wa, ba), (wx, bx)):
            cols.append(blockdiag(w[d]))
            bias.append(b[d].astype(F32).reshape(nb, 1, RG_CB))
    return jnp.concatenate(cols, axis=-1).astype(BF16), jnp.concatenate(bias, axis=-1)


def _rms_norm(x, g):
    return x * lax.rsqrt(jnp.mean(x * x, axis=-1, keepdims=True) + LN_EPS) * g


def _mixout_kernel(alpha, ys5_ref, yrg_ref, z_ref, h_ref, gw_ref, gb_ref, gn5_ref, gnr_ref,
                   w_ref, b_ref, g_ref, be_ref, o_ref):
    rows = NPH * CT
    d_s5 = ys5_ref.shape[-1]
    y = ys5_ref[...].reshape(rows, d_s5)
    glu = y * _sigmoid(jnp.dot(y.astype(BF16), gw_ref[...], preferred_element_type=F32) + gb_ref[...])
    ys5n = _rms_norm(glu, gn5_ref[...]).astype(BF16)
    yrg = yrg_ref[...].reshape(rows, -1) * _gelu(z_ref[...].reshape(rows, -1))
    yrgn = _rms_norm(yrg, gnr_ref[...]).astype(BF16)
    mix = (jnp.dot(ys5n, w_ref[:d_s5, :], preferred_element_type=F32)
           + jnp.dot(yrgn, w_ref[d_s5:, :], preferred_element_type=F32) + b_ref[...])
    h1 = _layer_norm(alpha * h_ref[...].reshape(rows, -1) + mix, g_ref[...], be_ref[...])
    o_ref[...] = h1.reshape(o_ref.shape)


def _mixout(alpha, ys5, yrg, zrg, h, glu_w, glu_b, gn_s5, gn_rg, w_out, b_out, ln_g, ln_b):
    _, n_chunks, d_model = h.shape
    d_s5, d_rg = ys5.shape[-1], yrg.shape[-1]
    return pl.pallas_call(
        functools.partial(_mixout_kernel, alpha),
        grid=(n_chunks // CT,),
        in_specs=[
            _tile3(d_s5), _tile3(d_rg), _tile3(d_rg), _tile3(d_model),
            _resident((d_s5, d_s5)), _resident((1, d_s5)), _resident((1, d_s5)), _resident((1, d_rg)),
            _resident((d_s5 + d_rg, d_model)), _resident((1, d_model)),
            _resident((1, d_model)), _resident((1, d_model)),
        ],
        out_specs=_tile3(d_model),
        out_shape=jax.ShapeDtypeStruct((NPH, n_chunks, d_model), F32),
        compiler_params=_cparams(1),
        name="mixout",
    )(ys5, yrg, zrg, h, glu_w, glu_b, gn_s5, gn_rg, w_out, b_out, ln_g, ln_b)


def _ffn_kernel(alpha, h_ref, p_ref, w1_ref, b1_ref, w2_ref, b2_ref, pw_ref, gw_ref, gb_ref,
                g_ref, be_ref, o_ref, hb_ref, acc_ref):
    j = pl.program_id(1)
    rows = NPH * CT

    @pl.when(j == 0)
    def _():
        hb = h_ref[...].reshape(rows, -1).astype(BF16)
        hb_ref[...] = hb
        gate = _sigmoid(jnp.dot(hb, gw_ref[...], preferred_element_type=F32) + gb_ref[...])
        pe = _to_phase_major(p_ref[...]).astype(BF16)
        acc_ref[...] = gate * jnp.dot(pe, pw_ref[...], preferred_element_type=F32) + b2_ref[...]

    a = jnp.dot(hb_ref[...], w1_ref[...], preferred_element_type=F32) + b1_ref[...]
    a = jnp.maximum(a, 0.0)
    acc_ref[...] += jnp.dot((a * a).astype(BF16), w2_ref[...], preferred_element_type=F32)

    @pl.when(j == pl.num_programs(1) - 1)
    def _():
        out = _layer_norm(alpha * h_ref[...].reshape(rows, -1) + acc_ref[...], g_ref[...], be_ref[...])
        o_ref[...] = _to_time_major(out)


def _ffn(alpha, h1, p2, w1, b1, w2, b2, ple_w, gate_w, gate_b, ln_g, ln_b, ff_tile):
    _, n_chunks, d_model = h1.shape
    d_ff = w1.shape[1]
    ple_dim = ple_w.shape[0]
    rows = NPH * CT
    return pl.pallas_call(
        functools.partial(_ffn_kernel, alpha),
        grid=(n_chunks // CT, d_ff // ff_tile),
        in_specs=[
            _tile3(d_model),
            pl.BlockSpec((rows, ple_dim), lambda i, j: (i, 0)),
            pl.BlockSpec((d_model, ff_tile), lambda i, j: (0, j)),
            pl.BlockSpec((1, ff_tile), lambda i, j: (0, j)),
            pl.BlockSpec((ff_tile, d_model), lambda i, j: (j, 0)),
            _resident((1, d_model)),
            _resident((ple_dim, d_model)), _resident((d_model, d_model)), _resident((1, d_model)),
            _resident((1, d_model)), _resident((1, d_model)),
        ],
        out_specs=pl.BlockSpec((rows, d_model), lambda i, j: (i, 0)),
        out_shape=jax.ShapeDtypeStruct((NPH * n_chunks, d_model), F32),
        scratch_shapes=[pltpu.VMEM((rows, d_model), BF16), pltpu.VMEM((rows, d_model), F32)],
        compiler_params=_cparams(2),
        name="ffn",
    )(h1, p2, w1, b1, w2, b2, ple_w, gate_w, gate_b, ln_g, ln_b)


def kernel(x, p, ln_in_g, ln_in_b, w_in, b_in, s5_lambda_re, s5_lambda_im, s5_log_step, s5_b_re, s5_b_im, s5_c_re, s5_c_im, s5_d, s5_glu_w, s5_glu_b, rg_conv_w, rg_conv_b, rg_wa, rg_ba, rg_wx, rg_bx, rg_lambda, gn_s5, gn_rg, w_out, b_out, ln1_g, ln1_b, w_ff1, b_ff1, w_ff2, b_ff2, ple_w, ple_gate_w, ple_gate_b, ln2_g, ln2_b):
    batch, seq, d_model = x.shape
    depth = w_in.shape[0]
    assert batch == 1 and depth == 1 and seq % (NPH * CT) == 0
    n_chunks = seq // NPH
    d_s5 = s5_glu_w.shape[-1]
    d_rg = rg_conv_w.shape[-1]
    alpha = (2.0 * depth) ** 0.25
    row = lambda v: v.reshape(1, -1).astype(F32)

    h, u3, xrg, zrg = _inproj(x.reshape(seq, d_model), row(ln_in_g), row(ln_in_b),
                              w_in[0].astype(BF16), row(b_in[0]), d_s5, d_rg)

    raw, dvec = _s5_pack_params(s5_lambda_re[0], s5_lambda_im[0], s5_log_step[0], s5_b_re[0], s5_b_im[0],
                                s5_c_re[0], s5_c_im[0], s5_d[0])
    ys5, (w_out_b, gate_w_b, w_ff1_b, w_ff2_b) = _s5(
        u3, raw, dvec, [w_out[0], ple_gate_w[0], w_ff1[0], w_ff2[0]])

    wg, bg = _rglru_gate_weights(rg_wa[0], rg_ba[0], rg_wx[0], rg_bx[0])
    yrg = _rglru(xrg, rg_conv_w[0].astype(F32), row(rg_conv_b[0]), wg, bg, rg_lambda[0].astype(F32))

    h1 = _mixout(alpha, ys5, yrg, zrg, h, s5_glu_w[0].astype(BF16), row(s5_glu_b[0]), row(gn_s5[0]),
                 row(gn_rg[0]), w_out_b, row(b_out[0]), row(ln1_g[0]), row(ln1_b[0]))

    out = _ffn(alpha, h1, p.reshape(seq, p.shape[-1]), w_ff1_b, row(b_ff1[0]),
               w_ff2_b, row(b_ff2[0]), ple_w[0].astype(BF16), gate_w_b,
               row(ple_gate_b[0]), row(ln2_g[0]), row(ln2_b[0]), ff_tile=1024)
    return out.reshape(batch, seq, d_model).astype(x.dtype)
```

```python
import functools
import math

import jax
import jax.numpy as jnp
from jax import lax
from jax.experimental import pallas as pl
from jax.experimental.pallas import tpu as pltpu

F32 = jnp.float32
BF16 = jnp.bfloat16

NPH = 16
CT = 32
S5_K = 16
S5_P = 64
S5_LB = 128
RG_CB = 128
CONV_W = 4
RG_C = 8.0
LN_EPS = 1e-5
VMEM_LIMIT_V7X = 56 * 1024 * 1024


def _cparams(n_axes):
    return pltpu.CompilerParams(
        dimension_semantics=("arbitrary",) * n_axes,
        vmem_limit_bytes=VMEM_LIMIT_V7X)


def _resident(shape):
    return pl.BlockSpec(shape, lambda *_: (0,) * len(shape), pipeline_mode=pl.Buffered(1))


def _tile3(width):
    return pl.BlockSpec((NPH, CT, width), lambda i, *_: (0, i, 0))


def _to_phase_major(x):
    w = x.shape[-1]
    return jnp.swapaxes(x.reshape(CT, NPH, w), 0, 1).reshape(NPH * CT, w)


def _to_time_major(x):
    w = x.shape[-1]
    return jnp.swapaxes(x.reshape(NPH, CT, w), 0, 1).reshape(CT * NPH, w)


def _layer_norm(x, g, b):
    mu = jnp.mean(x, axis=-1, keepdims=True)
    xc = x - mu
    var = jnp.mean(xc * xc, axis=-1, keepdims=True)
    return xc * lax.rsqrt(var + LN_EPS) * g + b


def _gelu(x):
    c = math.sqrt(2.0 / math.pi)
    return 0.5 * x * (1.0 + jnp.tanh(c * (x + 0.044715 * (x * x * x))))


def _sigmoid(x):
    return 1.0 / (1.0 + jnp.exp(-x))


def _inproj_kernel(x_ref, g_ref, b_ref, w_ref, bw_ref, h_ref, u_ref, xrg_ref, zrg_ref):
    d_s5, d_rg = u_ref.shape[-1], xrg_ref.shape[-1]
    h = _layer_norm(_to_phase_major(x_ref[...]), g_ref[...], b_ref[...])
    h_ref[...] = h.reshape(h_ref.shape)
    proj = jnp.dot(h.astype(BF16), w_ref[...], preferred_element_type=F32) + bw_ref[...]
    u_ref[...] = proj[:, :d_s5].astype(BF16).reshape(u_ref.shape)
    xrg_ref[...] = proj[:, d_s5:d_s5 + d_rg].reshape(xrg_ref.shape)
    zrg_ref[...] = proj[:, d_s5 + d_rg:].reshape(zrg_ref.shape)


def _inproj(x2, ln_g, ln_b, w_in, b_in, d_s5, d_rg):
    seq, d_model = x2.shape
    n_chunks = seq // NPH
    return pl.pallas_call(
        _inproj_kernel,
        grid=(n_chunks // CT,),
        in_specs=[
            pl.BlockSpec((CT * NPH, d_model), lambda i: (i, 0)),
            _resident((1, d_model)), _resident((1, d_model)),
            _resident(w_in.shape), _resident(b_in.shape),
        ],
        out_specs=[_tile3(d_model), _tile3(d_s5), _tile3(d_rg), _tile3(d_rg)],
        out_shape=[
            jax.ShapeDtypeStruct((NPH, n_chunks, d_model), F32),
            jax.ShapeDtypeStruct((NPH, n_chunks, d_s5), BF16),
            jax.ShapeDtypeStruct((NPH, n_chunks, d_rg), F32),
            jax.ShapeDtypeStruct((NPH, n_chunks, d_rg), F32),
        ],
        compiler_params=_cparams(1),
        name="inproj",
    )(x2, ln_g, ln_b, w_in, b_in)


def _row_scan_exclusive(sre, sim, qre, qim, reverse):
    n, w = sre.shape
    row = lax.broadcasted_iota(jnp.int32, (n, w), 0)

    def shift(v, k):
        if k % 8 == 0:
            z = jnp.zeros((k, w), v.dtype)
            return jnp.concatenate([v[k:], z] if reverse else [z, v[:n - k]], axis=0)
        if reverse:
            return jnp.where(row < n - k, pltpu.roll(v, n - k, axis=0), 0.0)
        return jnp.where(row >= k, pltpu.roll(v, k, axis=0), 0.0)

    xre, xim = shift(sre, 1), shift(sim, 1)
    k, i = 1, 0
    while k < n:
        pr, pi = qre[i:i + 1], qim[i:i + 1]
        if k % 8 == 0:
            keep = slice(n - k, n) if reverse else slice(0, k)
            dst = slice(0, n - k) if reverse else slice(k, n)
            src = slice(k, n) if reverse else slice(0, n - k)
            sr, si = xre[src], xim[src]
            nre = xre[dst] + pr * sr - pi * si
            nim = xim[dst] + pr * si + pi * sr
            order = (lambda new, old: [new, old]) if reverse else (lambda new, old: [old, new])
            xre = jnp.concatenate(order(nre, xre[keep]), axis=0)
            xim = jnp.concatenate(order(nim, xim[keep]), axis=0)
        else:
            sr, si = shift(xre, k), shift(xim, k)
            xre, xim = xre + pr * sr - pi * si, xim + pr * si + pi * sr
        k *= 2
        i += 1
    return xre, xim


def _dot(a, b):
    return jnp.dot(a.astype(BF16), b.astype(BF16), preferred_element_type=F32)


def _s5_chunk_operators(par, dvec, half):
    tk = NPH * S5_K
    lane = lax.broadcasted_iota(jnp.int32, par.shape, 1)
    par = jnp.where(lax.shift_right_logical(lane, 6) == half, par, 0.0)
    pa = par[0:128].T
    pb = par[128:256]
    lane16 = lax.broadcasted_iota(jnp.int32, (16, tk), 1)
    row16 = lax.broadcasted_iota(jnp.int32, (16, tk), 0)
    rep = (lax.shift_right_logical(lane16, 4) == row16).astype(F32)
    til = ((lane16 & 15) == row16).astype(F32)

    def cmul(ar, ai, br, bi):
        return ar * br - ai * bi, ar * bi + ai * br

    def w_of(c0):
        return cmul(_dot(pa[:, c0:c0 + 16], rep), _dot(pa[:, c0 + 16:c0 + 32], rep),
                    _dot(pa[:, c0 + 64:c0 + 80], til), _dot(pa[:, c0 + 80:c0 + 96], til))

    wf_re, wf_im = w_of(0)
    wb_re, wb_im = w_of(32)
    bc = jnp.concatenate([wf_re, wf_im, wb_re, wb_im], axis=0)

    def c_of(r0):
        c_re, c_im = pb[r0 + 64:r0 + 80], pb[r0 + 80:r0 + 96]
        blocks = [cmul(c_re, c_im, pb[r0 + t:r0 + t + 1], pb[r0 + 16 + t:r0 + 17 + t]) for t in range(NPH)]
        return (jnp.concatenate([b[0] for b in blocks], axis=0),
                -jnp.concatenate([b[1] for b in blocks], axis=0))

    cc = jnp.concatenate(list(c_of(0)) + list(c_of(32)), axis=1)

    kf = _dot(pb[64:80], wf_re) - _dot(pb[80:96], wf_im)
    kb = _dot(pb[96:112], wb_re) - _dot(pb[112:128], wb_im)
    blocks = []
    for t in range(NPH):
        left = S5_K * (NPH - 1 - t)
        right = S5_K * t
        f = kf if left == 0 else jnp.where(lane16 < tk - left, pltpu.roll(kf, tk - left, axis=1), 0.0)
        b = kb if right == 0 else jnp.where(lane16 >= right, pltpu.roll(kb, right, axis=1), 0.0)
        blocks.append(f + b)
    a = jnp.concatenate(blocks, axis=0)
    ri = lax.broadcasted_iota(jnp.int32, (tk, tk), 0)
    ci = lax.broadcasted_iota(jnp.int32, (tk, tk), 1)
    a = a + jnp.where(ri == ci, dvec, 0.0)
    return a.astype(BF16), bc.astype(BF16), cc.astype(BF16)


def _cast_stream(k, n_steps, srcs, dsts, inbufs, outbufs, sems):
    assert n_steps >= 2
    slot = lax.rem(k, 2)
    n_streams = len(srcs)
    rows = [src.shape[0] // n_steps for src in srcs]

    def read(i, step, sl):
        return pltpu.make_async_copy(srcs[i].at[pl.ds(step * rows[i], rows[i]), :], inbufs[i].at[sl],
                                     sems.at[i, 0, sl])

    def write(i, step, sl):
        return pltpu.make_async_copy(outbufs[i].at[sl], dsts[i].at[pl.ds(step * rows[i], rows[i]), :],
                                     sems.at[i, 1, sl])

    def begin():
        @pl.when(k == 0)
        def _():
            for i in range(n_streams):
                read(i, 0, 0).start()

        @pl.when(k + 1 < n_steps)
        def _():
            for i in range(n_streams):
                read(i, k + 1, 1 - slot).start()

        @pl.when(k >= 2)
        def _():
            for i in range(n_streams):
                write(i, k - 2, slot).wait()

        for i in range(n_streams):
            read(i, k, slot).wait()

    def cast():
        for i in range(n_streams):
            outbufs[i][slot] = inbufs[i][slot].astype(BF16)

    def end():
        for i in range(n_streams):
            write(i, k, slot).start()

        @pl.when(k == n_steps - 1)
        def _():
            for i in range(n_streams):
                write(i, k - 1, 1 - slot).wait()
                write(i, k, slot).wait()

    return begin, cast, end


def _s5_kernel(n_w, n_grid, u_ref, raw_ref, d_ref, *rest):
    w_src, rest = rest[:n_w], rest[n_w:]
    y_ref, rest = rest[0], rest[1:]
    w_dst, rest = rest[:n_w], rest[n_w:]
    xt_ref, yt_ref, a_ref, bc_ref, cc_ref, q_ref, rest = rest[:6] + (rest[6:],)
    w_in_buf, w_out_buf, w_sems = rest[:n_w], rest[n_w:2 * n_w], rest[2 * n_w]
    n = u_ref.shape[1]
    p2 = 2 * S5_P
    tk = NPH * S5_K
    n_groups = S5_LB // S5_K
    n_pairs = n_groups // 2

    def to_rows(s, _):
        xt_ref[s] = u_ref[s].T
        return 0
    lax.fori_loop(0, NPH, to_rows, 0)

    for g in range(n_groups):
        par, q_ref[g] = _s5_discretise(raw_ref[0, g], raw_ref[1, g])
        a_ref[g], bc_ref[g], cc_ref[g] = _s5_chunk_operators(par, d_ref[g], g % 2)

    def pair(gp, _):
        stream_begin, stream_cast, stream_end = _cast_stream(
            pl.program_id(0) * n_pairs + gp, n_grid * n_pairs, w_src, w_dst, w_in_buf, w_out_buf, w_sems)
        stream_begin()
        stream_cast()
        xs, ys = [], []
        for h in range(2):
            g = 2 * gp + h
            rows = pl.ds(pl.multiple_of(g * S5_K, S5_K), S5_K)
            x = xt_ref[:, rows, :].reshape(tk, n)
            xs.append(x)
            ys.append(jnp.dot(a_ref[g], x, preferred_element_type=F32))
        bc = jnp.concatenate([bc_ref[2 * gp], bc_ref[2 * gp + 1]], axis=1)
        s = jnp.dot(bc, jnp.concatenate(xs, axis=0), preferred_element_type=F32)
        st = [s[i * p2:(i + 1) * p2].T for i in range(4)]
        lane = lax.broadcasted_iota(jnp.int32, (4 * 16, p2), 1)
        q = jnp.where(lane < S5_P, q_ref[2 * gp], q_ref[2 * gp + 1])
        hf_re, hf_im = _row_scan_exclusive(st[0], st[1], q[0:16], q[16:32], False)
        hb_re, hb_im = _row_scan_exclusive(st[2], st[3], q[32:48], q[48:64], True)
        hin = jnp.concatenate([hf_re, hf_im, hb_re, hb_im], axis=1).astype(BF16)
        for h in range(2):
            g = 2 * gp + h
            rows = pl.ds(pl.multiple_of(g * S5_K, S5_K), S5_K)
            y = ys[h] + lax.dot_general(cc_ref[g], hin, (((1,), (1,)), ((), ())), preferred_element_type=F32)
            yt_ref[:, rows, :] = _gelu(y).reshape(NPH, S5_K, n)
        stream_end()
        return 0
    lax.fori_loop(0, n_pairs, pair, 0)

    def to_lanes(t, _):
        y_ref[t] = yt_ref[t].T
        return 0
    lax.fori_loop(0, NPH, to_lanes, 0)


def _s5(u3, raw, dvec, weights):
    _, n_chunks, d_s5 = u3.shape
    gpb = S5_LB // S5_K
    tk = NPH * S5_K
    n_grid = d_s5 // S5_LB
    n_steps = n_grid * (gpb // 2)
    chunk = lambda w: (w.shape[0] // n_steps, w.shape[1])
    assert all(w.shape[0] % (16 * n_steps) == 0 for w in weights)
    per_block = lambda n, *tail: pl.BlockSpec((n,) + tail, lambda b: (b,) + (0,) * len(tail))
    hbm = pl.BlockSpec(memory_space=pl.ANY)
    outs = pl.pallas_call(
        functools.partial(_s5_kernel, len(weights), n_grid),
        grid=(n_grid,),
        in_specs=[
            pl.BlockSpec((NPH, n_chunks, S5_LB), lambda b: (0, 0, b)),
            pl.BlockSpec((2, gpb) + raw.shape[2:], lambda b: (0, b, 0, 0)), per_block(gpb, 1, tk),
        ] + [hbm] * len(weights),
        out_specs=[pl.BlockSpec((NPH, n_chunks, S5_LB), lambda b: (0, 0, b))] + [hbm] * len(weights),
        out_shape=[jax.ShapeDtypeStruct((NPH, n_chunks, d_s5), F32)]
        + [jax.ShapeDtypeStruct(w.shape, BF16) for w in weights],
        scratch_shapes=[
            pltpu.VMEM((NPH, S5_LB, n_chunks), BF16),
            pltpu.VMEM((NPH, S5_LB, n_chunks), F32),
            pltpu.VMEM((gpb, tk, tk), BF16),
            pltpu.VMEM((gpb, 8 * S5_P, tk), BF16),
            pltpu.VMEM((gpb, tk, 8 * S5_P), BF16),
            pltpu.VMEM((gpb, 4 * 16, 2 * S5_P), F32),
        ] + [pltpu.VMEM((2,) + chunk(w), F32) for w in weights]
        + [pltpu.VMEM((2,) + chunk(w), BF16) for w in weights]
        + [pltpu.SemaphoreType.DMA((len(weights), 2, 2))],
        compiler_params=_cparams(1),
        name="s5",
    )(u3, raw, dvec, *weights)
    return outs[0], outs[1:]


def _s5_pack_params(lam_re, lam_im, log_step, b_re, b_im, c_re, c_im, d):
    assert NPH == 16 and S5_K == 16 and 2 * lam_re.shape[-1] == 128
    f = lambda v: v.astype(F32)
    head = jnp.stack([f(lam_re), f(lam_im), jnp.broadcast_to(f(log_step)[..., None], lam_re.shape)], axis=2)
    raw = jnp.concatenate(
        [head, jnp.zeros(lam_re.shape[:2] + (5, lam_re.shape[2]), F32), f(c_re), f(c_im),
         jnp.swapaxes(f(b_re), -1, -2), jnp.swapaxes(f(b_im), -1, -2)], axis=2)
    raw = jnp.concatenate([raw, raw], axis=-1)
    dvec = jnp.tile(f(d), (1, NPH))[:, None, :]
    return raw, dvec


def _s5_discretise(raw_f, raw_b):
    k = S5_K

    def cmul(x, y):
        return x[0] * y[0] - x[1] * y[1], x[0] * y[1] + x[1] * y[0]

    def rows(zs):
        return jnp.concatenate([z[0] for z in zs], axis=0), jnp.concatenate([z[1] for z in zs], axis=0)

    def one_direction(raw):
        lre, lim = jnp.minimum(raw[0:1], -1e-4), raw[1:2]
        step = jnp.exp(raw[2:3])
        are, aim = lre * step, lim * step
        mag = jnp.exp(are)
        lam_bar = (mag * jnp.cos(aim), mag * jnp.sin(aim))
        powers = [(jnp.ones_like(are), jnp.zeros_like(are))]
        for _ in range(NPH):
            powers.append(cmul(powers[-1], lam_bar))
        squares = [powers[NPH]]
        for _ in range(15):
            squares.append(cmul(squares[-1], squares[-1]))
        nr, ni = lam_bar[0] - 1.0, lam_bar[1]
        den = lre * lre + lim * lim
        z = ((nr * lre + ni * lim) / den, (ni * lre - nr * lim) / den)
        bbar = cmul(z, (raw[8 + 2 * k:8 + 3 * k], raw[8 + 3 * k:8 + 4 * k]))
        c = (raw[8:8 + k], raw[8 + k:8 + 2 * k])
        return powers, squares, bbar, c

    pf, qf, bbar_f, c_f = one_direction(raw_f)
    pb, qb, bbar_b, c_b = one_direction(raw_b)
    par = jnp.concatenate(
        list(rows(pf[NPH - 1::-1])) + list(rows(pb[:NPH])) + list(bbar_f) + list(bbar_b)
        + list(rows(pf[1:])) + list(rows(pb[NPH:0:-1])) + list(c_f) + list(c_b), axis=0)
    q_tab = jnp.concatenate(list(rows(qf)) + list(rows(qb)), axis=0)
    return par, q_tab


def _row_scan_carry(a, h, reverse):
    n, w = a.shape
    row = lax.broadcasted_iota(jnp.int32, (n, w), 0)

    def shift(v, k, fill):
        if k % 8 == 0:
            z = jnp.full((k, w), fill, v.dtype)
            if reverse:
                return jnp.concatenate([v[k:], z], axis=0)
            return jnp.concatenate([z, v[:n - k]], axis=0)
        if reverse:
            return jnp.where(row < n - k, pltpu.roll(v, n - k, axis=0), fill)
        return jnp.where(row >= k, pltpu.roll(v, k, axis=0), fill)

    k = 1
    while k < n:
        h = h + a * shift(h, k, 0.0)
        a = a * shift(a, k, 1.0)
        k *= 2
    return shift(h, 1, 0.0)


def _rglru_kernel(x_ref, cw_ref, cb_ref, wg_ref, bg_ref, lam_ref, y_ref, xe_ref, al_ref, hl_ref):
    n, w = x_ref.shape[1], x_ref.shape[2]
    row = lax.broadcasted_iota(jnp.int32, (n, w), 0)

    def from_prev_chunk(v):
        return jnp.where(row >= 1, pltpu.roll(v, 1, axis=0), 0.0)

    def from_next_chunk(v):
        return jnp.where(row < n - 1, pltpu.roll(v, n - 1, axis=0), 0.0)

    xe_ref[0] = from_prev_chunk(x_ref[NPH - 2])
    xe_ref[1] = from_prev_chunk(x_ref[NPH - 1])

    def copy_body(s, _):
        xe_ref[s + 2] = x_ref[s]
        return 0
    lax.fori_loop(0, NPH, copy_body, 0)
    xe_ref[NPH + 2] = from_next_chunk(x_ref[0])

    cw = cw_ref[...]
    cb = cb_ref[...]
    lam = lam_ref[...]
    neg = -lam
    softplus = jnp.maximum(neg, 0.0) + jnp.log(1.0 + jnp.exp(-jnp.abs(neg)))

    for d, reverse in ((0, False), (1, True)):
        rate = (-RG_C) * softplus[d:d + 1]
        wg = wg_ref[0][:, 2 * w * d:2 * w * (d + 1)]
        bg = bg_ref[0][:, 2 * w * d:2 * w * (d + 1)]
        init = NPH if reverse else 0
        al_ref[init] = jnp.ones((n, w), F32)
        hl_ref[init] = jnp.zeros((n, w), F32)

        def local_body(i, _):
            s = (NPH - 1 - i) if reverse else i
            xc = (cw[0:1] * xe_ref[s] + cw[1:2] * xe_ref[s + 1]
                  + cw[2:3] * xe_ref[s + 2] + cw[3:4] * xe_ref[s + 3] + cb)
            g = jnp.dot(xc.astype(BF16), wg, preferred_element_type=F32) + bg
            r = _sigmoid(g[:, :w])
            ig = _sigmoid(g[:, w:])
            a = jnp.exp(rate * r)
            bt = jnp.sqrt(1.0 - a * a) * (ig * xc)
            src = (s + 1) if reverse else s
            dst = s if reverse else (s + 1)
            hl_ref[dst] = a * hl_ref[src] + bt
            al_ref[dst] = a * al_ref[src]
            return 0
        lax.fori_loop(0, NPH, local_body, 0)

        last = 0 if reverse else NPH
        carry = _row_scan_carry(al_ref[last], hl_ref[last], reverse)

        def fix_body(s, _):
            slot = s if reverse else (s + 1)
            v = hl_ref[slot] + al_ref[slot] * carry
            if reverse:
                y_ref[s] = y_ref[s] + v
            else:
                y_ref[s] = v
            return 0
        lax.fori_loop(0, NPH, fix_body, 0)


def _rglru(xrg3, conv_w, conv_b, wg, bg, lam):
    _, n_chunks, d_rg = xrg3.shape
    nb = d_rg // RG_CB
    return pl.pallas_call(
        _rglru_kernel,
        grid=(nb,),
        in_specs=[
            pl.BlockSpec((NPH, n_chunks, RG_CB), lambda j: (0, 0, j)),
            pl.BlockSpec((CONV_W, RG_CB), lambda j: (0, j)),
            pl.BlockSpec((1, RG_CB), lambda j: (0, j)),
            pl.BlockSpec((1, RG_CB, 4 * RG_CB), lambda j: (j, 0, 0)),
            pl.BlockSpec((1, 1, 4 * RG_CB), lambda j: (j, 0, 0)),
            pl.BlockSpec((2, RG_CB), lambda j: (0, j)),
        ],
        out_specs=pl.BlockSpec((NPH, n_chunks, RG_CB), lambda j: (0, 0, j)),
        out_shape=jax.ShapeDtypeStruct((NPH, n_chunks, d_rg), F32),
        scratch_shapes=[
            pltpu.VMEM((NPH + 3, n_chunks, RG_CB), F32),
            pltpu.VMEM((NPH + 1, n_chunks, RG_CB), F32),
            pltpu.VMEM((NPH + 1, n_chunks, RG_CB), F32),
        ],
        compiler_params=_cparams(1),
        name="rglru",
    )(xrg3, conv_w, conv_b, wg, bg, lam)


def _rglru_gate_weights(wa, ba, wx, bx):
    n_dir, heads, hd, _ = wa.shape
    per = RG_CB // hd
    nb = heads // per
    eye = jnp.eye(per, dtype=F32)

    def blockdiag(w):
        w = w.astype(F32).reshape(nb, per, hd, hd)
        return jnp.einsum('bpij,pq->bpiqj', w, eye).reshape(nb, RG_CB, RG_CB)

    cols, bias = [], []
    for d in range(n_dir):
        for w, b in ((wa, ba), (wx, bx)):
            cols.append(blockdiag(w[d]))
            bias.append(b[d].astype(F32).reshape(nb, 1, RG_CB))
    return jnp.concatenate(cols, axis=-1).astype(BF16), jnp.concatenate(bias, axis=-1)


def _rms_norm(x, g):
    return x * lax.rsqrt(jnp.mean(x * x, axis=-1, keepdims=True) + LN_EPS) * g


def _mixout_kernel(alpha, ys5_ref, yrg_ref, z_ref, h_ref, gw_ref, gb_ref, gn5_ref, gnr_ref,
                   w_ref, b_ref, g_ref, be_ref, o_ref):
    rows = NPH * CT
    d_s5 = ys5_ref.shape[-1]
    y = ys5_ref[...].reshape(rows, d_s5)
    glu = y * _sigmoid(jnp.dot(y.astype(BF16), gw_ref[...], preferred_element_type=F32) + gb_ref[...])
    ys5n = _rms_norm(glu, gn5_ref[...]).astype(BF16)
    yrg = yrg_ref[...].reshape(rows, -1) * _gelu(z_ref[...].reshape(rows, -1))
    yrgn = _rms_norm(yrg, gnr_ref[...]).astype(BF16)
    mix = (jnp.dot(ys5n, w_ref[:d_s5, :], preferred_element_type=F32)
           + jnp.dot(yrgn, w_ref[d_s5:, :], preferred_element_type=F32) + b_ref[...])
    h1 = _layer_norm(alpha * h_ref[...].reshape(rows, -1) + mix, g_ref[...], be_ref[...])
    o_ref[...] = h1.reshape(o_ref.shape)


def _mixout(alpha, ys5, yrg, zrg, h, glu_w, glu_b, gn_s5, gn_rg, w_out, b_out, ln_g, ln_b):
    _, n_chunks, d_model = h.shape
    d_s5, d_rg = ys5.shape[-1], yrg.shape[-1]
    return pl.pallas_call(
        functools.partial(_mixout_kernel, alpha),
        grid=(n_chunks // CT,),
        in_specs=[
            _tile3(d_s5), _tile3(d_rg), _tile3(d_rg), _tile3(d_model),
            _resident((d_s5, d_s5)), _resident((1, d_s5)), _resident((1, d_s5)), _resident((1, d_rg)),
            _resident((d_s5 + d_rg, d_model)), _resident((1, d_model)),
            _resident((1, d_model)), _resident((1, d_model)),
        ],
        out_specs=_tile3(d_model),
        out_shape=jax.ShapeDtypeStruct((NPH, n_chunks, d_model), F32),
        compiler_params=_cparams(1),
        name="mixout",
    )(ys5, yrg, zrg, h, glu_w, glu_b, gn_s5, gn_rg, w_out, b_out, ln_g, ln_b)


def _ffn_kernel(alpha, h_ref, p_ref, w1_ref, b1_ref, w2_ref, b2_ref, pw_ref, gw_ref, gb_ref,
                g_ref, be_ref, o_ref, hb_ref, acc_ref):
    j = pl.program_id(1)
    rows = NPH * CT

    @pl.when(j == 0)
    def _():
        h = h_ref[...].reshape(rows, -1)
        hb = h.astype(BF16)
        hb_ref[...] = hb
        gate = _sigmoid(jnp.dot(hb, gw_ref[...], preferred_element_type=F32) + gb_ref[...])
        pe = _to_phase_major(p_ref[...]).astype(BF16)
        acc_ref[...] = alpha * h + gate * jnp.dot(pe, pw_ref[...], preferred_element_type=F32) + b2_ref[...]

    a = jnp.dot(hb_ref[...], w1_ref[...], preferred_element_type=F32) + b1_ref[...]
    a = jnp.maximum(a, 0.0)
    acc_ref[...] += jnp.dot((a * a).astype(BF16), w2_ref[...], preferred_element_type=F32)

    @pl.when(j == pl.num_programs(1) - 1)
    def _():
        o_ref[...] = _to_time_major(_layer_norm(acc_ref[...], g_ref[...], be_ref[...]))


def _ffn(alpha, h1, p2, w1, b1, w2, b2, ple_w, gate_w, gate_b, ln_g, ln_b, ff_tile):
    _, n_chunks, d_model = h1.shape
    d_ff = w1.shape[1]
    ple_dim = ple_w.shape[0]
    rows = NPH * CT
    return pl.pallas_call(
        functools.partial(_ffn_kernel, alpha),
        grid=(n_chunks // CT, d_ff // ff_tile),
        in_specs=[
            _tile3(d_model),
            pl.BlockSpec((rows, ple_dim), lambda i, j: (i, 0)),
            pl.BlockSpec((d_model, ff_tile), lambda i, j: (0, j)),
            pl.BlockSpec((1, ff_tile), lambda i, j: (0, j)),
            pl.BlockSpec((ff_tile, d_model), lambda i, j: (j, 0)),
            _resident((1, d_model)),
            _resident((ple_dim, d_model)), _resident((d_model, d_model)), _resident((1, d_model)),
            _resident((1, d_model)), _resident((1, d_model)),
        ],
        out_specs=pl.BlockSpec((rows, d_model), lambda i, j: (i, 0)),
        out_shape=jax.ShapeDtypeStruct((NPH * n_chunks, d_model), F32),
        scratch_shapes=[pltpu.VMEM((rows, d_model), BF16), pltpu.VMEM((rows, d_model), F32)],
        compiler_params=_cparams(2),
        name="ffn",
    )(h1, p2, w1, b1, w2, b2, ple_w, gate_w, gate_b, ln_g, ln_b)


def kernel(x, p, ln_in_g, ln_in_b, w_in, b_in, s5_lambda_re, s5_lambda_im, s5_log_step, s5_b_re, s5_b_im, s5_c_re, s5_c_im, s5_d, s5_glu_w, s5_glu_b, rg_conv_w, rg_conv_b, rg_wa, rg_ba, rg_wx, rg_bx, rg_lambda, gn_s5, gn_rg, w_out, b_out, ln1_g, ln1_b, w_ff1, b_ff1, w_ff2, b_ff2, ple_w, ple_gate_w, ple_gate_b, ln2_g, ln2_b):
    batch, seq, d_model = x.shape
    depth = w_in.shape[0]
    assert batch == 1 and depth == 1 and seq % (NPH * CT) == 0
    n_chunks = seq // NPH
    d_s5 = s5_glu_w.shape[-1]
    d_rg = rg_conv_w.shape[-1]
    alpha = (2.0 * depth) ** 0.25
    row = lambda v: v.reshape(1, -1).astype(F32)

    h, u3, xrg, zrg = _inproj(x.reshape(seq, d_model), row(ln_in_g), row(ln_in_b),
                              w_in[0].astype(BF16), row(b_in[0]), d_s5, d_rg)

    raw, dvec = _s5_pack_params(s5_lambda_re[0], s5_lambda_im[0], s5_log_step[0], s5_b_re[0], s5_b_im[0],
                                s5_c_re[0], s5_c_im[0], s5_d[0])
    ys5, (w_out_b, gate_w_b, w_ff1_b, w_ff2_b) = _s5(
        u3, raw, dvec, [w_out[0], ple_gate_w[0], w_ff1[0], w_ff2[0]])

    wg, bg = _rglru_gate_weights(rg_wa[0], rg_ba[0], rg_wx[0], rg_bx[0])
    yrg = _rglru(xrg, rg_conv_w[0].astype(F32), row(rg_conv_b[0]), wg, bg, rg_lambda[0].astype(F32))

    h1 = _mixout(alpha, ys5, yrg, zrg, h, s5_glu_w[0].astype(BF16), row(s5_glu_b[0]), row(gn_s5[0]),
                 row(gn_rg[0]), w_out_b, row(b_out[0]), row(ln1_g[0]), row(ln1_b[0]))

    out = _ffn(alpha, h1, p.reshape(seq, p.shape[-1]), w_ff1_b, row(b_ff1[0]),
               w_ff2_b, row(b_ff2[0]), ple_w[0].astype(BF16), gate_w_b,
               row(ple_gate_b[0]), row(ln2_g[0]), row(ln2_b[0]), ff_tile=1024)
    return out.reshape(batch, seq, d_model).astype(x.dtype)
```

```python
import functools
import math

import jax
import jax.numpy as jnp
from jax import lax
from jax.experimental import pallas as pl
from jax.experimental.pallas import tpu as pltpu

F32 = jnp.float32
BF16 = jnp.bfloat16

NPH = 16
CT = 32
S5_K = 16
S5_P = 64
S5_LB = 128
RG_CB = 128
CONV_W = 4
RG_C = 8.0
LN_EPS = 1e-5
VMEM_LIMIT_V7X = 56 * 1024 * 1024


def _cparams(n_axes):
    return pltpu.CompilerParams(
        dimension_semantics=("arbitrary",) * n_axes,
        vmem_limit_bytes=VMEM_LIMIT_V7X)


def _resident(shape):
    return pl.BlockSpec(shape, lambda *_: (0,) * len(shape), pipeline_mode=pl.Buffered(1))


def _tile3(width):
    return pl.BlockSpec((NPH, CT, width), lambda i, *_: (0, i, 0))


def _to_phase_major(x):
    w = x.shape[-1]
    return jnp.swapaxes(x.reshape(CT, NPH, w), 0, 1).reshape(NPH * CT, w)


def _to_time_major(x):
    w = x.shape[-1]
    return jnp.swapaxes(x.reshape(NPH, CT, w), 0, 1).reshape(CT * NPH, w)


def _layer_norm(x, g, b):
    mu = jnp.mean(x, axis=-1, keepdims=True)
    xc = x - mu
    var = jnp.mean(xc * xc, axis=-1, keepdims=True)
    return xc * lax.rsqrt(var + LN_EPS) * g + b


def _gelu(x):
    c = math.sqrt(2.0 / math.pi)
    return 0.5 * x * (1.0 + jnp.tanh(c * (x + 0.044715 * (x * x * x))))


def _sigmoid(x):
    return 1.0 / (1.0 + jnp.exp(-x))


def _inproj_kernel(x_ref, g_ref, b_ref, w_ref, bw_ref, h_ref, u_ref, xrg_ref, zrg_ref):
    d_s5, d_rg = u_ref.shape[-1], xrg_ref.shape[-1]
    h = _layer_norm(_to_phase_major(x_ref[...]), g_ref[...], b_ref[...])
    h_ref[...] = h.reshape(h_ref.shape)
    proj = jnp.dot(h.astype(BF16), w_ref[...], preferred_element_type=F32) + bw_ref[...]
    u_ref[...] = proj[:, :d_s5].astype(BF16).reshape(u_ref.shape)
    xrg_ref[...] = proj[:, d_s5:d_s5 + d_rg].reshape(xrg_ref.shape)
    zrg_ref[...] = proj[:, d_s5 + d_rg:].reshape(zrg_ref.shape)


def _inproj(x2, ln_g, ln_b, w_in, b_in, d_s5, d_rg):
    seq, d_model = x2.shape
    n_chunks = seq // NPH
    return pl.pallas_call(
        _inproj_kernel,
        grid=(n_chunks // CT,),
        in_specs=[
            pl.BlockSpec((CT * NPH, d_model), lambda i: (i, 0)),
            _resident((1, d_model)), _resident((1, d_model)),
            _resident(w_in.shape), _resident(b_in.shape),
        ],
        out_specs=[_tile3(d_model), _tile3(d_s5), _tile3(d_rg), _tile3(d_rg)],
        out_shape=[
            jax.ShapeDtypeStruct((NPH, n_chunks, d_model), F32),
            jax.ShapeDtypeStruct((NPH, n_chunks, d_s5), BF16),
            jax.ShapeDtypeStruct((NPH, n_chunks, d_rg), F32),
            jax.ShapeDtypeStruct((NPH, n_chunks, d_rg), F32),
        ],
        compiler_params=_cparams(1),
        name="inproj",
    )(x2, ln_g, ln_b, w_in, b_in)


def _row_scan_exclusive(sre, sim, qre, qim, reverse):
    n, w = sre.shape
    row = lax.broadcasted_iota(jnp.int32, (n, w), 0)

    def shift(v, k):
        if k % 8 == 0:
            z = jnp.zeros((k, w), v.dtype)
            return jnp.concatenate([v[k:], z] if reverse else [z, v[:n - k]], axis=0)
        if reverse:
            return jnp.where(row < n - k, pltpu.roll(v, n - k, axis=0), 0.0)
        return jnp.where(row >= k, pltpu.roll(v, k, axis=0), 0.0)

    xre, xim = shift(sre, 1), shift(sim, 1)
    k, i = 1, 0
    while k < n:
        pr, pi = qre[i:i + 1], qim[i:i + 1]
        if k % 8 == 0:
            keep = slice(n - k, n) if reverse else slice(0, k)
            dst = slice(0, n - k) if reverse else slice(k, n)
            src = slice(k, n) if reverse else slice(0, n - k)
            sr, si = xre[src], xim[src]
            nre = xre[dst] + pr * sr - pi * si
            nim = xim[dst] + pr * si + pi * sr
            order = (lambda new, old: [new, old]) if reverse else (lambda new, old: [old, new])
            xre = jnp.concatenate(order(nre, xre[keep]), axis=0)
            xim = jnp.concatenate(order(nim, xim[keep]), axis=0)
        else:
            sr, si = shift(xre, k), shift(xim, k)
            xre, xim = xre + pr * sr - pi * si, xim + pr * si + pi * sr
        k *= 2
        i += 1
    return xre, xim


def _dot(a, b):
    return jnp.dot(a.astype(BF16), b.astype(BF16), preferred_element_type=F32)


def _s5_chunk_operators(par, dvec, half):
    tk = NPH * S5_K
    lane = lax.broadcasted_iota(jnp.int32, par.shape, 1)
    par = jnp.where(lax.shift_right_logical(lane, 6) == half, par, 0.0)
    pa = par[0:128].T
    pb = par[128:256]
    lane16 = lax.broadcasted_iota(jnp.int32, (16, tk), 1)
    row16 = lax.broadcasted_iota(jnp.int32, (16, tk), 0)
    rep = (lax.shift_right_logical(lane16, 4) == row16).astype(F32)
    til = ((lane16 & 15) == row16).astype(F32)

    def cmul(ar, ai, br, bi):
        return ar * br - ai * bi, ar * bi + ai * br

    def w_of(c0):
        return cmul(_dot(pa[:, c0:c0 + 16], rep), _dot(pa[:, c0 + 16:c0 + 32], rep),
                    _dot(pa[:, c0 + 64:c0 + 80], til), _dot(pa[:, c0 + 80:c0 + 96], til))

    wf_re, wf_im = w_of(0)
    wb_re, wb_im = w_of(32)
    bc = jnp.concatenate([wf_re, wf_im, wb_re, wb_im], axis=0)

    def c_of(r0):
        c_re, c_im = pb[r0 + 64:r0 + 80], pb[r0 + 80:r0 + 96]
        blocks = [cmul(c_re, c_im, pb[r0 + t:r0 + t + 1], pb[r0 + 16 + t:r0 + 17 + t]) for t in range(NPH)]
        return (jnp.concatenate([b[0] for b in blocks], axis=0),
                -jnp.concatenate([b[1] for b in blocks], axis=0))

    cc = jnp.concatenate(list(c_of(0)) + list(c_of(32)), axis=1)

    kf = _dot(pb[64:80], wf_re) - _dot(pb[80:96], wf_im)
    kb = _dot(pb[96:112], wb_re) - _dot(pb[112:128], wb_im)
    blocks = []
    for t in range(NPH):
        left = S5_K * (NPH - 1 - t)
        right = S5_K * t
        f = kf if left == 0 else jnp.where(lane16 < tk - left, pltpu.roll(kf, tk - left, axis=1), 0.0)
        b = kb if right == 0 else jnp.where(lane16 >= right, pltpu.roll(kb, right, axis=1), 0.0)
        blocks.append(f + b)
    a = jnp.concatenate(blocks, axis=0)
    ri = lax.broadcasted_iota(jnp.int32, (tk, tk), 0)
    ci = lax.broadcasted_iota(jnp.int32, (tk, tk), 1)
    a = a + jnp.where(ri == ci, dvec, 0.0)
    return a.astype(BF16), bc.astype(BF16), cc.astype(BF16)


def _cast_stream(k, n_steps, srcs, dsts, inbufs, outbufs, sems):
    assert n_steps >= 2
    slot = lax.rem(k, 2)
    n_streams = len(srcs)
    rows = [src.shape[0] // n_steps for src in srcs]

    def read(i, step, sl):
        return pltpu.make_async_copy(srcs[i].at[pl.ds(step * rows[i], rows[i]), :], inbufs[i].at[sl],
                                     sems.at[i, 0, sl])

    def write(i, step, sl):
        return pltpu.make_async_copy(outbufs[i].at[sl], dsts[i].at[pl.ds(step * rows[i], rows[i]), :],
                                     sems.at[i, 1, sl])

    def begin():
        @pl.when(k == 0)
        def _():
            for i in range(n_streams):
                read(i, 0, 0).start()

        @pl.when(k + 1 < n_steps)
        def _():
            for i in range(n_streams):
                read(i, k + 1, 1 - slot).start()

        @pl.when(k >= 2)
        def _():
            for i in range(n_streams):
                write(i, k - 2, slot).wait()

        for i in range(n_streams):
            read(i, k, slot).wait()

    def cast():
        for i in range(n_streams):
            outbufs[i][slot] = inbufs[i][slot].astype(BF16)

    def end():
        for i in range(n_streams):
            write(i, k, slot).start()

        @pl.when(k == n_steps - 1)
        def _():
            for i in range(n_streams):
                write(i, k - 1, 1 - slot).wait()
                write(i, k, slot).wait()

    return begin, cast, end


def _s5_kernel(n_w, n_grid, u_ref, raw_ref, d_ref, *rest):
    w_src, rest = rest[:n_w], rest[n_w:]
    y_ref, rest = rest[0], rest[1:]
    w_dst, rest = rest[:n_w], rest[n_w:]
    xt_ref, yt_ref, a_ref, bc_ref, cc_ref, q_ref, rest = rest[:6] + (rest[6:],)
    w_in_buf, w_out_buf, w_sems = rest[:n_w], rest[n_w:2 * n_w], rest[2 * n_w]
    n = u_ref.shape[1]
    p2 = 2 * S5_P
    tk = NPH * S5_K
    n_groups = S5_LB // S5_K
    n_pairs = n_groups // 2

    def to_rows(s, _):
        xt_ref[s] = u_ref[s].T
        return 0
    lax.fori_loop(0, NPH, to_rows, 0, unroll=8)

    for g in range(n_groups):
        par, q_ref[g] = _s5_discretise(raw_ref[0, g], raw_ref[1, g])
        a_ref[g], bc_ref[g], cc_ref[g] = _s5_chunk_operators(par, d_ref[g], g % 2)

    def pair(gp, _):
        stream_begin, stream_cast, stream_end = _cast_stream(
            pl.program_id(0) * n_pairs + gp, n_grid * n_pairs, w_src, w_dst, w_in_buf, w_out_buf, w_sems)
        stream_begin()
        stream_cast()
        xs, ys = [], []
        for h in range(2):
            g = 2 * gp + h
            rows = pl.ds(pl.multiple_of(g * S5_K, S5_K), S5_K)
            x = xt_ref[:, rows, :].reshape(tk, n)
            xs.append(x)
            ys.append(jnp.dot(a_ref[g], x, preferred_element_type=F32))
        bc = jnp.concatenate([bc_ref[2 * gp], bc_ref[2 * gp + 1]], axis=1)
        s = jnp.dot(bc, jnp.concatenate(xs, axis=0), preferred_element_type=F32)
        st = [s[i * p2:(i + 1) * p2].T for i in range(4)]
        lane = lax.broadcasted_iota(jnp.int32, (4 * 16, p2), 1)
        q = jnp.where(lane < S5_P, q_ref[2 * gp], q_ref[2 * gp + 1])
        hf_re, hf_im = _row_scan_exclusive(st[0], st[1], q[0:16], q[16:32], False)
        hb_re, hb_im = _row_scan_exclusive(st[2], st[3], q[32:48], q[48:64], True)
        hin = jnp.concatenate([hf_re, hf_im, hb_re, hb_im], axis=1).astype(BF16)
        for h in range(2):
            g = 2 * gp + h
            rows = pl.ds(pl.multiple_of(g * S5_K, S5_K), S5_K)
            y = ys[h] + lax.dot_general(cc_ref[g], hin, (((1,), (1,)), ((), ())), preferred_element_type=F32)
            yt_ref[:, rows, :] = _gelu(y).reshape(NPH, S5_K, n)
        stream_end()
        return 0
    lax.fori_loop(0, n_pairs, pair, 0)

    def to_lanes(t, _):
        y_ref[t] = yt_ref[t].T
        return 0
    lax.fori_loop(0, NPH, to_lanes, 0, unroll=8)


def _s5(u3, raw, dvec, weights):
    _, n_chunks, d_s5 = u3.shape
    gpb = S5_LB // S5_K
    tk = NPH * S5_K
    n_grid = d_s5 // S5_LB
    n_steps = n_grid * (gpb // 2)
    chunk = lambda w: (w.shape[0] // n_steps, w.shape[1])
    assert all(w.shape[0] % (16 * n_steps) == 0 for w in weights)
    per_block = lambda n, *tail: pl.BlockSpec((n,) + tail, lambda b: (b,) + (0,) * len(tail))
    hbm = pl.BlockSpec(memory_space=pl.ANY)
    outs = pl.pallas_call(
        functools.partial(_s5_kernel, len(weights), n_grid),
        grid=(n_grid,),
        in_specs=[
            pl.BlockSpec((NPH, n_chunks, S5_LB), lambda b: (0, 0, b)),
            pl.BlockSpec((2, gpb) + raw.shape[2:], lambda b: (0, b, 0, 0)), per_block(gpb, 1, tk),
        ] + [hbm] * len(weights),
        out_specs=[pl.BlockSpec((NPH, n_chunks, S5_LB), lambda b: (0, 0, b))] + [hbm] * len(weights),
        out_shape=[jax.ShapeDtypeStruct((NPH, n_chunks, d_s5), F32)]
        + [jax.ShapeDtypeStruct(w.shape, BF16) for w in weights],
        scratch_shapes=[
            pltpu.VMEM((NPH, S5_LB, n_chunks), BF16),
            pltpu.VMEM((NPH, S5_LB, n_chunks), F32),
            pltpu.VMEM((gpb, tk, tk), BF16),
            pltpu.VMEM((gpb, 8 * S5_P, tk), BF16),
            pltpu.VMEM((gpb, tk, 8 * S5_P), BF16),
            pltpu.VMEM((gpb, 4 * 16, 2 * S5_P), F32),
        ] + [pltpu.VMEM((2,) + chunk(w), F32) for w in weights]
        + [pltpu.VMEM((2,) + chunk(w), BF16) for w in weights]
        + [pltpu.SemaphoreType.DMA((len(weights), 2, 2))],
        compiler_params=_cparams(1),
        name="s5",
    )(u3, raw, dvec, *weights)
    return outs[0], outs[1:]


def _s5_pack_params(lam_re, lam_im, log_step, b_re, b_im, c_re, c_im, d):
    assert NPH == 16 and S5_K == 16 and 2 * lam_re.shape[-1] == 128
    f = lambda v: v.astype(F32)
    head = jnp.stack([f(lam_re), f(lam_im), jnp.broadcast_to(f(log_step)[..., None], lam_re.shape)], axis=2)
    raw = jnp.concatenate(
        [head, jnp.zeros(lam_re.shape[:2] + (5, lam_re.shape[2]), F32), f(c_re), f(c_im),
         jnp.swapaxes(f(b_re), -1, -2), jnp.swapaxes(f(b_im), -1, -2)], axis=2)
    raw = jnp.concatenate([raw, raw], axis=-1)
    dvec = jnp.tile(f(d), (1, NPH))[:, None, :]
    return raw, dvec


def _s5_discretise(raw_f, raw_b):
    k = S5_K

    def cmul(x, y):
        return x[0] * y[0] - x[1] * y[1], x[0] * y[1] + x[1] * y[0]

    def rows(zs):
        return jnp.concatenate([z[0] for z in zs], axis=0), jnp.concatenate([z[1] for z in zs], axis=0)

    def one_direction(raw):
        lre, lim = jnp.minimum(raw[0:1], -1e-4), raw[1:2]
        step = jnp.exp(raw[2:3])
        are, aim = lre * step, lim * step
        mag = jnp.exp(are)
        lam_bar = (mag * jnp.cos(aim), mag * jnp.sin(aim))
        powers = [(jnp.ones_like(are), jnp.zeros_like(are))]
        for _ in range(NPH):
            powers.append(cmul(powers[-1], lam_bar))
        squares = [powers[NPH]]
        for _ in range(15):
            squares.append(cmul(squares[-1], squares[-1]))
        nr, ni = lam_bar[0] - 1.0, lam_bar[1]
        den = lre * lre + lim * lim
        z = ((nr * lre + ni * lim) / den, (ni * lre - nr * lim) / den)
        bbar = cmul(z, (raw[8 + 2 * k:8 + 3 * k], raw[8 + 3 * k:8 + 4 * k]))
        c = (raw[8:8 + k], raw[8 + k:8 + 2 * k])
        return powers, squares, bbar, c

    pf, qf, bbar_f, c_f = one_direction(raw_f)
    pb, qb, bbar_b, c_b = one_direction(raw_b)
    par = jnp.concatenate(
        list(rows(pf[NPH - 1::-1])) + list(rows(pb[:NPH])) + list(bbar_f) + list(bbar_b)
        + list(rows(pf[1:])) + list(rows(pb[NPH:0:-1])) + list(c_f) + list(c_b), axis=0)
    q_tab = jnp.concatenate(list(rows(qf)) + list(rows(qb)), axis=0)
    return par, q_tab


def _row_scan_carry(a, h, reverse):
    n, w = a.shape
    row = lax.broadcasted_iota(jnp.int32, (n, w), 0)

    def shift(v, k, fill):
        if k % 8 == 0:
            z = jnp.full((k, w), fill, v.dtype)
            if reverse:
                return jnp.concatenate([v[k:], z], axis=0)
            return jnp.concatenate([z, v[:n - k]], axis=0)
        if reverse:
            return jnp.where(row < n - k, pltpu.roll(v, n - k, axis=0), fill)
        return jnp.where(row >= k, pltpu.roll(v, k, axis=0), fill)

    k = 1
    while k < n:
        h = h + a * shift(h, k, 0.0)
        a = a * shift(a, k, 1.0)
        k *= 2
    return shift(h, 1, 0.0)


def _rglru_kernel(x_ref, cw_ref, cb_ref, wg_ref, bg_ref, lam_ref, y_ref, xe_ref, al_ref, hl_ref):
    n, w = x_ref.shape[1], x_ref.shape[2]
    row = lax.broadcasted_iota(jnp.int32, (n, w), 0)

    def from_prev_chunk(v):
        return jnp.where(row >= 1, pltpu.roll(v, 1, axis=0), 0.0)

    def from_next_chunk(v):
        return jnp.where(row < n - 1, pltpu.roll(v, n - 1, axis=0), 0.0)

    xe_ref[0] = from_prev_chunk(x_ref[NPH - 2])
    xe_ref[1] = from_prev_chunk(x_ref[NPH - 1])

    def copy_body(s, _):
        xe_ref[s + 2] = x_ref[s]
        return 0
    lax.fori_loop(0, NPH, copy_body, 0)
    xe_ref[NPH + 2] = from_next_chunk(x_ref[0])

    cw = cw_ref[...]
    cb = cb_ref[...]
    lam = lam_ref[...]
    neg = -lam
    softplus = jnp.maximum(neg, 0.0) + jnp.log(1.0 + jnp.exp(-jnp.abs(neg)))

    for d, reverse in ((0, False), (1, True)):
        rate = (-RG_C) * softplus[d:d + 1]
        wg = wg_ref[0][:, 2 * w * d:2 * w * (d + 1)]
        bg = bg_ref[0][:, 2 * w * d:2 * w * (d + 1)]
        init = NPH if reverse else 0
        al_ref[init] = jnp.ones((n, w), F32)
        hl_ref[init] = jnp.zeros((n, w), F32)

        def local_body(i, _):
            s = (NPH - 1 - i) if reverse else i
            xc = (cw[0:1] * xe_ref[s] + cw[1:2] * xe_ref[s + 1]
                  + cw[2:3] * xe_ref[s + 2] + cw[3:4] * xe_ref[s + 3] + cb)
            g = jnp.dot(xc.astype(BF16), wg, preferred_element_type=F32) + bg
            r = _sigmoid(g[:, :w])
            ig = _sigmoid(g[:, w:])
            a = jnp.exp(rate * r)
            bt = jnp.sqrt(1.0 - a * a) * (ig * xc)
            src = (s + 1) if reverse else s
            dst = s if reverse else (s + 1)
            hl_ref[dst] = a * hl_ref[src] + bt
            al_ref[dst] = a * al_ref[src]
            return 0
        lax.fori_loop(0, NPH, local_body, 0)

        last = 0 if reverse else NPH
        carry = _row_scan_carry(al_ref[last], hl_ref[last], reverse)

        def fix_body(s, _):
            slot = s if reverse else (s + 1)
            v = hl_ref[slot] + al_ref[slot] * carry
            if reverse:
                y_ref[s] = y_ref[s] + v
            else:
                y_ref[s] = v
            return 0
        lax.fori_loop(0, NPH, fix_body, 0)


def _rglru(xrg3, conv_w, conv_b, wg, bg, lam):
    _, n_chunks, d_rg = xrg3.shape
    nb = d_rg // RG_CB
    return pl.pallas_call(
        _rglru_kernel,
        grid=(nb,),
        in_specs=[
            pl.BlockSpec((NPH, n_chunks, RG_CB), lambda j: (0, 0, j)),
            pl.BlockSpec((CONV_W, RG_CB), lambda j: (0, j)),
            pl.BlockSpec((1, RG_CB), lambda j: (0, j)),
            pl.BlockSpec((1, RG_CB, 4 * RG_CB), lambda j: (j, 0, 0)),
            pl.BlockSpec((1, 1, 4 * RG_CB), lambda j: (j, 0, 0)),
            pl.BlockSpec((2, RG_CB), lambda j: (0, j)),
        ],
        out_specs=pl.BlockSpec((NPH, n_chunks, RG_CB), lambda j: (0, 0, j)),
        out_shape=jax.ShapeDtypeStruct((NPH, n_chunks, d_rg), F32),
        scratch_shapes=[
            pltpu.VMEM((NPH + 3, n_chunks, RG_CB), F32),
            pltpu.VMEM((NPH + 1, n_chunks, RG_CB), F32),
            pltpu.VMEM((NPH + 1, n_chunks, RG_CB), F32),
        ],
        compiler_params=_cparams(1),
        name="rglru",
    )(xrg3, conv_w, conv_b, wg, bg, lam)


def _rglru_gate_weights(wa, ba, wx, bx):
    n_dir, heads, hd, _ = wa.shape
    per = RG_CB // hd
    nb = heads // per
    eye = jnp.eye(per, dtype=F32)

    def blockdiag(w):
        w = w.astype(F32).reshape(nb, per, hd, hd)
        return jnp.einsum('bpij,pq->bpiqj', w, eye).reshape(nb, RG_CB, RG_CB)

    cols, bias = [], []
    for d in range(n_dir):
        for w, b in ((wa, ba), (wx, bx)):
            cols.append(blockdiag(w[d]))
            bias.append(b[d].astype(F32).reshape(nb, 1, RG_CB))
    return jnp.concatenate(cols, axis=-1).astype(BF16), jnp.concatenate(bias, axis=-1)


def _rms_norm(x, g):
    return x * lax.rsqrt(jnp.mean(x * x, axis=-1, keepdims=True) + LN_EPS) * g


def _mixout_kernel(alpha, ys5_ref, yrg_ref, z_ref, h_ref, gw_ref, gb_ref, gn5_ref, gnr_ref,
                   w_ref, b_ref, g_ref, be_ref, o_ref):
    rows = NPH * CT
    d_s5, d_model = ys5_ref.shape[-1], o_ref.shape[-1]
    pieces = lambda width: [slice(c, c + 256) for c in range(0, width, 256)]

    y = ys5_ref[...].reshape(rows, d_s5)
    yb = y.astype(BF16)
    glu, sq = [], 0.0
    for cols in pieces(d_s5):
        g = jnp.dot(yb, gw_ref[:, cols], preferred_element_type=F32) + gb_ref[:, cols]
        o = y[:, cols] * _sigmoid(g)
        sq = sq + jnp.sum(o * o, axis=-1, keepdims=True)
        glu.append(o)
    scale = lax.rsqrt(sq * (1.0 / d_s5) + LN_EPS)
    ys5n = jnp.concatenate([o * scale * gn5_ref[:, cols] for o, cols in zip(glu, pieces(d_s5))],
                           axis=1).astype(BF16)

    yrg = yrg_ref[...].reshape(rows, -1) * _gelu(z_ref[...].reshape(rows, -1))
    yrgn = _rms_norm(yrg, gnr_ref[...]).astype(BF16)

    s1, s2 = 0.0, 0.0
    for cols in pieces(d_model):
        v = (jnp.dot(ys5n, w_ref[:d_s5, cols], preferred_element_type=F32)
             + jnp.dot(yrgn, w_ref[d_s5:, cols], preferred_element_type=F32)
             + b_ref[:, cols] + alpha * h_ref[:, :, cols].reshape(rows, 256))
        s1 = s1 + jnp.sum(v, axis=-1, keepdims=True)
        s2 = s2 + jnp.sum(v * v, axis=-1, keepdims=True)
        o_ref[:, :, cols] = v.reshape(NPH, CT, 256)
    mean = s1 * (1.0 / d_model)
    rstd = lax.rsqrt(s2 * (1.0 / d_model) - mean * mean + LN_EPS)
    for cols in pieces(d_model):
        v = o_ref[:, :, cols].reshape(rows, 256)
        o_ref[:, :, cols] = ((v - mean) * rstd * g_ref[:, cols] + be_ref[:, cols]).reshape(NPH, CT, 256)


def _mixout(alpha, ys5, yrg, zrg, h, glu_w, glu_b, gn_s5, gn_rg, w_out, b_out, ln_g, ln_b):
    _, n_chunks, d_model = h.shape
    d_s5, d_rg = ys5.shape[-1], yrg.shape[-1]
    return pl.pallas_call(
        functools.partial(_mixout_kernel, alpha),
        grid=(n_chunks // CT,),
        in_specs=[
            _tile3(d_s5), _tile3(d_rg), _tile3(d_rg), _tile3(d_model),
            _resident((d_s5, d_s5)), _resident((1, d_s5)), _resident((1, d_s5)), _resident((1, d_rg)),
            _resident((d_s5 + d_rg, d_model)), _resident((1, d_model)),
            _resident((1, d_model)), _resident((1, d_model)),
        ],
        out_specs=_tile3(d_model),
        out_shape=jax.ShapeDtypeStruct((NPH, n_chunks, d_model), F32),
        compiler_params=_cparams(1),
        name="mixout",
    )(ys5, yrg, zrg, h, glu_w, glu_b, gn_s5, gn_rg, w_out, b_out, ln_g, ln_b)


def _ffn_kernel(alpha, h_ref, p_ref, w1_ref, b1_ref, w2_ref, b2_ref, pw_ref, gw_ref, gb_ref,
                g_ref, be_ref, o_ref, hb_ref, acc_ref):
    j = pl.program_id(1)
    rows = NPH * CT

    @pl.when(j == 0)
    def _():
        h = h_ref[...].reshape(rows, -1)
        hb = h.astype(BF16)
        hb_ref[...] = hb
        gate = _sigmoid(jnp.dot(hb, gw_ref[...], preferred_element_type=F32) + gb_ref[...])
        pe = _to_phase_major(p_ref[...]).astype(BF16)
        acc_ref[...] = alpha * h + gate * jnp.dot(pe, pw_ref[...], preferred_element_type=F32) + b2_ref[...]

    a = jnp.dot(hb_ref[...], w1_ref[...], preferred_element_type=F32) + b1_ref[...]
    a = jnp.maximum(a, 0.0)
    acc_ref[...] += jnp.dot((a * a).astype(BF16), w2_ref[...], preferred_element_type=F32)

    @pl.when(j == pl.num_programs(1) - 1)
    def _():
        o_ref[...] = _to_time_major(_layer_norm(acc_ref[...], g_ref[...], be_ref[...]))


def _ffn(alpha, h1, p2, w1, b1, w2, b2, ple_w, gate_w, gate_b, ln_g, ln_b, ff_tile):
    _, n_chunks, d_model = h1.shape
    d_ff = w1.shape[1]
    ple_dim = ple_w.shape[0]
    rows = NPH * CT
    return pl.pallas_call(
        functools.partial(_ffn_kernel, alpha),
        grid=(n_chunks // CT, d_ff // ff_tile),
        in_specs=[
            _tile3(d_model),
            pl.BlockSpec((rows, ple_dim), lambda i, j: (i, 0)),
            pl.BlockSpec((d_model, ff_tile), lambda i, j: (0, j)),
            pl.BlockSpec((1, ff_tile), lambda i, j: (0, j)),
            pl.BlockSpec((ff_tile, d_model), lambda i, j: (j, 0)),
            _resident((1, d_model)),
            _resident((ple_dim, d_model)), _resident((d_model, d_model)), _resident((1, d_model)),
            _resident((1, d_model)), _resident((1, d_model)),
        ],
        out_specs=pl.BlockSpec((rows, d_model), lambda i, j: (i, 0)),
        out_shape=jax.ShapeDtypeStruct((NPH * n_chunks, d_model), F32),
        scratch_shapes=[pltpu.VMEM((rows, d_model), BF16), pltpu.VMEM((rows, d_model), F32)],
        compiler_params=_cparams(2),
        name="ffn",
    )(h1, p2, w1, b1, w2, b2, ple_w, gate_w, gate_b, ln_g, ln_b)


def kernel(x, p, ln_in_g, ln_in_b, w_in, b_in, s5_lambda_re, s5_lambda_im, s5_log_step, s5_b_re, s5_b_im, s5_c_re, s5_c_im, s5_d, s5_glu_w, s5_glu_b, rg_conv_w, rg_conv_b, rg_wa, rg_ba, rg_wx, rg_bx, rg_lambda, gn_s5, gn_rg, w_out, b_out, ln1_g, ln1_b, w_ff1, b_ff1, w_ff2, b_ff2, ple_w, ple_gate_w, ple_gate_b, ln2_g, ln2_b):
    batch, seq, d_model = x.shape
    depth = w_in.shape[0]
    assert batch == 1 and depth == 1 and seq % (NPH * CT) == 0
    n_chunks = seq // NPH
    d_s5 = s5_glu_w.shape[-1]
    d_rg = rg_conv_w.shape[-1]
    alpha = (2.0 * depth) ** 0.25
    row = lambda v: v.reshape(1, -1).astype(F32)

    h, u3, xrg, zrg = _inproj(x.reshape(seq, d_model), row(ln_in_g), row(ln_in_b),
                              w_in[0].astype(BF16), row(b_in[0]), d_s5, d_rg)

    raw, dvec = _s5_pack_params(s5_lambda_re[0], s5_lambda_im[0], s5_log_step[0], s5_b_re[0], s5_b_im[0],
                                s5_c_re[0], s5_c_im[0], s5_d[0])
    ys5, (w_out_b, gate_w_b, w_ff1_b, w_ff2_b) = _s5(
        u3, raw, dvec, [w_out[0], ple_gate_w[0], w_ff1[0], w_ff2[0]])

    wg, bg = _rglru_gate_weights(rg_wa[0], rg_ba[0], rg_wx[0], rg_bx[0])
    yrg = _rglru(xrg, rg_conv_w[0].astype(F32), row(rg_conv_b[0]), wg, bg, rg_lambda[0].astype(F32))

    h1 = _mixout(alpha, ys5, yrg, zrg, h, s5_glu_w[0].astype(BF16), row(s5_glu_b[0]), row(gn_s5[0]),
                 row(gn_rg[0]), w_out_b, row(b_out[0]), row(ln1_g[0]), row(ln1_b[0]))

    out = _ffn(alpha, h1, p.reshape(seq, p.shape[-1]), w_ff1_b, row(b_ff1[0]),
               w_ff2_b, row(b_ff2[0]), ple_w[0].astype(BF16), gate_w_b,
               row(ple_gate_b[0]), row(ln2_g[0]), row(ln2_b[0]), ff_tile=1024)
    return out.reshape(batch, seq, d_model).astype(x.dtype)
```

```python
import functools
import math

import jax
import jax.numpy as jnp
from jax import lax
from jax.experimental import pallas as pl
from jax.experimental.pallas import tpu as pltpu

F32 = jnp.float32
BF16 = jnp.bfloat16

NPH = 16
CT = 32
S5_K = 16
S5_P = 64
S5_LB = 128
RG_CB = 128
CONV_W = 4
RG_C = 8.0
LN_EPS = 1e-5
VMEM_LIMIT_V7X = 56 * 1024 * 1024


def _cparams(n_axes):
    return pltpu.CompilerParams(
        dimension_semantics=("arbitrary",) * n_axes,
        vmem_limit_bytes=VMEM_LIMIT_V7X)


def _resident(shape):
    return pl.BlockSpec(shape, lambda *_: (0,) * len(shape), pipeline_mode=pl.Buffered(1))


def _tile3(width):
    return pl.BlockSpec((NPH, CT, width), lambda i, *_: (0, i, 0))


def _to_phase_major(x):
    w = x.shape[-1]
    return jnp.swapaxes(x.reshape(CT, NPH, w), 0, 1).reshape(NPH * CT, w)


def _to_time_major(x):
    w = x.shape[-1]
    return jnp.swapaxes(x.reshape(NPH, CT, w), 0, 1).reshape(CT * NPH, w)


def _layer_norm(x, g, b):
    mu = jnp.mean(x, axis=-1, keepdims=True)
    xc = x - mu
    var = jnp.mean(xc * xc, axis=-1, keepdims=True)
    return xc * lax.rsqrt(var + LN_EPS) * g + b


def _gelu(x):
    c = math.sqrt(2.0 / math.pi)
    return 0.5 * x * (1.0 + jnp.tanh(c * (x + 0.044715 * (x * x * x))))


def _sigmoid(x):
    return 1.0 / (1.0 + jnp.exp(-x))


def _inproj_kernel(x_ref, g_ref, b_ref, w_ref, bw_ref, h_ref, u_ref, xrg_ref, zrg_ref):
    d_s5, d_rg = u_ref.shape[-1], xrg_ref.shape[-1]
    h = _layer_norm(_to_phase_major(x_ref[...]), g_ref[...], b_ref[...])
    h_ref[...] = h.reshape(h_ref.shape)
    proj = jnp.dot(h.astype(BF16), w_ref[...], preferred_element_type=F32) + bw_ref[...]
    u_ref[...] = proj[:, :d_s5].astype(BF16).reshape(u_ref.shape)
    xrg_ref[...] = proj[:, d_s5:d_s5 + d_rg].reshape(xrg_ref.shape)
    zrg_ref[...] = proj[:, d_s5 + d_rg:].reshape(zrg_ref.shape)


def _inproj(x2, ln_g, ln_b, w_in, b_in, d_s5, d_rg):
    seq, d_model = x2.shape
    n_chunks = seq // NPH
    return pl.pallas_call(
        _inproj_kernel,
        grid=(n_chunks // CT,),
        in_specs=[
            pl.BlockSpec((CT * NPH, d_model), lambda i: (i, 0)),
            _resident((1, d_model)), _resident((1, d_model)),
            _resident(w_in.shape), _resident(b_in.shape),
        ],
        out_specs=[_tile3(d_model), _tile3(d_s5), _tile3(d_rg), _tile3(d_rg)],
        out_shape=[
            jax.ShapeDtypeStruct((NPH, n_chunks, d_model), F32),
            jax.ShapeDtypeStruct((NPH, n_chunks, d_s5), BF16),
            jax.ShapeDtypeStruct((NPH, n_chunks, d_rg), F32),
            jax.ShapeDtypeStruct((NPH, n_chunks, d_rg), F32),
        ],
        compiler_params=_cparams(1),
        name="inproj",
    )(x2, ln_g, ln_b, w_in, b_in)


def _row_scan_exclusive(sre, sim, qre, qim, reverse):
    n, w = sre.shape
    row = lax.broadcasted_iota(jnp.int32, (n, w), 0)

    def shift(v, k):
        if k % 8 == 0:
            z = jnp.zeros((k, w), v.dtype)
            return jnp.concatenate([v[k:], z] if reverse else [z, v[:n - k]], axis=0)
        if reverse:
            return jnp.where(row < n - k, pltpu.roll(v, n - k, axis=0), 0.0)
        return jnp.where(row >= k, pltpu.roll(v, k, axis=0), 0.0)

    xre, xim = shift(sre, 1), shift(sim, 1)
    k, i = 1, 0
    while k < n:
        pr, pi = qre[i:i + 1], qim[i:i + 1]
        if k % 8 == 0:
            keep = slice(n - k, n) if reverse else slice(0, k)
            dst = slice(0, n - k) if reverse else slice(k, n)
            src = slice(k, n) if reverse else slice(0, n - k)
            sr, si = xre[src], xim[src]
            nre = xre[dst] + pr * sr - pi * si
            nim = xim[dst] + pr * si + pi * sr
            order = (lambda new, old: [new, old]) if reverse else (lambda new, old: [old, new])
            xre = jnp.concatenate(order(nre, xre[keep]), axis=0)
            xim = jnp.concatenate(order(nim, xim[keep]), axis=0)
        else:
            sr, si = shift(xre, k), shift(xim, k)
            xre, xim = xre + pr * sr - pi * si, xim + pr * si + pi * sr
        k *= 2
        i += 1
    return xre, xim


def _dot(a, b):
    return jnp.dot(a.astype(BF16), b.astype(BF16), preferred_element_type=F32)


def _s5_chunk_operators(par, dvec, half):
    tk = NPH * S5_K
    lane = lax.broadcasted_iota(jnp.int32, par.shape, 1)
    par = jnp.where(lax.shift_right_logical(lane, 6) == half, par, 0.0)
    pa = par[0:128].T
    pb = par[128:256]
    lane16 = lax.broadcasted_iota(jnp.int32, (16, tk), 1)
    row16 = lax.broadcasted_iota(jnp.int32, (16, tk), 0)
    rep = (lax.shift_right_logical(lane16, 4) == row16).astype(F32)
    til = ((lane16 & 15) == row16).astype(F32)

    def cmul(ar, ai, br, bi):
        return ar * br - ai * bi, ar * bi + ai * br

    def w_of(c0):
        return cmul(_dot(pa[:, c0:c0 + 16], rep), _dot(pa[:, c0 + 16:c0 + 32], rep),
                    _dot(pa[:, c0 + 64:c0 + 80], til), _dot(pa[:, c0 + 80:c0 + 96], til))

    wf_re, wf_im = w_of(0)
    wb_re, wb_im = w_of(32)
    bc = jnp.concatenate([wf_re, wf_im, wb_re, wb_im], axis=0)

    def c_of(r0):
        c_re, c_im = pb[r0 + 64:r0 + 80], pb[r0 + 80:r0 + 96]
        blocks = [cmul(c_re, c_im, pb[r0 + t:r0 + t + 1], pb[r0 + 16 + t:r0 + 17 + t]) for t in range(NPH)]
        return (jnp.concatenate([b[0] for b in blocks], axis=0),
                -jnp.concatenate([b[1] for b in blocks], axis=0))

    cc = jnp.concatenate(list(c_of(0)) + list(c_of(32)), axis=1)

    kf = _dot(pb[64:80], wf_re) - _dot(pb[80:96], wf_im)
    kb = _dot(pb[96:112], wb_re) - _dot(pb[112:128], wb_im)
    blocks = []
    for t in range(NPH):
        left = S5_K * (NPH - 1 - t)
        right = S5_K * t
        f = kf if left == 0 else jnp.where(lane16 < tk - left, pltpu.roll(kf, tk - left, axis=1), 0.0)
        b = kb if right == 0 else jnp.where(lane16 >= right, pltpu.roll(kb, right, axis=1), 0.0)
        blocks.append(f + b)
    a = jnp.concatenate(blocks, axis=0)
    ri = lax.broadcasted_iota(jnp.int32, (tk, tk), 0)
    ci = lax.broadcasted_iota(jnp.int32, (tk, tk), 1)
    a = a + jnp.where(ri == ci, dvec, 0.0)
    return a.astype(BF16), bc.astype(BF16), cc.astype(BF16)


def _cast_stream(k, n_steps, srcs, dsts, inbufs, outbufs, sems):
    assert n_steps >= 2
    slot = lax.rem(k, 2)
    n_streams = len(srcs)
    rows = [src.shape[0] // n_steps for src in srcs]

    def read(i, step, sl):
        return pltpu.make_async_copy(srcs[i].at[pl.ds(step * rows[i], rows[i]), :], inbufs[i].at[sl],
                                     sems.at[i, 0, sl])

    def write(i, step, sl):
        return pltpu.make_async_copy(outbufs[i].at[sl], dsts[i].at[pl.ds(step * rows[i], rows[i]), :],
                                     sems.at[i, 1, sl])

    def begin():
        @pl.when(k == 0)
        def _():
            for i in range(n_streams):
                read(i, 0, 0).start()

        @pl.when(k + 1 < n_steps)
        def _():
            for i in range(n_streams):
                read(i, k + 1, 1 - slot).start()

        @pl.when(k >= 2)
        def _():
            for i in range(n_streams):
                write(i, k - 2, slot).wait()

        for i in range(n_streams):
            read(i, k, slot).wait()

    def cast():
        for i in range(n_streams):
            outbufs[i][slot] = inbufs[i][slot].astype(BF16)

    def end():
        for i in range(n_streams):
            write(i, k, slot).start()

        @pl.when(k == n_steps - 1)
        def _():
            for i in range(n_streams):
                write(i, k - 1, 1 - slot).wait()
                write(i, k, slot).wait()

    return begin, cast, end


def _s5_kernel(n_w, n_grid, u_ref, raw_ref, d_ref, *rest):
    w_src, rest = rest[:n_w], rest[n_w:]
    y_ref, rest = rest[0], rest[1:]
    w_dst, rest = rest[:n_w], rest[n_w:]
    xt_ref, yt_ref, a_ref, bc_ref, cc_ref, q_ref, rest = rest[:6] + (rest[6:],)
    w_in_buf, w_out_buf, w_sems = rest[:n_w], rest[n_w:2 * n_w], rest[2 * n_w]
    n = u_ref.shape[1]
    p2 = 2 * S5_P
    tk = NPH * S5_K
    n_groups = S5_LB // S5_K
    n_pairs = n_groups // 2

    def to_rows(s, _):
        xt_ref[s] = u_ref[s].T
        return 0
    lax.fori_loop(0, NPH, to_rows, 0, unroll=8)

    for g in range(n_groups):
        par, q_ref[g] = _s5_discretise(raw_ref[0, g], raw_ref[1, g])
        a_ref[g], bc_ref[g], cc_ref[g] = _s5_chunk_operators(par, d_ref[g], g % 2)

    def pair(gp, _):
        stream_begin, stream_cast, stream_end = _cast_stream(
            pl.program_id(0) * n_pairs + gp, n_grid * n_pairs, w_src, w_dst, w_in_buf, w_out_buf, w_sems)
        stream_begin()
        stream_cast()
        xs, ys = [], []
        for h in range(2):
            g = 2 * gp + h
            rows = pl.ds(pl.multiple_of(g * S5_K, S5_K), S5_K)
            x = xt_ref[:, rows, :].reshape(tk, n)
            xs.append(x)
            ys.append(jnp.dot(a_ref[g], x, preferred_element_type=F32))
        bc = jnp.concatenate([bc_ref[2 * gp], bc_ref[2 * gp + 1]], axis=1)
        s = jnp.dot(bc, jnp.concatenate(xs, axis=0), preferred_element_type=F32)
        st = [s[i * p2:(i + 1) * p2].T for i in range(4)]
        lane = lax.broadcasted_iota(jnp.int32, (4 * 16, p2), 1)
        q = jnp.where(lane < S5_P, q_ref[2 * gp], q_ref[2 * gp + 1])
        hf_re, hf_im = _row_scan_exclusive(st[0], st[1], q[0:16], q[16:32], False)
        hb_re, hb_im = _row_scan_exclusive(st[2], st[3], q[32:48], q[48:64], True)
        hin = jnp.concatenate([hf_re, hf_im, hb_re, hb_im], axis=1).astype(BF16)
        for h in range(2):
            g = 2 * gp + h
            rows = pl.ds(pl.multiple_of(g * S5_K, S5_K), S5_K)
            y = ys[h] + lax.dot_general(cc_ref[g], hin, (((1,), (1,)), ((), ())), preferred_element_type=F32)
            yt_ref[:, rows, :] = _gelu(y).reshape(NPH, S5_K, n)
        stream_end()
        return 0
    lax.fori_loop(0, n_pairs, pair, 0)

    def to_lanes(t, _):
        y_ref[t] = yt_ref[t].T
        return 0
    lax.fori_loop(0, NPH, to_lanes, 0, unroll=8)


def _s5(u3, raw, dvec, weights):
    _, n_chunks, d_s5 = u3.shape
    gpb = S5_LB // S5_K
    tk = NPH * S5_K
    n_grid = d_s5 // S5_LB
    n_steps = n_grid * (gpb // 2)
    chunk = lambda w: (w.shape[0] // n_steps, w.shape[1])
    assert all(w.shape[0] % (16 * n_steps) == 0 for w in weights)
    per_block = lambda n, *tail: pl.BlockSpec((n,) + tail, lambda b: (b,) + (0,) * len(tail))
    hbm = pl.BlockSpec(memory_space=pl.ANY)
    outs = pl.pallas_call(
        functools.partial(_s5_kernel, len(weights), n_grid),
        grid=(n_grid,),
        in_specs=[
            pl.BlockSpec((NPH, n_chunks, S5_LB), lambda b: (0, 0, b)),
            pl.BlockSpec((2, gpb) + raw.shape[2:], lambda b: (0, b, 0, 0)), per_block(gpb, 1, tk),
        ] + [hbm] * len(weights),
        out_specs=[pl.BlockSpec((NPH, n_chunks, S5_LB), lambda b: (0, 0, b))] + [hbm] * len(weights),
        out_shape=[jax.ShapeDtypeStruct((NPH, n_chunks, d_s5), F32)]
        + [jax.ShapeDtypeStruct(w.shape, BF16) for w in weights],
        scratch_shapes=[
            pltpu.VMEM((NPH, S5_LB, n_chunks), BF16),
            pltpu.VMEM((NPH, S5_LB, n_chunks), F32),
            pltpu.VMEM((gpb, tk, tk), BF16),
            pltpu.VMEM((gpb, 8 * S5_P, tk), BF16),
            pltpu.VMEM((gpb, tk, 8 * S5_P), BF16),
            pltpu.VMEM((gpb, 4 * 16, 2 * S5_P), F32),
        ] + [pltpu.VMEM((2,) + chunk(w), F32) for w in weights]
        + [pltpu.VMEM((2,) + chunk(w), BF16) for w in weights]
        + [pltpu.SemaphoreType.DMA((len(weights), 2, 2))],
        compiler_params=_cparams(1),
        name="s5",
    )(u3, raw, dvec, *weights)
    return outs[0], outs[1:]


def _s5_pack_params(lam_re, lam_im, log_step, b_re, b_im, c_re, c_im, d):
    assert NPH == 16 and S5_K == 16 and 2 * lam_re.shape[-1] == 128
    f = lambda v: v.astype(F32)
    head = jnp.stack([f(lam_re), f(lam_im), jnp.broadcast_to(f(log_step)[..., None], lam_re.shape)], axis=2)
    raw = jnp.concatenate(
        [head, jnp.zeros(lam_re.shape[:2] + (5, lam_re.shape[2]), F32), f(c_re), f(c_im),
         jnp.swapaxes(f(b_re), -1, -2), jnp.swapaxes(f(b_im), -1, -2)], axis=2)
    raw = jnp.concatenate([raw, raw], axis=-1)
    dvec = jnp.tile(f(d), (1, NPH))[:, None, :]
    return raw, dvec


def _s5_discretise(raw_f, raw_b):
    k = S5_K

    def cmul(x, y):
        return x[0] * y[0] - x[1] * y[1], x[0] * y[1] + x[1] * y[0]

    def rows(zs):
        return jnp.concatenate([z[0] for z in zs], axis=0), jnp.concatenate([z[1] for z in zs], axis=0)

    def one_direction(raw):
        lre, lim = jnp.minimum(raw[0:1], -1e-4), raw[1:2]
        step = jnp.exp(raw[2:3])
        are, aim = lre * step, lim * step
        mag = jnp.exp(are)
        lam_bar = (mag * jnp.cos(aim), mag * jnp.sin(aim))
        powers = [(jnp.ones_like(are), jnp.zeros_like(are))]
        for _ in range(NPH):
            powers.append(cmul(powers[-1], lam_bar))
        squares = [powers[NPH]]
        for _ in range(15):
            squares.append(cmul(squares[-1], squares[-1]))
        nr, ni = lam_bar[0] - 1.0, lam_bar[1]
        den = lre * lre + lim * lim
        z = ((nr * lre + ni * lim) / den, (ni * lre - nr * lim) / den)
        bbar = cmul(z, (raw[8 + 2 * k:8 + 3 * k], raw[8 + 3 * k:8 + 4 * k]))
        c = (raw[8:8 + k], raw[8 + k:8 + 2 * k])
        return powers, squares, bbar, c

    pf, qf, bbar_f, c_f = one_direction(raw_f)
    pb, qb, bbar_b, c_b = one_direction(raw_b)
    par = jnp.concatenate(
        list(rows(pf[NPH - 1::-1])) + list(rows(pb[:NPH])) + list(bbar_f) + list(bbar_b)
        + list(rows(pf[1:])) + list(rows(pb[NPH:0:-1])) + list(c_f) + list(c_b), axis=0)
    q_tab = jnp.concatenate(list(rows(qf)) + list(rows(qb)), axis=0)
    return par, q_tab


def _row_scan_carry(a, h, reverse):
    n, w = a.shape
    row = lax.broadcasted_iota(jnp.int32, (n, w), 0)

    def shift(v, k, fill):
        if k % 8 == 0:
            z = jnp.full((k, w), fill, v.dtype)
            if reverse:
                return jnp.concatenate([v[k:], z], axis=0)
            return jnp.concatenate([z, v[:n - k]], axis=0)
        if reverse:
            return jnp.where(row < n - k, pltpu.roll(v, n - k, axis=0), fill)
        return jnp.where(row >= k, pltpu.roll(v, k, axis=0), fill)

    k = 1
    while k < n:
        h = h + a * shift(h, k, 0.0)
        a = a * shift(a, k, 1.0)
        k *= 2
    return shift(h, 1, 0.0)


def _rglru_kernel(x_ref, cw_ref, cb_ref, wg_ref, bg_ref, lam_ref, y_ref, xe_ref, al_ref, hl_ref):
    n, w = x_ref.shape[1], x_ref.shape[2]
    row = lax.broadcasted_iota(jnp.int32, (n, w), 0)

    def from_prev_chunk(v):
        return jnp.where(row >= 1, pltpu.roll(v, 1, axis=0), 0.0)

    def from_next_chunk(v):
        return jnp.where(row < n - 1, pltpu.roll(v, n - 1, axis=0), 0.0)

    xe_ref[0] = from_prev_chunk(x_ref[NPH - 2])
    xe_ref[1] = from_prev_chunk(x_ref[NPH - 1])

    def copy_body(s, _):
        xe_ref[s + 2] = x_ref[s]
        return 0
    lax.fori_loop(0, NPH, copy_body, 0, unroll=4)
    xe_ref[NPH + 2] = from_next_chunk(x_ref[0])

    cw = cw_ref[...]
    cb = cb_ref[...]
    lam = lam_ref[...]
    neg = -lam
    softplus = jnp.maximum(neg, 0.0) + jnp.log(1.0 + jnp.exp(-jnp.abs(neg)))

    for d, reverse in ((0, False), (1, True)):
        rate = (-RG_C) * softplus[d:d + 1]
        wg = wg_ref[0][:, 2 * w * d:2 * w * (d + 1)]
        bg = bg_ref[0][:, 2 * w * d:2 * w * (d + 1)]
        init = NPH if reverse else 0
        al_ref[init] = jnp.ones((n, w), F32)
        hl_ref[init] = jnp.zeros((n, w), F32)

        def local_body(i, _):
            s = (NPH - 1 - i) if reverse else i
            xc = (cw[0:1] * xe_ref[s] + cw[1:2] * xe_ref[s + 1]
                  + cw[2:3] * xe_ref[s + 2] + cw[3:4] * xe_ref[s + 3] + cb)
            g = jnp.dot(xc.astype(BF16), wg, preferred_element_type=F32) + bg
            r = _sigmoid(g[:, :w])
            ig = _sigmoid(g[:, w:])
            a = jnp.exp(rate * r)
            bt = jnp.sqrt(1.0 - a * a) * (ig * xc)
            src = (s + 1) if reverse else s
            dst = s if reverse else (s + 1)
            hl_ref[dst] = a * hl_ref[src] + bt
            al_ref[dst] = a * al_ref[src]
            return 0
        lax.fori_loop(0, NPH, local_body, 0, unroll=2)

        last = 0 if reverse else NPH
        carry = _row_scan_carry(al_ref[last], hl_ref[last], reverse)

        def fix_body(s, _):
            slot = s if reverse else (s + 1)
            v = hl_ref[slot] + al_ref[slot] * carry
            if reverse:
                y_ref[s] = y_ref[s] + v
            else:
                y_ref[s] = v
            return 0
        lax.fori_loop(0, NPH, fix_body, 0, unroll=4)


def _rglru(xrg3, conv_w, conv_b, wg, bg, lam):
    _, n_chunks, d_rg = xrg3.shape
    nb = d_rg // RG_CB
    return pl.pallas_call(
        _rglru_kernel,
        grid=(nb,),
        in_specs=[
            pl.BlockSpec((NPH, n_chunks, RG_CB), lambda j: (0, 0, j)),
            pl.BlockSpec((CONV_W, RG_CB), lambda j: (0, j)),
            pl.BlockSpec((1, RG_CB), lambda j: (0, j)),
            pl.BlockSpec((1, RG_CB, 4 * RG_CB), lambda j: (j, 0, 0)),
            pl.BlockSpec((1, 1, 4 * RG_CB), lambda j: (j, 0, 0)),
            pl.BlockSpec((2, RG_CB), lambda j: (0, j)),
        ],
        out_specs=pl.BlockSpec((NPH, n_chunks, RG_CB), lambda j: (0, 0, j)),
        out_shape=jax.ShapeDtypeStruct((NPH, n_chunks, d_rg), F32),
        scratch_shapes=[
            pltpu.VMEM((NPH + 3, n_chunks, RG_CB), F32),
            pltpu.VMEM((NPH + 1, n_chunks, RG_CB), F32),
            pltpu.VMEM((NPH + 1, n_chunks, RG_CB), F32),
        ],
        compiler_params=_cparams(1),
        name="rglru",
    )(xrg3, conv_w, conv_b, wg, bg, lam)


def _rglru_gate_weights(wa, ba, wx, bx):
    n_dir, heads, hd, _ = wa.shape
    per = RG_CB // hd
    nb = heads // per
    eye = jnp.eye(per, dtype=F32)

    def blockdiag(w):
        w = w.astype(F32).reshape(nb, per, hd, hd)
        return jnp.einsum('bpij,pq->bpiqj', w, eye).reshape(nb, RG_CB, RG_CB)

    cols, bias = [], []
    for d in range(n_dir):
        for w, b in ((wa, ba), (wx, bx)):
            cols.append(blockdiag(w[d]))
            bias.append(b[d].astype(F32).reshape(nb, 1, RG_CB))
    return jnp.concatenate(cols, axis=-1).astype(BF16), jnp.concatenate(bias, axis=-1)


def _rms_norm(x, g):
    return x * lax.rsqrt(jnp.mean(x * x, axis=-1, keepdims=True) + LN_EPS) * g


def _mixout_kernel(alpha, ys5_ref, yrg_ref, z_ref, h_ref, gw_ref, gb_ref, gn5_ref, gnr_ref,
                   w_ref, b_ref, g_ref, be_ref, o_ref):
    rows = NPH * CT
    d_s5, d_model = ys5_ref.shape[-1], o_ref.shape[-1]
    pieces = lambda width: [slice(c, c + 256) for c in range(0, width, 256)]

    y = ys5_ref[...].reshape(rows, d_s5)
    yb = y.astype(BF16)
    glu, sq = [], 0.0
    for cols in pieces(d_s5):
        g = jnp.dot(yb, gw_ref[:, cols], preferred_element_type=F32) + gb_ref[:, cols]
        o = y[:, cols] * _sigmoid(g)
        sq = sq + jnp.sum(o * o, axis=-1, keepdims=True)
        glu.append(o)
    scale = lax.rsqrt(sq * (1.0 / d_s5) + LN_EPS)
    ys5n = jnp.concatenate([o * scale * gn5_ref[:, cols] for o, cols in zip(glu, pieces(d_s5))],
                           axis=1).astype(BF16)

    yrg = yrg_ref[...].reshape(rows, -1) * _gelu(z_ref[...].reshape(rows, -1))
    yrgn = _rms_norm(yrg, gnr_ref[...]).astype(BF16)

    s1, s2 = 0.0, 0.0
    for cols in pieces(d_model):
        v = (jnp.dot(ys5n, w_ref[:d_s5, cols], preferred_element_type=F32)
             + jnp.dot(yrgn, w_ref[d_s5:, cols], preferred_element_type=F32)
             + b_ref[:, cols] + alpha * h_ref[:, :, cols].reshape(rows, 256))
        s1 = s1 + jnp.sum(v, axis=-1, keepdims=True)
        s2 = s2 + jnp.sum(v * v, axis=-1, keepdims=True)
        o_ref[:, :, cols] = v.reshape(NPH, CT, 256)
    mean = s1 * (1.0 / d_model)
    rstd = lax.rsqrt(s2 * (1.0 / d_model) - mean * mean + LN_EPS)
    for cols in pieces(d_model):
        v = o_ref[:, :, cols].reshape(rows, 256)
        o_ref[:, :, cols] = ((v - mean) * rstd * g_ref[:, cols] + be_ref[:, cols]).reshape(NPH, CT, 256)


def _mixout(alpha, ys5, yrg, zrg, h, glu_w, glu_b, gn_s5, gn_rg, w_out, b_out, ln_g, ln_b):
    _, n_chunks, d_model = h.shape
    d_s5, d_rg = ys5.shape[-1], yrg.shape[-1]
    return pl.pallas_call(
        functools.partial(_mixout_kernel, alpha),
        grid=(n_chunks // CT,),
        in_specs=[
            _tile3(d_s5), _tile3(d_rg), _tile3(d_rg), _tile3(d_model),
            _resident((d_s5, d_s5)), _resident((1, d_s5)), _resident((1, d_s5)), _resident((1, d_rg)),
            _resident((d_s5 + d_rg, d_model)), _resident((1, d_model)),
            _resident((1, d_model)), _resident((1, d_model)),
        ],
        out_specs=_tile3(d_model),
        out_shape=jax.ShapeDtypeStruct((NPH, n_chunks, d_model), F32),
        compiler_params=_cparams(1),
        name="mixout",
    )(ys5, yrg, zrg, h, glu_w, glu_b, gn_s5, gn_rg, w_out, b_out, ln_g, ln_b)


def _ffn_kernel(alpha, h_ref, p_ref, w1_ref, b1_ref, w2_ref, b2_ref, pw_ref, gw_ref, gb_ref,
                g_ref, be_ref, o_ref, hb_ref, acc_ref):
    j = pl.program_id(1)
    rows = NPH * CT

    @pl.when(j == 0)
    def _():
        h = h_ref[...].reshape(rows, -1)
        hb = h.astype(BF16)
        hb_ref[...] = hb
        gate = _sigmoid(jnp.dot(hb, gw_ref[...], preferred_element_type=F32) + gb_ref[...])
        pe = _to_phase_major(p_ref[...]).astype(BF16)
        acc_ref[...] = alpha * h + gate * jnp.dot(pe, pw_ref[...], preferred_element_type=F32) + b2_ref[...]

    a = jnp.dot(hb_ref[...], w1_ref[...], preferred_element_type=F32) + b1_ref[...]
    a = jnp.maximum(a, 0.0)
    acc_ref[...] += jnp.dot((a * a).astype(BF16), w2_ref[...], preferred_element_type=F32)

    @pl.when(j == pl.num_programs(1) - 1)
    def _():
        o_ref[...] = _to_time_major(_layer_norm(acc_ref[...], g_ref[...], be_ref[...]))


def _ffn(alpha, h1, p2, w1, b1, w2, b2, ple_w, gate_w, gate_b, ln_g, ln_b, ff_tile):
    _, n_chunks, d_model = h1.shape
    d_ff = w1.shape[1]
    ple_dim = ple_w.shape[0]
    rows = NPH * CT
    return pl.pallas_call(
        functools.partial(_ffn_kernel, alpha),
        grid=(n_chunks // CT, d_ff // ff_tile),
        in_specs=[
            _tile3(d_model),
            pl.BlockSpec((rows, ple_dim), lambda i, j: (i, 0)),
            pl.BlockSpec((d_model, ff_tile), lambda i, j: (0, j)),
            pl.BlockSpec((1, ff_tile), lambda i, j: (0, j)),
            pl.BlockSpec((ff_tile, d_model), lambda i, j: (j, 0)),
            _resident((1, d_model)),
            _resident((ple_dim, d_model)), _resident((d_model, d_model)), _resident((1, d_model)),
            _resident((1, d_model)), _resident((1, d_model)),
        ],
        out_specs=pl.BlockSpec((rows, d_model), lambda i, j: (i, 0)),
        out_shape=jax.ShapeDtypeStruct((NPH * n_chunks, d_model), F32),
        scratch_shapes=[pltpu.VMEM((rows, d_model), BF16), pltpu.VMEM((rows, d_model), F32)],
        compiler_params=_cparams(2),
        name="ffn",
    )(h1, p2, w1, b1, w2, b2, ple_w, gate_w, gate_b, ln_g, ln_b)


def kernel(x, p, ln_in_g, ln_in_b, w_in, b_in, s5_lambda_re, s5_lambda_im, s5_log_step, s5_b_re, s5_b_im, s5_c_re, s5_c_im, s5_d, s5_glu_w, s5_glu_b, rg_conv_w, rg_conv_b, rg_wa, rg_ba, rg_wx, rg_bx, rg_lambda, gn_s5, gn_rg, w_out, b_out, ln1_g, ln1_b, w_ff1, b_ff1, w_ff2, b_ff2, ple_w, ple_gate_w, ple_gate_b, ln2_g, ln2_b):
    batch, seq, d_model = x.shape
    depth = w_in.shape[0]
    assert batch == 1 and depth == 1 and seq % (NPH * CT) == 0
    n_chunks = seq // NPH
    d_s5 = s5_glu_w.shape[-1]
    d_rg = rg_conv_w.shape[-1]
    alpha = (2.0 * depth) ** 0.25
    row = lambda v: v.reshape(1, -1).astype(F32)

    h, u3, xrg, zrg = _inproj(x.reshape(seq, d_model), row(ln_in_g), row(ln_in_b),
                              w_in[0].astype(BF16), row(b_in[0]), d_s5, d_rg)

    raw, dvec = _s5_pack_params(s5_lambda_re[0], s5_lambda_im[0], s5_log_step[0], s5_b_re[0], s5_b_im[0],
                                s5_c_re[0], s5_c_im[0], s5_d[0])
    ys5, (w_out_b, gate_w_b, w_ff1_b, w_ff2_b) = _s5(
        u3, raw, dvec, [w_out[0], ple_gate_w[0], w_ff1[0], w_ff2[0]])

    wg, bg = _rglru_gate_weights(rg_wa[0], rg_ba[0], rg_wx[0], rg_bx[0])
    yrg = _rglru(xrg, rg_conv_w[0].astype(F32), row(rg_conv_b[0]), wg, bg, rg_lambda[0].astype(F32))

    h1 = _mixout(alpha, ys5, yrg, zrg, h, s5_glu_w[0].astype(BF16), row(s5_glu_b[0]), row(gn_s5[0]),
                 row(gn_rg[0]), w_out_b, row(b_out[0]), row(ln1_g[0]), row(ln1_b[0]))

    out = _ffn(alpha, h1, p.reshape(seq, p.shape[-1]), w_ff1_b, row(b_ff1[0]),
               w_ff2_b, row(b_ff2[0]), ple_w[0].astype(BF16), gate_w_b,
               row(ple_gate_b[0]), row(ln2_g[0]), row(ln2_b[0]), ff_tile=1024)
    return out.reshape(batch, seq, d_model).astype(x.dtype)
```

```python
import functools
import math

import jax
import jax.numpy as jnp
from jax import lax
from jax.experimental import pallas as pl
from jax.experimental.pallas import tpu as pltpu

F32 = jnp.float32
BF16 = jnp.bfloat16

NPH = 16
CT = 32
S5_K = 16
S5_P = 64
S5_LB = 128
RG_CB = 128
CONV_W = 4
RG_C = 8.0
LN_EPS = 1e-5
VMEM_LIMIT_V7X = 56 * 1024 * 1024


def _cparams(n_axes):
    return pltpu.CompilerParams(
        dimension_semantics=("arbitrary",) * n_axes,
        vmem_limit_bytes=VMEM_LIMIT_V7X)


def _resident(shape):
    return pl.BlockSpec(shape, lambda *_: (0,) * len(shape), pipeline_mode=pl.Buffered(1))


def _tile3(width):
    return pl.BlockSpec((NPH, CT, width), lambda i, *_: (0, i, 0))


def _to_phase_major(x):
    w = x.shape[-1]
    return jnp.swapaxes(x.reshape(CT, NPH, w), 0, 1).reshape(NPH * CT, w)


def _to_time_major(x):
    w = x.shape[-1]
    return jnp.swapaxes(x.reshape(NPH, CT, w), 0, 1).reshape(CT * NPH, w)


def _layer_norm(x, g, b):
    mu = jnp.mean(x, axis=-1, keepdims=True)
    xc = x - mu
    var = jnp.mean(xc * xc, axis=-1, keepdims=True)
    return xc * lax.rsqrt(var + LN_EPS) * g + b


def _gelu(x):
    c = math.sqrt(2.0 / math.pi)
    return 0.5 * x * (1.0 + jnp.tanh(c * (x + 0.044715 * (x * x * x))))


def _sigmoid(x):
    return 1.0 / (1.0 + jnp.exp(-x))


def _inproj_kernel(x_ref, g_ref, b_ref, w_ref, bw_ref, h_ref, u_ref, xrg_ref, zrg_ref):
    d_s5, d_rg = u_ref.shape[-1], xrg_ref.shape[-1]
    h = _layer_norm(_to_phase_major(x_ref[...]), g_ref[...], b_ref[...])
    h_ref[...] = h.reshape(h_ref.shape)
    proj = jnp.dot(h.astype(BF16), w_ref[...], preferred_element_type=F32) + bw_ref[...]
    u_ref[...] = proj[:, :d_s5].astype(BF16).reshape(u_ref.shape)
    xrg_ref[...] = proj[:, d_s5:d_s5 + d_rg].reshape(xrg_ref.shape)
    zrg_ref[...] = proj[:, d_s5 + d_rg:].reshape(zrg_ref.shape)


def _inproj(x2, ln_g, ln_b, w_in, b_in, d_s5, d_rg):
    seq, d_model = x2.shape
    n_chunks = seq // NPH
    return pl.pallas_call(
        _inproj_kernel,
        grid=(n_chunks // CT,),
        in_specs=[
            pl.BlockSpec((CT * NPH, d_model), lambda i: (i, 0)),
            _resident((1, d_model)), _resident((1, d_model)),
            _resident(w_in.shape), _resident(b_in.shape),
        ],
        out_specs=[_tile3(d_model), _tile3(d_s5), _tile3(d_rg), _tile3(d_rg)],
        out_shape=[
            jax.ShapeDtypeStruct((NPH, n_chunks, d_model), F32),
            jax.ShapeDtypeStruct((NPH, n_chunks, d_s5), BF16),
            jax.ShapeDtypeStruct((NPH, n_chunks, d_rg), F32),
            jax.ShapeDtypeStruct((NPH, n_chunks, d_rg), F32),
        ],
        compiler_params=_cparams(1),
        name="inproj",
    )(x2, ln_g, ln_b, w_in, b_in)


def _row_scan_exclusive(sre, sim, qre, qim, reverse):
    n, w = sre.shape
    row = lax.broadcasted_iota(jnp.int32, (n, w), 0)

    def shift(v, k):
        if k % 8 == 0:
            z = jnp.zeros((k, w), v.dtype)
            return jnp.concatenate([v[k:], z] if reverse else [z, v[:n - k]], axis=0)
        if reverse:
            return jnp.where(row < n - k, pltpu.roll(v, n - k, axis=0), 0.0)
        return jnp.where(row >= k, pltpu.roll(v, k, axis=0), 0.0)

    xre, xim = shift(sre, 1), shift(sim, 1)
    k, i = 1, 0
    while k < n:
        pr, pi = qre[i:i + 1], qim[i:i + 1]
        if k % 8 == 0:
            keep = slice(n - k, n) if reverse else slice(0, k)
            dst = slice(0, n - k) if reverse else slice(k, n)
            src = slice(k, n) if reverse else slice(0, n - k)
            sr, si = xre[src], xim[src]
            nre = xre[dst] + pr * sr - pi * si
            nim = xim[dst] + pr * si + pi * sr
            order = (lambda new, old: [new, old]) if reverse else (lambda new, old: [old, new])
            xre = jnp.concatenate(order(nre, xre[keep]), axis=0)
            xim = jnp.concatenate(order(nim, xim[keep]), axis=0)
        else:
            sr, si = shift(xre, k), shift(xim, k)
            xre, xim = xre + pr * sr - pi * si, xim + pr * si + pi * sr
        k *= 2
        i += 1
    return xre, xim


def _dot(a, b):
    return jnp.dot(a.astype(BF16), b.astype(BF16), preferred_element_type=F32)


def _s5_chunk_operators(par, dvec, half):
    tk = NPH * S5_K
    lane = lax.broadcasted_iota(jnp.int32, par.shape, 1)
    par = jnp.where(lax.shift_right_logical(lane, 6) == half, par, 0.0)
    pa = par[0:128].T
    pb = par[128:256]
    lane16 = lax.broadcasted_iota(jnp.int32, (16, tk), 1)
    row16 = lax.broadcasted_iota(jnp.int32, (16, tk), 0)
    rep = (lax.shift_right_logical(lane16, 4) == row16).astype(F32)
    til = ((lane16 & 15) == row16).astype(F32)

    def cmul(ar, ai, br, bi):
        return ar * br - ai * bi, ar * bi + ai * br

    def w_of(c0):
        return cmul(_dot(pa[:, c0:c0 + 16], rep), _dot(pa[:, c0 + 16:c0 + 32], rep),
                    _dot(pa[:, c0 + 64:c0 + 80], til), _dot(pa[:, c0 + 80:c0 + 96], til))

    wf_re, wf_im = w_of(0)
    wb_re, wb_im = w_of(32)
    bc = jnp.concatenate([wf_re, wf_im, wb_re, wb_im], axis=0)

    def c_of(r0):
        c_re, c_im = pb[r0 + 64:r0 + 80], pb[r0 + 80:r0 + 96]
        blocks = [cmul(c_re, c_im, pb[r0 + t:r0 + t + 1], pb[r0 + 16 + t:r0 + 17 + t]) for t in range(NPH)]
        return (jnp.concatenate([b[0] for b in blocks], axis=0),
                -jnp.concatenate([b[1] for b in blocks], axis=0))

    cc = jnp.concatenate(list(c_of(0)) + list(c_of(32)), axis=1)

    kf = _dot(pb[64:80], wf_re) - _dot(pb[80:96], wf_im)
    kb = _dot(pb[96:112], wb_re) - _dot(pb[112:128], wb_im)
    blocks = []
    for t in range(NPH):
        left = S5_K * (NPH - 1 - t)
        right = S5_K * t
        f = kf if left == 0 else jnp.where(lane16 < tk - left, pltpu.roll(kf, tk - left, axis=1), 0.0)
        b = kb if right == 0 else jnp.where(lane16 >= right, pltpu.roll(kb, right, axis=1), 0.0)
        blocks.append(f + b)
    a = jnp.concatenate(blocks, axis=0)
    ri = lax.broadcasted_iota(jnp.int32, (tk, tk), 0)
    ci = lax.broadcasted_iota(jnp.int32, (tk, tk), 1)
    a = a + jnp.where(ri == ci, dvec, 0.0)
    return a.astype(BF16), bc.astype(BF16), cc.astype(BF16)


def _cast_stream(k, n_steps, srcs, dsts, inbufs, outbufs, sems):
    assert n_steps >= 2
    slot = lax.rem(k, 2)
    n_streams = len(srcs)
    rows = [src.shape[0] // n_steps for src in srcs]

    def read(i, step, sl):
        return pltpu.make_async_copy(srcs[i].at[pl.ds(step * rows[i], rows[i]), :], inbufs[i].at[sl],
                                     sems.at[i, 0, sl])

    def write(i, step, sl):
        return pltpu.make_async_copy(outbufs[i].at[sl], dsts[i].at[pl.ds(step * rows[i], rows[i]), :],
                                     sems.at[i, 1, sl])

    def begin():
        @pl.when(k == 0)
        def _():
            for i in range(n_streams):
                read(i, 0, 0).start()

        @pl.when(k + 1 < n_steps)
        def _():
            for i in range(n_streams):
                read(i, k + 1, 1 - slot).start()

        @pl.when(k >= 2)
        def _():
            for i in range(n_streams):
                write(i, k - 2, slot).wait()

        for i in range(n_streams):
            read(i, k, slot).wait()

    def cast():
        for i in range(n_streams):
            outbufs[i][slot] = inbufs[i][slot].astype(BF16)

    def end():
        for i in range(n_streams):
            write(i, k, slot).start()

        @pl.when(k == n_steps - 1)
        def _():
            for i in range(n_streams):
                write(i, k - 1, 1 - slot).wait()
                write(i, k, slot).wait()

    return begin, cast, end


def _s5_kernel(n_w, n_grid, u_ref, *rest):
    raw_refs, d_ref, rest = rest[:7], rest[7], rest[8:]
    w_src, rest = rest[:n_w], rest[n_w:]
    y_ref, rest = rest[0], rest[1:]
    w_dst, rest = rest[:n_w], rest[n_w:]
    xt_ref, yt_ref, a_ref, bc_ref, cc_ref, q_ref, rest = rest[:6] + (rest[6:],)
    w_in_buf, w_out_buf, w_sems = rest[:n_w], rest[n_w:2 * n_w], rest[2 * n_w]
    n = u_ref.shape[1]
    p2 = 2 * S5_P
    tk = NPH * S5_K
    n_groups = S5_LB // S5_K
    n_pairs = n_groups // 2

    def to_rows(s, _):
        xt_ref[s] = u_ref[s].T
        return 0
    lax.fori_loop(0, NPH, to_rows, 0, unroll=8)

    for g in range(n_groups):
        par, q_ref[g] = _s5_discretise(g, *raw_refs)
        a_ref[g], bc_ref[g], cc_ref[g] = _s5_chunk_operators(par, d_ref[g], g % 2)

    def pair(gp, _):
        stream_begin, stream_cast, stream_end = _cast_stream(
            pl.program_id(0) * n_pairs + gp, n_grid * n_pairs, w_src, w_dst, w_in_buf, w_out_buf, w_sems)
        stream_begin()
        stream_cast()
        xs, ys = [], []
        for h in range(2):
            g = 2 * gp + h
            rows = pl.ds(pl.multiple_of(g * S5_K, S5_K), S5_K)
            x = xt_ref[:, rows, :].reshape(tk, n)
            xs.append(x)
            ys.append(jnp.dot(a_ref[g], x, preferred_element_type=F32))
        bc = jnp.concatenate([bc_ref[2 * gp], bc_ref[2 * gp + 1]], axis=1)
        s = jnp.dot(bc, jnp.concatenate(xs, axis=0), preferred_element_type=F32)
        st = [s[i * p2:(i + 1) * p2].T for i in range(4)]
        lane = lax.broadcasted_iota(jnp.int32, (4 * 16, p2), 1)
        q = jnp.where(lane < S5_P, q_ref[2 * gp], q_ref[2 * gp + 1])
        hf_re, hf_im = _row_scan_exclusive(st[0], st[1], q[0:16], q[16:32], False)
        hb_re, hb_im = _row_scan_exclusive(st[2], st[3], q[32:48], q[48:64], True)
        hin = jnp.concatenate([hf_re, hf_im, hb_re, hb_im], axis=1).astype(BF16)
        for h in range(2):
            g = 2 * gp + h
            rows = pl.ds(pl.multiple_of(g * S5_K, S5_K), S5_K)
            y = ys[h] + lax.dot_general(cc_ref[g], hin, (((1,), (1,)), ((), ())), preferred_element_type=F32)
            yt_ref[:, rows, :] = _gelu(y).reshape(NPH, S5_K, n)
        stream_end()
        return 0
    lax.fori_loop(0, n_pairs, pair, 0)

    def to_lanes(t, _):
        y_ref[t] = yt_ref[t].T
        return 0
    lax.fori_loop(0, NPH, to_lanes, 0, unroll=8)


def _s5(u3, lam_re, lam_im, log_step, b_re, b_im, c_re, c_im, d, weights):
    assert NPH == 16 and S5_K == 16 and 2 * lam_re.shape[-1] == 128
    _, n_chunks, d_s5 = u3.shape
    raw = [lam_re, lam_im, log_step[..., None], b_re, b_im, c_re, c_im]
    dvec = jnp.tile(d.astype(F32), (1, NPH))[:, None, :]
    gpb = S5_LB // S5_K
    tk = NPH * S5_K
    n_grid = d_s5 // S5_LB
    n_steps = n_grid * (gpb // 2)
    chunk = lambda w: (w.shape[0] // n_steps, w.shape[1])
    assert all(w.shape[0] % (16 * n_steps) == 0 for w in weights)
    per_block = lambda n, *tail: pl.BlockSpec((n,) + tail, lambda b: (b,) + (0,) * len(tail))
    per_dir_block = lambda v: pl.BlockSpec((2, gpb) + v.shape[2:], lambda b: (0, b) + (0,) * (v.ndim - 2))
    hbm = pl.BlockSpec(memory_space=pl.ANY)
    outs = pl.pallas_call(
        functools.partial(_s5_kernel, len(weights), n_grid),
        grid=(n_grid,),
        in_specs=[pl.BlockSpec((NPH, n_chunks, S5_LB), lambda b: (0, 0, b))]
        + [per_dir_block(v) for v in raw] + [per_block(gpb, 1, tk)] + [hbm] * len(weights),
        out_specs=[pl.BlockSpec((NPH, n_chunks, S5_LB), lambda b: (0, 0, b))] + [hbm] * len(weights),
        out_shape=[jax.ShapeDtypeStruct((NPH, n_chunks, d_s5), F32)]
        + [jax.ShapeDtypeStruct(w.shape, BF16) for w in weights],
        scratch_shapes=[
            pltpu.VMEM((NPH, S5_LB, n_chunks), BF16),
            pltpu.VMEM((NPH, S5_LB, n_chunks), F32),
            pltpu.VMEM((gpb, tk, tk), BF16),
            pltpu.VMEM((gpb, 8 * S5_P, tk), BF16),
            pltpu.VMEM((gpb, tk, 8 * S5_P), BF16),
            pltpu.VMEM((gpb, 4 * 16, 2 * S5_P), F32),
        ] + [pltpu.VMEM((2,) + chunk(w), F32) for w in weights]
        + [pltpu.VMEM((2,) + chunk(w), BF16) for w in weights]
        + [pltpu.SemaphoreType.DMA((len(weights), 2, 2))],
        compiler_params=_cparams(1),
        name="s5",
    )(u3, *raw, dvec, *weights)
    return outs[0], outs[1:]


def _s5_discretise(g, lam_re_ref, lam_im_ref, log_step_ref, b_re_ref, b_im_ref, c_re_ref, c_im_ref):
    def cmul(x, y):
        return x[0] * y[0] - x[1] * y[1], x[0] * y[1] + x[1] * y[0]

    def rows(zs):
        return jnp.concatenate([z[0] for z in zs], axis=0), jnp.concatenate([z[1] for z in zs], axis=0)

    def one_direction(d):
        lre, lim = jnp.minimum(lam_re_ref[d, g:g + 1, :], -1e-4), lam_im_ref[d, g:g + 1, :]
        step = jnp.exp(log_step_ref[d, g:g + 1, :])
        are, aim = lre * step, lim * step
        mag = jnp.exp(are)
        lam_bar = (mag * jnp.cos(aim), mag * jnp.sin(aim))
        powers = [(jnp.ones_like(are), jnp.zeros_like(are))]
        for _ in range(NPH):
            powers.append(cmul(powers[-1], lam_bar))
        squares = [powers[NPH]]
        for _ in range(15):
            squares.append(cmul(squares[-1], squares[-1]))
        nr, ni = lam_bar[0] - 1.0, lam_bar[1]
        den = lre * lre + lim * lim
        z = ((nr * lre + ni * lim) / den, (ni * lre - nr * lim) / den)
        bbar = cmul(z, (b_re_ref[d, g].T, b_im_ref[d, g].T))
        c = (c_re_ref[d, g], c_im_ref[d, g])
        return powers, squares, bbar, c

    pf, qf, bbar_f, c_f = one_direction(0)
    pb, qb, bbar_b, c_b = one_direction(1)
    par = jnp.concatenate(
        list(rows(pf[NPH - 1::-1])) + list(rows(pb[:NPH])) + list(bbar_f) + list(bbar_b)
        + list(rows(pf[1:])) + list(rows(pb[NPH:0:-1])) + list(c_f) + list(c_b), axis=0)
    q_tab = jnp.concatenate(list(rows(qf)) + list(rows(qb)), axis=0)
    both_halves = lambda v: jnp.concatenate([v, v], axis=1)
    return both_halves(par), both_halves(q_tab)


def _row_scan_carry(a, h, reverse):
    n, w = a.shape
    row = lax.broadcasted_iota(jnp.int32, (n, w), 0)

    def shift(v, k, fill):
        if k % 8 == 0:
            z = jnp.full((k, w), fill, v.dtype)
            if reverse:
                return jnp.concatenate([v[k:], z], axis=0)
            return jnp.concatenate([z, v[:n - k]], axis=0)
        if reverse:
            return jnp.where(row < n - k, pltpu.roll(v, n - k, axis=0), fill)
        return jnp.where(row >= k, pltpu.roll(v, k, axis=0), fill)

    k = 1
    while k < n:
        h = h + a * shift(h, k, 0.0)
        a = a * shift(a, k, 1.0)
        k *= 2
    return shift(h, 1, 0.0)


def _rglru_kernel(x_ref, cw_ref, cb_ref, wa_ref, wx_ref, ba_ref, bx_ref, lam_ref, y_ref, xe_ref, al_ref, hl_ref):
    n, w = x_ref.shape[1], x_ref.shape[2]

    def block_diag(heads):
        per, hd, _ = heads.shape
        zeros = jnp.zeros((hd, hd), heads.dtype)
        return jnp.concatenate(
            [jnp.concatenate([heads[p] if q == p else zeros for q in range(per)], axis=1) for p in range(per)],
            axis=0)
    row = lax.broadcasted_iota(jnp.int32, (n, w), 0)

    def from_prev_chunk(v):
        return jnp.where(row >= 1, pltpu.roll(v, 1, axis=0), 0.0)

    def from_next_chunk(v):
        return jnp.where(row < n - 1, pltpu.roll(v, n - 1, axis=0), 0.0)

    xe_ref[0] = from_prev_chunk(x_ref[NPH - 2])
    xe_ref[1] = from_prev_chunk(x_ref[NPH - 1])

    def copy_body(s, _):
        xe_ref[s + 2] = x_ref[s]
        return 0
    lax.fori_loop(0, NPH, copy_body, 0, unroll=4)
    xe_ref[NPH + 2] = from_next_chunk(x_ref[0])

    cw = cw_ref[...]
    cb = cb_ref[...]
    lam = lam_ref[...]
    neg = -lam
    softplus = jnp.maximum(neg, 0.0) + jnp.log(1.0 + jnp.exp(-jnp.abs(neg)))

    for d, reverse in ((0, False), (1, True)):
        rate = (-RG_C) * softplus[d:d + 1]
        wg = jnp.concatenate([block_diag(wa_ref[d]), block_diag(wx_ref[d])], axis=1).astype(BF16)
        bg = jnp.concatenate([ba_ref[d, 0], bx_ref[d, 0]], axis=1)
        init = NPH if reverse else 0
        al_ref[init] = jnp.ones((n, w), F32)
        hl_ref[init] = jnp.zeros((n, w), F32)

        def local_body(i, _):
            s = (NPH - 1 - i) if reverse else i
            xc = (cw[0:1] * xe_ref[s] + cw[1:2] * xe_ref[s + 1]
                  + cw[2:3] * xe_ref[s + 2] + cw[3:4] * xe_ref[s + 3] + cb)
            g = jnp.dot(xc.astype(BF16), wg, preferred_element_type=F32) + bg
            r = _sigmoid(g[:, :w])
            ig = _sigmoid(g[:, w:])
            a = jnp.exp(rate * r)
            bt = jnp.sqrt(1.0 - a * a) * (ig * xc)
            src = (s + 1) if reverse else s
            dst = s if reverse else (s + 1)
            hl_ref[dst] = a * hl_ref[src] + bt
            al_ref[dst] = a * al_ref[src]
            return 0
        lax.fori_loop(0, NPH, local_body, 0, unroll=2)

        last = 0 if reverse else NPH
        carry = _row_scan_carry(al_ref[last], hl_ref[last], reverse)

        def fix_body(s, _):
            slot = s if reverse else (s + 1)
            v = hl_ref[slot] + al_ref[slot] * carry
            if reverse:
                y_ref[s] = y_ref[s] + v
            else:
                y_ref[s] = v
            return 0
        lax.fori_loop(0, NPH, fix_body, 0, unroll=4)


def _rglru(xrg3, conv_w, conv_b, wa, ba, wx, bx, lam):
    _, n_chunks, d_rg = xrg3.shape
    nb = d_rg // RG_CB
    n_dir, heads, hd, _ = wa.shape
    per = RG_CB // hd
    assert heads == nb * per
    gate_w = pl.BlockSpec((n_dir, per, hd, hd), lambda j: (0, j, 0, 0))
    gate_b = pl.BlockSpec((n_dir, 1, 1, RG_CB), lambda j: (0, j, 0, 0))
    bias4 = lambda b: b.reshape(n_dir, nb, 1, RG_CB)
    return pl.pallas_call(
        _rglru_kernel,
        grid=(nb,),
        in_specs=[
            pl.BlockSpec((NPH, n_chunks, RG_CB), lambda j: (0, 0, j)),
            pl.BlockSpec((CONV_W, RG_CB), lambda j: (0, j)),
            pl.BlockSpec((1, RG_CB), lambda j: (0, j)),
            gate_w, gate_w, gate_b, gate_b,
            pl.BlockSpec((2, RG_CB), lambda j: (0, j)),
        ],
        out_specs=pl.BlockSpec((NPH, n_chunks, RG_CB), lambda j: (0, 0, j)),
        out_shape=jax.ShapeDtypeStruct((NPH, n_chunks, d_rg), F32),
        scratch_shapes=[
            pltpu.VMEM((NPH + 3, n_chunks, RG_CB), F32),
            pltpu.VMEM((NPH + 1, n_chunks, RG_CB), F32),
            pltpu.VMEM((NPH + 1, n_chunks, RG_CB), F32),
        ],
        compiler_params=_cparams(1),
        name="rglru",
    )(xrg3, conv_w, conv_b, wa, wx, bias4(ba), bias4(bx), lam)


def _rms_norm(x, g):
    return x * lax.rsqrt(jnp.mean(x * x, axis=-1, keepdims=True) + LN_EPS) * g


def _mixout_kernel(alpha, ys5_ref, yrg_ref, z_ref, h_ref, gw_ref, gb_ref, gn5_ref, gnr_ref,
                   w_ref, b_ref, g_ref, be_ref, o_ref):
    rows = NPH * CT
    d_s5, d_model = ys5_ref.shape[-1], o_ref.shape[-1]
    pieces = lambda width: [slice(c, c + 256) for c in range(0, width, 256)]

    y = ys5_ref[...].reshape(rows, d_s5)
    yb = y.astype(BF16)
    glu, sq = [], 0.0
    for cols in pieces(d_s5):
        g = jnp.dot(yb, gw_ref[:, cols], preferred_element_type=F32) + gb_ref[:, cols]
        o = y[:, cols] * _sigmoid(g)
        sq = sq + jnp.sum(o * o, axis=-1, keepdims=True)
        glu.append(o)
    scale = lax.rsqrt(sq * (1.0 / d_s5) + LN_EPS)
    ys5n = jnp.concatenate([o * scale * gn5_ref[:, cols] for o, cols in zip(glu, pieces(d_s5))],
                           axis=1).astype(BF16)

    yrg = yrg_ref[...].reshape(rows, -1) * _gelu(z_ref[...].reshape(rows, -1))
    yrgn = _rms_norm(yrg, gnr_ref[...]).astype(BF16)

    s1, s2 = 0.0, 0.0
    for cols in pieces(d_model):
        v = (jnp.dot(ys5n, w_ref[:d_s5, cols], preferred_element_type=F32)
             + jnp.dot(yrgn, w_ref[d_s5:, cols], preferred_element_type=F32)
             + b_ref[:, cols] + alpha * h_ref[:, :, cols].reshape(rows, 256))
        s1 = s1 + jnp.sum(v, axis=-1, keepdims=True)
        s2 = s2 + jnp.sum(v * v, axis=-1, keepdims=True)
        o_ref[:, :, cols] = v.reshape(NPH, CT, 256)
    mean = s1 * (1.0 / d_model)
    rstd = lax.rsqrt(s2 * (1.0 / d_model) - mean * mean + LN_EPS)
    for cols in pieces(d_model):
        v = o_ref[:, :, cols].reshape(rows, 256)
        o_ref[:, :, cols] = ((v - mean) * rstd * g_ref[:, cols] + be_ref[:, cols]).reshape(NPH, CT, 256)


def _mixout(alpha, ys5, yrg, zrg, h, glu_w, glu_b, gn_s5, gn_rg, w_out, b_out, ln_g, ln_b):
    _, n_chunks, d_model = h.shape
    d_s5, d_rg = ys5.shape[-1], yrg.shape[-1]
    return pl.pallas_call(
        functools.partial(_mixout_kernel, alpha),
        grid=(n_chunks // CT,),
        in_specs=[
            _tile3(d_s5), _tile3(d_rg), _tile3(d_rg), _tile3(d_model),
            _resident((d_s5, d_s5)), _resident((1, d_s5)), _resident((1, d_s5)), _resident((1, d_rg)),
            _resident((d_s5 + d_rg, d_model)), _resident((1, d_model)),
            _resident((1, d_model)), _resident((1, d_model)),
        ],
        out_specs=_tile3(d_model),
        out_shape=jax.ShapeDtypeStruct((NPH, n_chunks, d_model), F32),
        compiler_params=_cparams(1),
        name="mixout",
    )(ys5, yrg, zrg, h, glu_w, glu_b, gn_s5, gn_rg, w_out, b_out, ln_g, ln_b)


def _ffn_kernel(alpha, h_ref, p_ref, w1_ref, b1_ref, w2_ref, b2_ref, pw_ref, gw_ref, gb_ref,
                g_ref, be_ref, o_ref, hb_ref, acc_ref):
    j = pl.program_id(1)
    rows = NPH * CT

    @pl.when(j == 0)
    def _():
        h = h_ref[...].reshape(rows, -1)
        hb = h.astype(BF16)
        hb_ref[...] = hb
        gate = _sigmoid(jnp.dot(hb, gw_ref[...], preferred_element_type=F32) + gb_ref[...])
        pe = _to_phase_major(p_ref[...]).astype(BF16)
        acc_ref[...] = alpha * h + gate * jnp.dot(pe, pw_ref[...], preferred_element_type=F32) + b2_ref[...]

    a = jnp.dot(hb_ref[...], w1_ref[...], preferred_element_type=F32) + b1_ref[...]
    a = jnp.maximum(a, 0.0)
    acc_ref[...] += jnp.dot((a * a).astype(BF16), w2_ref[...], preferred_element_type=F32)

    @pl.when(j == pl.num_programs(1) - 1)
    def _():
        o_ref[...] = _to_time_major(_layer_norm(acc_ref[...], g_ref[...], be_ref[...]))


def _ffn(alpha, h1, p2, w1, b1, w2, b2, ple_w, gate_w, gate_b, ln_g, ln_b, ff_tile):
    _, n_chunks, d_model = h1.shape
    d_ff = w1.shape[1]
    ple_dim = ple_w.shape[0]
    rows = NPH * CT
    return pl.pallas_call(
        functools.partial(_ffn_kernel, alpha),
        grid=(n_chunks // CT, d_ff // ff_tile),
        in_specs=[
            _tile3(d_model),
            pl.BlockSpec((rows, ple_dim), lambda i, j: (i, 0)),
            pl.BlockSpec((d_model, ff_tile), lambda i, j: (0, j)),
            pl.BlockSpec((1, ff_tile), lambda i, j: (0, j)),
            pl.BlockSpec((ff_tile, d_model), lambda i, j: (j, 0)),
            _resident((1, d_model)),
            _resident((ple_dim, d_model)), _resident((d_model, d_model)), _resident((1, d_model)),
            _resident((1, d_model)), _resident((1, d_model)),
        ],
        out_specs=pl.BlockSpec((rows, d_model), lambda i, j: (i, 0)),
        out_shape=jax.ShapeDtypeStruct((NPH * n_chunks, d_model), F32),
        scratch_shapes=[pltpu.VMEM((rows, d_model), BF16), pltpu.VMEM((rows, d_model), F32)],
        compiler_params=_cparams(2),
        name="ffn",
    )(h1, p2, w1, b1, w2, b2, ple_w, gate_w, gate_b, ln_g, ln_b)


def kernel(x, p, ln_in_g, ln_in_b, w_in, b_in, s5_lambda_re, s5_lambda_im, s5_log_step, s5_b_re, s5_b_im, s5_c_re, s5_c_im, s5_d, s5_glu_w, s5_glu_b, rg_conv_w, rg_conv_b, rg_wa, rg_ba, rg_wx, rg_bx, rg_lambda, gn_s5, gn_rg, w_out, b_out, ln1_g, ln1_b, w_ff1, b_ff1, w_ff2, b_ff2, ple_w, ple_gate_w, ple_gate_b, ln2_g, ln2_b):
    batch, seq, d_model = x.shape
    depth = w_in.shape[0]
    assert batch == 1 and depth == 1 and seq % (NPH * CT) == 0
    n_chunks = seq // NPH
    d_s5 = s5_glu_w.shape[-1]
    d_rg = rg_conv_w.shape[-1]
    alpha = (2.0 * depth) ** 0.25
    row = lambda v: v.reshape(1, -1).astype(F32)

    h, u3, xrg, zrg = _inproj(x.reshape(seq, d_model), row(ln_in_g), row(ln_in_b),
                              w_in[0].astype(BF16), row(b_in[0]), d_s5, d_rg)

    ys5, (w_out_b, gate_w_b, w_ff1_b, w_ff2_b) = _s5(
        u3, s5_lambda_re[0], s5_lambda_im[0], s5_log_step[0], s5_b_re[0], s5_b_im[0], s5_c_re[0], s5_c_im[0],
        s5_d[0], [w_out[0], ple_gate_w[0], w_ff1[0], w_ff2[0]])

    yrg = _rglru(xrg, rg_conv_w[0].astype(F32), row(rg_conv_b[0]), rg_wa[0], rg_ba[0], rg_wx[0], rg_bx[0],
                 rg_lambda[0].astype(F32))

    h1 = _mixout(alpha, ys5, yrg, zrg, h, s5_glu_w[0].astype(BF16), row(s5_glu_b[0]), row(gn_s5[0]),
                 row(gn_rg[0]), w_out_b, row(b_out[0]), row(ln1_g[0]), row(ln1_b[0]))

    out = _ffn(alpha, h1, p.reshape(seq, p.shape[-1]), w_ff1_b, row(b_ff1[0]),
               w_ff2_b, row(b_ff2[0]), ple_w[0].astype(BF16), gate_w_b,
               row(ple_gate_b[0]), row(ln2_g[0]), row(ln2_b[0]), ff_tile=1024)
    return out.reshape(batch, seq, d_model).astype(x.dtype)
```

```python
import functools
import math

import jax
import jax.numpy as jnp
from jax import lax
from jax.experimental import pallas as pl
from jax.experimental.pallas import tpu as pltpu

F32 = jnp.float32
BF16 = jnp.bfloat16

NPH = 16
CT = 32
S5_K = 16
S5_P = 64
S5_LB = 128
RG_CB = 128
CONV_W = 4
RG_C = 8.0
LN_EPS = 1e-5
VMEM_LIMIT_V7X = 56 * 1024 * 1024


def _cparams(n_axes):
    return pltpu.CompilerParams(
        dimension_semantics=("arbitrary",) * n_axes,
        vmem_limit_bytes=VMEM_LIMIT_V7X)


def _resident(shape):
    return pl.BlockSpec(shape, lambda *_: (0,) * len(shape), pipeline_mode=pl.Buffered(1))


def _tile3(width):
    return pl.BlockSpec((NPH, CT, width), lambda i, *_: (0, i, 0))


def _to_phase_major(x):
    w = x.shape[-1]
    return jnp.swapaxes(x.reshape(CT, NPH, w), 0, 1).reshape(NPH * CT, w)


def _to_time_major(x):
    w = x.shape[-1]
    return jnp.swapaxes(x.reshape(NPH, CT, w), 0, 1).reshape(CT * NPH, w)


def _layer_norm(x, g, b):
    mu = jnp.mean(x, axis=-1, keepdims=True)
    xc = x - mu
    var = jnp.mean(xc * xc, axis=-1, keepdims=True)
    return xc * lax.rsqrt(var + LN_EPS) * g + b


def _gelu(x):
    c = math.sqrt(2.0 / math.pi)
    return 0.5 * x * (1.0 + jnp.tanh(c * (x + 0.044715 * (x * x * x))))


def _sigmoid(x):
    return 1.0 / (1.0 + jnp.exp(-x))


def _inproj_kernel(x_ref, g_ref, b_ref, w_ref, bw_ref, h_ref, u_ref, xrg_ref, zrg_ref):
    d_s5, d_rg = u_ref.shape[-1], xrg_ref.shape[-1]
    h = _layer_norm(_to_phase_major(x_ref[...]), g_ref[...], b_ref[...])
    h_ref[...] = h.reshape(h_ref.shape)
    proj = jnp.dot(h.astype(BF16), w_ref[...], preferred_element_type=F32) + bw_ref[...]
    u_ref[...] = proj[:, :d_s5].astype(BF16).reshape(u_ref.shape)
    xrg_ref[...] = proj[:, d_s5:d_s5 + d_rg].reshape(xrg_ref.shape)
    zrg_ref[...] = proj[:, d_s5 + d_rg:].reshape(zrg_ref.shape)


def _inproj(x2, ln_g, ln_b, w_in, b_in, d_s5, d_rg):
    seq, d_model = x2.shape
    n_chunks = seq // NPH
    return pl.pallas_call(
        _inproj_kernel,
        grid=(n_chunks // CT,),
        in_specs=[
            pl.BlockSpec((CT * NPH, d_model), lambda i: (i, 0)),
            _resident((1, d_model)), _resident((1, d_model)),
            _resident(w_in.shape), _resident(b_in.shape),
        ],
        out_specs=[_tile3(d_model), _tile3(d_s5), _tile3(d_rg), _tile3(d_rg)],
        out_shape=[
            jax.ShapeDtypeStruct((NPH, n_chunks, d_model), F32),
            jax.ShapeDtypeStruct((NPH, n_chunks, d_s5), BF16),
            jax.ShapeDtypeStruct((NPH, n_chunks, d_rg), F32),
            jax.ShapeDtypeStruct((NPH, n_chunks, d_rg), F32),
        ],
        compiler_params=_cparams(1),
        name="inproj",
    )(x2, ln_g, ln_b, w_in, b_in)


def _row_scan_exclusive(sre, sim, qre, qim, reverse):
    n, w = sre.shape
    row = lax.broadcasted_iota(jnp.int32, (n, w), 0)

    def shift(v, k):
        if k % 8 == 0:
            z = jnp.zeros((k, w), v.dtype)
            return jnp.concatenate([v[k:], z] if reverse else [z, v[:n - k]], axis=0)
        if reverse:
            return jnp.where(row < n - k, pltpu.roll(v, n - k, axis=0), 0.0)
        return jnp.where(row >= k, pltpu.roll(v, k, axis=0), 0.0)

    xre, xim = shift(sre, 1), shift(sim, 1)
    k, i = 1, 0
    while k < n:
        pr, pi = qre[i:i + 1], qim[i:i + 1]
        if k % 8 == 0:
            keep = slice(n - k, n) if reverse else slice(0, k)
            dst = slice(0, n - k) if reverse else slice(k, n)
            src = slice(k, n) if reverse else slice(0, n - k)
            sr, si = xre[src], xim[src]
            nre = xre[dst] + pr * sr - pi * si
            nim = xim[dst] + pr * si + pi * sr
            order = (lambda new, old: [new, old]) if reverse else (lambda new, old: [old, new])
            xre = jnp.concatenate(order(nre, xre[keep]), axis=0)
            xim = jnp.concatenate(order(nim, xim[keep]), axis=0)
        else:
            sr, si = shift(xre, k), shift(xim, k)
            xre, xim = xre + pr * sr - pi * si, xim + pr * si + pi * sr
        k *= 2
        i += 1
    return xre, xim


def _dot(a, b):
    return jnp.dot(a.astype(BF16), b.astype(BF16), preferred_element_type=F32)


def _s5_chunk_operators(par, dvec, half):
    tk = NPH * S5_K
    lane = lax.broadcasted_iota(jnp.int32, par.shape, 1)
    par = jnp.where(lax.shift_right_logical(lane, 6) == half, par, 0.0)
    pa = par[0:128].T
    pb = par[128:256]
    lane16 = lax.broadcasted_iota(jnp.int32, (16, tk), 1)
    row16 = lax.broadcasted_iota(jnp.int32, (16, tk), 0)
    rep = (lax.shift_right_logical(lane16, 4) == row16).astype(F32)
    til = ((lane16 & 15) == row16).astype(F32)

    def cmul(ar, ai, br, bi):
        return ar * br - ai * bi, ar * bi + ai * br

    def c_of(r0):
        c_re, c_im = pb[r0 + 64:r0 + 80], pb[r0 + 80:r0 + 96]
        blocks = [cmul(c_re, c_im, pb[r0 + t:r0 + t + 1], pb[r0 + 16 + t:r0 + 17 + t]) for t in range(NPH)]
        return (jnp.concatenate([b[0] for b in blocks], axis=0),
                -jnp.concatenate([b[1] for b in blocks], axis=0))

    cc = jnp.concatenate(list(c_of(0)) + list(c_of(32)), axis=1)
    yield

    spread = {c0: (_dot(pa[:, c0:c0 + 16], rep), _dot(pa[:, c0 + 16:c0 + 32], rep),
                   _dot(pa[:, c0 + 64:c0 + 80], til), _dot(pa[:, c0 + 80:c0 + 96], til)) for c0 in (0, 32)}
    yield

    wf_re, wf_im = cmul(*spread[0])
    wb_re, wb_im = cmul(*spread[32])
    bc = jnp.concatenate([wf_re, wf_im, wb_re, wb_im], axis=0)
    yield

    kf = _dot(pb[64:80], wf_re) - _dot(pb[80:96], wf_im)
    kb = _dot(pb[96:112], wb_re) - _dot(pb[112:128], wb_im)
    yield

    blocks = []
    for t in range(NPH):
        left = S5_K * (NPH - 1 - t)
        right = S5_K * t
        f = kf if left == 0 else jnp.where(lane16 < tk - left, pltpu.roll(kf, tk - left, axis=1), 0.0)
        b = kb if right == 0 else jnp.where(lane16 >= right, pltpu.roll(kb, right, axis=1), 0.0)
        blocks.append(f + b)
    a = jnp.concatenate(blocks, axis=0)
    ri = lax.broadcasted_iota(jnp.int32, (tk, tk), 0)
    ci = lax.broadcasted_iota(jnp.int32, (tk, tk), 1)
    a = a + jnp.where(ri == ci, dvec, 0.0)
    return a.astype(BF16), bc.astype(BF16), cc.astype(BF16)


def _in_lockstep(generators):
    results = [None] * len(generators)
    pending = list(enumerate(generators))
    while pending:
        still = []
        for i, gen in pending:
            try:
                next(gen)
                still.append((i, gen))
            except StopIteration as stop:
                results[i] = stop.value
        pending = still
    return results


def _cast_stream(k, n_steps, srcs, dsts, inbufs, outbufs, sems):
    assert n_steps >= 2
    slot = lax.rem(k, 2)
    n_streams = len(srcs)
    rows = [src.shape[0] // n_steps for src in srcs]

    def read(i, step, sl):
        return pltpu.make_async_copy(srcs[i].at[pl.ds(step * rows[i], rows[i]), :], inbufs[i].at[sl],
                                     sems.at[i, 0, sl])

    def write(i, step, sl):
        return pltpu.make_async_copy(outbufs[i].at[sl], dsts[i].at[pl.ds(step * rows[i], rows[i]), :],
                                     sems.at[i, 1, sl])

    def begin():
        @pl.when(k == 0)
        def _():
            for i in range(n_streams):
                read(i, 0, 0).start()

        @pl.when(k + 1 < n_steps)
        def _():
            for i in range(n_streams):
                read(i, k + 1, 1 - slot).start()

        @pl.when(k >= 2)
        def _():
            for i in range(n_streams):
                write(i, k - 2, slot).wait()

        for i in range(n_streams):
            read(i, k, slot).wait()

    def cast():
        for i in range(n_streams):
            outbufs[i][slot] = inbufs[i][slot].astype(BF16)

    def end():
        for i in range(n_streams):
            write(i, k, slot).start()

        @pl.when(k == n_steps - 1)
        def _():
            for i in range(n_streams):
                write(i, k - 1, 1 - slot).wait()
                write(i, k, slot).wait()

    return begin, cast, end


def _s5_kernel(n_w, n_grid, u_ref, *rest):
    raw_refs, d_ref, rest = rest[:7], rest[7], rest[8:]
    w_src, rest = rest[:n_w], rest[n_w:]
    y_ref, rest = rest[0], rest[1:]
    w_dst, rest = rest[:n_w], rest[n_w:]
    xt_ref, yt_ref, a_ref, bc_ref, cc_ref, q_ref, rest = rest[:6] + (rest[6:],)
    w_in_buf, w_out_buf, w_sems = rest[:n_w], rest[n_w:2 * n_w], rest[2 * n_w]
    n = u_ref.shape[1]
    p2 = 2 * S5_P
    tk = NPH * S5_K
    n_groups = S5_LB // S5_K
    n_pairs = n_groups // 2

    def to_rows(s, _):
        xt_ref[s] = u_ref[s].T
        return 0
    lax.fori_loop(0, NPH, to_rows, 0, unroll=8)

    pars = []
    for g in range(n_groups):
        par, q_ref[g] = _s5_discretise(g, *raw_refs)
        pars.append(par)
    stages = [_s5_chunk_operators(pars[g], d_ref[g], g % 2) for g in range(n_groups)]
    for g, (a, bc, cc) in enumerate(_in_lockstep(stages)):
        a_ref[g], bc_ref[g], cc_ref[g] = a, bc, cc

    def pair(gp, _):
        stream_begin, stream_cast, stream_end = _cast_stream(
            pl.program_id(0) * n_pairs + gp, n_grid * n_pairs, w_src, w_dst, w_in_buf, w_out_buf, w_sems)
        stream_begin()
        stream_cast()
        xs, ys = [], []
        for h in range(2):
            g = 2 * gp + h
            rows = pl.ds(pl.multiple_of(g * S5_K, S5_K), S5_K)
            x = xt_ref[:, rows, :].reshape(tk, n)
            xs.append(x)
            ys.append(jnp.dot(a_ref[g], x, preferred_element_type=F32))
        bc = jnp.concatenate([bc_ref[2 * gp], bc_ref[2 * gp + 1]], axis=1)
        s = jnp.dot(bc, jnp.concatenate(xs, axis=0), preferred_element_type=F32)
        st = [s[i * p2:(i + 1) * p2].T for i in range(4)]
        lane = lax.broadcasted_iota(jnp.int32, (4 * 16, p2), 1)
        q = jnp.where(lane < S5_P, q_ref[2 * gp], q_ref[2 * gp + 1])
        hf_re, hf_im = _row_scan_exclusive(st[0], st[1], q[0:16], q[16:32], False)
        hb_re, hb_im = _row_scan_exclusive(st[2], st[3], q[32:48], q[48:64], True)
        hin = jnp.concatenate([hf_re, hf_im, hb_re, hb_im], axis=1).astype(BF16)
        for h in range(2):
            g = 2 * gp + h
            rows = pl.ds(pl.multiple_of(g * S5_K, S5_K), S5_K)
            y = ys[h] + lax.dot_general(cc_ref[g], hin, (((1,), (1,)), ((), ())), preferred_element_type=F32)
            yt_ref[:, rows, :] = _gelu(y).reshape(NPH, S5_K, n)
        stream_end()
        return 0
    lax.fori_loop(0, n_pairs, pair, 0)

    def to_lanes(t, _):
        y_ref[t] = yt_ref[t].T
        return 0
    lax.fori_loop(0, NPH, to_lanes, 0, unroll=8)


def _s5(u3, lam_re, lam_im, log_step, b_re, b_im, c_re, c_im, d, weights):
    assert NPH == 16 and S5_K == 16 and 2 * lam_re.shape[-1] == 128
    _, n_chunks, d_s5 = u3.shape
    raw = [lam_re, lam_im, log_step[..., None], b_re, b_im, c_re, c_im]
    dvec = jnp.tile(d.astype(F32), (1, NPH))[:, None, :]
    gpb = S5_LB // S5_K
    tk = NPH * S5_K
    n_grid = d_s5 // S5_LB
    n_steps = n_grid * (gpb // 2)
    chunk = lambda w: (w.shape[0] // n_steps, w.shape[1])
    assert all(w.shape[0] % (16 * n_steps) == 0 for w in weights)
    per_block = lambda n, *tail: pl.BlockSpec((n,) + tail, lambda b: (b,) + (0,) * len(tail))
    per_dir_block = lambda v: pl.BlockSpec((2, gpb) + v.shape[2:], lambda b: (0, b) + (0,) * (v.ndim - 2))
    hbm = pl.BlockSpec(memory_space=pl.ANY)
    outs = pl.pallas_call(
        functools.partial(_s5_kernel, len(weights), n_grid),
        grid=(n_grid,),
        in_specs=[pl.BlockSpec((NPH, n_chunks, S5_LB), lambda b: (0, 0, b))]
        + [per_dir_block(v) for v in raw] + [per_block(gpb, 1, tk)] + [hbm] * len(weights),
        out_specs=[pl.BlockSpec((NPH, n_chunks, S5_LB), lambda b: (0, 0, b))] + [hbm] * len(weights),
        out_shape=[jax.ShapeDtypeStruct((NPH, n_chunks, d_s5), F32)]
        + [jax.ShapeDtypeStruct(w.shape, BF16) for w in weights],
        scratch_shapes=[
            pltpu.VMEM((NPH, S5_LB, n_chunks), BF16),
            pltpu.VMEM((NPH, S5_LB, n_chunks), F32),
            pltpu.VMEM((gpb, tk, tk), BF16),
            pltpu.VMEM((gpb, 8 * S5_P, tk), BF16),
            pltpu.VMEM((gpb, tk, 8 * S5_P), BF16),
            pltpu.VMEM((gpb, 4 * 16, 2 * S5_P), F32),
        ] + [pltpu.VMEM((2,) + chunk(w), F32) for w in weights]
        + [pltpu.VMEM((2,) + chunk(w), BF16) for w in weights]
        + [pltpu.SemaphoreType.DMA((len(weights), 2, 2))],
        compiler_params=_cparams(1),
        name="s5",
    )(u3, *raw, dvec, *weights)
    return outs[0], outs[1:]


def _s5_discretise(g, lam_re_ref, lam_im_ref, log_step_ref, b_re_ref, b_im_ref, c_re_ref, c_im_ref):
    def cmul(x, y):
        return x[0] * y[0] - x[1] * y[1], x[0] * y[1] + x[1] * y[0]

    def rows(zs):
        return jnp.concatenate([z[0] for z in zs], axis=0), jnp.concatenate([z[1] for z in zs], axis=0)

    def one_direction(d):
        lre, lim = jnp.minimum(lam_re_ref[d, g:g + 1, :], -1e-4), lam_im_ref[d, g:g + 1, :]
        step = jnp.exp(log_step_ref[d, g:g + 1, :])
        are, aim = lre * step, lim * step
        mag = jnp.exp(are)
        lam_bar = (mag * jnp.cos(aim), mag * jnp.sin(aim))
        powers = [(jnp.ones_like(are), jnp.zeros_like(are))]
        for _ in range(NPH):
            powers.append(cmul(powers[-1], lam_bar))
        squares = [powers[NPH]]
        for _ in range(15):
            squares.append(cmul(squares[-1], squares[-1]))
        nr, ni = lam_bar[0] - 1.0, lam_bar[1]
        den = lre * lre + lim * lim
        z = ((nr * lre + ni * lim) / den, (ni * lre - nr * lim) / den)
        bbar = cmul(z, (b_re_ref[d, g].T, b_im_ref[d, g].T))
        c = (c_re_ref[d, g], c_im_ref[d, g])
        return powers, squares, bbar, c

    pf, qf, bbar_f, c_f = one_direction(0)
    pb, qb, bbar_b, c_b = one_direction(1)
    par = jnp.concatenate(
        list(rows(pf[NPH - 1::-1])) + list(rows(pb[:NPH])) + list(bbar_f) + list(bbar_b)
        + list(rows(pf[1:])) + list(rows(pb[NPH:0:-1])) + list(c_f) + list(c_b), axis=0)
    q_tab = jnp.concatenate(list(rows(qf)) + list(rows(qb)), axis=0)
    both_halves = lambda v: jnp.concatenate([v, v], axis=1)
    return both_halves(par), both_halves(q_tab)


def _row_scan_carry(a, h, reverse):
    n, w = a.shape
    row = lax.broadcasted_iota(jnp.int32, (n, w), 0)

    def shift(v, k, fill):
        if k % 8 == 0:
            z = jnp.full((k, w), fill, v.dtype)
            if reverse:
                return jnp.concatenate([v[k:], z], axis=0)
            return jnp.concatenate([z, v[:n - k]], axis=0)
        if reverse:
            return jnp.where(row < n - k, pltpu.roll(v, n - k, axis=0), fill)
        return jnp.where(row >= k, pltpu.roll(v, k, axis=0), fill)

    k = 1
    while k < n:
        h = h + a * shift(h, k, 0.0)
        a = a * shift(a, k, 1.0)
        k *= 2
    return shift(h, 1, 0.0)


def _rglru_kernel(x_ref, cw_ref, cb_ref, wa_ref, wx_ref, ba_ref, bx_ref, lam_ref, y_ref, xe_ref, al_ref, hl_ref):
    n, w = x_ref.shape[1], x_ref.shape[2]

    def block_diag(heads):
        per, hd, _ = heads.shape
        zeros = jnp.zeros((hd, hd), heads.dtype)
        return jnp.concatenate(
            [jnp.concatenate([heads[p] if q == p else zeros for q in range(per)], axis=1) for p in range(per)],
            axis=0)
    row = lax.broadcasted_iota(jnp.int32, (n, w), 0)

    def from_prev_chunk(v):
        return jnp.where(row >= 1, pltpu.roll(v, 1, axis=0), 0.0)

    def from_next_chunk(v):
        return jnp.where(row < n - 1, pltpu.roll(v, n - 1, axis=0), 0.0)

    xe_ref[0] = from_prev_chunk(x_ref[NPH - 2])
    xe_ref[1] = from_prev_chunk(x_ref[NPH - 1])

    def copy_body(s, _):
        xe_ref[s + 2] = x_ref[s]
        return 0
    lax.fori_loop(0, NPH, copy_body, 0, unroll=4)
    xe_ref[NPH + 2] = from_next_chunk(x_ref[0])

    cw = cw_ref[...]
    cb = cb_ref[...]
    lam = lam_ref[...]
    neg = -lam
    softplus = jnp.maximum(neg, 0.0) + jnp.log(1.0 + jnp.exp(-jnp.abs(neg)))

    for d, reverse in ((0, False), (1, True)):
        rate = (-RG_C) * softplus[d:d + 1]
        wg = jnp.concatenate([block_diag(wa_ref[d]), block_diag(wx_ref[d])], axis=1).astype(BF16)
        bg = jnp.concatenate([ba_ref[d, 0], bx_ref[d, 0]], axis=1)
        init = NPH if reverse else 0
        al_ref[init] = jnp.ones((n, w), F32)
        hl_ref[init] = jnp.zeros((n, w), F32)

        def local_body(i, _):
            s = (NPH - 1 - i) if reverse else i
            xc = (cw[0:1] * xe_ref[s] + cw[1:2] * xe_ref[s + 1]
                  + cw[2:3] * xe_ref[s + 2] + cw[3:4] * xe_ref[s + 3] + cb)
            g = jnp.dot(xc.astype(BF16), wg, preferred_element_type=F32) + bg
            r = _sigmoid(g[:, :w])
            ig = _sigmoid(g[:, w:])
            a = jnp.exp(rate * r)
            bt = jnp.sqrt(1.0 - a * a) * (ig * xc)
            src = (s + 1) if reverse else s
            dst = s if reverse else (s + 1)
            hl_ref[dst] = a * hl_ref[src] + bt
            al_ref[dst] = a * al_ref[src]
            return 0
        lax.fori_loop(0, NPH, local_body, 0, unroll=2)

        last = 0 if reverse else NPH
        carry = _row_scan_carry(al_ref[last], hl_ref[last], reverse)

        def fix_body(s, _):
            slot = s if reverse else (s + 1)
            v = hl_ref[slot] + al_ref[slot] * carry
            if reverse:
                y_ref[s] = y_ref[s] + v
            else:
                y_ref[s] = v
            return 0
        lax.fori_loop(0, NPH, fix_body, 0, unroll=4)


def _rglru(xrg3, conv_w, conv_b, wa, ba, wx, bx, lam):
    _, n_chunks, d_rg = xrg3.shape
    nb = d_rg // RG_CB
    n_dir, heads, hd, _ = wa.shape
    per = RG_CB // hd
    assert heads == nb * per
    gate_w = pl.BlockSpec((n_dir, per, hd, hd), lambda j: (0, j, 0, 0))
    gate_b = pl.BlockSpec((n_dir, 1, 1, RG_CB), lambda j: (0, j, 0, 0))
    bias4 = lambda b: b.reshape(n_dir, nb, 1, RG_CB)
    return pl.pallas_call(
        _rglru_kernel,
        grid=(nb,),
        in_specs=[
            pl.BlockSpec((NPH, n_chunks, RG_CB), lambda j: (0, 0, j)),
            pl.BlockSpec((CONV_W, RG_CB), lambda j: (0, j)),
            pl.BlockSpec((1, RG_CB), lambda j: (0, j)),
            gate_w, gate_w, gate_b, gate_b,
            pl.BlockSpec((2, RG_CB), lambda j: (0, j)),
        ],
        out_specs=pl.BlockSpec((NPH, n_chunks, RG_CB), lambda j: (0, 0, j)),
        out_shape=jax.ShapeDtypeStruct((NPH, n_chunks, d_rg), F32),
        scratch_shapes=[
            pltpu.VMEM((NPH + 3, n_chunks, RG_CB), F32),
            pltpu.VMEM((NPH + 1, n_chunks, RG_CB), F32),
            pltpu.VMEM((NPH + 1, n_chunks, RG_CB), F32),
        ],
        compiler_params=_cparams(1),
        name="rglru",
    )(xrg3, conv_w, conv_b, wa, wx, bias4(ba), bias4(bx), lam)


def _rms_norm(x, g):
    return x * lax.rsqrt(jnp.mean(x * x, axis=-1, keepdims=True) + LN_EPS) * g


def _mixout_kernel(alpha, ys5_ref, yrg_ref, z_ref, h_ref, gw_ref, gb_ref, gn5_ref, gnr_ref,
                   w_ref, b_ref, g_ref, be_ref, o_ref):
    rows = NPH * CT
    d_s5, d_model = ys5_ref.shape[-1], o_ref.shape[-1]
    pieces = lambda width: [slice(c, c + 256) for c in range(0, width, 256)]

    y = ys5_ref[...].reshape(rows, d_s5)
    yb = y.astype(BF16)
    glu, sq = [], 0.0
    for cols in pieces(d_s5):
        g = jnp.dot(yb, gw_ref[:, cols], preferred_element_type=F32) + gb_ref[:, cols]
        o = y[:, cols] * _sigmoid(g)
        sq = sq + jnp.sum(o * o, axis=-1, keepdims=True)
        glu.append(o)
    scale = lax.rsqrt(sq * (1.0 / d_s5) + LN_EPS)
    ys5n = jnp.concatenate([o * scale * gn5_ref[:, cols] for o, cols in zip(glu, pieces(d_s5))],
                           axis=1).astype(BF16)

    yrg = yrg_ref[...].reshape(rows, -1) * _gelu(z_ref[...].reshape(rows, -1))
    yrgn = _rms_norm(yrg, gnr_ref[...]).astype(BF16)

    s1, s2 = 0.0, 0.0
    for cols in pieces(d_model):
        v = (jnp.dot(ys5n, w_ref[:d_s5, cols], preferred_element_type=F32)
             + jnp.dot(yrgn, w_ref[d_s5:, cols], preferred_element_type=F32)
             + b_ref[:, cols] + alpha * h_ref[:, :, cols].reshape(rows, 256))
        s1 = s1 + jnp.sum(v, axis=-1, keepdims=True)
        s2 = s2 + jnp.sum(v * v, axis=-1, keepdims=True)
        o_ref[:, :, cols] = v.reshape(NPH, CT, 256)
    mean = s1 * (1.0 / d_model)
    rstd = lax.rsqrt(s2 * (1.0 / d_model) - mean * mean + LN_EPS)
    for cols in pieces(d_model):
        v = o_ref[:, :, cols].reshape(rows, 256)
        o_ref[:, :, cols] = ((v - mean) * rstd * g_ref[:, cols] + be_ref[:, cols]).reshape(NPH, CT, 256)


def _mixout(alpha, ys5, yrg, zrg, h, glu_w, glu_b, gn_s5, gn_rg, w_out, b_out, ln_g, ln_b):
    _, n_chunks, d_model = h.shape
    d_s5, d_rg = ys5.shape[-1], yrg.shape[-1]
    return pl.pallas_call(
        functools.partial(_mixout_kernel, alpha),
        grid=(n_chunks // CT,),
        in_specs=[
            _tile3(d_s5), _tile3(d_rg), _tile3(d_rg), _tile3(d_model),
            _resident((d_s5, d_s5)), _resident((1, d_s5)), _resident((1, d_s5)), _resident((1, d_rg)),
            _resident((d_s5 + d_rg, d_model)), _resident((1, d_model)),
            _resident((1, d_model)), _resident((1, d_model)),
        ],
        out_specs=_tile3(d_model),
        out_shape=jax.ShapeDtypeStruct((NPH, n_chunks, d_model), F32),
        compiler_params=_cparams(1),
        name="mixout",
    )(ys5, yrg, zrg, h, glu_w, glu_b, gn_s5, gn_rg, w_out, b_out, ln_g, ln_b)


def _ffn_kernel(alpha, h_ref, p_ref, w1_ref, b1_ref, w2_ref, b2_ref, pw_ref, gw_ref, gb_ref,
                g_ref, be_ref, o_ref, hb_ref, acc_ref):
    j = pl.program_id(1)
    rows = NPH * CT

    @pl.when(j == 0)
    def _():
        h = h_ref[...].reshape(rows, -1)
        hb = h.astype(BF16)
        hb_ref[...] = hb
        gate = _sigmoid(jnp.dot(hb, gw_ref[...], preferred_element_type=F32) + gb_ref[...])
        pe = _to_phase_major(p_ref[...]).astype(BF16)
        acc_ref[...] = alpha * h + gate * jnp.dot(pe, pw_ref[...], preferred_element_type=F32) + b2_ref[...]

    a = jnp.dot(hb_ref[...], w1_ref[...], preferred_element_type=F32) + b1_ref[...]
    a = jnp.maximum(a, 0.0)
    acc_ref[...] += jnp.dot((a * a).astype(BF16), w2_ref[...], preferred_element_type=F32)

    @pl.when(j == pl.num_programs(1) - 1)
    def _():
        o_ref[...] = _to_time_major(_layer_norm(acc_ref[...], g_ref[...], be_ref[...]))


def _ffn(alpha, h1, p2, w1, b1, w2, b2, ple_w, gate_w, gate_b, ln_g, ln_b, ff_tile):
    _, n_chunks, d_model = h1.shape
    d_ff = w1.shape[1]
    ple_dim = ple_w.shape[0]
    rows = NPH * CT
    return pl.pallas_call(
        functools.partial(_ffn_kernel, alpha),
        grid=(n_chunks // CT, d_ff // ff_tile),
        in_specs=[
            _tile3(d_model),
            pl.BlockSpec((rows, ple_dim), lambda i, j: (i, 0)),
            pl.BlockSpec((d_model, ff_tile), lambda i, j: (0, j)),
            pl.BlockSpec((1, ff_tile), lambda i, j: (0, j)),
            pl.BlockSpec((ff_tile, d_model), lambda i, j: (j, 0)),
            _resident((1, d_model)),
            _resident((ple_dim, d_model)), _resident((d_model, d_model)), _resident((1, d_model)),
            _resident((1, d_model)), _resident((1, d_model)),
        ],
        out_specs=pl.BlockSpec((rows, d_model), lambda i, j: (i, 0)),
        out_shape=jax.ShapeDtypeStruct((NPH * n_chunks, d_model), F32),
        scratch_shapes=[pltpu.VMEM((rows, d_model), BF16), pltpu.VMEM((rows, d_model), F32)],
        compiler_params=_cparams(2),
        name="ffn",
    )(h1, p2, w1, b1, w2, b2, ple_w, gate_w, gate_b, ln_g, ln_b)


def kernel(x, p, ln_in_g, ln_in_b, w_in, b_in, s5_lambda_re, s5_lambda_im, s5_log_step, s5_b_re, s5_b_im, s5_c_re, s5_c_im, s5_d, s5_glu_w, s5_glu_b, rg_conv_w, rg_conv_b, rg_wa, rg_ba, rg_wx, rg_bx, rg_lambda, gn_s5, gn_rg, w_out, b_out, ln1_g, ln1_b, w_ff1, b_ff1, w_ff2, b_ff2, ple_w, ple_gate_w, ple_gate_b, ln2_g, ln2_b):
    batch, seq, d_model = x.shape
    depth = w_in.shape[0]
    assert batch == 1 and depth == 1 and seq % (NPH * CT) == 0
    n_chunks = seq // NPH
    d_s5 = s5_glu_w.shape[-1]
    d_rg = rg_conv_w.shape[-1]
    alpha = (2.0 * depth) ** 0.25
    row = lambda v: v.reshape(1, -1).astype(F32)

    h, u3, xrg, zrg = _inproj(x.reshape(seq, d_model), row(ln_in_g), row(ln_in_b),
                              w_in[0].astype(BF16), row(b_in[0]), d_s5, d_rg)

    ys5, (w_out_b, gate_w_b, w_ff1_b, w_ff2_b) = _s5(
        u3, s5_lambda_re[0], s5_lambda_im[0], s5_log_step[0], s5_b_re[0], s5_b_im[0], s5_c_re[0], s5_c_im[0],
        s5_d[0], [w_out[0], ple_gate_w[0], w_ff1[0], w_ff2[0]])

    yrg = _rglru(xrg, rg_conv_w[0].astype(F32), row(rg_conv_b[0]), rg_wa[0], rg_ba[0], rg_wx[0], rg_bx[0],
                 rg_lambda[0].astype(F32))

    h1 = _mixout(alpha, ys5, yrg, zrg, h, s5_glu_w[0].astype(BF16), row(s5_glu_b[0]), row(gn_s5[0]),
                 row(gn_rg[0]), w_out_b, row(b_out[0]), row(ln1_g[0]), row(ln1_b[0]))

    out = _ffn(alpha, h1, p.reshape(seq, p.shape[-1]), w_ff1_b, row(b_ff1[0]),
               w_ff2_b, row(b_ff2[0]), ple_w[0].astype(BF16), gate_w_b,
               row(ple_gate_b[0]), row(ln2_g[0]), row(ln2_b[0]), ff_tile=1024)
    return out.reshape(batch, seq, d_model).astype(x.dtype)
```

```python
import functools
import math

import jax
import jax.numpy as jnp
from jax import lax
from jax.experimental import pallas as pl
from jax.experimental.pallas import tpu as pltpu

F32 = jnp.float32
BF16 = jnp.bfloat16

NPH = 16
CT = 32
S5_K = 16
S5_P = 64
S5_LB = 128
RG_CB = 128
CONV_W = 4
RG_C = 8.0
LN_EPS = 1e-5
LOG2_E = math.log2(math.e)
VMEM_LIMIT_V7X = 56 * 1024 * 1024


def _cparams(n_axes):
    return pltpu.CompilerParams(
        dimension_semantics=("arbitrary",) * n_axes,
        vmem_limit_bytes=VMEM_LIMIT_V7X)


def _resident(shape):
    return pl.BlockSpec(shape, lambda *_: (0,) * len(shape), pipeline_mode=pl.Buffered(1))


def _tile3(width):
    return pl.BlockSpec((NPH, CT, width), lambda i, *_: (0, i, 0))


def _to_phase_major(x):
    w = x.shape[-1]
    return jnp.swapaxes(x.reshape(CT, NPH, w), 0, 1).reshape(NPH * CT, w)


def _to_time_major(x):
    w = x.shape[-1]
    return jnp.swapaxes(x.reshape(NPH, CT, w), 0, 1).reshape(CT * NPH, w)


def _layer_norm(x, g, b):
    mu = jnp.mean(x, axis=-1, keepdims=True)
    xc = x - mu
    var = jnp.mean(xc * xc, axis=-1, keepdims=True)
    return xc * lax.rsqrt(var + LN_EPS) * g + b


def _gelu(x):
    c = math.sqrt(2.0 / math.pi)
    return 0.5 * x * (1.0 + jnp.tanh(c * (x + 0.044715 * (x * x * x))))


def _sigmoid(x):
    return 1.0 / (1.0 + jnp.exp(-x))


def _inproj_kernel(x_ref, g_ref, b_ref, w_ref, bw_ref, h_ref, u_ref, xrg_ref, gz_ref):
    d_s5, d_rg = u_ref.shape[-1], xrg_ref.shape[-1]
    h = _layer_norm(_to_phase_major(x_ref[...]), g_ref[...], b_ref[...])
    h_ref[...] = h.reshape(h_ref.shape)
    proj = jnp.dot(h.astype(BF16), w_ref[...], preferred_element_type=F32) + bw_ref[...]
    u_ref[...] = proj[:, :d_s5].astype(BF16).reshape(u_ref.shape)
    xrg_ref[...] = proj[:, d_s5:d_s5 + d_rg].reshape(xrg_ref.shape)
    gz_ref[...] = _gelu(proj[:, d_s5 + d_rg:]).reshape(gz_ref.shape)


def _inproj(x2, ln_g, ln_b, w_in, b_in, d_s5, d_rg):
    seq, d_model = x2.shape
    n_chunks = seq // NPH
    return pl.pallas_call(
        _inproj_kernel,
        grid=(n_chunks // CT,),
        in_specs=[
            pl.BlockSpec((CT * NPH, d_model), lambda i: (i, 0)),
            _resident((1, d_model)), _resident((1, d_model)),
            _resident(w_in.shape), _resident(b_in.shape),
        ],
        out_specs=[_tile3(d_model), _tile3(d_s5), _tile3(d_rg), _tile3(d_rg)],
        out_shape=[
            jax.ShapeDtypeStruct((NPH, n_chunks, d_model), F32),
            jax.ShapeDtypeStruct((NPH, n_chunks, d_s5), BF16),
            jax.ShapeDtypeStruct((NPH, n_chunks, d_rg), F32),
            jax.ShapeDtypeStruct((NPH, n_chunks, d_rg), F32),
        ],
        compiler_params=_cparams(1),
        name="inproj",
    )(x2, ln_g, ln_b, w_in, b_in)


def _row_scan_exclusive(sre, sim, qre, qim, reverse):
    n, w = sre.shape
    row = lax.broadcasted_iota(jnp.int32, (n, w), 0)

    def shift(v, k):
        if k % 8 == 0:
            z = jnp.zeros((k, w), v.dtype)
            return jnp.concatenate([v[k:], z] if reverse else [z, v[:n - k]], axis=0)
        if reverse:
            return jnp.where(row < n - k, pltpu.roll(v, n - k, axis=0), 0.0)
        return jnp.where(row >= k, pltpu.roll(v, k, axis=0), 0.0)

    xre, xim = shift(sre, 1), shift(sim, 1)
    k, i = 1, 0
    while k < n:
        pr, pi = qre[i:i + 1], qim[i:i + 1]
        if k % 8 == 0:
            keep = slice(n - k, n) if reverse else slice(0, k)
            dst = slice(0, n - k) if reverse else slice(k, n)
            src = slice(k, n) if reverse else slice(0, n - k)
            sr, si = xre[src], xim[src]
            nre = xre[dst] + pr * sr - pi * si
            nim = xim[dst] + pr * si + pi * sr
            order = (lambda new, old: [new, old]) if reverse else (lambda new, old: [old, new])
            xre = jnp.concatenate(order(nre, xre[keep]), axis=0)
            xim = jnp.concatenate(order(nim, xim[keep]), axis=0)
        else:
            sr, si = shift(xre, k), shift(xim, k)
            xre, xim = xre + pr * sr - pi * si, xim + pr * si + pi * sr
        k *= 2
        i += 1
    return xre, xim


def _dot(a, b):
    return jnp.dot(a.astype(BF16), b.astype(BF16), preferred_element_type=F32)


def _s5_chunk_operators(par, dvec, half):
    tk = NPH * S5_K
    lane = lax.broadcasted_iota(jnp.int32, par.shape, 1)
    par = jnp.where(lax.shift_right_logical(lane, 6) == half, par, 0.0)
    pa = par[0:128].T
    pb = par[128:256]
    lane16 = lax.broadcasted_iota(jnp.int32, (16, tk), 1)
    row16 = lax.broadcasted_iota(jnp.int32, (16, tk), 0)
    rep = (lax.shift_right_logical(lane16, 4) == row16).astype(F32)
    til = ((lane16 & 15) == row16).astype(F32)

    def cmul(ar, ai, br, bi):
        return ar * br - ai * bi, ar * bi + ai * br

    def c_of(r0):
        c_re, c_im = pb[r0 + 64:r0 + 80], pb[r0 + 80:r0 + 96]
        blocks = [cmul(c_re, c_im, pb[r0 + t:r0 + t + 1], pb[r0 + 16 + t:r0 + 17 + t]) for t in range(NPH)]
        return (jnp.concatenate([b[0] for b in blocks], axis=0),
                -jnp.concatenate([b[1] for b in blocks], axis=0))

    cc = jnp.concatenate(list(c_of(0)) + list(c_of(32)), axis=1)
    yield

    spread = {c0: (_dot(pa[:, c0:c0 + 16], rep), _dot(pa[:, c0 + 16:c0 + 32], rep),
                   _dot(pa[:, c0 + 64:c0 + 80], til), _dot(pa[:, c0 + 80:c0 + 96], til)) for c0 in (0, 32)}
    yield

    wf_re, wf_im = cmul(*spread[0])
    wb_re, wb_im = cmul(*spread[32])
    bc = jnp.concatenate([wf_re, wf_im, wb_re, wb_im], axis=0)
    yield

    kf = _dot(pb[64:80], wf_re) - _dot(pb[80:96], wf_im)
    kb = _dot(pb[96:112], wb_re) - _dot(pb[112:128], wb_im)
    yield

    blocks = []
    for t in range(NPH):
        left = S5_K * (NPH - 1 - t)
        right = S5_K * t
        f = kf if left == 0 else jnp.where(lane16 < tk - left, pltpu.roll(kf, tk - left, axis=1), 0.0)
        b = kb if right == 0 else jnp.where(lane16 >= right, pltpu.roll(kb, right, axis=1), 0.0)
        blocks.append(f + b)
    a = jnp.concatenate(blocks, axis=0)
    ri = lax.broadcasted_iota(jnp.int32, (tk, tk), 0)
    ci = lax.broadcasted_iota(jnp.int32, (tk, tk), 1)
    a = a + jnp.where(ri == ci, dvec, 0.0)
    return a.astype(BF16), bc.astype(BF16), cc.astype(BF16)


def _in_lockstep(generators):
    results = [None] * len(generators)
    pending = list(enumerate(generators))
    while pending:
        still = []
        for i, gen in pending:
            try:
                next(gen)
                still.append((i, gen))
            except StopIteration as stop:
                results[i] = stop.value
        pending = still
    return results


def _cast_stream(k, n_steps, srcs, dsts, inbufs, outbufs, sems):
    assert n_steps >= 2
    slot = lax.rem(k, 2)
    n_streams = len(srcs)
    rows = [src.shape[0] // n_steps for src in srcs]

    def read(i, step, sl):
        return pltpu.make_async_copy(srcs[i].at[pl.ds(step * rows[i], rows[i]), :], inbufs[i].at[sl],
                                     sems.at[i, 0, sl])

    def write(i, step, sl):
        return pltpu.make_async_copy(outbufs[i].at[sl], dsts[i].at[pl.ds(step * rows[i], rows[i]), :],
                                     sems.at[i, 1, sl])

    def begin():
        @pl.when(k == 0)
        def _():
            for i in range(n_streams):
                read(i, 0, 0).start()

        @pl.when(k + 1 < n_steps)
        def _():
            for i in range(n_streams):
                read(i, k + 1, 1 - slot).start()

        @pl.when(k >= 2)
        def _():
            for i in range(n_streams):
                write(i, k - 2, slot).wait()

        for i in range(n_streams):
            read(i, k, slot).wait()

    def cast():
        for i in range(n_streams):
            outbufs[i][slot] = inbufs[i][slot].astype(BF16)

    def end():
        for i in range(n_streams):
            write(i, k, slot).start()

        @pl.when(k == n_steps - 1)
        def _():
            for i in range(n_streams):
                write(i, k - 1, 1 - slot).wait()
                write(i, k, slot).wait()

    return begin, cast, end


def _s5_kernel(n_w, n_grid, u_ref, *rest):
    raw_refs, d_ref, rest = rest[:7], rest[7], rest[8:]
    w_src, rest = rest[:n_w], rest[n_w:]
    y_ref, rest = rest[0], rest[1:]
    w_dst, rest = rest[:n_w], rest[n_w:]
    xt_ref, yt_ref, a_ref, bc_ref, cc_ref, q_ref, rest = rest[:6] + (rest[6:],)
    w_in_buf, w_out_buf, w_sems = rest[:n_w], rest[n_w:2 * n_w], rest[2 * n_w]
    n = u_ref.shape[1]
    p2 = 2 * S5_P
    tk = NPH * S5_K
    n_groups = S5_LB // S5_K
    n_pairs = n_groups // 2

    def to_rows(s, _):
        xt_ref[s] = u_ref[s].T
        return 0
    lax.fori_loop(0, NPH, to_rows, 0, unroll=8)

    pars = []
    for g in range(n_groups):
        par, q_ref[g] = _s5_discretise(g, *raw_refs)
        pars.append(par)
    stages = [_s5_chunk_operators(pars[g], d_ref[g], g % 2) for g in range(n_groups)]
    for g, (a, bc, cc) in enumerate(_in_lockstep(stages)):
        a_ref[g], bc_ref[g], cc_ref[g] = a, bc, cc

    def pair(gp, _):
        stream_begin, stream_cast, stream_end = _cast_stream(
            pl.program_id(0) * n_pairs + gp, n_grid * n_pairs, w_src, w_dst, w_in_buf, w_out_buf, w_sems)
        stream_begin()
        stream_cast()
        xs, ys = [], []
        for h in range(2):
            g = 2 * gp + h
            rows = pl.ds(pl.multiple_of(g * S5_K, S5_K), S5_K)
            x = xt_ref[:, rows, :].reshape(tk, n)
            xs.append(x)
            ys.append(jnp.dot(a_ref[g], x, preferred_element_type=F32))
        bc = jnp.concatenate([bc_ref[2 * gp], bc_ref[2 * gp + 1]], axis=1)
        s = jnp.dot(bc, jnp.concatenate(xs, axis=0), preferred_element_type=F32)
        st = [s[i * p2:(i + 1) * p2].T for i in range(4)]
        lane = lax.broadcasted_iota(jnp.int32, (4 * 16, p2), 1)
        q = jnp.where(lane < S5_P, q_ref[2 * gp], q_ref[2 * gp + 1])
        hf_re, hf_im = _row_scan_exclusive(st[0], st[1], q[0:16], q[16:32], False)
        hb_re, hb_im = _row_scan_exclusive(st[2], st[3], q[32:48], q[48:64], True)
        hin = jnp.concatenate([hf_re, hf_im, hb_re, hb_im], axis=1).astype(BF16)
        for h in range(2):
            g = 2 * gp + h
            rows = pl.ds(pl.multiple_of(g * S5_K, S5_K), S5_K)
            y = ys[h] + lax.dot_general(cc_ref[g], hin, (((1,), (1,)), ((), ())), preferred_element_type=F32)
            yt_ref[:, rows, :] = _gelu(y).reshape(NPH, S5_K, n)
        stream_end()
        return 0
    lax.fori_loop(0, n_pairs, pair, 0)

    def to_lanes(t, _):
        y_ref[t] = yt_ref[t].T
        return 0
    lax.fori_loop(0, NPH, to_lanes, 0, unroll=8)


def _s5(u3, lam_re, lam_im, log_step, b_re, b_im, c_re, c_im, d, weights):
    assert NPH == 16 and S5_K == 16 and 2 * lam_re.shape[-1] == 128
    _, n_chunks, d_s5 = u3.shape
    raw = [lam_re, lam_im, log_step[..., None], b_re, b_im, c_re, c_im]
    dvec = jnp.tile(d.astype(F32), (1, NPH))[:, None, :]
    gpb = S5_LB // S5_K
    tk = NPH * S5_K
    n_grid = d_s5 // S5_LB
    n_steps = n_grid * (gpb // 2)
    chunk = lambda w: (w.shape[0] // n_steps, w.shape[1])
    assert all(w.shape[0] % (16 * n_steps) == 0 for w in weights)
    per_block = lambda n, *tail: pl.BlockSpec((n,) + tail, lambda b: (b,) + (0,) * len(tail))
    per_dir_block = lambda v: pl.BlockSpec((2, gpb) + v.shape[2:], lambda b: (0, b) + (0,) * (v.ndim - 2))
    hbm = pl.BlockSpec(memory_space=pl.ANY)
    outs = pl.pallas_call(
        functools.partial(_s5_kernel, len(weights), n_grid),
        grid=(n_grid,),
        in_specs=[pl.BlockSpec((NPH, n_chunks, S5_LB), lambda b: (0, 0, b))]
        + [per_dir_block(v) for v in raw] + [per_block(gpb, 1, tk)] + [hbm] * len(weights),
        out_specs=[pl.BlockSpec((NPH, n_chunks, S5_LB), lambda b: (0, 0, b))] + [hbm] * len(weights),
        out_shape=[jax.ShapeDtypeStruct((NPH, n_chunks, d_s5), F32)]
        + [jax.ShapeDtypeStruct(w.shape, BF16) for w in weights],
        scratch_shapes=[
            pltpu.VMEM((NPH, S5_LB, n_chunks), BF16),
            pltpu.VMEM((NPH, S5_LB, n_chunks), F32),
            pltpu.VMEM((gpb, tk, tk), BF16),
            pltpu.VMEM((gpb, 8 * S5_P, tk), BF16),
            pltpu.VMEM((gpb, tk, 8 * S5_P), BF16),
            pltpu.VMEM((gpb, 4 * 16, 2 * S5_P), F32),
        ] + [pltpu.VMEM((2,) + chunk(w), F32) for w in weights]
        + [pltpu.VMEM((2,) + chunk(w), BF16) for w in weights]
        + [pltpu.SemaphoreType.DMA((len(weights), 2, 2))],
        compiler_params=_cparams(1),
        name="s5",
    )(u3, *raw, dvec, *weights)
    return outs[0], outs[1:]


def _s5_discretise(g, lam_re_ref, lam_im_ref, log_step_ref, b_re_ref, b_im_ref, c_re_ref, c_im_ref):
    def cmul(x, y):
        return x[0] * y[0] - x[1] * y[1], x[0] * y[1] + x[1] * y[0]

    def rows(zs):
        return jnp.concatenate([z[0] for z in zs], axis=0), jnp.concatenate([z[1] for z in zs], axis=0)

    def one_direction(d):
        lre, lim = jnp.minimum(lam_re_ref[d, g:g + 1, :], -1e-4), lam_im_ref[d, g:g + 1, :]
        step = jnp.exp(log_step_ref[d, g:g + 1, :])
        are, aim = lre * step, lim * step
        mag = jnp.exp(are)
        lam_bar = (mag * jnp.cos(aim), mag * jnp.sin(aim))
        powers = [(jnp.ones_like(are), jnp.zeros_like(are))]
        for _ in range(NPH):
            powers.append(cmul(powers[-1], lam_bar))
        squares = [powers[NPH]]
        for _ in range(15):
            squares.append(cmul(squares[-1], squares[-1]))
        nr, ni = lam_bar[0] - 1.0, lam_bar[1]
        den = lre * lre + lim * lim
        z = ((nr * lre + ni * lim) / den, (ni * lre - nr * lim) / den)
        bbar = cmul(z, (b_re_ref[d, g].T, b_im_ref[d, g].T))
        c = (c_re_ref[d, g], c_im_ref[d, g])
        return powers, squares, bbar, c

    pf, qf, bbar_f, c_f = one_direction(0)
    pb, qb, bbar_b, c_b = one_direction(1)
    par = jnp.concatenate(
        list(rows(pf[NPH - 1::-1])) + list(rows(pb[:NPH])) + list(bbar_f) + list(bbar_b)
        + list(rows(pf[1:])) + list(rows(pb[NPH:0:-1])) + list(c_f) + list(c_b), axis=0)
    q_tab = jnp.concatenate(list(rows(qf)) + list(rows(qb)), axis=0)
    both_halves = lambda v: jnp.concatenate([v, v], axis=1)
    return both_halves(par), both_halves(q_tab)


def _row_scan_carry(a, h, reverse):
    n, w = a.shape
    row = lax.broadcasted_iota(jnp.int32, (n, w), 0)

    def shift(v, k, fill):
        if k % 8 == 0:
            z = jnp.full((k, w), fill, v.dtype)
            if reverse:
                return jnp.concatenate([v[k:], z], axis=0)
            return jnp.concatenate([z, v[:n - k]], axis=0)
        if reverse:
            return jnp.where(row < n - k, pltpu.roll(v, n - k, axis=0), fill)
        return jnp.where(row >= k, pltpu.roll(v, k, axis=0), fill)

    k = 1
    while k < n:
        h = h + a * shift(h, k, 0.0)
        a = a * shift(a, k, 1.0)
        k *= 2
    return shift(h, 1, 0.0)


def _rglru_kernel(x_ref, cw_ref, cb_ref, wa_ref, wx_ref, ba_ref, bx_ref, lam_ref, y_ref, xe_ref, al_ref, hl_ref):
    n, w = x_ref.shape[1], x_ref.shape[2]

    def block_diag(heads):
        per, hd, _ = heads.shape
        zeros = jnp.zeros((hd, hd), heads.dtype)
        return jnp.concatenate(
            [jnp.concatenate([heads[p] if q == p else zeros for q in range(per)], axis=1) for p in range(per)],
            axis=0)
    row = lax.broadcasted_iota(jnp.int32, (n, w), 0)

    def from_prev_chunk(v):
        return jnp.where(row >= 1, pltpu.roll(v, 1, axis=0), 0.0)

    def from_next_chunk(v):
        return jnp.where(row < n - 1, pltpu.roll(v, n - 1, axis=0), 0.0)

    xe_ref[0] = from_prev_chunk(x_ref[NPH - 2])
    xe_ref[1] = from_prev_chunk(x_ref[NPH - 1])

    def copy_body(s, _):
        xe_ref[s + 2] = x_ref[s]
        return 0
    lax.fori_loop(0, NPH, copy_body, 0, unroll=4)
    xe_ref[NPH + 2] = from_next_chunk(x_ref[0])

    cw = cw_ref[...]
    cb = cb_ref[...]
    lam = lam_ref[...]
    neg = -lam
    softplus = jnp.maximum(neg, 0.0) + jnp.log(1.0 + jnp.exp(-jnp.abs(neg)))

    for d, reverse in ((0, False), (1, True)):
        rate2 = (-RG_C * LOG2_E) * softplus[d:d + 1]
        wg = (-LOG2_E * jnp.concatenate([block_diag(wa_ref[d]), block_diag(wx_ref[d])], axis=1)).astype(BF16)
        bg = -LOG2_E * jnp.concatenate([ba_ref[d, 0], bx_ref[d, 0]], axis=1)
        init = NPH if reverse else 0
        al_ref[init] = jnp.ones((n, w), F32)
        hl_ref[init] = jnp.zeros((n, w), F32)

        def local_body(i, _):
            s = (NPH - 1 - i) if reverse else i
            xc = (cw[0:1] * xe_ref[s] + cw[1:2] * xe_ref[s + 1]
                  + cw[2:3] * xe_ref[s + 2] + cw[3:4] * xe_ref[s + 3] + cb)
            g = jnp.dot(xc.astype(BF16), wg, preferred_element_type=F32) + bg
            r = 1.0 / (1.0 + jnp.exp2(g[:, :w]))
            ig = 1.0 / (1.0 + jnp.exp2(g[:, w:]))
            a = jnp.exp2(rate2 * r)
            bt = jnp.sqrt(1.0 - a * a) * (ig * xc)
            src = (s + 1) if reverse else s
            dst = s if reverse else (s + 1)
            hl_ref[dst] = a * hl_ref[src] + bt
            al_ref[dst] = a * al_ref[src]
            return 0
        lax.fori_loop(0, NPH, local_body, 0, unroll=2)

        last = 0 if reverse else NPH
        carry = _row_scan_carry(al_ref[last], hl_ref[last], reverse)

        def fix_body(s, _):
            slot = s if reverse else (s + 1)
            v = hl_ref[slot] + al_ref[slot] * carry
            if reverse:
                y_ref[s] = y_ref[s] + v
            else:
                y_ref[s] = v
            return 0
        lax.fori_loop(0, NPH, fix_body, 0, unroll=4)


def _rglru(xrg3, conv_w, conv_b, wa, ba, wx, bx, lam):
    _, n_chunks, d_rg = xrg3.shape
    nb = d_rg // RG_CB
    n_dir, heads, hd, _ = wa.shape
    per = RG_CB // hd
    assert heads == nb * per
    gate_w = pl.BlockSpec((n_dir, per, hd, hd), lambda j: (0, j, 0, 0))
    gate_b = pl.BlockSpec((n_dir, 1, 1, RG_CB), lambda j: (0, j, 0, 0))
    bias4 = lambda b: b.reshape(n_dir, nb, 1, RG_CB)
    return pl.pallas_call(
        _rglru_kernel,
        grid=(nb,),
        in_specs=[
            pl.BlockSpec((NPH, n_chunks, RG_CB), lambda j: (0, 0, j)),
            pl.BlockSpec((CONV_W, RG_CB), lambda j: (0, j)),
            pl.BlockSpec((1, RG_CB), lambda j: (0, j)),
            gate_w, gate_w, gate_b, gate_b,
            pl.BlockSpec((2, RG_CB), lambda j: (0, j)),
        ],
        out_specs=pl.BlockSpec((NPH, n_chunks, RG_CB), lambda j: (0, 0, j)),
        out_shape=jax.ShapeDtypeStruct((NPH, n_chunks, d_rg), F32),
        scratch_shapes=[
            pltpu.VMEM((NPH + 3, n_chunks, RG_CB), F32),
            pltpu.VMEM((NPH + 1, n_chunks, RG_CB), F32),
            pltpu.VMEM((NPH + 1, n_chunks, RG_CB), F32),
        ],
        compiler_params=_cparams(1),
        name="rglru",
    )(xrg3, conv_w, conv_b, wa, wx, bias4(ba), bias4(bx), lam)


def _rms_norm(x, g):
    return x * lax.rsqrt(jnp.mean(x * x, axis=-1, keepdims=True) + LN_EPS) * g


def _mixout_kernel(alpha, ys5_ref, yrg_ref, gz_ref, h_ref, gw_ref, gb_ref, gn5_ref, gnr_ref,
                   w_ref, b_ref, g_ref, be_ref, o_ref):
    rows = NPH * CT
    d_s5, d_model = ys5_ref.shape[-1], o_ref.shape[-1]
    pieces = lambda width: [slice(c, c + 256) for c in range(0, width, 256)]

    y = ys5_ref[...].reshape(rows, d_s5)
    yb = y.astype(BF16)
    glu, sq = [], 0.0
    for cols in pieces(d_s5):
        g = jnp.dot(yb, gw_ref[:, cols], preferred_element_type=F32) + gb_ref[:, cols]
        o = y[:, cols] * _sigmoid(g)
        sq = sq + jnp.sum(o * o, axis=-1, keepdims=True)
        glu.append(o)
    scale = lax.rsqrt(sq * (1.0 / d_s5) + LN_EPS)
    ys5n = jnp.concatenate([o * scale * gn5_ref[:, cols] for o, cols in zip(glu, pieces(d_s5))],
                           axis=1).astype(BF16)

    yrg = yrg_ref[...].reshape(rows, -1) * gz_ref[...].reshape(rows, -1)
    yrgn = _rms_norm(yrg, gnr_ref[...]).astype(BF16)

    s1, s2 = 0.0, 0.0
    for cols in pieces(d_model):
        v = (jnp.dot(ys5n, w_ref[:d_s5, cols], preferred_element_type=F32)
             + jnp.dot(yrgn, w_ref[d_s5:, cols], preferred_element_type=F32)
             + b_ref[:, cols] + alpha * h_ref[:, :, cols].reshape(rows, 256))
        s1 = s1 + jnp.sum(v, axis=-1, keepdims=True)
        s2 = s2 + jnp.sum(v * v, axis=-1, keepdims=True)
        o_ref[:, :, cols] = v.reshape(NPH, CT, 256)
    mean = s1 * (1.0 / d_model)
    rstd = lax.rsqrt(s2 * (1.0 / d_model) - mean * mean + LN_EPS)
    for cols in pieces(d_model):
        v = o_ref[:, :, cols].reshape(rows, 256)
        o_ref[:, :, cols] = ((v - mean) * rstd * g_ref[:, cols] + be_ref[:, cols]).reshape(NPH, CT, 256)


def _mixout(alpha, ys5, yrg, gz, h, glu_w, glu_b, gn_s5, gn_rg, w_out, b_out, ln_g, ln_b):
    _, n_chunks, d_model = h.shape
    d_s5, d_rg = ys5.shape[-1], yrg.shape[-1]
    return pl.pallas_call(
        functools.partial(_mixout_kernel, alpha),
        grid=(n_chunks // CT,),
        in_specs=[
            _tile3(d_s5), _tile3(d_rg), _tile3(d_rg), _tile3(d_model),
            _resident((d_s5, d_s5)), _resident((1, d_s5)), _resident((1, d_s5)), _resident((1, d_rg)),
            _resident((d_s5 + d_rg, d_model)), _resident((1, d_model)),
            _resident((1, d_model)), _resident((1, d_model)),
        ],
        out_specs=_tile3(d_model),
        out_shape=jax.ShapeDtypeStruct((NPH, n_chunks, d_model), F32),
        compiler_params=_cparams(1),
        name="mixout",
    )(ys5, yrg, gz, h, glu_w, glu_b, gn_s5, gn_rg, w_out, b_out, ln_g, ln_b)


def _ffn_kernel(alpha, h_ref, p_ref, w1_ref, b1_ref, w2_ref, b2_ref, pw_ref, gw_ref, gb_ref,
                g_ref, be_ref, o_ref, hb_ref, acc_ref):
    j = pl.program_id(1)
    rows = NPH * CT

    @pl.when(j == 0)
    def _():
        h = h_ref[...].reshape(rows, -1)
        hb = h.astype(BF16)
        hb_ref[...] = hb
        gate = _sigmoid(jnp.dot(hb, gw_ref[...], preferred_element_type=F32) + gb_ref[...])
        pe = _to_phase_major(p_ref[...]).astype(BF16)
        acc_ref[...] = alpha * h + gate * jnp.dot(pe, pw_ref[...], preferred_element_type=F32) + b2_ref[...]

    a = jnp.dot(hb_ref[...], w1_ref[...], preferred_element_type=F32) + b1_ref[...]
    a = jnp.maximum(a, 0.0)
    acc_ref[...] += jnp.dot((a * a).astype(BF16), w2_ref[...], preferred_element_type=F32)

    @pl.when(j == pl.num_programs(1) - 1)
    def _():
        o_ref[...] = _to_time_major(_layer_norm(acc_ref[...], g_ref[...], be_ref[...]))


def _ffn(alpha, h1, p2, w1, b1, w2, b2, ple_w, gate_w, gate_b, ln_g, ln_b, ff_tile):
    _, n_chunks, d_model = h1.shape
    d_ff = w1.shape[1]
    ple_dim = ple_w.shape[0]
    rows = NPH * CT
    return pl.pallas_call(
        functools.partial(_ffn_kernel, alpha),
        grid=(n_chunks // CT, d_ff // ff_tile),
        in_specs=[
            _tile3(d_model),
            pl.BlockSpec((rows, ple_dim), lambda i, j: (i, 0)),
            pl.BlockSpec((d_model, ff_tile), lambda i, j: (0, j)),
            pl.BlockSpec((1, ff_tile), lambda i, j: (0, j)),
            pl.BlockSpec((ff_tile, d_model), lambda i, j: (j, 0)),
            _resident((1, d_model)),
            _resident((ple_dim, d_model)), _resident((d_model, d_model)), _resident((1, d_model)),
            _resident((1, d_model)), _resident((1, d_model)),
        ],
        out_specs=pl.BlockSpec((rows, d_model), lambda i, j: (i, 0)),
        out_shape=jax.ShapeDtypeStruct((NPH * n_chunks, d_model), F32),
        scratch_shapes=[pltpu.VMEM((rows, d_model), BF16), pltpu.VMEM((rows, d_model), F32)],
        compiler_params=_cparams(2),
        name="ffn",
    )(h1, p2, w1, b1, w2, b2, ple_w, gate_w, gate_b, ln_g, ln_b)


def kernel(x, p, ln_in_g, ln_in_b, w_in, b_in, s5_lambda_re, s5_lambda_im, s5_log_step, s5_b_re, s5_b_im, s5_c_re, s5_c_im, s5_d, s5_glu_w, s5_glu_b, rg_conv_w, rg_conv_b, rg_wa, rg_ba, rg_wx, rg_bx, rg_lambda, gn_s5, gn_rg, w_out, b_out, ln1_g, ln1_b, w_ff1, b_ff1, w_ff2, b_ff2, ple_w, ple_gate_w, ple_gate_b, ln2_g, ln2_b):
    batch, seq, d_model = x.shape
    depth = w_in.shape[0]
    assert batch == 1 and depth == 1 and seq % (NPH * CT) == 0
    n_chunks = seq // NPH
    d_s5 = s5_glu_w.shape[-1]
    d_rg = rg_conv_w.shape[-1]
    alpha = (2.0 * depth) ** 0.25
    row = lambda v: v.reshape(1, -1).astype(F32)

    h, u3, xrg, gz = _inproj(x.reshape(seq, d_model), row(ln_in_g), row(ln_in_b),
                              w_in[0].astype(BF16), row(b_in[0]), d_s5, d_rg)

    ys5, (w_out_b, gate_w_b, w_ff1_b, w_ff2_b) = _s5(
        u3, s5_lambda_re[0], s5_lambda_im[0], s5_log_step[0], s5_b_re[0], s5_b_im[0], s5_c_re[0], s5_c_im[0],
        s5_d[0], [w_out[0], ple_gate_w[0], w_ff1[0], w_ff2[0]])

    yrg = _rglru(xrg, rg_conv_w[0].astype(F32), row(rg_conv_b[0]), rg_wa[0], rg_ba[0], rg_wx[0], rg_bx[0],
                 rg_lambda[0].astype(F32))

    h1 = _mixout(alpha, ys5, yrg, gz, h, s5_glu_w[0].astype(BF16), row(s5_glu_b[0]), row(gn_s5[0]),
                 row(gn_rg[0]), w_out_b, row(b_out[0]), row(ln1_g[0]), row(ln1_b[0]))

    out = _ffn(alpha, h1, p.reshape(seq, p.shape[-1]), w_ff1_b, row(b_ff1[0]),
               w_ff2_b, row(b_ff2[0]), ple_w[0].astype(BF16), gate_w_b,
               row(ple_gate_b[0]), row(ln2_g[0]), row(ln2_b[0]), ff_tile=1024)
    return out.reshape(batch, seq, d_model).astype(x.dtype)
```

```python
import functools
import math

import jax
import jax.numpy as jnp
from jax import lax
from jax.experimental import pallas as pl
from jax.experimental.pallas import tpu as pltpu

F32 = jnp.float32
BF16 = jnp.bfloat16

NPH = 16
CT = 32
S5_K = 16
S5_P = 64
S5_LB = 128
RG_CB = 128
CONV_W = 4
RG_C = 8.0
LN_EPS = 1e-5
LOG2_E = math.log2(math.e)
VMEM_LIMIT_V7X = 56 * 1024 * 1024


def _cparams(n_axes):
    return pltpu.CompilerParams(
        dimension_semantics=("arbitrary",) * n_axes,
        vmem_limit_bytes=VMEM_LIMIT_V7X)


def _resident(shape):
    return pl.BlockSpec(shape, lambda *_: (0,) * len(shape), pipeline_mode=pl.Buffered(1))


def _tile3(width):
    return pl.BlockSpec((NPH, CT, width), lambda i, *_: (0, i, 0))


def _to_phase_major(x):
    w = x.shape[-1]
    return jnp.swapaxes(x.reshape(CT, NPH, w), 0, 1).reshape(NPH * CT, w)


def _to_time_major(x):
    w = x.shape[-1]
    return jnp.swapaxes(x.reshape(NPH, CT, w), 0, 1).reshape(CT * NPH, w)


def _layer_norm(x, g, b):
    mu = jnp.mean(x, axis=-1, keepdims=True)
    xc = x - mu
    var = jnp.mean(xc * xc, axis=-1, keepdims=True)
    return xc * lax.rsqrt(var + LN_EPS) * g + b


def _gelu(x):
    c = math.sqrt(2.0 / math.pi)
    return 0.5 * x * (1.0 + jnp.tanh(c * (x + 0.044715 * (x * x * x))))


def _sigmoid(x):
    return 1.0 / (1.0 + jnp.exp(-x))


def _inproj_kernel(x_ref, g_ref, b_ref, w_ref, bw_ref, h_ref, u_ref, xrg_ref, gz_ref):
    d_s5, d_rg = u_ref.shape[-1], xrg_ref.shape[-1]
    h = _layer_norm(_to_phase_major(x_ref[...]), g_ref[...], b_ref[...])
    h_ref[...] = h.reshape(h_ref.shape)
    proj = jnp.dot(h.astype(BF16), w_ref[...], preferred_element_type=F32) + bw_ref[...]
    u_ref[...] = proj[:, :d_s5].astype(BF16).reshape(u_ref.shape)
    xrg_ref[...] = proj[:, d_s5:d_s5 + d_rg].reshape(xrg_ref.shape)
    gz_ref[...] = _gelu(proj[:, d_s5 + d_rg:]).reshape(gz_ref.shape)


def _inproj(x2, ln_g, ln_b, w_in, b_in, d_s5, d_rg):
    seq, d_model = x2.shape
    n_chunks = seq // NPH
    return pl.pallas_call(
        _inproj_kernel,
        grid=(n_chunks // CT,),
        in_specs=[
            pl.BlockSpec((CT * NPH, d_model), lambda i: (i, 0)),
            _resident((1, d_model)), _resident((1, d_model)),
            _resident(w_in.shape), _resident(b_in.shape),
        ],
        out_specs=[_tile3(d_model), _tile3(d_s5), _tile3(d_rg), _tile3(d_rg)],
        out_shape=[
            jax.ShapeDtypeStruct((NPH, n_chunks, d_model), F32),
            jax.ShapeDtypeStruct((NPH, n_chunks, d_s5), BF16),
            jax.ShapeDtypeStruct((NPH, n_chunks, d_rg), F32),
            jax.ShapeDtypeStruct((NPH, n_chunks, d_rg), F32),
        ],
        compiler_params=_cparams(1),
        name="inproj",
    )(x2, ln_g, ln_b, w_in, b_in)


def _row_scan_exclusive(sre, sim, qre, qim, reverse):
    n, w = sre.shape
    row = lax.broadcasted_iota(jnp.int32, (n, w), 0)

    def shift(v, k):
        if k % 8 == 0:
            z = jnp.zeros((k, w), v.dtype)
            return jnp.concatenate([v[k:], z] if reverse else [z, v[:n - k]], axis=0)
        if reverse:
            return jnp.where(row < n - k, pltpu.roll(v, n - k, axis=0), 0.0)
        return jnp.where(row >= k, pltpu.roll(v, k, axis=0), 0.0)

    xre, xim = shift(sre, 1), shift(sim, 1)
    k, i = 1, 0
    while k < n:
        pr, pi = qre[i:i + 1], qim[i:i + 1]
        if k % 8 == 0:
            keep = slice(n - k, n) if reverse else slice(0, k)
            dst = slice(0, n - k) if reverse else slice(k, n)
            src = slice(k, n) if reverse else slice(0, n - k)
            sr, si = xre[src], xim[src]
            nre = xre[dst] + pr * sr - pi * si
            nim = xim[dst] + pr * si + pi * sr
            order = (lambda new, old: [new, old]) if reverse else (lambda new, old: [old, new])
            xre = jnp.concatenate(order(nre, xre[keep]), axis=0)
            xim = jnp.concatenate(order(nim, xim[keep]), axis=0)
        else:
            sr, si = shift(xre, k), shift(xim, k)
            xre, xim = xre + pr * sr - pi * si, xim + pr * si + pi * sr
        k *= 2
        i += 1
    return xre, xim


def _dot(a, b):
    return jnp.dot(a.astype(BF16), b.astype(BF16), preferred_element_type=F32)


def _s5_chunk_operators(par, dvec, half):
    tk = NPH * S5_K
    lane = lax.broadcasted_iota(jnp.int32, par.shape, 1)
    par = jnp.where(lax.shift_right_logical(lane, 6) == half, par, 0.0)
    pa = par[0:128].T
    pb = par[128:256]
    lane16 = lax.broadcasted_iota(jnp.int32, (16, tk), 1)
    row16 = lax.broadcasted_iota(jnp.int32, (16, tk), 0)
    rep = (lax.shift_right_logical(lane16, 4) == row16).astype(F32)
    til = ((lane16 & 15) == row16).astype(F32)

    def cmul(ar, ai, br, bi):
        return ar * br - ai * bi, ar * bi + ai * br

    def c_of(r0):
        c_re, c_im = pb[r0 + 64:r0 + 80], pb[r0 + 80:r0 + 96]
        blocks = [cmul(c_re, c_im, pb[r0 + t:r0 + t + 1], pb[r0 + 16 + t:r0 + 17 + t]) for t in range(NPH)]
        return (jnp.concatenate([b[0] for b in blocks], axis=0),
                -jnp.concatenate([b[1] for b in blocks], axis=0))

    cc = jnp.concatenate(list(c_of(0)) + list(c_of(32)), axis=1)
    yield

    spread = {c0: (_dot(pa[:, c0:c0 + 16], rep), _dot(pa[:, c0 + 16:c0 + 32], rep),
                   _dot(pa[:, c0 + 64:c0 + 80], til), _dot(pa[:, c0 + 80:c0 + 96], til)) for c0 in (0, 32)}
    yield

    wf_re, wf_im = cmul(*spread[0])
    wb_re, wb_im = cmul(*spread[32])
    bc = jnp.concatenate([wf_re, wf_im, wb_re, wb_im], axis=0)
    yield

    kf = _dot(pb[64:80], wf_re) - _dot(pb[80:96], wf_im)
    kb = _dot(pb[96:112], wb_re) - _dot(pb[112:128], wb_im)
    yield

    blocks = []
    for t in range(NPH):
        left = S5_K * (NPH - 1 - t)
        right = S5_K * t
        f = kf if left == 0 else jnp.where(lane16 < tk - left, pltpu.roll(kf, tk - left, axis=1), 0.0)
        b = kb if right == 0 else jnp.where(lane16 >= right, pltpu.roll(kb, right, axis=1), 0.0)
        blocks.append(f + b)
    a = jnp.concatenate(blocks, axis=0)
    ri = lax.broadcasted_iota(jnp.int32, (tk, tk), 0)
    ci = lax.broadcasted_iota(jnp.int32, (tk, tk), 1)
    a = a + jnp.where(ri == ci, dvec, 0.0)
    return a.astype(BF16), bc.astype(BF16), cc.astype(BF16)


def _in_lockstep(generators):
    results = [None] * len(generators)
    pending = list(enumerate(generators))
    while pending:
        still = []
        for i, gen in pending:
            try:
                next(gen)
                still.append((i, gen))
            except StopIteration as stop:
                results[i] = stop.value
        pending = still
    return results


def _cast_stream_step(k, n_steps, srcs, dsts, inbufs, outbufs, sems):
    assert n_steps >= 2
    slot = lax.rem(k, 2)
    n_streams = len(srcs)
    rows = [src.shape[0] // n_steps for src in srcs]

    def read(i, step, sl):
        return pltpu.make_async_copy(srcs[i].at[pl.ds(step * rows[i], rows[i]), :], inbufs[i].at[sl],
                                     sems.at[i, 0, sl])

    def write(i, step, sl):
        return pltpu.make_async_copy(outbufs[i].at[sl], dsts[i].at[pl.ds(step * rows[i], rows[i]), :],
                                     sems.at[i, 1, sl])

    @pl.when(k == 0)
    def _():
        for i in range(n_streams):
            read(i, 0, 0).start()
            read(i, 1, 1).start()

    @pl.when(k >= 2)
    def _():
        for i in range(n_streams):
            write(i, k - 2, slot).wait()

    for i in range(n_streams):
        read(i, k, slot).wait()
        outbufs[i][slot] = inbufs[i][slot].astype(BF16)

    @pl.when(k + 2 < n_steps)
    def _():
        for i in range(n_streams):
            read(i, k + 2, slot).start()

    for i in range(n_streams):
        write(i, k, slot).start()

    @pl.when(k == n_steps - 1)
    def _():
        for i in range(n_streams):
            write(i, k - 1, 1 - slot).wait()
            write(i, k, slot).wait()


def _s5_kernel(n_w, n_grid, u_ref, *rest):
    raw_refs, d_ref, rest = rest[:7], rest[7], rest[8:]
    w_src, rest = rest[:n_w], rest[n_w:]
    y_ref, rest = rest[0], rest[1:]
    w_dst, rest = rest[:n_w], rest[n_w:]
    xt_ref, yt_ref, a_ref, bc_ref, cc_ref, q_ref, rest = rest[:6] + (rest[6:],)
    w_in_buf, w_out_buf, w_sems = rest[:n_w], rest[n_w:2 * n_w], rest[2 * n_w]
    n = u_ref.shape[1]
    p2 = 2 * S5_P
    tk = NPH * S5_K
    n_groups = S5_LB // S5_K
    n_pairs = n_groups // 2

    def to_rows(s, _):
        xt_ref[s] = u_ref[s].T
        return 0
    lax.fori_loop(0, NPH, to_rows, 0, unroll=8)

    pars = []
    for g in range(n_groups):
        par, q_ref[g] = _s5_discretise(g, *raw_refs)
        pars.append(par)
    stages = [_s5_chunk_operators(pars[g], d_ref[g], g % 2) for g in range(n_groups)]
    for g, (a, bc, cc) in enumerate(_in_lockstep(stages)):
        a_ref[g], bc_ref[g], cc_ref[g] = a, bc, cc

    def pair(gp):
        xs, ys = [], []
        for h in range(2):
            g = 2 * gp + h
            rows = pl.ds(pl.multiple_of(g * S5_K, S5_K), S5_K)
            x = xt_ref[:, rows, :].reshape(tk, n)
            xs.append(x)
            ys.append(jnp.dot(a_ref[g], x, preferred_element_type=F32))
        bc = jnp.concatenate([bc_ref[2 * gp], bc_ref[2 * gp + 1]], axis=1)
        s = jnp.dot(bc, jnp.concatenate(xs, axis=0), preferred_element_type=F32)
        yield
        st = [s[i * p2:(i + 1) * p2].T for i in range(4)]
        lane = lax.broadcasted_iota(jnp.int32, (4 * 16, p2), 1)
        q = jnp.where(lane < S5_P, q_ref[2 * gp], q_ref[2 * gp + 1])
        yield
        hf_re, hf_im = _row_scan_exclusive(st[0], st[1], q[0:16], q[16:32], False)
        hb_re, hb_im = _row_scan_exclusive(st[2], st[3], q[32:48], q[48:64], True)
        hin = jnp.concatenate([hf_re, hf_im, hb_re, hb_im], axis=1).astype(BF16)
        yield
        for h in range(2):
            g = 2 * gp + h
            rows = pl.ds(pl.multiple_of(g * S5_K, S5_K), S5_K)
            y = ys[h] + lax.dot_general(cc_ref[g], hin, (((1,), (1,)), ((), ())), preferred_element_type=F32)
            yt_ref[:, rows, :] = _gelu(y).reshape(NPH, S5_K, n)

    def two_pairs(it, _):
        first = pl.program_id(0) * n_pairs + 2 * it
        stream = functools.partial(_cast_stream_step, n_steps=n_grid * n_pairs, srcs=w_src, dsts=w_dst,
                                   inbufs=w_in_buf, outbufs=w_out_buf, sems=w_sems)
        stream(first)
        _in_lockstep([pair(2 * it), pair(2 * it + 1)])
        stream(first + 1)
        return 0
    lax.fori_loop(0, n_pairs // 2, two_pairs, 0)

    def to_lanes(t, _):
        y_ref[t] = yt_ref[t].T
        return 0
    lax.fori_loop(0, NPH, to_lanes, 0, unroll=8)


def _s5(u3, lam_re, lam_im, log_step, b_re, b_im, c_re, c_im, d, weights):
    assert NPH == 16 and S5_K == 16 and 2 * lam_re.shape[-1] == 128
    _, n_chunks, d_s5 = u3.shape
    raw = [lam_re, lam_im, log_step[..., None], b_re, b_im, c_re, c_im]
    dvec = jnp.tile(d.astype(F32), (1, NPH))[:, None, :]
    gpb = S5_LB // S5_K
    tk = NPH * S5_K
    n_grid = d_s5 // S5_LB
    n_steps = n_grid * (gpb // 2)
    chunk = lambda w: (w.shape[0] // n_steps, w.shape[1])
    assert all(w.shape[0] % (16 * n_steps) == 0 for w in weights)
    per_block = lambda n, *tail: pl.BlockSpec((n,) + tail, lambda b: (b,) + (0,) * len(tail))
    per_dir_block = lambda v: pl.BlockSpec((2, gpb) + v.shape[2:], lambda b: (0, b) + (0,) * (v.ndim - 2))
    hbm = pl.BlockSpec(memory_space=pl.ANY)
    outs = pl.pallas_call(
        functools.partial(_s5_kernel, len(weights), n_grid),
        grid=(n_grid,),
        in_specs=[pl.BlockSpec((NPH, n_chunks, S5_LB), lambda b: (0, 0, b))]
        + [per_dir_block(v) for v in raw] + [per_block(gpb, 1, tk)] + [hbm] * len(weights),
        out_specs=[pl.BlockSpec((NPH, n_chunks, S5_LB), lambda b: (0, 0, b))] + [hbm] * len(weights),
        out_shape=[jax.ShapeDtypeStruct((NPH, n_chunks, d_s5), F32)]
        + [jax.ShapeDtypeStruct(w.shape, BF16) for w in weights],
        scratch_shapes=[
            pltpu.VMEM((NPH, S5_LB, n_chunks), BF16),
            pltpu.VMEM((NPH, S5_LB, n_chunks), F32),
            pltpu.VMEM((gpb, tk, tk), BF16),
            pltpu.VMEM((gpb, 8 * S5_P, tk), BF16),
            pltpu.VMEM((gpb, tk, 8 * S5_P), BF16),
            pltpu.VMEM((gpb, 4 * 16, 2 * S5_P), F32),
        ] + [pltpu.VMEM((2,) + chunk(w), F32) for w in weights]
        + [pltpu.VMEM((2,) + chunk(w), BF16) for w in weights]
        + [pltpu.SemaphoreType.DMA((len(weights), 2, 2))],
        compiler_params=_cparams(1),
        name="s5",
    )(u3, *raw, dvec, *weights)
    return outs[0], outs[1:]


def _s5_discretise(g, lam_re_ref, lam_im_ref, log_step_ref, b_re_ref, b_im_ref, c_re_ref, c_im_ref):
    def cmul(x, y):
        return x[0] * y[0] - x[1] * y[1], x[0] * y[1] + x[1] * y[0]

    def rows(zs):
        return jnp.concatenate([z[0] for z in zs], axis=0), jnp.concatenate([z[1] for z in zs], axis=0)

    def one_direction(d):
        lre, lim = jnp.minimum(lam_re_ref[d, g:g + 1, :], -1e-4), lam_im_ref[d, g:g + 1, :]
        step = jnp.exp(log_step_ref[d, g:g + 1, :])
        are, aim = lre * step, lim * step
        mag = jnp.exp(are)
        lam_bar = (mag * jnp.cos(aim), mag * jnp.sin(aim))
        powers = [(jnp.ones_like(are), jnp.zeros_like(are))]
        for _ in range(NPH):
            powers.append(cmul(powers[-1], lam_bar))
        squares = [powers[NPH]]
        for _ in range(15):
            squares.append(cmul(squares[-1], squares[-1]))
        nr, ni = lam_bar[0] - 1.0, lam_bar[1]
        den = lre * lre + lim * lim
        z = ((nr * lre + ni * lim) / den, (ni * lre - nr * lim) / den)
        bbar = cmul(z, (b_re_ref[d, g].T, b_im_ref[d, g].T))
        c = (c_re_ref[d, g], c_im_ref[d, g])
        return powers, squares, bbar, c

    pf, qf, bbar_f, c_f = one_direction(0)
    pb, qb, bbar_b, c_b = one_direction(1)
    par = jnp.concatenate(
        list(rows(pf[NPH - 1::-1])) + list(rows(pb[:NPH])) + list(bbar_f) + list(bbar_b)
        + list(rows(pf[1:])) + list(rows(pb[NPH:0:-1])) + list(c_f) + list(c_b), axis=0)
    q_tab = jnp.concatenate(list(rows(qf)) + list(rows(qb)), axis=0)
    both_halves = lambda v: jnp.concatenate([v, v], axis=1)
    return both_halves(par), both_halves(q_tab)


def _row_scan_carry(a, h, reverse):
    n, w = a.shape
    row = lax.broadcasted_iota(jnp.int32, (n, w), 0)

    def shift(v, k, fill):
        if k % 8 == 0:
            z = jnp.full((k, w), fill, v.dtype)
            if reverse:
                return jnp.concatenate([v[k:], z], axis=0)
            return jnp.concatenate([z, v[:n - k]], axis=0)
        if reverse:
            return jnp.where(row < n - k, pltpu.roll(v, n - k, axis=0), fill)
        return jnp.where(row >= k, pltpu.roll(v, k, axis=0), fill)

    k = 1
    while k < n:
        h = h + a * shift(h, k, 0.0)
        a = a * shift(a, k, 1.0)
        k *= 2
    return shift(h, 1, 0.0)


def _rglru_kernel(x_ref, cw_ref, cb_ref, wa_ref, wx_ref, ba_ref, bx_ref, lam_ref, y_ref, xe_ref, al_ref, hl_ref):
    n, w = x_ref.shape[1], x_ref.shape[2]

    def block_diag(heads):
        per, hd, _ = heads.shape
        zeros = jnp.zeros((hd, hd), heads.dtype)
        return jnp.concatenate(
            [jnp.concatenate([heads[p] if q == p else zeros for q in range(per)], axis=1) for p in range(per)],
            axis=0)
    row = lax.broadcasted_iota(jnp.int32, (n, w), 0)

    def from_prev_chunk(v):
        return jnp.where(row >= 1, pltpu.roll(v, 1, axis=0), 0.0)

    def from_next_chunk(v):
        return jnp.where(row < n - 1, pltpu.roll(v, n - 1, axis=0), 0.0)

    xe_ref[0] = from_prev_chunk(x_ref[NPH - 2])
    xe_ref[1] = from_prev_chunk(x_ref[NPH - 1])

    def copy_body(s, _):
        xe_ref[s + 2] = x_ref[s]
        return 0
    lax.fori_loop(0, NPH, copy_body, 0, unroll=4)
    xe_ref[NPH + 2] = from_next_chunk(x_ref[0])

    cw = cw_ref[...]
    cb = cb_ref[...]
    lam = lam_ref[...]
    neg = -lam
    softplus = jnp.maximum(neg, 0.0) + jnp.log(1.0 + jnp.exp(-jnp.abs(neg)))

    for d, reverse in ((0, False), (1, True)):
        rate2 = (-RG_C * LOG2_E) * softplus[d:d + 1]
        wg = (-LOG2_E * jnp.concatenate([block_diag(wa_ref[d]), block_diag(wx_ref[d])], axis=1)).astype(BF16)
        bg = -LOG2_E * jnp.concatenate([ba_ref[d, 0], bx_ref[d, 0]], axis=1)
        init = NPH if reverse else 0
        al_ref[init] = jnp.ones((n, w), F32)
        hl_ref[init] = jnp.zeros((n, w), F32)

        def local_body(i, _):
            s = (NPH - 1 - i) if reverse else i
            xc = (cw[0:1] * xe_ref[s] + cw[1:2] * xe_ref[s + 1]
                  + cw[2:3] * xe_ref[s + 2] + cw[3:4] * xe_ref[s + 3] + cb)
            g = jnp.dot(xc.astype(BF16), wg, preferred_element_type=F32) + bg
            r = 1.0 / (1.0 + jnp.exp2(g[:, :w]))
            ig = 1.0 / (1.0 + jnp.exp2(g[:, w:]))
            a = jnp.exp2(rate2 * r)
            bt = jnp.sqrt(1.0 - a * a) * (ig * xc)
            src = (s + 1) if reverse else s
            dst = s if reverse else (s + 1)
            hl_ref[dst] = a * hl_ref[src] + bt
            al_ref[dst] = a * al_ref[src]
            return 0
        lax.fori_loop(0, NPH, local_body, 0, unroll=2)

        last = 0 if reverse else NPH
        carry = _row_scan_carry(al_ref[last], hl_ref[last], reverse)

        def fix_body(s, _):
            slot = s if reverse else (s + 1)
            v = hl_ref[slot] + al_ref[slot] * carry
            if reverse:
                y_ref[s] = y_ref[s] + v
            else:
                y_ref[s] = v
            return 0
        lax.fori_loop(0, NPH, fix_body, 0, unroll=4)


def _rglru(xrg3, conv_w, conv_b, wa, ba, wx, bx, lam):
    _, n_chunks, d_rg = xrg3.shape
    nb = d_rg // RG_CB
    n_dir, heads, hd, _ = wa.shape
    per = RG_CB // hd
    assert heads == nb * per
    gate_w = pl.BlockSpec((n_dir, per, hd, hd), lambda j: (0, j, 0, 0))
    gate_b = pl.BlockSpec((n_dir, 1, 1, RG_CB), lambda j: (0, j, 0, 0))
    bias4 = lambda b: b.reshape(n_dir, nb, 1, RG_CB)
    return pl.pallas_call(
        _rglru_kernel,
        grid=(nb,),
        in_specs=[
            pl.BlockSpec((NPH, n_chunks, RG_CB), lambda j: (0, 0, j)),
            pl.BlockSpec((CONV_W, RG_CB), lambda j: (0, j)),
            pl.BlockSpec((1, RG_CB), lambda j: (0, j)),
            gate_w, gate_w, gate_b, gate_b,
            pl.BlockSpec((2, RG_CB), lambda j: (0, j)),
        ],
        out_specs=pl.BlockSpec((NPH, n_chunks, RG_CB), lambda j: (0, 0, j)),
        out_shape=jax.ShapeDtypeStruct((NPH, n_chunks, d_rg), F32),
        scratch_shapes=[
            pltpu.VMEM((NPH + 3, n_chunks, RG_CB), F32),
            pltpu.VMEM((NPH + 1, n_chunks, RG_CB), F32),
            pltpu.VMEM((NPH + 1, n_chunks, RG_CB), F32),
        ],
        compiler_params=_cparams(1),
        name="rglru",
    )(xrg3, conv_w, conv_b, wa, wx, bias4(ba), bias4(bx), lam)


def _rms_norm(x, g):
    return x * lax.rsqrt(jnp.mean(x * x, axis=-1, keepdims=True) + LN_EPS) * g


def _mixout_kernel(alpha, ys5_ref, yrg_ref, gz_ref, h_ref, gw_ref, gb_ref, gn5_ref, gnr_ref,
                   w_ref, b_ref, g_ref, be_ref, o_ref):
    rows = NPH * CT
    d_s5, d_model = ys5_ref.shape[-1], o_ref.shape[-1]
    pieces = lambda width: [slice(c, c + 256) for c in range(0, width, 256)]

    y = ys5_ref[...].reshape(rows, d_s5)
    yb = y.astype(BF16)
    glu, sq = [], 0.0
    for cols in pieces(d_s5):
        g = jnp.dot(yb, gw_ref[:, cols], preferred_element_type=F32) + gb_ref[:, cols]
        o = y[:, cols] * _sigmoid(g)
        sq = sq + jnp.sum(o * o, axis=-1, keepdims=True)
        glu.append(o)
    scale = lax.rsqrt(sq * (1.0 / d_s5) + LN_EPS)
    ys5n = jnp.concatenate([o * scale * gn5_ref[:, cols] for o, cols in zip(glu, pieces(d_s5))],
                           axis=1).astype(BF16)

    yrg = yrg_ref[...].reshape(rows, -1) * gz_ref[...].reshape(rows, -1)
    yrgn = _rms_norm(yrg, gnr_ref[...]).astype(BF16)

    s1, s2 = 0.0, 0.0
    for cols in pieces(d_model):
        v = (jnp.dot(ys5n, w_ref[:d_s5, cols], preferred_element_type=F32)
             + jnp.dot(yrgn, w_ref[d_s5:, cols], preferred_element_type=F32)
             + b_ref[:, cols] + alpha * h_ref[:, :, cols].reshape(rows, 256))
        s1 = s1 + jnp.sum(v, axis=-1, keepdims=True)
        s2 = s2 + jnp.sum(v * v, axis=-1, keepdims=True)
        o_ref[:, :, cols] = v.reshape(NPH, CT, 256)
    mean = s1 * (1.0 / d_model)
    rstd = lax.rsqrt(s2 * (1.0 / d_model) - mean * mean + LN_EPS)
    for cols in pieces(d_model):
        v = o_ref[:, :, cols].reshape(rows, 256)
        o_ref[:, :, cols] = ((v - mean) * rstd * g_ref[:, cols] + be_ref[:, cols]).reshape(NPH, CT, 256)


def _mixout(alpha, ys5, yrg, gz, h, glu_w, glu_b, gn_s5, gn_rg, w_out, b_out, ln_g, ln_b):
    _, n_chunks, d_model = h.shape
    d_s5, d_rg = ys5.shape[-1], yrg.shape[-1]
    return pl.pallas_call(
        functools.partial(_mixout_kernel, alpha),
        grid=(n_chunks // CT,),
        in_specs=[
            _tile3(d_s5), _tile3(d_rg), _tile3(d_rg), _tile3(d_model),
            _resident((d_s5, d_s5)), _resident((1, d_s5)), _resident((1, d_s5)), _resident((1, d_rg)),
            _resident((d_s5 + d_rg, d_model)), _resident((1, d_model)),
            _resident((1, d_model)), _resident((1, d_model)),
        ],
        out_specs=_tile3(d_model),
        out_shape=jax.ShapeDtypeStruct((NPH, n_chunks, d_model), F32),
        compiler_params=_cparams(1),
        name="mixout",
    )(ys5, yrg, gz, h, glu_w, glu_b, gn_s5, gn_rg, w_out, b_out, ln_g, ln_b)


def _ffn_kernel(alpha, h_ref, p_ref, w1_ref, b1_ref, w2_ref, b2_ref, pw_ref, gw_ref, gb_ref,
                g_ref, be_ref, o_ref, hb_ref, acc_ref):
    j = pl.program_id(1)
    rows = NPH * CT

    @pl.when(j == 0)
    def _():
        h = h_ref[...].reshape(rows, -1)
        hb = h.astype(BF16)
        hb_ref[...] = hb
        gate = _sigmoid(jnp.dot(hb, gw_ref[...], preferred_element_type=F32) + gb_ref[...])
        pe = _to_phase_major(p_ref[...]).astype(BF16)
        acc_ref[...] = alpha * h + gate * jnp.dot(pe, pw_ref[...], preferred_element_type=F32) + b2_ref[...]

    a = jnp.dot(hb_ref[...], w1_ref[...], preferred_element_type=F32) + b1_ref[...]
    a = jnp.maximum(a, 0.0)
    acc_ref[...] += jnp.dot((a * a).astype(BF16), w2_ref[...], preferred_element_type=F32)

    @pl.when(j == pl.num_programs(1) - 1)
    def _():
        o_ref[...] = _to_time_major(_layer_norm(acc_ref[...], g_ref[...], be_ref[...]))


def _ffn(alpha, h1, p2, w1, b1, w2, b2, ple_w, gate_w, gate_b, ln_g, ln_b, ff_tile):
    _, n_chunks, d_model = h1.shape
    d_ff = w1.shape[1]
    ple_dim = ple_w.shape[0]
    rows = NPH * CT
    return pl.pallas_call(
        functools.partial(_ffn_kernel, alpha),
        grid=(n_chunks // CT, d_ff // ff_tile),
        in_specs=[
            _tile3(d_model),
            pl.BlockSpec((rows, ple_dim), lambda i, j: (i, 0)),
            pl.BlockSpec((d_model, ff_tile), lambda i, j: (0, j)),
            pl.BlockSpec((1, ff_tile), lambda i, j: (0, j)),
            pl.BlockSpec((ff_tile, d_model), lambda i, j: (j, 0)),
            _resident((1, d_model)),
            _resident((ple_dim, d_model)), _resident((d_model, d_model)), _resident((1, d_model)),
            _resident((1, d_model)), _resident((1, d_model)),
        ],
        out_specs=pl.BlockSpec((rows, d_model), lambda i, j: (i, 0)),
        out_shape=jax.ShapeDtypeStruct((NPH * n_chunks, d_model), F32),
        scratch_shapes=[pltpu.VMEM((rows, d_model), BF16), pltpu.VMEM((rows, d_model), F32)],
        compiler_params=_cparams(2),
        name="ffn",
    )(h1, p2, w1, b1, w2, b2, ple_w, gate_w, gate_b, ln_g, ln_b)


def kernel(x, p, ln_in_g, ln_in_b, w_in, b_in, s5_lambda_re, s5_lambda_im, s5_log_step, s5_b_re, s5_b_im, s5_c_re, s5_c_im, s5_d, s5_glu_w, s5_glu_b, rg_conv_w, rg_conv_b, rg_wa, rg_ba, rg_wx, rg_bx, rg_lambda, gn_s5, gn_rg, w_out, b_out, ln1_g, ln1_b, w_ff1, b_ff1, w_ff2, b_ff2, ple_w, ple_gate_w, ple_gate_b, ln2_g, ln2_b):
    batch, seq, d_model = x.shape
    depth = w_in.shape[0]
    assert batch == 1 and depth == 1 and seq % (NPH * CT) == 0
    n_chunks = seq // NPH
    d_s5 = s5_glu_w.shape[-1]
    d_rg = rg_conv_w.shape[-1]
    alpha = (2.0 * depth) ** 0.25
    row = lambda v: v.reshape(1, -1).astype(F32)

    h, u3, xrg, gz = _inproj(x.reshape(seq, d_model), row(ln_in_g), row(ln_in_b),
                              w_in[0].astype(BF16), row(b_in[0]), d_s5, d_rg)

    ys5, (w_out_b, gate_w_b, w_ff1_b, w_ff2_b) = _s5(
        u3, s5_lambda_re[0], s5_lambda_im[0], s5_log_step[0], s5_b_re[0], s5_b_im[0], s5_c_re[0], s5_c_im[0],
        s5_d[0], [w_out[0], ple_gate_w[0], w_ff1[0], w_ff2[0]])

    yrg = _rglru(xrg, rg_conv_w[0].astype(F32), row(rg_conv_b[0]), rg_wa[0], rg_ba[0], rg_wx[0], rg_bx[0],
                 rg_lambda[0].astype(F32))

    h1 = _mixout(alpha, ys5, yrg, gz, h, s5_glu_w[0].astype(BF16), row(s5_glu_b[0]), row(gn_s5[0]),
                 row(gn_rg[0]), w_out_b, row(b_out[0]), row(ln1_g[0]), row(ln1_b[0]))

    out = _ffn(alpha, h1, p.reshape(seq, p.shape[-1]), w_ff1_b, row(b_ff1[0]),
               w_ff2_b, row(b_ff2[0]), ple_w[0].astype(BF16), gate_w_b,
               row(ple_gate_b[0]), row(ln2_g[0]), row(ln2_b[0]), ff_tile=1024)
    return out.reshape(batch, seq, d_model).astype(x.dtype)
```

```python
import functools
import math

import jax
import jax.numpy as jnp
from jax import lax
from jax.experimental import pallas as pl
from jax.experimental.pallas import tpu as pltpu

F32 = jnp.float32
BF16 = jnp.bfloat16

NPH = 16
CT = 32
S5_K = 16
S5_P = 64
S5_LB = 128
RG_CB = 128
CONV_W = 4
RG_C = 8.0
LN_EPS = 1e-5
LOG2_E = math.log2(math.e)
VMEM_LIMIT_V7X = 56 * 1024 * 1024


def _cparams(n_axes):
    return pltpu.CompilerParams(
        dimension_semantics=("arbitrary",) * n_axes,
        vmem_limit_bytes=VMEM_LIMIT_V7X)


def _resident(shape):
    return pl.BlockSpec(shape, lambda *_: (0,) * len(shape), pipeline_mode=pl.Buffered(1))


def _tile3(width):
    return pl.BlockSpec((NPH, CT, width), lambda i, *_: (0, i, 0))


def _to_phase_major(x):
    w = x.shape[-1]
    return jnp.swapaxes(x.reshape(CT, NPH, w), 0, 1).reshape(NPH * CT, w)


def _to_time_major(x):
    w = x.shape[-1]
    return jnp.swapaxes(x.reshape(NPH, CT, w), 0, 1).reshape(CT * NPH, w)


def _layer_norm(x, g, b):
    mu = jnp.mean(x, axis=-1, keepdims=True)
    xc = x - mu
    var = jnp.mean(xc * xc, axis=-1, keepdims=True)
    return xc * lax.rsqrt(var + LN_EPS) * g + b


def _gelu(x):
    c = math.sqrt(2.0 / math.pi)
    return 0.5 * x * (1.0 + jnp.tanh(c * (x + 0.044715 * (x * x * x))))


def _sigmoid(x):
    return 1.0 / (1.0 + jnp.exp(-x))


def _inproj_kernel(x_ref, g_ref, b_ref, w_ref, bw_ref, h_ref, u_ref, xrg_ref, gz_ref):
    d_s5, d_rg = u_ref.shape[-1], xrg_ref.shape[-1]
    h = _layer_norm(_to_phase_major(x_ref[...]), g_ref[...], b_ref[...])
    h_ref[...] = h.reshape(h_ref.shape)
    proj = jnp.dot(h.astype(BF16), w_ref[...], preferred_element_type=F32) + bw_ref[...]
    u_ref[...] = proj[:, :d_s5].astype(BF16).reshape(u_ref.shape)
    xrg_ref[...] = proj[:, d_s5:d_s5 + d_rg].reshape(xrg_ref.shape)
    gz_ref[...] = _gelu(proj[:, d_s5 + d_rg:]).reshape(gz_ref.shape)


def _inproj(x2, ln_g, ln_b, w_in, b_in, d_s5, d_rg):
    seq, d_model = x2.shape
    n_chunks = seq // NPH
    return pl.pallas_call(
        _inproj_kernel,
        grid=(n_chunks // CT,),
        in_specs=[
            pl.BlockSpec((CT * NPH, d_model), lambda i: (i, 0)),
            _resident((1, d_model)), _resident((1, d_model)),
            _resident(w_in.shape), _resident(b_in.shape),
        ],
        out_specs=[_tile3(d_model), _tile3(d_s5), _tile3(d_rg), _tile3(d_rg)],
        out_shape=[
            jax.ShapeDtypeStruct((NPH, n_chunks, d_model), F32),
            jax.ShapeDtypeStruct((NPH, n_chunks, d_s5), BF16),
            jax.ShapeDtypeStruct((NPH, n_chunks, d_rg), F32),
            jax.ShapeDtypeStruct((NPH, n_chunks, d_rg), F32),
        ],
        compiler_params=_cparams(1),
        name="inproj",
    )(x2, ln_g, ln_b, w_in, b_in)


def _row_scan_exclusive(sre, sim, qre, qim, reverse):
    n, w = sre.shape
    row = lax.broadcasted_iota(jnp.int32, (n, w), 0)

    def shift(v, k):
        if k % 8 == 0:
            z = jnp.zeros((k, w), v.dtype)
            return jnp.concatenate([v[k:], z] if reverse else [z, v[:n - k]], axis=0)
        if reverse:
            return jnp.where(row < n - k, pltpu.roll(v, n - k, axis=0), 0.0)
        return jnp.where(row >= k, pltpu.roll(v, k, axis=0), 0.0)

    xre, xim = shift(sre, 1), shift(sim, 1)
    k, i = 1, 0
    while k < n:
        pr, pi = qre[i:i + 1], qim[i:i + 1]
        if k % 8 == 0:
            keep = slice(n - k, n) if reverse else slice(0, k)
            dst = slice(0, n - k) if reverse else slice(k, n)
            src = slice(k, n) if reverse else slice(0, n - k)
            sr, si = xre[src], xim[src]
            nre = xre[dst] + pr * sr - pi * si
            nim = xim[dst] + pr * si + pi * sr
            order = (lambda new, old: [new, old]) if reverse else (lambda new, old: [old, new])
            xre = jnp.concatenate(order(nre, xre[keep]), axis=0)
            xim = jnp.concatenate(order(nim, xim[keep]), axis=0)
        else:
            sr, si = shift(xre, k), shift(xim, k)
            xre, xim = xre + pr * sr - pi * si, xim + pr * si + pi * sr
        k *= 2
        i += 1
    return xre, xim


def _dot(a, b):
    return jnp.dot(a.astype(BF16), b.astype(BF16), preferred_element_type=F32)


def _s5_chunk_operators(par, dvec, half):
    tk = NPH * S5_K
    lane = lax.broadcasted_iota(jnp.int32, par.shape, 1)
    par = jnp.where(lax.shift_right_logical(lane, 6) == half, par, 0.0)
    pa = par[0:128].T
    pb = par[128:256]
    lane16 = lax.broadcasted_iota(jnp.int32, (16, tk), 1)
    row16 = lax.broadcasted_iota(jnp.int32, (16, tk), 0)
    rep = (lax.shift_right_logical(lane16, 4) == row16).astype(F32)
    til = ((lane16 & 15) == row16).astype(F32)

    def cmul(ar, ai, br, bi):
        return ar * br - ai * bi, ar * bi + ai * br

    def c_of(r0):
        c_re, c_im = pb[r0 + 64:r0 + 80], pb[r0 + 80:r0 + 96]
        blocks = [cmul(c_re, c_im, pb[r0 + t:r0 + t + 1], pb[r0 + 16 + t:r0 + 17 + t]) for t in range(NPH)]
        return (jnp.concatenate([b[0] for b in blocks], axis=0),
                -jnp.concatenate([b[1] for b in blocks], axis=0))

    cc = jnp.concatenate(list(c_of(0)) + list(c_of(32)), axis=1)
    yield

    spread = {c0: (_dot(pa[:, c0:c0 + 16], rep), _dot(pa[:, c0 + 16:c0 + 32], rep),
                   _dot(pa[:, c0 + 64:c0 + 80], til), _dot(pa[:, c0 + 80:c0 + 96], til)) for c0 in (0, 32)}
    yield

    wf_re, wf_im = cmul(*spread[0])
    wb_re, wb_im = cmul(*spread[32])
    bc = jnp.concatenate([wf_re, wf_im, wb_re, wb_im], axis=0)
    yield

    kf = _dot(pb[64:80], wf_re) - _dot(pb[80:96], wf_im)
    kb = _dot(pb[96:112], wb_re) - _dot(pb[112:128], wb_im)
    yield

    blocks = []
    for t in range(NPH):
        left = S5_K * (NPH - 1 - t)
        right = S5_K * t
        f = kf if left == 0 else jnp.where(lane16 < tk - left, pltpu.roll(kf, tk - left, axis=1), 0.0)
        b = kb if right == 0 else jnp.where(lane16 >= right, pltpu.roll(kb, right, axis=1), 0.0)
        blocks.append(f + b)
    a = jnp.concatenate(blocks, axis=0)
    ri = lax.broadcasted_iota(jnp.int32, (tk, tk), 0)
    ci = lax.broadcasted_iota(jnp.int32, (tk, tk), 1)
    a = a + jnp.where(ri == ci, dvec, 0.0)
    return a.astype(BF16), bc.astype(BF16), cc.astype(BF16)


def _in_lockstep(generators):
    results = [None] * len(generators)
    pending = list(enumerate(generators))
    while pending:
        still = []
        for i, gen in pending:
            try:
                next(gen)
                still.append((i, gen))
            except StopIteration as stop:
                results[i] = stop.value
        pending = still
    return results


def _cast_stream_step(k, n_steps, srcs, dsts, inbufs, outbufs, sems):
    assert n_steps >= 2
    slot = lax.rem(k, 2)
    n_streams = len(srcs)
    rows = [src.shape[0] // n_steps for src in srcs]

    def read(i, step, sl):
        return pltpu.make_async_copy(srcs[i].at[pl.ds(step * rows[i], rows[i]), :], inbufs[i].at[sl],
                                     sems.at[i, 0, sl])

    def write(i, step, sl):
        return pltpu.make_async_copy(outbufs[i].at[sl], dsts[i].at[pl.ds(step * rows[i], rows[i]), :],
                                     sems.at[i, 1, sl])

    @pl.when(k == 0)
    def _():
        for i in range(n_streams):
            read(i, 0, 0).start()
            read(i, 1, 1).start()

    @pl.when(k >= 2)
    def _():
        for i in range(n_streams):
            write(i, k - 2, slot).wait()

    for i in range(n_streams):
        read(i, k, slot).wait()
        outbufs[i][slot] = inbufs[i][slot].astype(BF16)

    @pl.when(k + 2 < n_steps)
    def _():
        for i in range(n_streams):
            read(i, k + 2, slot).start()

    for i in range(n_streams):
        write(i, k, slot).start()

    @pl.when(k == n_steps - 1)
    def _():
        for i in range(n_streams):
            write(i, k - 1, 1 - slot).wait()
            write(i, k, slot).wait()


def _s5_kernel(n_w, n_grid, u_ref, *rest):
    raw_refs, d_ref, rest = rest[:7], rest[7], rest[8:]
    w_src, rest = rest[:n_w], rest[n_w:]
    y_ref, rest = rest[0], rest[1:]
    w_dst, rest = rest[:n_w], rest[n_w:]
    xt_ref, yt_ref, a_ref, bc_ref, cc_ref, q_ref, rest = rest[:6] + (rest[6:],)
    w_in_buf, w_out_buf, w_sems = rest[:n_w], rest[n_w:2 * n_w], rest[2 * n_w]
    n = u_ref.shape[1]
    p2 = 2 * S5_P
    tk = NPH * S5_K
    n_groups = S5_LB // S5_K
    n_pairs = n_groups // 2

    def to_rows(s, _):
        xt_ref[s] = u_ref[s].T
        return 0
    lax.fori_loop(0, NPH, to_rows, 0, unroll=8)

    pars = []
    for g in range(n_groups):
        par, q_ref[g] = _s5_discretise(g, *raw_refs)
        pars.append(par)
    stages = [_s5_chunk_operators(pars[g], d_ref[g], g % 2) for g in range(n_groups)]
    for g, (a, bc, cc) in enumerate(_in_lockstep(stages)):
        a_ref[g], bc_ref[g], cc_ref[g] = a, bc, cc

    def pair(gp):
        xs, ys = [], []
        for h in range(2):
            g = 2 * gp + h
            rows = pl.ds(pl.multiple_of(g * S5_K, S5_K), S5_K)
            x = xt_ref[:, rows, :].reshape(tk, n)
            xs.append(x)
            ys.append(jnp.dot(a_ref[g], x, preferred_element_type=F32))
        bc = jnp.concatenate([bc_ref[2 * gp], bc_ref[2 * gp + 1]], axis=1)
        s = jnp.dot(bc, jnp.concatenate(xs, axis=0), preferred_element_type=F32)
        yield
        st = [s[i * p2:(i + 1) * p2].T for i in range(4)]
        lane = lax.broadcasted_iota(jnp.int32, (4 * 16, p2), 1)
        q = jnp.where(lane < S5_P, q_ref[2 * gp], q_ref[2 * gp + 1])
        yield
        hf_re, hf_im = _row_scan_exclusive(st[0], st[1], q[0:16], q[16:32], False)
        hb_re, hb_im = _row_scan_exclusive(st[2], st[3], q[32:48], q[48:64], True)
        hin = jnp.concatenate([hf_re, hf_im, hb_re, hb_im], axis=1).astype(BF16)
        yield
        for h in range(2):
            g = 2 * gp + h
            rows = pl.ds(pl.multiple_of(g * S5_K, S5_K), S5_K)
            y = ys[h] + lax.dot_general(cc_ref[g], hin, (((1,), (1,)), ((), ())), preferred_element_type=F32)
            yt_ref[:, rows, :] = _gelu(y).reshape(NPH, S5_K, n)

    first = pl.program_id(0) * n_pairs
    stream = functools.partial(_cast_stream_step, n_steps=n_grid * n_pairs, srcs=w_src, dsts=w_dst,
                               inbufs=w_in_buf, outbufs=w_out_buf, sems=w_sems)
    for gp in range(n_pairs // 2):
        stream(first + gp)
    _in_lockstep([pair(gp) for gp in range(n_pairs)])
    for gp in range(n_pairs // 2, n_pairs):
        stream(first + gp)

    def to_lanes(t, _):
        y_ref[t] = yt_ref[t].T
        return 0
    lax.fori_loop(0, NPH, to_lanes, 0, unroll=8)


def _s5(u3, lam_re, lam_im, log_step, b_re, b_im, c_re, c_im, d, weights):
    assert NPH == 16 and S5_K == 16 and 2 * lam_re.shape[-1] == 128
    _, n_chunks, d_s5 = u3.shape
    raw = [lam_re, lam_im, log_step[..., None], b_re, b_im, c_re, c_im]
    dvec = jnp.tile(d.astype(F32), (1, NPH))[:, None, :]
    gpb = S5_LB // S5_K
    tk = NPH * S5_K
    n_grid = d_s5 // S5_LB
    n_steps = n_grid * (gpb // 2)
    chunk = lambda w: (w.shape[0] // n_steps, w.shape[1])
    assert all(w.shape[0] % (16 * n_steps) == 0 for w in weights)
    per_block = lambda n, *tail: pl.BlockSpec((n,) + tail, lambda b: (b,) + (0,) * len(tail))
    per_dir_block = lambda v: pl.BlockSpec((2, gpb) + v.shape[2:], lambda b: (0, b) + (0,) * (v.ndim - 2))
    hbm = pl.BlockSpec(memory_space=pl.ANY)
    outs = pl.pallas_call(
        functools.partial(_s5_kernel, len(weights), n_grid),
        grid=(n_grid,),
        in_specs=[pl.BlockSpec((NPH, n_chunks, S5_LB), lambda b: (0, 0, b))]
        + [per_dir_block(v) for v in raw] + [per_block(gpb, 1, tk)] + [hbm] * len(weights),
        out_specs=[pl.BlockSpec((NPH, n_chunks, S5_LB), lambda b: (0, 0, b))] + [hbm] * len(weights),
        out_shape=[jax.ShapeDtypeStruct((NPH, n_chunks, d_s5), F32)]
        + [jax.ShapeDtypeStruct(w.shape, BF16) for w in weights],
        scratch_shapes=[
            pltpu.VMEM((NPH, S5_LB, n_chunks), BF16),
            pltpu.VMEM((NPH, S5_LB, n_chunks), F32),
            pltpu.VMEM((gpb, tk, tk), BF16),
            pltpu.VMEM((gpb, 8 * S5_P, tk), BF16),
            pltpu.VMEM((gpb, tk, 8 * S5_P), BF16),
            pltpu.VMEM((gpb, 4 * 16, 2 * S5_P), F32),
        ] + [pltpu.VMEM((2,) + chunk(w), F32) for w in weights]
        + [pltpu.VMEM((2,) + chunk(w), BF16) for w in weights]
        + [pltpu.SemaphoreType.DMA((len(weights), 2, 2))],
        compiler_params=_cparams(1),
        name="s5",
    )(u3, *raw, dvec, *weights)
    return outs[0], outs[1:]


def _s5_discretise(g, lam_re_ref, lam_im_ref, log_step_ref, b_re_ref, b_im_ref, c_re_ref, c_im_ref):
    def cmul(x, y):
        return x[0] * y[0] - x[1] * y[1], x[0] * y[1] + x[1] * y[0]

    def rows(zs):
        return jnp.concatenate([z[0] for z in zs], axis=0), jnp.concatenate([z[1] for z in zs], axis=0)

    def one_direction(d):
        lre, lim = jnp.minimum(lam_re_ref[d, g:g + 1, :], -1e-4), lam_im_ref[d, g:g + 1, :]
        step = jnp.exp(log_step_ref[d, g:g + 1, :])
        are, aim = lre * step, lim * step
        mag = jnp.exp(are)
        lam_bar = (mag * jnp.cos(aim), mag * jnp.sin(aim))
        powers = [(jnp.ones_like(are), jnp.zeros_like(are))]
        for _ in range(NPH):
            powers.append(cmul(powers[-1], lam_bar))
        squares = [powers[NPH]]
        for _ in range(15):
            squares.append(cmul(squares[-1], squares[-1]))
        nr, ni = lam_bar[0] - 1.0, lam_bar[1]
        den = lre * lre + lim * lim
        z = ((nr * lre + ni * lim) / den, (ni * lre - nr * lim) / den)
        bbar = cmul(z, (b_re_ref[d, g].T, b_im_ref[d, g].T))
        c = (c_re_ref[d, g], c_im_ref[d, g])
        return powers, squares, bbar, c

    pf, qf, bbar_f, c_f = one_direction(0)
    pb, qb, bbar_b, c_b = one_direction(1)
    par = jnp.concatenate(
        list(rows(pf[NPH - 1::-1])) + list(rows(pb[:NPH])) + list(bbar_f) + list(bbar_b)
        + list(rows(pf[1:])) + list(rows(pb[NPH:0:-1])) + list(c_f) + list(c_b), axis=0)
    q_tab = jnp.concatenate(list(rows(qf)) + list(rows(qb)), axis=0)
    both_halves = lambda v: jnp.concatenate([v, v], axis=1)
    return both_halves(par), both_halves(q_tab)


def _row_scan_carry(a, h, reverse):
    n, w = a.shape
    row = lax.broadcasted_iota(jnp.int32, (n, w), 0)

    def shift(v, k, fill):
        if k % 8 == 0:
            z = jnp.full((k, w), fill, v.dtype)
            if reverse:
                return jnp.concatenate([v[k:], z], axis=0)
            return jnp.concatenate([z, v[:n - k]], axis=0)
        if reverse:
            return jnp.where(row < n - k, pltpu.roll(v, n - k, axis=0), fill)
        return jnp.where(row >= k, pltpu.roll(v, k, axis=0), fill)

    k = 1
    while k < n:
        h = h + a * shift(h, k, 0.0)
        a = a * shift(a, k, 1.0)
        k *= 2
    return shift(h, 1, 0.0)


def _rglru_kernel(x_ref, cw_ref, cb_ref, wa_ref, wx_ref, ba_ref, bx_ref, lam_ref, y_ref, xe_ref,
                  alf_ref, hlf_ref, alb_ref, hlb_ref):
    n, w = x_ref.shape[1], x_ref.shape[2]

    def block_diag(heads):
        per, hd, _ = heads.shape
        zeros = jnp.zeros((hd, hd), heads.dtype)
        return jnp.concatenate(
            [jnp.concatenate([heads[p] if q == p else zeros for q in range(per)], axis=1) for p in range(per)],
            axis=0)
    row = lax.broadcasted_iota(jnp.int32, (n, w), 0)

    def from_prev_chunk(v):
        return jnp.where(row >= 1, pltpu.roll(v, 1, axis=0), 0.0)

    def from_next_chunk(v):
        return jnp.where(row < n - 1, pltpu.roll(v, n - 1, axis=0), 0.0)

    xe_ref[0] = from_prev_chunk(x_ref[NPH - 2])
    xe_ref[1] = from_prev_chunk(x_ref[NPH - 1])

    def copy_body(s, _):
        xe_ref[s + 2] = x_ref[s]
        return 0
    lax.fori_loop(0, NPH, copy_body, 0, unroll=4)
    xe_ref[NPH + 2] = from_next_chunk(x_ref[0])

    cw = cw_ref[...]
    cb = cb_ref[...]
    lam = lam_ref[...]
    neg = -lam
    softplus = jnp.maximum(neg, 0.0) + jnp.log(1.0 + jnp.exp(-jnp.abs(neg)))

    directions = []
    for d, (reverse, al_ref, hl_ref) in enumerate(((False, alf_ref, hlf_ref), (True, alb_ref, hlb_ref))):
        rate2 = (-RG_C * LOG2_E) * softplus[d:d + 1]
        wg = (-LOG2_E * jnp.concatenate([block_diag(wa_ref[d]), block_diag(wx_ref[d])], axis=1)).astype(BF16)
        bg = -LOG2_E * jnp.concatenate([ba_ref[d, 0], bx_ref[d, 0]], axis=1)
        init = NPH if reverse else 0
        al_ref[init] = jnp.ones((n, w), F32)
        hl_ref[init] = jnp.zeros((n, w), F32)
        directions.append((reverse, al_ref, hl_ref, rate2, wg, bg))

    def local_step(i, reverse, al_ref, hl_ref, rate2, wg, bg):
        s = (NPH - 1 - i) if reverse else i
        xc = (cw[0:1] * xe_ref[s] + cw[1:2] * xe_ref[s + 1]
              + cw[2:3] * xe_ref[s + 2] + cw[3:4] * xe_ref[s + 3] + cb)
        yield
        g = jnp.dot(xc.astype(BF16), wg, preferred_element_type=F32) + bg
        yield
        r = 1.0 / (1.0 + jnp.exp2(g[:, :w]))
        ig = 1.0 / (1.0 + jnp.exp2(g[:, w:]))
        a = jnp.exp2(rate2 * r)
        bt = jnp.sqrt(1.0 - a * a) * (ig * xc)
        yield
        src = (s + 1) if reverse else s
        dst = s if reverse else (s + 1)
        hl_ref[dst] = a * hl_ref[src] + bt
        al_ref[dst] = a * al_ref[src]

    def local_body(i, _):
        _in_lockstep([local_step(i, *direction) for direction in directions])
        return 0
    lax.fori_loop(0, NPH, local_body, 0, unroll=2)

    carry_f = _row_scan_carry(alf_ref[NPH], hlf_ref[NPH], False)
    carry_b = _row_scan_carry(alb_ref[0], hlb_ref[0], True)

    def fix_body(s, _):
        y_ref[s] = (hlf_ref[s + 1] + alf_ref[s + 1] * carry_f) + (hlb_ref[s] + alb_ref[s] * carry_b)
        return 0
    lax.fori_loop(0, NPH, fix_body, 0, unroll=4)


def _rglru(xrg3, conv_w, conv_b, wa, ba, wx, bx, lam):
    _, n_chunks, d_rg = xrg3.shape
    nb = d_rg // RG_CB
    n_dir, heads, hd, _ = wa.shape
    per = RG_CB // hd
    assert heads == nb * per
    gate_w = pl.BlockSpec((n_dir, per, hd, hd), lambda j: (0, j, 0, 0))
    gate_b = pl.BlockSpec((n_dir, 1, 1, RG_CB), lambda j: (0, j, 0, 0))
    bias4 = lambda b: b.reshape(n_dir, nb, 1, RG_CB)
    return pl.pallas_call(
        _rglru_kernel,
        grid=(nb,),
        in_specs=[
            pl.BlockSpec((NPH, n_chunks, RG_CB), lambda j: (0, 0, j)),
            pl.BlockSpec((CONV_W, RG_CB), lambda j: (0, j)),
            pl.BlockSpec((1, RG_CB), lambda j: (0, j)),
            gate_w, gate_w, gate_b, gate_b,
            pl.BlockSpec((2, RG_CB), lambda j: (0, j)),
        ],
        out_specs=pl.BlockSpec((NPH, n_chunks, RG_CB), lambda j: (0, 0, j)),
        out_shape=jax.ShapeDtypeStruct((NPH, n_chunks, d_rg), F32),
        scratch_shapes=[
            pltpu.VMEM((NPH + 3, n_chunks, RG_CB), F32),
        ] + [pltpu.VMEM((NPH + 1, n_chunks, RG_CB), F32)] * 4,
        compiler_params=_cparams(1),
        name="rglru",
    )(xrg3, conv_w, conv_b, wa, wx, bias4(ba), bias4(bx), lam)


def _rms_norm(x, g):
    return x * lax.rsqrt(jnp.mean(x * x, axis=-1, keepdims=True) + LN_EPS) * g


def _mixout_kernel(alpha, ys5_ref, yrg_ref, gz_ref, h_ref, gw_ref, gb_ref, gn5_ref, gnr_ref,
                   w_ref, b_ref, g_ref, be_ref, o_ref):
    rows = NPH * CT
    d_s5, d_model = ys5_ref.shape[-1], o_ref.shape[-1]
    pieces = lambda width: [slice(c, c + 256) for c in range(0, width, 256)]

    y = ys5_ref[...].reshape(rows, d_s5)
    yb = y.astype(BF16)
    glu, sq = [], 0.0
    for cols in pieces(d_s5):
        g = jnp.dot(yb, gw_ref[:, cols], preferred_element_type=F32) + gb_ref[:, cols]
        o = y[:, cols] * _sigmoid(g)
        sq = sq + jnp.sum(o * o, axis=-1, keepdims=True)
        glu.append(o)
    scale = lax.rsqrt(sq * (1.0 / d_s5) + LN_EPS)
    ys5n = jnp.concatenate([o * scale * gn5_ref[:, cols] for o, cols in zip(glu, pieces(d_s5))],
                           axis=1).astype(BF16)

    yrg = yrg_ref[...].reshape(rows, -1) * gz_ref[...].reshape(rows, -1)
    yrgn = _rms_norm(yrg, gnr_ref[...]).astype(BF16)

    s1, s2 = 0.0, 0.0
    for cols in pieces(d_model):
        v = (jnp.dot(ys5n, w_ref[:d_s5, cols], preferred_element_type=F32)
             + jnp.dot(yrgn, w_ref[d_s5:, cols], preferred_element_type=F32)
             + b_ref[:, cols] + alpha * h_ref[:, :, cols].reshape(rows, 256))
        s1 = s1 + jnp.sum(v, axis=-1, keepdims=True)
        s2 = s2 + jnp.sum(v * v, axis=-1, keepdims=True)
        o_ref[:, :, cols] = v.reshape(NPH, CT, 256)
    mean = s1 * (1.0 / d_model)
    rstd = lax.rsqrt(s2 * (1.0 / d_model) - mean * mean + LN_EPS)
    for cols in pieces(d_model):
        v = o_ref[:, :, cols].reshape(rows, 256)
        o_ref[:, :, cols] = ((v - mean) * rstd * g_ref[:, cols] + be_ref[:, cols]).reshape(NPH, CT, 256)


def _mixout(alpha, ys5, yrg, gz, h, glu_w, glu_b, gn_s5, gn_rg, w_out, b_out, ln_g, ln_b):
    _, n_chunks, d_model = h.shape
    d_s5, d_rg = ys5.shape[-1], yrg.shape[-1]
    return pl.pallas_call(
        functools.partial(_mixout_kernel, alpha),
        grid=(n_chunks // CT,),
        in_specs=[
            _tile3(d_s5), _tile3(d_rg), _tile3(d_rg), _tile3(d_model),
            _resident((d_s5, d_s5)), _resident((1, d_s5)), _resident((1, d_s5)), _resident((1, d_rg)),
            _resident((d_s5 + d_rg, d_model)), _resident((1, d_model)),
            _resident((1, d_model)), _resident((1, d_model)),
        ],
        out_specs=_tile3(d_model),
        out_shape=jax.ShapeDtypeStruct((NPH, n_chunks, d_model), F32),
        compiler_params=_cparams(1),
        name="mixout",
    )(ys5, yrg, gz, h, glu_w, glu_b, gn_s5, gn_rg, w_out, b_out, ln_g, ln_b)


def _ffn_kernel(alpha, h_ref, p_ref, w1_ref, b1_ref, w2_ref, b2_ref, pw_ref, gw_ref, gb_ref,
                g_ref, be_ref, o_ref, hb_ref, acc_ref):
    j = pl.program_id(1)
    rows = NPH * CT

    @pl.when(j == 0)
    def _():
        h = h_ref[...].reshape(rows, -1)
        hb = h.astype(BF16)
        hb_ref[...] = hb
        gate = _sigmoid(jnp.dot(hb, gw_ref[...], preferred_element_type=F32) + gb_ref[...])
        pe = _to_phase_major(p_ref[...]).astype(BF16)
        acc_ref[...] = alpha * h + gate * jnp.dot(pe, pw_ref[...], preferred_element_type=F32) + b2_ref[...]

    a = jnp.dot(hb_ref[...], w1_ref[...], preferred_element_type=F32) + b1_ref[...]
    a = jnp.maximum(a, 0.0)
    acc_ref[...] += jnp.dot((a * a).astype(BF16), w2_ref[...], preferred_element_type=F32)

    @pl.when(j == pl.num_programs(1) - 1)
    def _():
        o_ref[...] = _to_time_major(_layer_norm(acc_ref[...], g_ref[...], be_ref[...]))


def _ffn(alpha, h1, p2, w1, b1, w2, b2, ple_w, gate_w, gate_b, ln_g, ln_b, ff_tile):
    _, n_chunks, d_model = h1.shape
    d_ff = w1.shape[1]
    ple_dim = ple_w.shape[0]
    rows = NPH * CT
    return pl.pallas_call(
        functools.partial(_ffn_kernel, alpha),
        grid=(n_chunks // CT, d_ff // ff_tile),
        in_specs=[
            _tile3(d_model),
            pl.BlockSpec((rows, ple_dim), lambda i, j: (i, 0)),
            pl.BlockSpec((d_model, ff_tile), lambda i, j: (0, j)),
            pl.BlockSpec((1, ff_tile), lambda i, j: (0, j)),
            pl.BlockSpec((ff_tile, d_model), lambda i, j: (j, 0)),
            _resident((1, d_model)),
            _resident((ple_dim, d_model)), _resident((d_model, d_model)), _resident((1, d_model)),
            _resident((1, d_model)), _resident((1, d_model)),
        ],
        out_specs=pl.BlockSpec((rows, d_model), lambda i, j: (i, 0)),
        out_shape=jax.ShapeDtypeStruct((NPH * n_chunks, d_model), F32),
        scratch_shapes=[pltpu.VMEM((rows, d_model), BF16), pltpu.VMEM((rows, d_model), F32)],
        compiler_params=_cparams(2),
        name="ffn",
    )(h1, p2, w1, b1, w2, b2, ple_w, gate_w, gate_b, ln_g, ln_b)


def kernel(x, p, ln_in_g, ln_in_b, w_in, b_in, s5_lambda_re, s5_lambda_im, s5_log_step, s5_b_re, s5_b_im, s5_c_re, s5_c_im, s5_d, s5_glu_w, s5_glu_b, rg_conv_w, rg_conv_b, rg_wa, rg_ba, rg_wx, rg_bx, rg_lambda, gn_s5, gn_rg, w_out, b_out, ln1_g, ln1_b, w_ff1, b_ff1, w_ff2, b_ff2, ple_w, ple_gate_w, ple_gate_b, ln2_g, ln2_b):
    batch, seq, d_model = x.shape
    depth = w_in.shape[0]
    assert batch == 1 and depth == 1 and seq % (NPH * CT) == 0
    n_chunks = seq // NPH
    d_s5 = s5_glu_w.shape[-1]
    d_rg = rg_conv_w.shape[-1]
    alpha = (2.0 * depth) ** 0.25
    row = lambda v: v.reshape(1, -1).astype(F32)

    h, u3, xrg, gz = _inproj(x.reshape(seq, d_model), row(ln_in_g), row(ln_in_b),
                              w_in[0].astype(BF16), row(b_in[0]), d_s5, d_rg)

    ys5, (w_out_b, gate_w_b, w_ff1_b, w_ff2_b) = _s5(
        u3, s5_lambda_re[0], s5_lambda_im[0], s5_log_step[0], s5_b_re[0], s5_b_im[0], s5_c_re[0], s5_c_im[0],
        s5_d[0], [w_out[0], ple_gate_w[0], w_ff1[0], w_ff2[0]])

    yrg = _rglru(xrg, rg_conv_w[0].astype(F32), row(rg_conv_b[0]), rg_wa[0], rg_ba[0], rg_wx[0], rg_bx[0],
                 rg_lambda[0].astype(F32))

    h1 = _mixout(alpha, ys5, yrg, gz, h, s5_glu_w[0].astype(BF16), row(s5_glu_b[0]), row(gn_s5[0]),
                 row(gn_rg[0]), w_out_b, row(b_out[0]), row(ln1_g[0]), row(ln1_b[0]))

    out = _ffn(alpha, h1, p.reshape(seq, p.shape[-1]), w_ff1_b, row(b_ff1[0]),
               w_ff2_b, row(b_ff2[0]), ple_w[0].astype(BF16), gate_w_b,
               row(ple_gate_b[0]), row(ln2_g[0]), row(ln2_b[0]), ff_tile=1024)
    return out.reshape(batch, seq, d_model).astype(x.dtype)
```

```python
import functools
import math

import jax
import jax.numpy as jnp
from jax import lax
from jax.experimental import pallas as pl
from jax.experimental.pallas import tpu as pltpu

F32 = jnp.float32
BF16 = jnp.bfloat16

NPH = 16
CT = 32
S5_K = 16
S5_P = 64
S5_LB = 128
RG_CB = 128
CONV_W = 4
RG_C = 8.0
LN_EPS = 1e-5
LOG2_E = math.log2(math.e)
VMEM_LIMIT_V7X = 56 * 1024 * 1024
MXU_COLS_V7X = 256


def _cparams(n_axes):
    return pltpu.CompilerParams(
        dimension_semantics=("arbitrary",) * n_axes,
        vmem_limit_bytes=VMEM_LIMIT_V7X)


def _resident(shape):
    return pl.BlockSpec(shape, lambda *_: (0,) * len(shape), pipeline_mode=pl.Buffered(1))


def _tile3(width):
    return pl.BlockSpec((NPH, CT, width), lambda i, *_: (0, i, 0))


def _to_phase_major(x):
    w = x.shape[-1]
    return jnp.swapaxes(x.reshape(CT, NPH, w), 0, 1).reshape(NPH * CT, w)


def _to_time_major(x):
    w = x.shape[-1]
    return jnp.swapaxes(x.reshape(NPH, CT, w), 0, 1).reshape(CT * NPH, w)


def _layer_norm(x, g, b):
    mu = jnp.mean(x, axis=-1, keepdims=True)
    xc = x - mu
    var = jnp.mean(xc * xc, axis=-1, keepdims=True)
    return xc * lax.rsqrt(var + LN_EPS) * g + b


def _gelu(x):
    c = math.sqrt(2.0 / math.pi)
    return 0.5 * x * (1.0 + jnp.tanh(c * (x + 0.044715 * (x * x * x))))


def _logistic_base2(t):
    return 1.0 / (1.0 + jnp.exp2(t))


def _inproj_kernel(x_ref, g_ref, b_ref, w_ref, bw_ref, h_ref, u_ref, xrg_ref, gz_ref):
    d_s5, d_rg = u_ref.shape[-1], xrg_ref.shape[-1]
    h = _layer_norm(_to_phase_major(x_ref[...]), g_ref[...], b_ref[...])
    h_ref[...] = h.reshape(h_ref.shape)
    proj = jnp.dot(h.astype(BF16), w_ref[...], preferred_element_type=F32) + bw_ref[...]
    u_ref[...] = proj[:, :d_s5].astype(BF16).reshape(u_ref.shape)
    xrg_ref[...] = proj[:, d_s5:d_s5 + d_rg].reshape(xrg_ref.shape)
    gz_ref[...] = _gelu(proj[:, d_s5 + d_rg:]).reshape(gz_ref.shape)


def _inproj(x2, ln_g, ln_b, w_in, b_in, d_s5, d_rg):
    seq, d_model = x2.shape
    n_chunks = seq // NPH
    return pl.pallas_call(
        _inproj_kernel,
        grid=(n_chunks // CT,),
        in_specs=[
            pl.BlockSpec((CT * NPH, d_model), lambda i: (i, 0)),
            _resident((1, d_model)), _resident((1, d_model)),
            _resident(w_in.shape), _resident(b_in.shape),
        ],
        out_specs=[_tile3(d_model), _tile3(d_s5), _tile3(d_rg), _tile3(d_rg)],
        out_shape=[
            jax.ShapeDtypeStruct((NPH, n_chunks, d_model), F32),
            jax.ShapeDtypeStruct((NPH, n_chunks, d_s5), BF16),
            jax.ShapeDtypeStruct((NPH, n_chunks, d_rg), F32),
            jax.ShapeDtypeStruct((NPH, n_chunks, d_rg), F32),
        ],
        compiler_params=_cparams(1),
        name="inproj",
    )(x2, ln_g, ln_b, w_in, b_in)


def _row_scan_exclusive(sre, sim, qre, qim, reverse):
    n, w = sre.shape
    row = lax.broadcasted_iota(jnp.int32, (n, w), 0)

    def shift(v, k):
        if k % 8 == 0:
            z = jnp.zeros((k, w), v.dtype)
            return jnp.concatenate([v[k:], z] if reverse else [z, v[:n - k]], axis=0)
        if reverse:
            return jnp.where(row < n - k, pltpu.roll(v, n - k, axis=0), 0.0)
        return jnp.where(row >= k, pltpu.roll(v, k, axis=0), 0.0)

    xre, xim = shift(sre, 1), shift(sim, 1)
    k, i = 1, 0
    while k < n:
        pr, pi = qre[i:i + 1], qim[i:i + 1]
        if k % 8 == 0:
            keep = slice(n - k, n) if reverse else slice(0, k)
            dst = slice(0, n - k) if reverse else slice(k, n)
            src = slice(k, n) if reverse else slice(0, n - k)
            sr, si = xre[src], xim[src]
            nre = xre[dst] + pr * sr - pi * si
            nim = xim[dst] + pr * si + pi * sr
            order = (lambda new, old: [new, old]) if reverse else (lambda new, old: [old, new])
            xre = jnp.concatenate(order(nre, xre[keep]), axis=0)
            xim = jnp.concatenate(order(nim, xim[keep]), axis=0)
        else:
            sr, si = shift(xre, k), shift(xim, k)
            xre, xim = xre + pr * sr - pi * si, xim + pr * si + pi * sr
        k *= 2
        i += 1
    return xre, xim


def _dot(a, b):
    return jnp.dot(a.astype(BF16), b.astype(BF16), preferred_element_type=F32)


def _s5_chunk_operators(par, dvec, half):
    tk = NPH * S5_K
    lane = lax.broadcasted_iota(jnp.int32, par.shape, 1)
    par = jnp.where((lane >= S5_P) == bool(half), par, 0.0)
    pa = par[0:128].T
    pb = par[128:256]
    lane16 = lax.broadcasted_iota(jnp.int32, (16, tk), 1)
    row16 = lax.broadcasted_iota(jnp.int32, (16, tk), 0)
    rep = (lax.shift_right_logical(lane16, 4) == row16).astype(F32)
    til = ((lane16 & 15) == row16).astype(F32)

    def cmul(ar, ai, br, bi):
        return ar * br - ai * bi, ar * bi + ai * br

    def c_of(r0):
        c_re, c_im = pb[r0 + 64:r0 + 80], pb[r0 + 80:r0 + 96]
        blocks = [cmul(c_re, c_im, pb[r0 + t:r0 + t + 1], pb[r0 + 16 + t:r0 + 17 + t]) for t in range(NPH)]
        return (jnp.concatenate([b[0] for b in blocks], axis=0),
                -jnp.concatenate([b[1] for b in blocks], axis=0))

    cc = jnp.concatenate(list(c_of(0)) + list(c_of(32)), axis=1)
    yield

    spread = {c0: (_dot(pa[:, c0:c0 + 16], rep), _dot(pa[:, c0 + 16:c0 + 32], rep),
                   _dot(pa[:, c0 + 64:c0 + 80], til), _dot(pa[:, c0 + 80:c0 + 96], til)) for c0 in (0, 32)}
    yield

    wf_re, wf_im = cmul(*spread[0])
    wb_re, wb_im = cmul(*spread[32])
    bc = jnp.concatenate([wf_re, wf_im, wb_re, wb_im], axis=0)
    yield

    kf = _dot(pb[64:80], wf_re) - _dot(pb[80:96], wf_im)
    kb = _dot(pb[96:112], wb_re) - _dot(pb[112:128], wb_im)
    yield

    blocks = []
    for t in range(NPH):
        left = S5_K * (NPH - 1 - t)
        right = S5_K * t
        f = kf if left == 0 else jnp.where(lane16 < tk - left, pltpu.roll(kf, tk - left, axis=1), 0.0)
        b = kb if right == 0 else jnp.where(lane16 >= right, pltpu.roll(kb, right, axis=1), 0.0)
        blocks.append(f + b)
    a = jnp.concatenate(blocks, axis=0)
    ri = lax.broadcasted_iota(jnp.int32, (tk, tk), 0)
    ci = lax.broadcasted_iota(jnp.int32, (tk, tk), 1)
    a = a + jnp.where(ri == ci, dvec, 0.0)
    return a.astype(BF16), bc.astype(BF16), cc.astype(BF16)


def _in_lockstep(generators):
    results = [None] * len(generators)
    pending = list(enumerate(generators))
    while pending:
        still = []
        for i, gen in pending:
            try:
                next(gen)
                still.append((i, gen))
            except StopIteration as stop:
                results[i] = stop.value
        pending = still
    return results


def _cast_stream_step(k, n_steps, srcs, dsts, scales, inbufs, outbufs, sems):
    assert n_steps >= 2
    slot = lax.rem(k, 2)
    n_streams = len(srcs)
    rows = [src.shape[0] // n_steps for src in srcs]

    def read(i, step, sl):
        return pltpu.make_async_copy(srcs[i].at[pl.ds(step * rows[i], rows[i]), :], inbufs[i].at[sl],
                                     sems.at[i, 0, sl])

    def write(i, step, sl):
        return pltpu.make_async_copy(outbufs[i].at[sl], dsts[i].at[pl.ds(step * rows[i], rows[i]), :],
                                     sems.at[i, 1, sl])

    @pl.when(k == 0)
    def _():
        for i in range(n_streams):
            read(i, 0, 0).start()
            read(i, 1, 1).start()

    @pl.when(k >= 2)
    def _():
        for i in range(n_streams):
            write(i, k - 2, slot).wait()

    for i in range(n_streams):
        read(i, k, slot).wait()
        chunk = inbufs[i][slot]
        outbufs[i][slot] = (chunk if scales[i] == 1.0 else chunk * scales[i]).astype(BF16)

    @pl.when(k + 2 < n_steps)
    def _():
        for i in range(n_streams):
            read(i, k + 2, slot).start()

    for i in range(n_streams):
        write(i, k, slot).start()

    @pl.when(k == n_steps - 1)
    def _():
        for i in range(n_streams):
            write(i, k - 1, 1 - slot).wait()
            write(i, k, slot).wait()


def _s5_kernel(scales, n_grid, u_ref, *rest):
    n_w = len(scales)
    raw_refs, d_ref, rest = rest[:7], rest[7], rest[8:]
    w_src, rest = rest[:n_w], rest[n_w:]
    y_ref, rest = rest[0], rest[1:]
    w_dst, rest = rest[:n_w], rest[n_w:]
    xt_ref, yt_ref, a_ref, bc_ref, cc_ref, q_ref, rest = rest[:6] + (rest[6:],)
    w_in_buf, w_out_buf, w_sems = rest[:n_w], rest[n_w:2 * n_w], rest[2 * n_w]
    n = u_ref.shape[1]
    p2 = 2 * S5_P
    tk = NPH * S5_K
    n_groups = S5_LB // S5_K
    n_pairs = n_groups // 2

    def to_rows(s, _):
        xt_ref[s] = u_ref[s].T
        return 0
    lax.fori_loop(0, NPH, to_rows, 0, unroll=8)

    pars = []
    for g in range(n_groups):
        par, q_ref[g] = _s5_discretise(g, *raw_refs)
        pars.append(par)
    stages = [_s5_chunk_operators(pars[g], d_ref[g], g % 2) for g in range(n_groups)]
    for g, (a, bc, cc) in enumerate(_in_lockstep(stages)):
        a_ref[g], bc_ref[g], cc_ref[g] = a, bc, cc

    def pair(gp):
        xs, ys = [], []
        for h in range(2):
            g = 2 * gp + h
            rows = pl.ds(pl.multiple_of(g * S5_K, S5_K), S5_K)
            x = xt_ref[:, rows, :].reshape(tk, n)
            xs.append(x)
            ys.append(jnp.dot(a_ref[g], x, preferred_element_type=F32))
        bc = jnp.concatenate([bc_ref[2 * gp], bc_ref[2 * gp + 1]], axis=1)
        s = jnp.dot(bc, jnp.concatenate(xs, axis=0), preferred_element_type=F32)
        yield
        st = [s[i * p2:(i + 1) * p2].T for i in range(4)]
        lane = lax.broadcasted_iota(jnp.int32, (4 * 16, p2), 1)
        q = jnp.where(lane < S5_P, q_ref[2 * gp], q_ref[2 * gp + 1])
        yield
        hf_re, hf_im = _row_scan_exclusive(st[0], st[1], q[0:16], q[16:32], False)
        hb_re, hb_im = _row_scan_exclusive(st[2], st[3], q[32:48], q[48:64], True)
        hin = jnp.concatenate([hf_re, hf_im, hb_re, hb_im], axis=1).astype(BF16)
        yield
        for h in range(2):
            g = 2 * gp + h
            rows = pl.ds(pl.multiple_of(g * S5_K, S5_K), S5_K)
            y = ys[h] + lax.dot_general(cc_ref[g], hin, (((1,), (1,)), ((), ())), preferred_element_type=F32)
            yt_ref[:, rows, :] = _gelu(y).reshape(NPH, S5_K, n)

    def two_pairs(it, _):
        first = pl.program_id(0) * n_pairs + 2 * it
        stream = functools.partial(_cast_stream_step, n_steps=n_grid * n_pairs, srcs=w_src, dsts=w_dst,
                                   scales=scales, inbufs=w_in_buf, outbufs=w_out_buf, sems=w_sems)
        stream(first)
        _in_lockstep([pair(2 * it), pair(2 * it + 1)])
        stream(first + 1)
        return 0
    lax.fori_loop(0, n_pairs // 2, two_pairs, 0)

    def to_lanes(t, _):
        y_ref[t] = yt_ref[t].T
        return 0
    lax.fori_loop(0, NPH, to_lanes, 0, unroll=8)


def _s5(u3, lam_re, lam_im, log_step, b_re, b_im, c_re, c_im, d, weights, scales):
    assert NPH == 16 and S5_K == 16 and 2 * lam_re.shape[-1] == 128
    _, n_chunks, d_s5 = u3.shape
    raw = [lam_re, lam_im, log_step[..., None], b_re, b_im, c_re, c_im]
    dvec = jnp.tile(d.astype(F32), (1, NPH))[:, None, :]
    gpb = S5_LB // S5_K
    tk = NPH * S5_K
    n_grid = d_s5 // S5_LB
    n_steps = n_grid * (gpb // 2)
    chunk = lambda w: (w.shape[0] // n_steps, w.shape[1])
    assert all(w.shape[0] % (16 * n_steps) == 0 for w in weights)
    per_block = lambda n, *tail: pl.BlockSpec((n,) + tail, lambda b: (b,) + (0,) * len(tail))
    per_dir_block = lambda v: pl.BlockSpec((2, gpb) + v.shape[2:], lambda b: (0, b) + (0,) * (v.ndim - 2))
    hbm = pl.BlockSpec(memory_space=pl.ANY)
    outs = pl.pallas_call(
        functools.partial(_s5_kernel, tuple(scales), n_grid),
        grid=(n_grid,),
        in_specs=[pl.BlockSpec((NPH, n_chunks, S5_LB), lambda b: (0, 0, b))]
        + [per_dir_block(v) for v in raw] + [per_block(gpb, 1, tk)] + [hbm] * len(weights),
        out_specs=[pl.BlockSpec((NPH, n_chunks, S5_LB), lambda b: (0, 0, b))] + [hbm] * len(weights),
        out_shape=[jax.ShapeDtypeStruct((NPH, n_chunks, d_s5), F32)]
        + [jax.ShapeDtypeStruct(w.shape, BF16) for w in weights],
        scratch_shapes=[
            pltpu.VMEM((NPH, S5_LB, n_chunks), BF16),
            pltpu.VMEM((NPH, S5_LB, n_chunks), F32),
            pltpu.VMEM((gpb, tk, tk), BF16),
            pltpu.VMEM((gpb, 8 * S5_P, tk), BF16),
            pltpu.VMEM((gpb, tk, 8 * S5_P), BF16),
            pltpu.VMEM((gpb, 4 * 16, 2 * S5_P), F32),
        ] + [pltpu.VMEM((2,) + chunk(w), F32) for w in weights]
        + [pltpu.VMEM((2,) + chunk(w), BF16) for w in weights]
        + [pltpu.SemaphoreType.DMA((len(weights), 2, 2))],
        compiler_params=_cparams(1),
        name="s5",
    )(u3, *raw, dvec, *weights)
    return outs[0], outs[1:]


def _s5_discretise(g, lam_re_ref, lam_im_ref, log_step_ref, b_re_ref, b_im_ref, c_re_ref, c_im_ref):
    def cmul(x, y):
        return x[0] * y[0] - x[1] * y[1], x[0] * y[1] + x[1] * y[0]

    def rows(zs):
        return jnp.concatenate([z[0] for z in zs], axis=0), jnp.concatenate([z[1] for z in zs], axis=0)

    def one_direction(d):
        lre, lim = jnp.minimum(lam_re_ref[d, g:g + 1, :], -1e-4), lam_im_ref[d, g:g + 1, :]
        step = jnp.exp(log_step_ref[d, g:g + 1, :])
        are, aim = lre * step, lim * step
        mag = jnp.exp(are)
        lam_bar = (mag * jnp.cos(aim), mag * jnp.sin(aim))
        powers = [(jnp.ones_like(are), jnp.zeros_like(are))]
        for _ in range(NPH):
            powers.append(cmul(powers[-1], lam_bar))
        squares = [powers[NPH]]
        for _ in range(15):
            squares.append(cmul(squares[-1], squares[-1]))
        nr, ni = lam_bar[0] - 1.0, lam_bar[1]
        den = lre * lre + lim * lim
        z = ((nr * lre + ni * lim) / den, (ni * lre - nr * lim) / den)
        bbar = cmul(z, (b_re_ref[d, g].T, b_im_ref[d, g].T))
        c = (c_re_ref[d, g], c_im_ref[d, g])
        return powers, squares, bbar, c

    pf, qf, bbar_f, c_f = one_direction(0)
    pb, qb, bbar_b, c_b = one_direction(1)
    par = jnp.concatenate(
        list(rows(pf[NPH - 1::-1])) + list(rows(pb[:NPH])) + list(bbar_f) + list(bbar_b)
        + list(rows(pf[1:])) + list(rows(pb[NPH:0:-1])) + list(c_f) + list(c_b), axis=0)
    q_tab = jnp.concatenate(list(rows(qf)) + list(rows(qb)), axis=0)
    both_halves = lambda v: jnp.concatenate([v, v], axis=1)
    return both_halves(par), both_halves(q_tab)


def _row_scan_carry(a, h, reverse):
    n, w = a.shape
    row = lax.broadcasted_iota(jnp.int32, (n, w), 0)

    def shift(v, k, fill):
        if k % 8 == 0:
            z = jnp.full((k, w), fill, v.dtype)
            if reverse:
                return jnp.concatenate([v[k:], z], axis=0)
            return jnp.concatenate([z, v[:n - k]], axis=0)
        if reverse:
            return jnp.where(row < n - k, pltpu.roll(v, n - k, axis=0), fill)
        return jnp.where(row >= k, pltpu.roll(v, k, axis=0), fill)

    k = 1
    while k < n:
        h = h + a * shift(h, k, 0.0)
        a = a * shift(a, k, 1.0)
        k *= 2
    return shift(h, 1, 0.0)


def _rglru_kernel(x_ref, cw_ref, cb_ref, wa_ref, wx_ref, ba_ref, bx_ref, lam_ref, y_ref, xe_ref,
                  alf_ref, hlf_ref, alb_ref, hlb_ref):
    n, w = x_ref.shape[1], x_ref.shape[2]

    def block_diag(heads):
        per, hd, _ = heads.shape
        zeros = jnp.zeros((hd, hd), heads.dtype)
        return jnp.concatenate(
            [jnp.concatenate([heads[p] if q == p else zeros for q in range(per)], axis=1) for p in range(per)],
            axis=0)
    row = lax.broadcasted_iota(jnp.int32, (n, w), 0)

    def from_prev_chunk(v):
        return jnp.where(row >= 1, pltpu.roll(v, 1, axis=0), 0.0)

    def from_next_chunk(v):
        return jnp.where(row < n - 1, pltpu.roll(v, n - 1, axis=0), 0.0)

    xe_ref[0] = from_prev_chunk(x_ref[NPH - 2])
    xe_ref[1] = from_prev_chunk(x_ref[NPH - 1])

    def copy_body(s, _):
        xe_ref[s + 2] = x_ref[s]
        return 0
    lax.fori_loop(0, NPH, copy_body, 0, unroll=4)
    xe_ref[NPH + 2] = from_next_chunk(x_ref[0])

    cw = cw_ref[...]
    cb = cb_ref[...]
    lam = lam_ref[...]
    neg = -lam
    softplus = jnp.maximum(neg, 0.0) + jnp.log(1.0 + jnp.exp(-jnp.abs(neg)))

    directions = []
    for d, (reverse, al_ref, hl_ref) in enumerate(((False, alf_ref, hlf_ref), (True, alb_ref, hlb_ref))):
        rate2 = (-RG_C * LOG2_E) * softplus[d:d + 1]
        wg = (-LOG2_E * jnp.concatenate([block_diag(wa_ref[d]), block_diag(wx_ref[d])], axis=1)).astype(BF16)
        bg = -LOG2_E * jnp.concatenate([ba_ref[d, 0], bx_ref[d, 0]], axis=1)
        init = NPH if reverse else 0
        al_ref[init] = jnp.ones((n, w), F32)
        hl_ref[init] = jnp.zeros((n, w), F32)
        directions.append((reverse, al_ref, hl_ref, rate2, wg, bg))

    def local_step(i, reverse, al_ref, hl_ref, rate2, wg, bg):
        s = (NPH - 1 - i) if reverse else i
        xc = (cw[0:1] * xe_ref[s] + cw[1:2] * xe_ref[s + 1]
              + cw[2:3] * xe_ref[s + 2] + cw[3:4] * xe_ref[s + 3] + cb)
        yield
        g = jnp.dot(xc.astype(BF16), wg, preferred_element_type=F32) + bg
        yield
        r = 1.0 / (1.0 + jnp.exp2(g[:, :w]))
        ig = 1.0 / (1.0 + jnp.exp2(g[:, w:]))
        a = jnp.exp2(rate2 * r)
        bt = jnp.sqrt(1.0 - a * a) * (ig * xc)
        yield
        src = (s + 1) if reverse else s
        dst = s if reverse else (s + 1)
        hl_ref[dst] = a * hl_ref[src] + bt
        al_ref[dst] = a * al_ref[src]

    def local_body(i, _):
        _in_lockstep([local_step(i, *direction) for direction in directions])
        return 0
    lax.fori_loop(0, NPH, local_body, 0, unroll=4)

    carry_f = _row_scan_carry(alf_ref[NPH], hlf_ref[NPH], False)
    carry_b = _row_scan_carry(alb_ref[0], hlb_ref[0], True)

    def fix_body(s, _):
        y_ref[s] = (hlf_ref[s + 1] + alf_ref[s + 1] * carry_f) + (hlb_ref[s] + alb_ref[s] * carry_b)
        return 0
    lax.fori_loop(0, NPH, fix_body, 0, unroll=4)


def _rglru(xrg3, conv_w, conv_b, wa, ba, wx, bx, lam):
    _, n_chunks, d_rg = xrg3.shape
    nb = d_rg // RG_CB
    n_dir, heads, hd, _ = wa.shape
    per = RG_CB // hd
    assert heads == nb * per
    gate_w = pl.BlockSpec((n_dir, per, hd, hd), lambda j: (0, j, 0, 0))
    gate_b = pl.BlockSpec((n_dir, 1, 1, RG_CB), lambda j: (0, j, 0, 0))
    bias4 = lambda b: b.reshape(n_dir, nb, 1, RG_CB)
    return pl.pallas_call(
        _rglru_kernel,
        grid=(nb,),
        in_specs=[
            pl.BlockSpec((NPH, n_chunks, RG_CB), lambda j: (0, 0, j)),
            pl.BlockSpec((CONV_W, RG_CB), lambda j: (0, j)),
            pl.BlockSpec((1, RG_CB), lambda j: (0, j)),
            gate_w, gate_w, gate_b, gate_b,
            pl.BlockSpec((2, RG_CB), lambda j: (0, j)),
        ],
        out_specs=pl.BlockSpec((NPH, n_chunks, RG_CB), lambda j: (0, 0, j)),
        out_shape=jax.ShapeDtypeStruct((NPH, n_chunks, d_rg), F32),
        scratch_shapes=[
            pltpu.VMEM((NPH + 3, n_chunks, RG_CB), F32),
        ] + [pltpu.VMEM((NPH + 1, n_chunks, RG_CB), F32)] * 4,
        compiler_params=_cparams(1),
        name="rglru",
    )(xrg3, conv_w, conv_b, wa, wx, bias4(ba), bias4(bx), lam)


def _rms_norm(x, g):
    return x * lax.rsqrt(jnp.mean(x * x, axis=-1, keepdims=True) + LN_EPS) * g


def _mixout_kernel(alpha, ys5_ref, yrg_ref, gz_ref, h_ref, gw_ref, gb_ref, gn5_ref, gnr_ref,
                   w_ref, b_ref, g_ref, be_ref, o_ref):
    rows = NPH * CT
    d_s5, d_model = ys5_ref.shape[-1], o_ref.shape[-1]
    pw = MXU_COLS_V7X
    pieces = lambda width: [slice(c, c + pw) for c in range(0, width, pw)]

    y = ys5_ref[...].reshape(rows, d_s5)
    yb = y.astype(BF16)
    glu, sq = [], 0.0
    for cols in pieces(d_s5):
        g = jnp.dot(yb, gw_ref[:, cols], preferred_element_type=F32) + gb_ref[:, cols]
        o = y[:, cols] * _logistic_base2(g)
        sq = sq + jnp.sum(o * o, axis=-1, keepdims=True)
        glu.append(o)
    scale = lax.rsqrt(sq * (1.0 / d_s5) + LN_EPS)
    ys5n = jnp.concatenate([o * scale * gn5_ref[:, cols] for o, cols in zip(glu, pieces(d_s5))],
                           axis=1).astype(BF16)

    yrg = yrg_ref[...].reshape(rows, -1) * gz_ref[...].reshape(rows, -1)
    yrgn = _rms_norm(yrg, gnr_ref[...]).astype(BF16)

    s1, s2 = 0.0, 0.0
    for cols in pieces(d_model):
        v = (jnp.dot(ys5n, w_ref[:d_s5, cols], preferred_element_type=F32)
             + jnp.dot(yrgn, w_ref[d_s5:, cols], preferred_element_type=F32)
             + b_ref[:, cols] + alpha * h_ref[:, :, cols].reshape(rows, pw))
        s1 = s1 + jnp.sum(v, axis=-1, keepdims=True)
        s2 = s2 + jnp.sum(v * v, axis=-1, keepdims=True)
        o_ref[:, :, cols] = v.reshape(NPH, CT, pw)
    mean = s1 * (1.0 / d_model)
    rstd = lax.rsqrt(s2 * (1.0 / d_model) - mean * mean + LN_EPS)
    for cols in pieces(d_model):
        v = o_ref[:, :, cols].reshape(rows, pw)
        o_ref[:, :, cols] = ((v - mean) * rstd * g_ref[:, cols] + be_ref[:, cols]).reshape(NPH, CT, pw)


def _mixout(alpha, ys5, yrg, gz, h, glu_w, glu_b, gn_s5, gn_rg, w_out, b_out, ln_g, ln_b):
    _, n_chunks, d_model = h.shape
    d_s5, d_rg = ys5.shape[-1], yrg.shape[-1]
    return pl.pallas_call(
        functools.partial(_mixout_kernel, alpha),
        grid=(n_chunks // CT,),
        in_specs=[
            _tile3(d_s5), _tile3(d_rg), _tile3(d_rg), _tile3(d_model),
            _resident((d_s5, d_s5)), _resident((1, d_s5)), _resident((1, d_s5)), _resident((1, d_rg)),
            _resident((d_s5 + d_rg, d_model)), _resident((1, d_model)),
            _resident((1, d_model)), _resident((1, d_model)),
        ],
        out_specs=_tile3(d_model),
        out_shape=jax.ShapeDtypeStruct((NPH, n_chunks, d_model), F32),
        compiler_params=_cparams(1),
        name="mixout",
    )(ys5, yrg, gz, h, glu_w, glu_b, gn_s5, gn_rg, w_out, b_out, ln_g, ln_b)


def _ffn_kernel(alpha, h_ref, p_ref, w1_ref, b1_ref, w2_ref, b2_ref, pw_ref, gw_ref, gb_ref,
                g_ref, be_ref, o_ref, hb_ref, acc_ref):
    j = pl.program_id(1)
    rows = NPH * CT

    @pl.when(j == 0)
    def _():
        h = h_ref[...].reshape(rows, -1)
        hb = h.astype(BF16)
        hb_ref[...] = hb
        gate = _logistic_base2(jnp.dot(hb, gw_ref[...], preferred_element_type=F32) + gb_ref[...])
        pe = _to_phase_major(p_ref[...]).astype(BF16)
        acc_ref[...] = alpha * h + gate * jnp.dot(pe, pw_ref[...], preferred_element_type=F32) + b2_ref[...]

    a = jnp.dot(hb_ref[...], w1_ref[...], preferred_element_type=F32) + b1_ref[...]
    a = jnp.maximum(a, 0.0)
    acc_ref[...] += jnp.dot((a * a).astype(BF16), w2_ref[...], preferred_element_type=F32)

    @pl.when(j == pl.num_programs(1) - 1)
    def _():
        o_ref[...] = _to_time_major(_layer_norm(acc_ref[...], g_ref[...], be_ref[...]))


def _ffn(alpha, h1, p2, w1, b1, w2, b2, ple_w, gate_w, gate_b, ln_g, ln_b, ff_tile):
    _, n_chunks, d_model = h1.shape
    d_ff = w1.shape[1]
    ple_dim = ple_w.shape[0]
    rows = NPH * CT
    return pl.pallas_call(
        functools.partial(_ffn_kernel, alpha),
        grid=(n_chunks // CT, d_ff // ff_tile),
        in_specs=[
            _tile3(d_model),
            pl.BlockSpec((rows, ple_dim), lambda i, j: (i, 0)),
            pl.BlockSpec((d_model, ff_tile), lambda i, j: (0, j)),
            pl.BlockSpec((1, ff_tile), lambda i, j: (0, j)),
            pl.BlockSpec((ff_tile, d_model), lambda i, j: (j, 0)),
            _resident((1, d_model)),
            _resident((ple_dim, d_model)), _resident((d_model, d_model)), _resident((1, d_model)),
            _resident((1, d_model)), _resident((1, d_model)),
        ],
        out_specs=pl.BlockSpec((rows, d_model), lambda i, j: (i, 0)),
        out_shape=jax.ShapeDtypeStruct((NPH * n_chunks, d_model), F32),
        scratch_shapes=[pltpu.VMEM((rows, d_model), BF16), pltpu.VMEM((rows, d_model), F32)],
        compiler_params=_cparams(2),
        name="ffn",
    )(h1, p2, w1, b1, w2, b2, ple_w, gate_w, gate_b, ln_g, ln_b)


def kernel(x, p, ln_in_g, ln_in_b, w_in, b_in, s5_lambda_re, s5_lambda_im, s5_log_step, s5_b_re, s5_b_im, s5_c_re, s5_c_im, s5_d, s5_glu_w, s5_glu_b, rg_conv_w, rg_conv_b, rg_wa, rg_ba, rg_wx, rg_bx, rg_lambda, gn_s5, gn_rg, w_out, b_out, ln1_g, ln1_b, w_ff1, b_ff1, w_ff2, b_ff2, ple_w, ple_gate_w, ple_gate_b, ln2_g, ln2_b):
    batch, seq, d_model = x.shape
    depth = w_in.shape[0]
    assert batch == 1 and depth == 1 and seq % (NPH * CT) == 0
    n_chunks = seq // NPH
    d_s5 = s5_glu_w.shape[-1]
    d_rg = rg_conv_w.shape[-1]
    alpha = (2.0 * depth) ** 0.25
    row = lambda v: v.reshape(1, -1).astype(F32)

    h, u3, xrg, gz = _inproj(x.reshape(seq, d_model), row(ln_in_g), row(ln_in_b),
                              w_in[0].astype(BF16), row(b_in[0]), d_s5, d_rg)

    ys5, (w_out_b, gate_w_b, w_ff1_b, w_ff2_b) = _s5(
        u3, s5_lambda_re[0], s5_lambda_im[0], s5_log_step[0], s5_b_re[0], s5_b_im[0], s5_c_re[0], s5_c_im[0],
        s5_d[0], [w_out[0], ple_gate_w[0], w_ff1[0], w_ff2[0]], scales=(1.0, -LOG2_E, 1.0, 1.0))

    yrg = _rglru(xrg, rg_conv_w[0].astype(F32), row(rg_conv_b[0]), rg_wa[0], rg_ba[0], rg_wx[0], rg_bx[0],
                 rg_lambda[0].astype(F32))

    h1 = _mixout(alpha, ys5, yrg, gz, h, (-LOG2_E * s5_glu_w[0]).astype(BF16), -LOG2_E * row(s5_glu_b[0]), row(gn_s5[0]),
                 row(gn_rg[0]), w_out_b, row(b_out[0]), row(ln1_g[0]), row(ln1_b[0]))

    out = _ffn(alpha, h1, p.reshape(seq, p.shape[-1]), w_ff1_b, row(b_ff1[0]),
               w_ff2_b, row(b_ff2[0]), ple_w[0].astype(BF16), gate_w_b,
               -LOG2_E * row(ple_gate_b[0]), row(ln2_g[0]), row(ln2_b[0]), ff_tile=1024)
    return out.reshape(batch, seq, d_model).astype(x.dtype)
```

```python
import functools
import math

import jax
import jax.numpy as jnp
from jax import lax
from jax.experimental import pallas as pl
from jax.experimental.pallas import tpu as pltpu

F32 = jnp.float32
BF16 = jnp.bfloat16

NPH = 16
CT = 32
S5_K = 16
S5_P = 64
S5_LB = 128
RG_CB = 128
CONV_W = 4
RG_C = 8.0
LN_EPS = 1e-5
LOG2_E = math.log2(math.e)
VMEM_LIMIT_V7X = 56 * 1024 * 1024
MXU_COLS_V7X = 256


def _cparams(n_axes):
    return pltpu.CompilerParams(
        dimension_semantics=("arbitrary",) * n_axes,
        vmem_limit_bytes=VMEM_LIMIT_V7X)


def _resident(shape):
    return pl.BlockSpec(shape, lambda *_: (0,) * len(shape), pipeline_mode=pl.Buffered(1))


def _tile3(width):
    return pl.BlockSpec((NPH, CT, width), lambda i, *_: (0, i, 0))


def _to_phase_major(x):
    w = x.shape[-1]
    return jnp.swapaxes(x.reshape(CT, NPH, w), 0, 1).reshape(NPH * CT, w)


def _to_time_major(x):
    w = x.shape[-1]
    return jnp.swapaxes(x.reshape(NPH, CT, w), 0, 1).reshape(CT * NPH, w)


def _layer_norm(x, g, b):
    mu = jnp.mean(x, axis=-1, keepdims=True)
    xc = x - mu
    var = jnp.mean(xc * xc, axis=-1, keepdims=True)
    return xc * lax.rsqrt(var + LN_EPS) * g + b


def _gelu(x):
    c = math.sqrt(2.0 / math.pi)
    return 0.5 * x * (1.0 + jnp.tanh(c * (x + 0.044715 * (x * x * x))))


def _sigmoid(x):
    return 1.0 / (1.0 + jnp.exp(-x))


def _inproj_kernel(x_ref, g_ref, b_ref, w_ref, bw_ref, h_ref, u_ref, xrg_ref, gz_ref):
    d_s5, d_rg = u_ref.shape[-1], xrg_ref.shape[-1]
    h = _layer_norm(_to_phase_major(x_ref[...]), g_ref[...], b_ref[...])
    h_ref[...] = h.reshape(h_ref.shape)
    proj = jnp.dot(h.astype(BF16), w_ref[...], preferred_element_type=F32) + bw_ref[...]
    u_ref[...] = proj[:, :d_s5].astype(BF16).reshape(u_ref.shape)
    xrg_ref[...] = proj[:, d_s5:d_s5 + d_rg].reshape(xrg_ref.shape)
    gz_ref[...] = _gelu(proj[:, d_s5 + d_rg:]).reshape(gz_ref.shape)


def _inproj(x2, ln_g, ln_b, w_in, b_in, d_s5, d_rg):
    seq, d_model = x2.shape
    n_chunks = seq // NPH
    return pl.pallas_call(
        _inproj_kernel,
        grid=(n_chunks // CT,),
        in_specs=[
            pl.BlockSpec((CT * NPH, d_model), lambda i: (i, 0)),
            _resident((1, d_model)), _resident((1, d_model)),
            _resident(w_in.shape), _resident(b_in.shape),
        ],
        out_specs=[_tile3(d_model), _tile3(d_s5), _tile3(d_rg), _tile3(d_rg)],
        out_shape=[
            jax.ShapeDtypeStruct((NPH, n_chunks, d_model), F32),
            jax.ShapeDtypeStruct((NPH, n_chunks, d_s5), BF16),
            jax.ShapeDtypeStruct((NPH, n_chunks, d_rg), F32),
            jax.ShapeDtypeStruct((NPH, n_chunks, d_rg), F32),
        ],
        compiler_params=_cparams(1),
        name="inproj",
    )(x2, ln_g, ln_b, w_in, b_in)


def _row_scan_exclusive(sre, sim, qre, qim, reverse):
    n, w = sre.shape
    row = lax.broadcasted_iota(jnp.int32, (n, w), 0)

    def shift(v, k):
        if k % 8 == 0:
            z = jnp.zeros((k, w), v.dtype)
            return jnp.concatenate([v[k:], z] if reverse else [z, v[:n - k]], axis=0)
        if reverse:
            return jnp.where(row < n - k, pltpu.roll(v, n - k, axis=0), 0.0)
        return jnp.where(row >= k, pltpu.roll(v, k, axis=0), 0.0)

    xre, xim = shift(sre, 1), shift(sim, 1)
    k, i = 1, 0
    while k < n:
        pr, pi = qre[i:i + 1], qim[i:i + 1]
        if k % 8 == 0:
            keep = slice(n - k, n) if reverse else slice(0, k)
            dst = slice(0, n - k) if reverse else slice(k, n)
            src = slice(k, n) if reverse else slice(0, n - k)
            sr, si = xre[src], xim[src]
            nre = xre[dst] + pr * sr - pi * si
            nim = xim[dst] + pr * si + pi * sr
            order = (lambda new, old: [new, old]) if reverse else (lambda new, old: [old, new])
            xre = jnp.concatenate(order(nre, xre[keep]), axis=0)
            xim = jnp.concatenate(order(nim, xim[keep]), axis=0)
        else:
            sr, si = shift(xre, k), shift(xim, k)
            xre, xim = xre + pr * sr - pi * si, xim + pr * si + pi * sr
        k *= 2
        i += 1
    return xre, xim


def _dot(a, b):
    return jnp.dot(a.astype(BF16), b.astype(BF16), preferred_element_type=F32)


def _s5_chunk_operators(par, dvec, half):
    tk = NPH * S5_K
    lane = lax.broadcasted_iota(jnp.int32, par.shape, 1)
    par = jnp.where((lane >= S5_P) == bool(half), par, 0.0)
    pa = par[0:128].T
    pb = par[128:256]
    lane16 = lax.broadcasted_iota(jnp.int32, (16, tk), 1)
    row16 = lax.broadcasted_iota(jnp.int32, (16, tk), 0)
    rep = (lax.shift_right_logical(lane16, 4) == row16).astype(F32)
    til = ((lane16 & 15) == row16).astype(F32)

    def cmul(ar, ai, br, bi):
        return ar * br - ai * bi, ar * bi + ai * br

    def c_of(r0):
        c_re, c_im = pb[r0 + 64:r0 + 80], pb[r0 + 80:r0 + 96]
        blocks = [cmul(c_re, c_im, pb[r0 + t:r0 + t + 1], pb[r0 + 16 + t:r0 + 17 + t]) for t in range(NPH)]
        return (jnp.concatenate([b[0] for b in blocks], axis=0),
                -jnp.concatenate([b[1] for b in blocks], axis=0))

    cc = jnp.concatenate(list(c_of(0)) + list(c_of(32)), axis=1)
    yield

    spread = {c0: (_dot(pa[:, c0:c0 + 16], rep), _dot(pa[:, c0 + 16:c0 + 32], rep),
                   _dot(pa[:, c0 + 64:c0 + 80], til), _dot(pa[:, c0 + 80:c0 + 96], til)) for c0 in (0, 32)}
    yield

    wf_re, wf_im = cmul(*spread[0])
    wb_re, wb_im = cmul(*spread[32])
    bc = jnp.concatenate([wf_re, wf_im, wb_re, wb_im], axis=0)
    yield

    kf = _dot(pb[64:80], wf_re) - _dot(pb[80:96], wf_im)
    kb = _dot(pb[96:112], wb_re) - _dot(pb[112:128], wb_im)
    yield

    blocks = []
    for t in range(NPH):
        left = S5_K * (NPH - 1 - t)
        right = S5_K * t
        f = kf if left == 0 else jnp.where(lane16 < tk - left, pltpu.roll(kf, tk - left, axis=1), 0.0)
        b = kb if right == 0 else jnp.where(lane16 >= right, pltpu.roll(kb, right, axis=1), 0.0)
        blocks.append(f + b)
    a = jnp.concatenate(blocks, axis=0)
    ri = lax.broadcasted_iota(jnp.int32, (tk, tk), 0)
    ci = lax.broadcasted_iota(jnp.int32, (tk, tk), 1)
    a = a + jnp.where(ri == ci, dvec, 0.0)
    return a.astype(BF16), bc.astype(BF16), cc.astype(BF16)


def _in_lockstep(generators):
    results = [None] * len(generators)
    pending = list(enumerate(generators))
    while pending:
        still = []
        for i, gen in pending:
            try:
                next(gen)
                still.append((i, gen))
            except StopIteration as stop:
                results[i] = stop.value
        pending = still
    return results


def _cast_stream_step(k, n_steps, srcs, dsts, inbufs, outbufs, sems):
    assert n_steps >= 2
    slot = lax.rem(k, 2)
    n_streams = len(srcs)
    rows = [src.shape[0] // n_steps for src in srcs]

    def read(i, step, sl):
        return pltpu.make_async_copy(srcs[i].at[pl.ds(step * rows[i], rows[i]), :], inbufs[i].at[sl],
                                     sems.at[i, 0, sl])

    def write(i, step, sl):
        return pltpu.make_async_copy(outbufs[i].at[sl], dsts[i].at[pl.ds(step * rows[i], rows[i]), :],
                                     sems.at[i, 1, sl])

    @pl.when(k == 0)
    def _():
        for i in range(n_streams):
            read(i, 0, 0).start()
            read(i, 1, 1).start()

    @pl.when(k >= 2)
    def _():
        for i in range(n_streams):
            write(i, k - 2, slot).wait()

    for i in range(n_streams):
        read(i, k, slot).wait()
        outbufs[i][slot] = inbufs[i][slot].astype(BF16)

    @pl.when(k + 2 < n_steps)
    def _():
        for i in range(n_streams):
            read(i, k + 2, slot).start()

    for i in range(n_streams):
        write(i, k, slot).start()

    @pl.when(k == n_steps - 1)
    def _():
        for i in range(n_streams):
            write(i, k - 1, 1 - slot).wait()
            write(i, k, slot).wait()


def _s5_kernel(n_w, n_grid, u_ref, *rest):
    raw_refs, d_ref, rest = rest[:7], rest[7], rest[8:]
    w_src, rest = rest[:n_w], rest[n_w:]
    y_ref, rest = rest[0], rest[1:]
    w_dst, rest = rest[:n_w], rest[n_w:]
    xt_ref, yt_ref, a_ref, bc_ref, cc_ref, q_ref, rest = rest[:6] + (rest[6:],)
    w_in_buf, w_out_buf, w_sems = rest[:n_w], rest[n_w:2 * n_w], rest[2 * n_w]
    n = u_ref.shape[1]
    p2 = 2 * S5_P
    tk = NPH * S5_K
    n_groups = S5_LB // S5_K
    n_pairs = n_groups // 2

    def to_rows(s, _):
        xt_ref[s] = u_ref[s].T
        return 0
    lax.fori_loop(0, NPH, to_rows, 0, unroll=8)

    pars = []
    for g in range(n_groups):
        par, q_ref[g] = _s5_discretise(g, *raw_refs)
        pars.append(par)
    stages = [_s5_chunk_operators(pars[g], d_ref[g], g % 2) for g in range(n_groups)]
    for g, (a, bc, cc) in enumerate(_in_lockstep(stages)):
        a_ref[g], bc_ref[g], cc_ref[g] = a, bc, cc

    def pair(gp):
        xs, ys = [], []
        for h in range(2):
            g = 2 * gp + h
            rows = pl.ds(pl.multiple_of(g * S5_K, S5_K), S5_K)
            x = xt_ref[:, rows, :].reshape(tk, n)
            xs.append(x)
            ys.append(jnp.dot(a_ref[g], x, preferred_element_type=F32))
        bc = jnp.concatenate([bc_ref[2 * gp], bc_ref[2 * gp + 1]], axis=1)
        s = jnp.dot(bc, jnp.concatenate(xs, axis=0), preferred_element_type=F32)
        yield
        st = [s[i * p2:(i + 1) * p2].T for i in range(4)]
        lane = lax.broadcasted_iota(jnp.int32, (4 * 16, p2), 1)
        q = jnp.where(lane < S5_P, q_ref[2 * gp], q_ref[2 * gp + 1])
        yield
        hf_re, hf_im = _row_scan_exclusive(st[0], st[1], q[0:16], q[16:32], False)
        hb_re, hb_im = _row_scan_exclusive(st[2], st[3], q[32:48], q[48:64], True)
        hin = jnp.concatenate([hf_re, hf_im, hb_re, hb_im], axis=1).astype(BF16)
        yield
        for h in range(2):
            g = 2 * gp + h
            rows = pl.ds(pl.multiple_of(g * S5_K, S5_K), S5_K)
            y = ys[h] + lax.dot_general(cc_ref[g], hin, (((1,), (1,)), ((), ())), preferred_element_type=F32)
            yt_ref[:, rows, :] = _gelu(y).reshape(NPH, S5_K, n)

    def two_pairs(it, _):
        first = pl.program_id(0) * n_pairs + 2 * it
        stream = functools.partial(_cast_stream_step, n_steps=n_grid * n_pairs, srcs=w_src, dsts=w_dst,
                                   inbufs=w_in_buf, outbufs=w_out_buf, sems=w_sems)
        stream(first)
        _in_lockstep([pair(2 * it), pair(2 * it + 1)])
        stream(first + 1)
        return 0
    lax.fori_loop(0, n_pairs // 2, two_pairs, 0)

    def to_lanes(t, _):
        y_ref[t] = yt_ref[t].T
        return 0
    lax.fori_loop(0, NPH, to_lanes, 0, unroll=8)


def _s5(u3, lam_re, lam_im, log_step, b_re, b_im, c_re, c_im, d, weights):
    assert NPH == 16 and S5_K == 16 and 2 * lam_re.shape[-1] == 128
    _, n_chunks, d_s5 = u3.shape
    raw = [lam_re, lam_im, log_step[..., None], b_re, b_im, c_re, c_im]
    dvec = jnp.tile(d.astype(F32), (1, NPH))[:, None, :]
    gpb = S5_LB // S5_K
    tk = NPH * S5_K
    n_grid = d_s5 // S5_LB
    n_steps = n_grid * (gpb // 2)
    chunk = lambda w: (w.shape[0] // n_steps, w.shape[1])
    assert all(w.shape[0] % (16 * n_steps) == 0 for w in weights)
    per_block = lambda n, *tail: pl.BlockSpec((n,) + tail, lambda b: (b,) + (0,) * len(tail))
    per_dir_block = lambda v: pl.BlockSpec((2, gpb) + v.shape[2:], lambda b: (0, b) + (0,) * (v.ndim - 2))
    hbm = pl.BlockSpec(memory_space=pl.ANY)
    outs = pl.pallas_call(
        functools.partial(_s5_kernel, len(weights), n_grid),
        grid=(n_grid,),
        in_specs=[pl.BlockSpec((NPH, n_chunks, S5_LB), lambda b: (0, 0, b))]
        + [per_dir_block(v) for v in raw] + [per_block(gpb, 1, tk)] + [hbm] * len(weights),
        out_specs=[pl.BlockSpec((NPH, n_chunks, S5_LB), lambda b: (0, 0, b))] + [hbm] * len(weights),
        out_shape=[jax.ShapeDtypeStruct((NPH, n_chunks, d_s5), F32)]
        + [jax.ShapeDtypeStruct(w.shape, BF16) for w in weights],
        scratch_shapes=[
            pltpu.VMEM((NPH, S5_LB, n_chunks), BF16),
            pltpu.VMEM((NPH, S5_LB, n_chunks), F32),
            pltpu.VMEM((gpb, tk, tk), BF16),
            pltpu.VMEM((gpb, 8 * S5_P, tk), BF16),
            pltpu.VMEM((gpb, tk, 8 * S5_P), BF16),
            pltpu.VMEM((gpb, 4 * 16, 2 * S5_P), F32),
        ] + [pltpu.VMEM((2,) + chunk(w), F32) for w in weights]
        + [pltpu.VMEM((2,) + chunk(w), BF16) for w in weights]
        + [pltpu.SemaphoreType.DMA((len(weights), 2, 2))],
        compiler_params=_cparams(1),
        name="s5",
    )(u3, *raw, dvec, *weights)
    return outs[0], outs[1:]


def _s5_discretise(g, lam_re_ref, lam_im_ref, log_step_ref, b_re_ref, b_im_ref, c_re_ref, c_im_ref):
    def cmul(x, y):
        return x[0] * y[0] - x[1] * y[1], x[0] * y[1] + x[1] * y[0]

    def rows(zs):
        return jnp.concatenate([z[0] for z in zs], axis=0), jnp.concatenate([z[1] for z in zs], axis=0)

    def one_direction(d):
        lre, lim = jnp.minimum(lam_re_ref[d, g:g + 1, :], -1e-4), lam_im_ref[d, g:g + 1, :]
        step = jnp.exp(log_step_ref[d, g:g + 1, :])
        are, aim = lre * step, lim * step
        mag = jnp.exp(are)
        lam_bar = (mag * jnp.cos(aim), mag * jnp.sin(aim))
        powers = [(jnp.ones_like(are), jnp.zeros_like(are))]
        for _ in range(NPH):
            powers.append(cmul(powers[-1], lam_bar))
        squares = [powers[NPH]]
        for _ in range(15):
            squares.append(cmul(squares[-1], squares[-1]))
        nr, ni = lam_bar[0] - 1.0, lam_bar[1]
        den = lre * lre + lim * lim
        z = ((nr * lre + ni * lim) / den, (ni * lre - nr * lim) / den)
        bbar = cmul(z, (b_re_ref[d, g].T, b_im_ref[d, g].T))
        c = (c_re_ref[d, g], c_im_ref[d, g])
        return powers, squares, bbar, c

    pf, qf, bbar_f, c_f = one_direction(0)
    pb, qb, bbar_b, c_b = one_direction(1)
    par = jnp.concatenate(
        list(rows(pf[NPH - 1::-1])) + list(rows(pb[:NPH])) + list(bbar_f) + list(bbar_b)
        + list(rows(pf[1:])) + list(rows(pb[NPH:0:-1])) + list(c_f) + list(c_b), axis=0)
    q_tab = jnp.concatenate(list(rows(qf)) + list(rows(qb)), axis=0)
    both_halves = lambda v: jnp.concatenate([v, v], axis=1)
    return both_halves(par), both_halves(q_tab)


def _row_scan_carry(a, h, reverse):
    n, w = a.shape
    row = lax.broadcasted_iota(jnp.int32, (n, w), 0)

    def shift(v, k, fill):
        if k % 8 == 0:
            z = jnp.full((k, w), fill, v.dtype)
            if reverse:
                return jnp.concatenate([v[k:], z], axis=0)
            return jnp.concatenate([z, v[:n - k]], axis=0)
        if reverse:
            return jnp.where(row < n - k, pltpu.roll(v, n - k, axis=0), fill)
        return jnp.where(row >= k, pltpu.roll(v, k, axis=0), fill)

    k = 1
    while k < n:
        h = h + a * shift(h, k, 0.0)
        a = a * shift(a, k, 1.0)
        k *= 2
    return shift(h, 1, 0.0)


def _rglru_kernel(x_ref, cw_ref, cb_ref, wa_ref, wx_ref, ba_ref, bx_ref, lam_ref, y_ref, xe_ref,
                  alf_ref, hlf_ref, alb_ref, hlb_ref):
    n, w = x_ref.shape[1], x_ref.shape[2]

    def block_diag(heads):
        per, hd, _ = heads.shape
        zeros = jnp.zeros((hd, hd), heads.dtype)
        return jnp.concatenate(
            [jnp.concatenate([heads[p] if q == p else zeros for q in range(per)], axis=1) for p in range(per)],
            axis=0)
    row = lax.broadcasted_iota(jnp.int32, (n, w), 0)

    def from_prev_chunk(v):
        return jnp.where(row >= 1, pltpu.roll(v, 1, axis=0), 0.0)

    def from_next_chunk(v):
        return jnp.where(row < n - 1, pltpu.roll(v, n - 1, axis=0), 0.0)

    xe_ref[0] = from_prev_chunk(x_ref[NPH - 2])
    xe_ref[1] = from_prev_chunk(x_ref[NPH - 1])

    def copy_body(s, _):
        xe_ref[s + 2] = x_ref[s]
        return 0
    lax.fori_loop(0, NPH, copy_body, 0, unroll=4)
    xe_ref[NPH + 2] = from_next_chunk(x_ref[0])

    cw = cw_ref[...]
    cb = cb_ref[...]
    lam = lam_ref[...]
    neg = -lam
    softplus = jnp.maximum(neg, 0.0) + jnp.log(1.0 + jnp.exp(-jnp.abs(neg)))

    directions = []
    for d, (reverse, al_ref, hl_ref) in enumerate(((False, alf_ref, hlf_ref), (True, alb_ref, hlb_ref))):
        rate2 = (-RG_C * LOG2_E) * softplus[d:d + 1]
        wg = (-LOG2_E * jnp.concatenate([block_diag(wa_ref[d]), block_diag(wx_ref[d])], axis=1)).astype(BF16)
        bg = -LOG2_E * jnp.concatenate([ba_ref[d, 0], bx_ref[d, 0]], axis=1)
        init = NPH if reverse else 0
        al_ref[init] = jnp.ones((n, w), F32)
        hl_ref[init] = jnp.zeros((n, w), F32)
        directions.append((reverse, al_ref, hl_ref, rate2, wg, bg))

    def local_step(i, reverse, al_ref, hl_ref, rate2, wg, bg):
        s = (NPH - 1 - i) if reverse else i
        xc = (cw[0:1] * xe_ref[s] + cw[1:2] * xe_ref[s + 1]
              + cw[2:3] * xe_ref[s + 2] + cw[3:4] * xe_ref[s + 3] + cb)
        yield
        g = jnp.dot(xc.astype(BF16), wg, preferred_element_type=F32) + bg
        yield
        r = 1.0 / (1.0 + jnp.exp2(g[:, :w]))
        ig = 1.0 / (1.0 + jnp.exp2(g[:, w:]))
        a = jnp.exp2(rate2 * r)
        bt = jnp.sqrt(1.0 - a * a) * (ig * xc)
        yield
        src = (s + 1) if reverse else s
        dst = s if reverse else (s + 1)
        hl_ref[dst] = a * hl_ref[src] + bt
        al_ref[dst] = a * al_ref[src]

    def local_body(i, _):
        _in_lockstep([local_step(i, *direction) for direction in directions])
        return 0
    lax.fori_loop(0, NPH, local_body, 0, unroll=4)

    carry_f = _row_scan_carry(alf_ref[NPH], hlf_ref[NPH], False)
    carry_b = _row_scan_carry(alb_ref[0], hlb_ref[0], True)

    def fix_body(s, _):
        y_ref[s] = (hlf_ref[s + 1] + alf_ref[s + 1] * carry_f) + (hlb_ref[s] + alb_ref[s] * carry_b)
        return 0
    lax.fori_loop(0, NPH, fix_body, 0, unroll=4)


def _rglru(xrg3, conv_w, conv_b, wa, ba, wx, bx, lam):
    _, n_chunks, d_rg = xrg3.shape
    nb = d_rg // RG_CB
    n_dir, heads, hd, _ = wa.shape
    per = RG_CB // hd
    assert heads == nb * per
    gate_w = pl.BlockSpec((n_dir, per, hd, hd), lambda j: (0, j, 0, 0))
    gate_b = pl.BlockSpec((n_dir, 1, 1, RG_CB), lambda j: (0, j, 0, 0))
    bias4 = lambda b: b.reshape(n_dir, nb, 1, RG_CB)
    return pl.pallas_call(
        _rglru_kernel,
        grid=(nb,),
        in_specs=[
            pl.BlockSpec((NPH, n_chunks, RG_CB), lambda j: (0, 0, j)),
            pl.BlockSpec((CONV_W, RG_CB), lambda j: (0, j)),
            pl.BlockSpec((1, RG_CB), lambda j: (0, j)),
            gate_w, gate_w, gate_b, gate_b,
            pl.BlockSpec((2, RG_CB), lambda j: (0, j)),
        ],
        out_specs=pl.BlockSpec((NPH, n_chunks, RG_CB), lambda j: (0, 0, j)),
        out_shape=jax.ShapeDtypeStruct((NPH, n_chunks, d_rg), F32),
        scratch_shapes=[
            pltpu.VMEM((NPH + 3, n_chunks, RG_CB), F32),
        ] + [pltpu.VMEM((NPH + 1, n_chunks, RG_CB), F32)] * 4,
        compiler_params=_cparams(1),
        name="rglru",
    )(xrg3, conv_w, conv_b, wa, wx, bias4(ba), bias4(bx), lam)


def _rms_norm(x, g):
    return x * lax.rsqrt(jnp.mean(x * x, axis=-1, keepdims=True) + LN_EPS) * g


def _mixout_kernel(alpha, ys5_ref, yrg_ref, gz_ref, h_ref, gw_ref, gb_ref, gn5_ref, gnr_ref,
                   w_ref, b_ref, g_ref, be_ref, o_ref):
    rows = NPH * CT
    d_s5, d_model = ys5_ref.shape[-1], o_ref.shape[-1]
    pw = MXU_COLS_V7X
    pieces = lambda width: [slice(c, c + pw) for c in range(0, width, pw)]

    y = ys5_ref[...].reshape(rows, d_s5)
    yb = y.astype(BF16)
    glu, sq = [], 0.0
    for cols in pieces(d_s5):
        g = jnp.dot(yb, gw_ref[:, cols], preferred_element_type=F32) + gb_ref[:, cols]
        o = y[:, cols] * _sigmoid(g)
        sq = sq + jnp.sum(o * o, axis=-1, keepdims=True)
        glu.append(o)
    scale = lax.rsqrt(sq * (1.0 / d_s5) + LN_EPS)
    ys5n = jnp.concatenate([o * scale * gn5_ref[:, cols] for o, cols in zip(glu, pieces(d_s5))],
                           axis=1).astype(BF16)

    yrg = yrg_ref[...].reshape(rows, -1) * gz_ref[...].reshape(rows, -1)
    yrgn = _rms_norm(yrg, gnr_ref[...]).astype(BF16)

    s1, s2 = 0.0, 0.0
    for cols in pieces(d_model):
        v = (jnp.dot(ys5n, w_ref[:d_s5, cols], preferred_element_type=F32)
             + jnp.dot(yrgn, w_ref[d_s5:, cols], preferred_element_type=F32)
             + b_ref[:, cols] + alpha * h_ref[:, :, cols].reshape(rows, pw))
        s1 = s1 + jnp.sum(v, axis=-1, keepdims=True)
        s2 = s2 + jnp.sum(v * v, axis=-1, keepdims=True)
        o_ref[:, :, cols] = v.reshape(NPH, CT, pw)
    mean = s1 * (1.0 / d_model)
    var = jnp.maximum(s2 * (1.0 / d_model) - mean * mean, 0.0)
    rstd = lax.rsqrt(var + LN_EPS)
    for cols in pieces(d_model):
        v = o_ref[:, :, cols].reshape(rows, pw)
        o_ref[:, :, cols] = ((v - mean) * rstd * g_ref[:, cols] + be_ref[:, cols]).reshape(NPH, CT, pw)


def _mixout(alpha, ys5, yrg, gz, h, glu_w, glu_b, gn_s5, gn_rg, w_out, b_out, ln_g, ln_b):
    _, n_chunks, d_model = h.shape
    d_s5, d_rg = ys5.shape[-1], yrg.shape[-1]
    return pl.pallas_call(
        functools.partial(_mixout_kernel, alpha),
        grid=(n_chunks // CT,),
        in_specs=[
            _tile3(d_s5), _tile3(d_rg), _tile3(d_rg), _tile3(d_model),
            _resident((d_s5, d_s5)), _resident((1, d_s5)), _resident((1, d_s5)), _resident((1, d_rg)),
            _resident((d_s5 + d_rg, d_model)), _resident((1, d_model)),
            _resident((1, d_model)), _resident((1, d_model)),
        ],
        out_specs=_tile3(d_model),
        out_shape=jax.ShapeDtypeStruct((NPH, n_chunks, d_model), F32),
        compiler_params=_cparams(1),
        name="mixout",
    )(ys5, yrg, gz, h, glu_w, glu_b, gn_s5, gn_rg, w_out, b_out, ln_g, ln_b)


def _ffn_kernel(alpha, h_ref, p_ref, w1_ref, b1_ref, w2_ref, b2_ref, pw_ref, gw_ref, gb_ref,
                g_ref, be_ref, o_ref, hb_ref, acc_ref):
    j = pl.program_id(1)
    rows = NPH * CT

    @pl.when(j == 0)
    def _():
        h = h_ref[...].reshape(rows, -1)
        hb = h.astype(BF16)
        hb_ref[...] = hb
        gate = _sigmoid(jnp.dot(hb, gw_ref[...], preferred_element_type=F32) + gb_ref[...])
        pe = _to_phase_major(p_ref[...]).astype(BF16)
        acc_ref[...] = alpha * h + gate * jnp.dot(pe, pw_ref[...], preferred_element_type=F32) + b2_ref[...]

    a = jnp.dot(hb_ref[...], w1_ref[...], preferred_element_type=F32) + b1_ref[...]
    a = jnp.maximum(a, 0.0)
    acc_ref[...] += jnp.dot((a * a).astype(BF16), w2_ref[...], preferred_element_type=F32)

    @pl.when(j == pl.num_programs(1) - 1)
    def _():
        o_ref[...] = _to_time_major(_layer_norm(acc_ref[...], g_ref[...], be_ref[...]))


def _ffn(alpha, h1, p2, w1, b1, w2, b2, ple_w, gate_w, gate_b, ln_g, ln_b, ff_tile):
    _, n_chunks, d_model = h1.shape
    d_ff = w1.shape[1]
    ple_dim = ple_w.shape[0]
    rows = NPH * CT
    return pl.pallas_call(
        functools.partial(_ffn_kernel, alpha),
        grid=(n_chunks // CT, d_ff // ff_tile),
        in_specs=[
            _tile3(d_model),
            pl.BlockSpec((rows, ple_dim), lambda i, j: (i, 0)),
            pl.BlockSpec((d_model, ff_tile), lambda i, j: (0, j)),
            pl.BlockSpec((1, ff_tile), lambda i, j: (0, j)),
            pl.BlockSpec((ff_tile, d_model), lambda i, j: (j, 0)),
            _resident((1, d_model)),
            _resident((ple_dim, d_model)), _resident((d_model, d_model)), _resident((1, d_model)),
            _resident((1, d_model)), _resident((1, d_model)),
        ],
        out_specs=pl.BlockSpec((rows, d_model), lambda i, j: (i, 0)),
        out_shape=jax.ShapeDtypeStruct((NPH * n_chunks, d_model), F32),
        scratch_shapes=[pltpu.VMEM((rows, d_model), BF16), pltpu.VMEM((rows, d_model), F32)],
        compiler_params=_cparams(2),
        name="ffn",
    )(h1, p2, w1, b1, w2, b2, ple_w, gate_w, gate_b, ln_g, ln_b)


def kernel(x, p, ln_in_g, ln_in_b, w_in, b_in, s5_lambda_re, s5_lambda_im, s5_log_step, s5_b_re, s5_b_im, s5_c_re, s5_c_im, s5_d, s5_glu_w, s5_glu_b, rg_conv_w, rg_conv_b, rg_wa, rg_ba, rg_wx, rg_bx, rg_lambda, gn_s5, gn_rg, w_out, b_out, ln1_g, ln1_b, w_ff1, b_ff1, w_ff2, b_ff2, ple_w, ple_gate_w, ple_gate_b, ln2_g, ln2_b):
    batch, seq, d_model = x.shape
    depth = w_in.shape[0]
    assert batch == 1 and depth == 1 and seq % (NPH * CT) == 0
    n_chunks = seq // NPH
    d_s5 = s5_glu_w.shape[-1]
    d_rg = rg_conv_w.shape[-1]
    alpha = (2.0 * depth) ** 0.25
    row = lambda v: v.reshape(1, -1).astype(F32)

    h, u3, xrg, gz = _inproj(x.reshape(seq, d_model), row(ln_in_g), row(ln_in_b),
                              w_in[0].astype(BF16), row(b_in[0]), d_s5, d_rg)

    ys5, (w_out_b, gate_w_b, w_ff1_b, w_ff2_b) = _s5(
        u3, s5_lambda_re[0], s5_lambda_im[0], s5_log_step[0], s5_b_re[0], s5_b_im[0], s5_c_re[0], s5_c_im[0],
        s5_d[0], [w_out[0], ple_gate_w[0], w_ff1[0], w_ff2[0]])

    yrg = _rglru(xrg, rg_conv_w[0].astype(F32), row(rg_conv_b[0]), rg_wa[0], rg_ba[0], rg_wx[0], rg_bx[0],
                 rg_lambda[0].astype(F32))

    h1 = _mixout(alpha, ys5, yrg, gz, h, s5_glu_w[0].astype(BF16), row(s5_glu_b[0]), row(gn_s5[0]),
                 row(gn_rg[0]), w_out_b, row(b_out[0]), row(ln1_g[0]), row(ln1_b[0]))

    out = _ffn(alpha, h1, p.reshape(seq, p.shape[-1]), w_ff1_b, row(b_ff1[0]),
               w_ff2_b, row(b_ff2[0]), ple_w[0].astype(BF16), gate_w_b,
               row(ple_gate_b[0]), row(ln2_g[0]), row(ln2_b[0]), ff_tile=1024)
    return out.reshape(batch, seq, d_model).astype(x.dtype)
```

```python
import functools
import math

import jax
import jax.numpy as jnp
from jax import lax
from jax.experimental import pallas as pl
from jax.experimental.pallas import tpu as pltpu

F32 = jnp.float32
BF16 = jnp.bfloat16

NPH = 16
CT = 32
S5_K = 16
S5_P = 64
S5_LB = 128
RG_CB = 128
CONV_W = 4
RG_C = 8.0
LN_EPS = 1e-5
LOG2_E = math.log2(math.e)
VMEM_LIMIT_V7X = 56 * 1024 * 1024
MXU_COLS_V7X = 256


def _cparams(n_axes):
    return pltpu.CompilerParams(
        dimension_semantics=("arbitrary",) * n_axes,
        vmem_limit_bytes=VMEM_LIMIT_V7X)


def _resident(shape):
    return pl.BlockSpec(shape, lambda *_: (0,) * len(shape), pipeline_mode=pl.Buffered(1))


def _tile3(width):
    return pl.BlockSpec((NPH, CT, width), lambda i, *_: (0, i, 0))


def _to_phase_major(x):
    w = x.shape[-1]
    return jnp.swapaxes(x.reshape(CT, NPH, w), 0, 1).reshape(NPH * CT, w)


def _to_time_major(x):
    w = x.shape[-1]
    return jnp.swapaxes(x.reshape(NPH, CT, w), 0, 1).reshape(CT * NPH, w)


def _layer_norm(x, g, b):
    mu = jnp.mean(x, axis=-1, keepdims=True)
    xc = x - mu
    var = jnp.mean(xc * xc, axis=-1, keepdims=True)
    return xc * lax.rsqrt(var + LN_EPS) * g + b


def _gelu(x):
    c = math.sqrt(2.0 / math.pi)
    return 0.5 * x * (1.0 + jnp.tanh(c * (x + 0.044715 * (x * x * x))))


def _sigmoid(x):
    return 1.0 / (1.0 + jnp.exp(-x))


def _inproj_kernel(x_ref, g_ref, b_ref, w_ref, bw_ref, h_ref, u_ref, xrg_ref, gz_ref):
    d_s5, d_rg = u_ref.shape[-1], xrg_ref.shape[-1]
    h = _layer_norm(_to_phase_major(x_ref[...]), g_ref[...], b_ref[...])
    h_ref[...] = h.reshape(h_ref.shape)
    proj = jnp.dot(h.astype(BF16), w_ref[...], preferred_element_type=F32) + bw_ref[...]
    u_ref[...] = proj[:, :d_s5].astype(BF16).reshape(u_ref.shape)
    xrg_ref[...] = proj[:, d_s5:d_s5 + d_rg].reshape(xrg_ref.shape)
    gz_ref[...] = _gelu(proj[:, d_s5 + d_rg:]).reshape(gz_ref.shape)


def _inproj(x2, ln_g, ln_b, w_in, b_in, d_s5, d_rg):
    seq, d_model = x2.shape
    n_chunks = seq // NPH
    return pl.pallas_call(
        _inproj_kernel,
        grid=(n_chunks // CT,),
        in_specs=[
            pl.BlockSpec((CT * NPH, d_model), lambda i: (i, 0)),
            _resident((1, d_model)), _resident((1, d_model)),
            _resident(w_in.shape), _resident(b_in.shape),
        ],
        out_specs=[_tile3(d_model), _tile3(d_s5), _tile3(d_rg), _tile3(d_rg)],
        out_shape=[
            jax.ShapeDtypeStruct((NPH, n_chunks, d_model), F32),
            jax.ShapeDtypeStruct((NPH, n_chunks, d_s5), BF16),
            jax.ShapeDtypeStruct((NPH, n_chunks, d_rg), F32),
            jax.ShapeDtypeStruct((NPH, n_chunks, d_rg), F32),
        ],
        compiler_params=_cparams(1),
        name="inproj",
    )(x2, ln_g, ln_b, w_in, b_in)


def _row_scan_exclusive(sre, sim, qre, qim, reverse):
    n, w = sre.shape
    row = lax.broadcasted_iota(jnp.int32, (n, w), 0)

    def shift(v, k):
        if k % 8 == 0:
            z = jnp.zeros((k, w), v.dtype)
            return jnp.concatenate([v[k:], z] if reverse else [z, v[:n - k]], axis=0)
        if reverse:
            return jnp.where(row < n - k, pltpu.roll(v, n - k, axis=0), 0.0)
        return jnp.where(row >= k, pltpu.roll(v, k, axis=0), 0.0)

    xre, xim = shift(sre, 1), shift(sim, 1)
    k, i = 1, 0
    while k < n:
        pr, pi = qre[i:i + 1], qim[i:i + 1]
        if k % 8 == 0:
            keep = slice(n - k, n) if reverse else slice(0, k)
            dst = slice(0, n - k) if reverse else slice(k, n)
            src = slice(k, n) if reverse else slice(0, n - k)
            sr, si = xre[src], xim[src]
            nre = xre[dst] + pr * sr - pi * si
            nim = xim[dst] + pr * si + pi * sr
            order = (lambda new, old: [new, old]) if reverse else (lambda new, old: [old, new])
            xre = jnp.concatenate(order(nre, xre[keep]), axis=0)
            xim = jnp.concatenate(order(nim, xim[keep]), axis=0)
        else:
            sr, si = shift(xre, k), shift(xim, k)
            xre, xim = xre + pr * sr - pi * si, xim + pr * si + pi * sr
        k *= 2
        i += 1
    return xre, xim


def _dot(a, b):
    return jnp.dot(a.astype(BF16), b.astype(BF16), preferred_element_type=F32)


def _s5_chunk_operators(par, dvec, half):
    tk = NPH * S5_K
    lane = lax.broadcasted_iota(jnp.int32, par.shape, 1)
    par = jnp.where((lane >= S5_P) == bool(half), par, 0.0)
    pa = par[0:128].T
    pb = par[128:256]
    lane16 = lax.broadcasted_iota(jnp.int32, (16, tk), 1)
    row16 = lax.broadcasted_iota(jnp.int32, (16, tk), 0)
    rep = (lax.shift_right_logical(lane16, 4) == row16).astype(F32)
    til = ((lane16 & 15) == row16).astype(F32)

    def cmul(ar, ai, br, bi):
        return ar * br - ai * bi, ar * bi + ai * br

    def c_of(r0):
        c_re, c_im = pb[r0 + 64:r0 + 80], pb[r0 + 80:r0 + 96]
        blocks = [cmul(c_re, c_im, pb[r0 + t:r0 + t + 1], pb[r0 + 16 + t:r0 + 17 + t]) for t in range(NPH)]
        return (jnp.concatenate([b[0] for b in blocks], axis=0),
                -jnp.concatenate([b[1] for b in blocks], axis=0))

    cc = jnp.concatenate(list(c_of(0)) + list(c_of(32)), axis=1)
    yield

    spread = {c0: (_dot(pa[:, c0:c0 + 16], rep), _dot(pa[:, c0 + 16:c0 + 32], rep),
                   _dot(pa[:, c0 + 64:c0 + 80], til), _dot(pa[:, c0 + 80:c0 + 96], til)) for c0 in (0, 32)}
    yield

    wf_re, wf_im = cmul(*spread[0])
    wb_re, wb_im = cmul(*spread[32])
    bc = jnp.concatenate([wf_re, wf_im, wb_re, wb_im], axis=0)
    yield

    kf = _dot(pb[64:80], wf_re) - _dot(pb[80:96], wf_im)
    kb = _dot(pb[96:112], wb_re) - _dot(pb[112:128], wb_im)
    yield

    blocks = []
    for t in range(NPH):
        left = S5_K * (NPH - 1 - t)
        right = S5_K * t
        f = kf if left == 0 else jnp.where(lane16 < tk - left, pltpu.roll(kf, tk - left, axis=1), 0.0)
        b = kb if right == 0 else jnp.where(lane16 >= right, pltpu.roll(kb, right, axis=1), 0.0)
        blocks.append(f + b)
    a = jnp.concatenate(blocks, axis=0)
    ri = lax.broadcasted_iota(jnp.int32, (tk, tk), 0)
    ci = lax.broadcasted_iota(jnp.int32, (tk, tk), 1)
    a = a + jnp.where(ri == ci, dvec, 0.0)
    return a.astype(BF16), bc.astype(BF16), cc.astype(BF16)


def _in_lockstep(generators):
    results = [None] * len(generators)
    pending = list(enumerate(generators))
    while pending:
        still = []
        for i, gen in pending:
            try:
                next(gen)
                still.append((i, gen))
            except StopIteration as stop:
                results[i] = stop.value
        pending = still
    return results


def _cast_stream_step(k, n_steps, srcs, dsts, inbufs, outbufs, sems):
    assert n_steps >= 2
    slot = lax.rem(k, 2)
    n_streams = len(srcs)
    rows = [src.shape[0] // n_steps for src in srcs]

    def read(i, step, sl):
        return pltpu.make_async_copy(srcs[i].at[pl.ds(step * rows[i], rows[i]), :], inbufs[i].at[sl],
                                     sems.at[i, 0, sl])

    def write(i, step, sl):
        return pltpu.make_async_copy(outbufs[i].at[sl], dsts[i].at[pl.ds(step * rows[i], rows[i]), :],
                                     sems.at[i, 1, sl])

    @pl.when(k == 0)
    def _():
        for i in range(n_streams):
            read(i, 0, 0).start()
            read(i, 1, 1).start()

    @pl.when(k >= 2)
    def _():
        for i in range(n_streams):
            write(i, k - 2, slot).wait()

    for i in range(n_streams):
        read(i, k, slot).wait()
        outbufs[i][slot] = inbufs[i][slot].astype(BF16)

    @pl.when(k + 2 < n_steps)
    def _():
        for i in range(n_streams):
            read(i, k + 2, slot).start()

    for i in range(n_streams):
        write(i, k, slot).start()

    @pl.when(k == n_steps - 1)
    def _():
        for i in range(n_streams):
            write(i, k - 1, 1 - slot).wait()
            write(i, k, slot).wait()


def _s5_kernel(n_w, n_grid, u_ref, *rest):
    raw_refs, d_ref, rest = rest[:7], rest[7], rest[8:]
    w_src, rest = rest[:n_w], rest[n_w:]
    y_ref, rest = rest[0], rest[1:]
    w_dst, rest = rest[:n_w], rest[n_w:]
    xt_ref, yt_ref, a_ref, bc_ref, cc_ref, q_ref, rest = rest[:6] + (rest[6:],)
    w_in_buf, w_out_buf, w_sems = rest[:n_w], rest[n_w:2 * n_w], rest[2 * n_w]
    n = u_ref.shape[1]
    p2 = 2 * S5_P
    tk = NPH * S5_K
    n_groups = S5_LB // S5_K
    n_pairs = n_groups // 2

    def to_rows(s, _):
        xt_ref[s] = u_ref[s].T
        return 0
    lax.fori_loop(0, NPH, to_rows, 0, unroll=8)

    pars = []
    for g in range(n_groups):
        par, q_ref[g] = _s5_discretise(g, *raw_refs)
        pars.append(par)
    stages = [_s5_chunk_operators(pars[g], d_ref[g], g % 2) for g in range(n_groups)]
    for g, (a, bc, cc) in enumerate(_in_lockstep(stages)):
        a_ref[g], bc_ref[g], cc_ref[g] = a, bc, cc

    def pair(gp):
        xs, ys = [], []
        for h in range(2):
            g = 2 * gp + h
            rows = pl.ds(pl.multiple_of(g * S5_K, S5_K), S5_K)
            x = xt_ref[:, rows, :].reshape(tk, n)
            xs.append(x)
            ys.append(jnp.dot(a_ref[g], x, preferred_element_type=F32))
        bc = jnp.concatenate([bc_ref[2 * gp], bc_ref[2 * gp + 1]], axis=1)
        s = jnp.dot(bc, jnp.concatenate(xs, axis=0), preferred_element_type=F32)
        yield
        st = [s[i * p2:(i + 1) * p2].T for i in range(4)]
        lane = lax.broadcasted_iota(jnp.int32, (4 * 16, p2), 1)
        q = jnp.where(lane < S5_P, q_ref[2 * gp], q_ref[2 * gp + 1])
        yield
        hf_re, hf_im = _row_scan_exclusive(st[0], st[1], q[0:16], q[16:32], False)
        hb_re, hb_im = _row_scan_exclusive(st[2], st[3], q[32:48], q[48:64], True)
        hin = jnp.concatenate([hf_re, hf_im, hb_re, hb_im], axis=1).astype(BF16)
        yield
        for h in range(2):
            g = 2 * gp + h
            rows = pl.ds(pl.multiple_of(g * S5_K, S5_K), S5_K)
            y = ys[h] + lax.dot_general(cc_ref[g], hin, (((1,), (1,)), ((), ())), preferred_element_type=F32)
            yt_ref[:, rows, :] = _gelu(y).reshape(NPH, S5_K, n)

    def two_pairs(it, _):
        first = pl.program_id(0) * n_pairs + 2 * it
        stream = functools.partial(_cast_stream_step, n_steps=n_grid * n_pairs, srcs=w_src, dsts=w_dst,
                                   inbufs=w_in_buf, outbufs=w_out_buf, sems=w_sems)
        stream(first)
        _in_lockstep([pair(2 * it), pair(2 * it + 1)])
        stream(first + 1)
        return 0
    lax.fori_loop(0, n_pairs // 2, two_pairs, 0)

    def to_lanes(t, _):
        y_ref[t] = yt_ref[t].T
        return 0
    lax.fori_loop(0, NPH, to_lanes, 0, unroll=8)


def _s5(u3, lam_re, lam_im, log_step, b_re, b_im, c_re, c_im, d, weights):
    assert NPH == 16 and S5_K == 16 and 2 * lam_re.shape[-1] == 128
    _, n_chunks, d_s5 = u3.shape
    raw = [lam_re, lam_im, log_step[..., None], b_re, b_im, c_re, c_im]
    dvec = jnp.tile(d.astype(F32), (1, NPH))[:, None, :]
    gpb = S5_LB // S5_K
    tk = NPH * S5_K
    n_grid = d_s5 // S5_LB
    n_steps = n_grid * (gpb // 2)
    chunk = lambda w: (w.shape[0] // n_steps, w.shape[1])
    assert all(w.shape[0] % (16 * n_steps) == 0 for w in weights)
    per_block = lambda n, *tail: pl.BlockSpec((n,) + tail, lambda b: (b,) + (0,) * len(tail))
    per_dir_block = lambda v: pl.BlockSpec((2, gpb) + v.shape[2:], lambda b: (0, b) + (0,) * (v.ndim - 2))
    hbm = pl.BlockSpec(memory_space=pl.ANY)
    outs = pl.pallas_call(
        functools.partial(_s5_kernel, len(weights), n_grid),
        grid=(n_grid,),
        in_specs=[pl.BlockSpec((NPH, n_chunks, S5_LB), lambda b: (0, 0, b))]
        + [per_dir_block(v) for v in raw] + [per_block(gpb, 1, tk)] + [hbm] * len(weights),
        out_specs=[pl.BlockSpec((NPH, n_chunks, S5_LB), lambda b: (0, 0, b))] + [hbm] * len(weights),
        out_shape=[jax.ShapeDtypeStruct((NPH, n_chunks, d_s5), F32)]
        + [jax.ShapeDtypeStruct(w.shape, BF16) for w in weights],
        scratch_shapes=[
            pltpu.VMEM((NPH, S5_LB, n_chunks), BF16),
            pltpu.VMEM((NPH, S5_LB, n_chunks), F32),
            pltpu.VMEM((gpb, tk, tk), BF16),
            pltpu.VMEM((gpb, 8 * S5_P, tk), BF16),
            pltpu.VMEM((gpb, tk, 8 * S5_P), BF16),
            pltpu.VMEM((gpb, 4 * 16, 2 * S5_P), F32),
        ] + [pltpu.VMEM((2,) + chunk(w), F32) for w in weights]
        + [pltpu.VMEM((2,) + chunk(w), BF16) for w in weights]
        + [pltpu.SemaphoreType.DMA((len(weights), 2, 2))],
        compiler_params=_cparams(1),
        name="s5",
    )(u3, *raw, dvec, *weights)
    return outs[0], outs[1:]


def _s5_discretise(g, lam_re_ref, lam_im_ref, log_step_ref, b_re_ref, b_im_ref, c_re_ref, c_im_ref):
    def cmul(x, y):
        return x[0] * y[0] - x[1] * y[1], x[0] * y[1] + x[1] * y[0]

    def rows(zs):
        return jnp.concatenate([z[0] for z in zs], axis=0), jnp.concatenate([z[1] for z in zs], axis=0)

    def one_direction(d):
        lre, lim = jnp.minimum(lam_re_ref[d, g:g + 1, :], -1e-4), lam_im_ref[d, g:g + 1, :]
        step = jnp.exp(log_step_ref[d, g:g + 1, :])
        are, aim = lre * step, lim * step
        mag = jnp.exp(are)
        lam_bar = (mag * jnp.cos(aim), mag * jnp.sin(aim))
        powers = [(jnp.ones_like(are), jnp.zeros_like(are))]
        for _ in range(NPH):
            powers.append(cmul(powers[-1], lam_bar))
        squares = [powers[NPH]]
        for _ in range(15):
            squares.append(cmul(squares[-1], squares[-1]))
        nr, ni = lam_bar[0] - 1.0, lam_bar[1]
        den = lre * lre + lim * lim
        z = ((nr * lre + ni * lim) / den, (ni * lre - nr * lim) / den)
        bbar = cmul(z, (b_re_ref[d, g].T, b_im_ref[d, g].T))
        c = (c_re_ref[d, g], c_im_ref[d, g])
        return powers, squares, bbar, c

    pf, qf, bbar_f, c_f = one_direction(0)
    pb, qb, bbar_b, c_b = one_direction(1)
    par = jnp.concatenate(
        list(rows(pf[NPH - 1::-1])) + list(rows(pb[:NPH])) + list(bbar_f) + list(bbar_b)
        + list(rows(pf[1:])) + list(rows(pb[NPH:0:-1])) + list(c_f) + list(c_b), axis=0)
    q_tab = jnp.concatenate(list(rows(qf)) + list(rows(qb)), axis=0)
    both_halves = lambda v: jnp.concatenate([v, v], axis=1)
    return both_halves(par), both_halves(q_tab)


def _row_scan_carry(a, h, reverse):
    n, w = a.shape
    row = lax.broadcasted_iota(jnp.int32, (n, w), 0)

    def shift(v, k, fill):
        if k % 8 == 0:
            z = jnp.full((k, w), fill, v.dtype)
            if reverse:
                return jnp.concatenate([v[k:], z], axis=0)
            return jnp.concatenate([z, v[:n - k]], axis=0)
        if reverse:
            return jnp.where(row < n - k, pltpu.roll(v, n - k, axis=0), fill)
        return jnp.where(row >= k, pltpu.roll(v, k, axis=0), fill)

    k = 1
    while k < n:
        h = h + a * shift(h, k, 0.0)
        a = a * shift(a, k, 1.0)
        k *= 2
    return shift(h, 1, 0.0)


def _rglru_kernel(x_ref, cw_ref, cb_ref, wa_ref, wx_ref, ba_ref, bx_ref, lam_ref, y_ref, xe_ref,
                  alf_ref, hlf_ref, alb_ref, hlb_ref):
    n, w = x_ref.shape[1], x_ref.shape[2]

    def block_diag(heads):
        per, hd, _ = heads.shape
        zeros = jnp.zeros((hd, hd), heads.dtype)
        return jnp.concatenate(
            [jnp.concatenate([heads[p] if q == p else zeros for q in range(per)], axis=1) for p in range(per)],
            axis=0)
    row = lax.broadcasted_iota(jnp.int32, (n, w), 0)

    def from_prev_chunk(v):
        return jnp.where(row >= 1, pltpu.roll(v, 1, axis=0), 0.0)

    def from_next_chunk(v):
        return jnp.where(row < n - 1, pltpu.roll(v, n - 1, axis=0), 0.0)

    xe_ref[0] = from_prev_chunk(x_ref[NPH - 2])
    xe_ref[1] = from_prev_chunk(x_ref[NPH - 1])

    def copy_body(s, _):
        xe_ref[s + 2] = x_ref[s]
        return 0
    lax.fori_loop(0, NPH, copy_body, 0, unroll=4)
    xe_ref[NPH + 2] = from_next_chunk(x_ref[0])

    cw = cw_ref[...]
    cb = cb_ref[...]
    lam = lam_ref[...]
    neg = -lam
    softplus = jnp.maximum(neg, 0.0) + jnp.log(1.0 + jnp.exp(-jnp.abs(neg)))

    directions = []
    for d, (reverse, al_ref, hl_ref) in enumerate(((False, alf_ref, hlf_ref), (True, alb_ref, hlb_ref))):
        rate2 = (-RG_C * LOG2_E) * softplus[d:d + 1]
        wg = (-LOG2_E * jnp.concatenate([block_diag(wa_ref[d]), block_diag(wx_ref[d])], axis=1)).astype(BF16)
        bg = -LOG2_E * jnp.concatenate([ba_ref[d, 0], bx_ref[d, 0]], axis=1)
        init = NPH if reverse else 0
        al_ref[init] = jnp.ones((n, w), F32)
        hl_ref[init] = jnp.zeros((n, w), F32)
        directions.append((reverse, al_ref, hl_ref, rate2, wg, bg))

    def local_step(i, reverse, al_ref, hl_ref, rate2, wg, bg):
        s = (NPH - 1 - i) if reverse else i
        xc = (cw[0:1] * xe_ref[s] + cw[1:2] * xe_ref[s + 1]
              + cw[2:3] * xe_ref[s + 2] + cw[3:4] * xe_ref[s + 3] + cb)
        yield
        g = jnp.dot(xc.astype(BF16), wg, preferred_element_type=F32) + bg
        yield
        r = 1.0 / (1.0 + jnp.exp2(g[:, :w]))
        ig = 1.0 / (1.0 + jnp.exp2(g[:, w:]))
        a = jnp.exp2(rate2 * r)
        bt = jnp.sqrt(1.0 - a * a) * (ig * xc)
        yield
        src = (s + 1) if reverse else s
        dst = s if reverse else (s + 1)
        hl_ref[dst] = a * hl_ref[src] + bt
        al_ref[dst] = a * al_ref[src]

    def local_body(i, _):
        _in_lockstep([local_step(i, *direction) for direction in directions])
        return 0
    lax.fori_loop(0, NPH, local_body, 0, unroll=4)

    carry_f = _row_scan_carry(alf_ref[NPH], hlf_ref[NPH], False)
    carry_b = _row_scan_carry(alb_ref[0], hlb_ref[0], True)

    def fix_body(s, _):
        y_ref[s] = (hlf_ref[s + 1] + alf_ref[s + 1] * carry_f) + (hlb_ref[s] + alb_ref[s] * carry_b)
        return 0
    lax.fori_loop(0, NPH, fix_body, 0, unroll=4)


def _rglru(xrg3, conv_w, conv_b, wa, ba, wx, bx, lam):
    _, n_chunks, d_rg = xrg3.shape
    nb = d_rg // RG_CB
    n_dir, heads, hd, _ = wa.shape
    per = RG_CB // hd
    assert heads == nb * per
    gate_w = pl.BlockSpec((n_dir, per, hd, hd), lambda j: (0, j, 0, 0))
    gate_b = pl.BlockSpec((n_dir, 1, 1, RG_CB), lambda j: (0, j, 0, 0))
    bias4 = lambda b: b.reshape(n_dir, nb, 1, RG_CB)
    return pl.pallas_call(
        _rglru_kernel,
        grid=(nb,),
        in_specs=[
            pl.BlockSpec((NPH, n_chunks, RG_CB), lambda j: (0, 0, j)),
            pl.BlockSpec((CONV_W, RG_CB), lambda j: (0, j)),
            pl.BlockSpec((1, RG_CB), lambda j: (0, j)),
            gate_w, gate_w, gate_b, gate_b,
            pl.BlockSpec((2, RG_CB), lambda j: (0, j)),
        ],
        out_specs=pl.BlockSpec((NPH, n_chunks, RG_CB), lambda j: (0, 0, j)),
        out_shape=jax.ShapeDtypeStruct((NPH, n_chunks, d_rg), F32),
        scratch_shapes=[
            pltpu.VMEM((NPH + 3, n_chunks, RG_CB), F32),
        ] + [pltpu.VMEM((NPH + 1, n_chunks, RG_CB), F32)] * 4,
        compiler_params=_cparams(1),
        name="rglru",
    )(xrg3, conv_w, conv_b, wa, wx, bias4(ba), bias4(bx), lam)


def _rms_norm(x, g):
    return x * lax.rsqrt(jnp.mean(x * x, axis=-1, keepdims=True) + LN_EPS) * g


def _mixout_kernel(alpha, ys5_ref, yrg_ref, gz_ref, h_ref, gw_ref, gb_ref, gn5_ref, gnr_ref,
                   w_ref, b_ref, g_ref, be_ref, o_ref):
    rows = NPH * CT
    d_s5, d_model = ys5_ref.shape[-1], o_ref.shape[-1]
    pw = MXU_COLS_V7X
    pieces = lambda width: [slice(c, c + pw) for c in range(0, width, pw)]

    y = ys5_ref[...].reshape(rows, d_s5)
    yb = y.astype(BF16)
    glu, sq = [], 0.0
    for cols in pieces(d_s5):
        g = jnp.dot(yb, gw_ref[:, cols], preferred_element_type=F32) + gb_ref[:, cols]
        o = y[:, cols] * _sigmoid(g)
        sq = sq + jnp.sum(o * o, axis=-1, keepdims=True)
        glu.append(o)
    scale = lax.rsqrt(sq * (1.0 / d_s5) + LN_EPS)
    ys5n = jnp.concatenate([o * scale * gn5_ref[:, cols] for o, cols in zip(glu, pieces(d_s5))],
                           axis=1).astype(BF16)

    yrg = yrg_ref[...].reshape(rows, -1) * gz_ref[...].reshape(rows, -1)
    yrgn = _rms_norm(yrg, gnr_ref[...]).astype(BF16)

    s1 = 0.0
    for cols in pieces(d_model):
        v = (jnp.dot(ys5n, w_ref[:d_s5, cols], preferred_element_type=F32)
             + jnp.dot(yrgn, w_ref[d_s5:, cols], preferred_element_type=F32)
             + b_ref[:, cols] + alpha * h_ref[:, :, cols].reshape(rows, pw))
        s1 = s1 + jnp.sum(v, axis=-1, keepdims=True)
        o_ref[:, :, cols] = v.reshape(NPH, CT, pw)
    mean = s1 * (1.0 / d_model)
    s2 = 0.0
    for cols in pieces(d_model):
        d = o_ref[:, :, cols].reshape(rows, pw) - mean
        s2 = s2 + jnp.sum(d * d, axis=-1, keepdims=True)
        o_ref[:, :, cols] = d.reshape(NPH, CT, pw)
    rstd = lax.rsqrt(s2 * (1.0 / d_model) + LN_EPS)
    for cols in pieces(d_model):
        d = o_ref[:, :, cols].reshape(rows, pw)
        o_ref[:, :, cols] = (d * rstd * g_ref[:, cols] + be_ref[:, cols]).reshape(NPH, CT, pw)


def _mixout(alpha, ys5, yrg, gz, h, glu_w, glu_b, gn_s5, gn_rg, w_out, b_out, ln_g, ln_b):
    _, n_chunks, d_model = h.shape
    d_s5, d_rg = ys5.shape[-1], yrg.shape[-1]
    return pl.pallas_call(
        functools.partial(_mixout_kernel, alpha),
        grid=(n_chunks // CT,),
        in_specs=[
            _tile3(d_s5), _tile3(d_rg), _tile3(d_rg), _tile3(d_model),
            _resident((d_s5, d_s5)), _resident((1, d_s5)), _resident((1, d_s5)), _resident((1, d_rg)),
            _resident((d_s5 + d_rg, d_model)), _resident((1, d_model)),
            _resident((1, d_model)), _resident((1, d_model)),
        ],
        out_specs=_tile3(d_model),
        out_shape=jax.ShapeDtypeStruct((NPH, n_chunks, d_model), F32),
        compiler_params=_cparams(1),
        name="mixout",
    )(ys5, yrg, gz, h, glu_w, glu_b, gn_s5, gn_rg, w_out, b_out, ln_g, ln_b)


def _ffn_kernel(alpha, h_ref, p_ref, w1_ref, b1_ref, w2_ref, b2_ref, pw_ref, gw_ref, gb_ref,
                g_ref, be_ref, o_ref, hb_ref, acc_ref):
    j = pl.program_id(1)
    rows = NPH * CT

    @pl.when(j == 0)
    def _():
        h = h_ref[...].reshape(rows, -1)
        hb = h.astype(BF16)
        hb_ref[...] = hb
        gate = _sigmoid(jnp.dot(hb, gw_ref[...], preferred_element_type=F32) + gb_ref[...])
        pe = _to_phase_major(p_ref[...]).astype(BF16)
        acc_ref[...] = alpha * h + gate * jnp.dot(pe, pw_ref[...], preferred_element_type=F32) + b2_ref[...]

    a = jnp.dot(hb_ref[...], w1_ref[...], preferred_element_type=F32) + b1_ref[...]
    a = jnp.maximum(a, 0.0)
    acc_ref[...] += jnp.dot((a * a).astype(BF16), w2_ref[...], preferred_element_type=F32)

    @pl.when(j == pl.num_programs(1) - 1)
    def _():
        o_ref[...] = _to_time_major(_layer_norm(acc_ref[...], g_ref[...], be_ref[...]))


def _ffn(alpha, h1, p2, w1, b1, w2, b2, ple_w, gate_w, gate_b, ln_g, ln_b, ff_tile):
    _, n_chunks, d_model = h1.shape
    d_ff = w1.shape[1]
    ple_dim = ple_w.shape[0]
    rows = NPH * CT
    return pl.pallas_call(
        functools.partial(_ffn_kernel, alpha),
        grid=(n_chunks // CT, d_ff // ff_tile),
        in_specs=[
            _tile3(d_model),
            pl.BlockSpec((rows, ple_dim), lambda i, j: (i, 0)),
            pl.BlockSpec((d_model, ff_tile), lambda i, j: (0, j)),
            pl.BlockSpec((1, ff_tile), lambda i, j: (0, j)),
            pl.BlockSpec((ff_tile, d_model), lambda i, j: (j, 0)),
            _resident((1, d_model)),
            _resident((ple_dim, d_model)), _resident((d_model, d_model)), _resident((1, d_model)),
            _resident((1, d_model)), _resident((1, d_model)),
        ],
        out_specs=pl.BlockSpec((rows, d_model), lambda i, j: (i, 0)),
        out_shape=jax.ShapeDtypeStruct((NPH * n_chunks, d_model), F32),
        scratch_shapes=[pltpu.VMEM((rows, d_model), BF16), pltpu.VMEM((rows, d_model), F32)],
        compiler_params=_cparams(2),
        name="ffn",
    )(h1, p2, w1, b1, w2, b2, ple_w, gate_w, gate_b, ln_g, ln_b)


def kernel(x, p, ln_in_g, ln_in_b, w_in, b_in, s5_lambda_re, s5_lambda_im, s5_log_step, s5_b_re, s5_b_im, s5_c_re, s5_c_im, s5_d, s5_glu_w, s5_glu_b, rg_conv_w, rg_conv_b, rg_wa, rg_ba, rg_wx, rg_bx, rg_lambda, gn_s5, gn_rg, w_out, b_out, ln1_g, ln1_b, w_ff1, b_ff1, w_ff2, b_ff2, ple_w, ple_gate_w, ple_gate_b, ln2_g, ln2_b):
    batch, seq, d_model = x.shape
    depth = w_in.shape[0]
    assert batch == 1 and depth == 1 and seq % (NPH * CT) == 0
    n_chunks = seq // NPH
    d_s5 = s5_glu_w.shape[-1]
    d_rg = rg_conv_w.shape[-1]
    alpha = (2.0 * depth) ** 0.25
    row = lambda v: v.reshape(1, -1).astype(F32)

    h, u3, xrg, gz = _inproj(x.reshape(seq, d_model), row(ln_in_g), row(ln_in_b),
                              w_in[0].astype(BF16), row(b_in[0]), d_s5, d_rg)

    ys5, (w_out_b, gate_w_b, w_ff1_b, w_ff2_b) = _s5(
        u3, s5_lambda_re[0], s5_lambda_im[0], s5_log_step[0], s5_b_re[0], s5_b_im[0], s5_c_re[0], s5_c_im[0],
        s5_d[0], [w_out[0], ple_gate_w[0], w_ff1[0], w_ff2[0]])

    yrg = _rglru(xrg, rg_conv_w[0].astype(F32), row(rg_conv_b[0]), rg_wa[0], rg_ba[0], rg_wx[0], rg_bx[0],
                 rg_lambda[0].astype(F32))

    h1 = _mixout(alpha, ys5, yrg, gz, h, s5_glu_w[0].astype(BF16), row(s5_glu_b[0]), row(gn_s5[0]),
                 row(gn_rg[0]), w_out_b, row(b_out[0]), row(ln1_g[0]), row(ln1_b[0]))

    out = _ffn(alpha, h1, p.reshape(seq, p.shape[-1]), w_ff1_b, row(b_ff1[0]),
               w_ff2_b, row(b_ff2[0]), ple_w[0].astype(BF16), gate_w_b,
               row(ple_gate_b[0]), row(ln2_g[0]), row(ln2_b[0]), ff_tile=1024)
    return out.reshape(batch, seq, d_model).astype(x.dtype)
```

```python
import functools
import math

import jax
import jax.numpy as jnp
from jax import lax
from jax.experimental import pallas as pl
from jax.experimental.pallas import tpu as pltpu

F32 = jnp.float32
BF16 = jnp.bfloat16

NPH = 16
CT = 32
S5_K = 16
S5_P = 64
S5_LB = 128
RG_CB = 128
CONV_W = 4
RG_C = 8.0
LN_EPS = 1e-5
LOG2_E = math.log2(math.e)
VMEM_LIMIT_V7X = 56 * 1024 * 1024
MXU_COLS_V7X = 256


def _cparams(n_axes):
    return pltpu.CompilerParams(
        dimension_semantics=("arbitrary",) * n_axes,
        vmem_limit_bytes=VMEM_LIMIT_V7X)


def _resident(shape):
    return pl.BlockSpec(shape, lambda *_: (0,) * len(shape), pipeline_mode=pl.Buffered(1))


def _tile3(width):
    return pl.BlockSpec((NPH, CT, width), lambda i, *_: (0, i, 0))


def _tile_rows(width):
    return pl.BlockSpec((CT * NPH, width), lambda i, *_: (i, 0))


def _to_phase_major(x):
    w = x.shape[-1]
    return jnp.swapaxes(x.reshape(CT, NPH, w), 0, 1).reshape(NPH * CT, w)


def _to_time_major(x):
    w = x.shape[-1]
    return jnp.swapaxes(x.reshape(NPH, CT, w), 0, 1).reshape(CT * NPH, w)


def _layer_norm(x, g, b):
    mu = jnp.mean(x, axis=-1, keepdims=True)
    xc = x - mu
    var = jnp.mean(xc * xc, axis=-1, keepdims=True)
    return xc * lax.rsqrt(var + LN_EPS) * g + b


def _gelu(x):
    c = math.sqrt(2.0 / math.pi)
    return 0.5 * x * (1.0 + jnp.tanh(c * (x + 0.044715 * (x * x * x))))


def _sigmoid(x):
    return 1.0 / (1.0 + jnp.exp(-x))


def _inproj_kernel(x_ref, g_ref, b_ref, w_ref, bw_ref, h_ref, u_ref, xrg_ref, gz_ref):
    d_s5, d_rg = u_ref.shape[-1], xrg_ref.shape[-1]
    h = _layer_norm(x_ref[...], g_ref[...], b_ref[...])
    h_ref[...] = h
    proj = jnp.dot(_to_phase_major(h.astype(BF16)), w_ref[...], preferred_element_type=F32) + bw_ref[...]
    u_ref[...] = proj[:, :d_s5].astype(BF16).reshape(u_ref.shape)
    xrg_ref[...] = proj[:, d_s5:d_s5 + d_rg].reshape(xrg_ref.shape)
    gz_ref[...] = _gelu(proj[:, d_s5 + d_rg:]).reshape(gz_ref.shape)


def _inproj(x2, ln_g, ln_b, w_in, b_in, d_s5, d_rg):
    seq, d_model = x2.shape
    n_chunks = seq // NPH
    return pl.pallas_call(
        _inproj_kernel,
        grid=(n_chunks // CT,),
        in_specs=[
            _tile_rows(d_model),
            _resident((1, d_model)), _resident((1, d_model)),
            _resident(w_in.shape), _resident(b_in.shape),
        ],
        out_specs=[_tile_rows(d_model), _tile3(d_s5), _tile3(d_rg), _tile3(d_rg)],
        out_shape=[
            jax.ShapeDtypeStruct((seq, d_model), F32),
            jax.ShapeDtypeStruct((NPH, n_chunks, d_s5), BF16),
            jax.ShapeDtypeStruct((NPH, n_chunks, d_rg), F32),
            jax.ShapeDtypeStruct((NPH, n_chunks, d_rg), F32),
        ],
        compiler_params=_cparams(1),
        name="inproj",
    )(x2, ln_g, ln_b, w_in, b_in)


def _row_scan_exclusive(sre, sim, qre, qim, reverse):
    n, w = sre.shape
    row = lax.broadcasted_iota(jnp.int32, (n, w), 0)

    def shift(v, k):
        if k % 8 == 0:
            z = jnp.zeros((k, w), v.dtype)
            return jnp.concatenate([v[k:], z] if reverse else [z, v[:n - k]], axis=0)
        if reverse:
            return jnp.where(row < n - k, pltpu.roll(v, n - k, axis=0), 0.0)
        return jnp.where(row >= k, pltpu.roll(v, k, axis=0), 0.0)

    xre, xim = shift(sre, 1), shift(sim, 1)
    k, i = 1, 0
    while k < n:
        pr, pi = qre[i:i + 1], qim[i:i + 1]
        if k % 8 == 0:
            keep = slice(n - k, n) if reverse else slice(0, k)
            dst = slice(0, n - k) if reverse else slice(k, n)
            src = slice(k, n) if reverse else slice(0, n - k)
            sr, si = xre[src], xim[src]
            nre = xre[dst] + pr * sr - pi * si
            nim = xim[dst] + pr * si + pi * sr
            order = (lambda new, old: [new, old]) if reverse else (lambda new, old: [old, new])
            xre = jnp.concatenate(order(nre, xre[keep]), axis=0)
            xim = jnp.concatenate(order(nim, xim[keep]), axis=0)
        else:
            sr, si = shift(xre, k), shift(xim, k)
            xre, xim = xre + pr * sr - pi * si, xim + pr * si + pi * sr
        k *= 2
        i += 1
    return xre, xim


def _dot(a, b):
    return jnp.dot(a.astype(BF16), b.astype(BF16), preferred_element_type=F32)


def _s5_chunk_operators(par, dvec, half):
    tk = NPH * S5_K
    lane = lax.broadcasted_iota(jnp.int32, par.shape, 1)
    par = jnp.where((lane >= S5_P) == bool(half), par, 0.0)
    pa = par[0:128].T
    pb = par[128:256]
    lane16 = lax.broadcasted_iota(jnp.int32, (16, tk), 1)
    row16 = lax.broadcasted_iota(jnp.int32, (16, tk), 0)
    rep = (lax.shift_right_logical(lane16, 4) == row16).astype(F32)
    til = ((lane16 & 15) == row16).astype(F32)

    def cmul(ar, ai, br, bi):
        return ar * br - ai * bi, ar * bi + ai * br

    def c_of(r0):
        c_re, c_im = pb[r0 + 64:r0 + 80], pb[r0 + 80:r0 + 96]
        blocks = [cmul(c_re, c_im, pb[r0 + t:r0 + t + 1], pb[r0 + 16 + t:r0 + 17 + t]) for t in range(NPH)]
        return (jnp.concatenate([b[0] for b in blocks], axis=0),
                -jnp.concatenate([b[1] for b in blocks], axis=0))

    cc = jnp.concatenate(list(c_of(0)) + list(c_of(32)), axis=1)
    yield

    spread = {c0: (_dot(pa[:, c0:c0 + 16], rep), _dot(pa[:, c0 + 16:c0 + 32], rep),
                   _dot(pa[:, c0 + 64:c0 + 80], til), _dot(pa[:, c0 + 80:c0 + 96], til)) for c0 in (0, 32)}
    yield

    wf_re, wf_im = cmul(*spread[0])
    wb_re, wb_im = cmul(*spread[32])
    bc = jnp.concatenate([wf_re, wf_im, wb_re, wb_im], axis=0)
    yield

    kf = _dot(pb[64:80], wf_re) - _dot(pb[80:96], wf_im)
    kb = _dot(pb[96:112], wb_re) - _dot(pb[112:128], wb_im)
    yield

    blocks = []
    for t in range(NPH):
        left = S5_K * (NPH - 1 - t)
        right = S5_K * t
        f = kf if left == 0 else jnp.where(lane16 < tk - left, pltpu.roll(kf, tk - left, axis=1), 0.0)
        b = kb if right == 0 else jnp.where(lane16 >= right, pltpu.roll(kb, right, axis=1), 0.0)
        blocks.append(f + b)
    a = jnp.concatenate(blocks, axis=0)
    ri = lax.broadcasted_iota(jnp.int32, (tk, tk), 0)
    ci = lax.broadcasted_iota(jnp.int32, (tk, tk), 1)
    a = a + jnp.where(ri == ci, dvec, 0.0)
    return a.astype(BF16), bc.astype(BF16), cc.astype(BF16)


def _in_lockstep(generators):
    results = [None] * len(generators)
    pending = list(enumerate(generators))
    while pending:
        still = []
        for i, gen in pending:
            try:
                next(gen)
                still.append((i, gen))
            except StopIteration as stop:
                results[i] = stop.value
        pending = still
    return results


def _cast_stream_step(k, n_steps, srcs, dsts, inbufs, outbufs, sems):
    assert n_steps >= 2
    slot = lax.rem(k, 2)
    n_streams = len(srcs)
    rows = [src.shape[0] // n_steps for src in srcs]

    def read(i, step, sl):
        return pltpu.make_async_copy(srcs[i].at[pl.ds(step * rows[i], rows[i]), :], inbufs[i].at[sl],
                                     sems.at[i, 0, sl])

    def write(i, step, sl):
        return pltpu.make_async_copy(outbufs[i].at[sl], dsts[i].at[pl.ds(step * rows[i], rows[i]), :],
                                     sems.at[i, 1, sl])

    @pl.when(k == 0)
    def _():
        for i in range(n_streams):
            read(i, 0, 0).start()
            read(i, 1, 1).start()

    @pl.when(k >= 2)
    def _():
        for i in range(n_streams):
            write(i, k - 2, slot).wait()

    for i in range(n_streams):
        read(i, k, slot).wait()
        outbufs[i][slot] = inbufs[i][slot].astype(BF16)

    @pl.when(k + 2 < n_steps)
    def _():
        for i in range(n_streams):
            read(i, k + 2, slot).start()

    for i in range(n_streams):
        write(i, k, slot).start()

    @pl.when(k == n_steps - 1)
    def _():
        for i in range(n_streams):
            write(i, k - 1, 1 - slot).wait()
            write(i, k, slot).wait()


def _s5_kernel(n_w, n_grid, u_ref, *rest):
    raw_refs, d_ref, rest = rest[:7], rest[7], rest[8:]
    w_src, rest = rest[:n_w], rest[n_w:]
    y_ref, rest = rest[0], rest[1:]
    w_dst, rest = rest[:n_w], rest[n_w:]
    xt_ref, yt_ref, a_ref, bc_ref, cc_ref, q_ref, rest = rest[:6] + (rest[6:],)
    w_in_buf, w_out_buf, w_sems = rest[:n_w], rest[n_w:2 * n_w], rest[2 * n_w]
    n = u_ref.shape[1]
    p2 = 2 * S5_P
    tk = NPH * S5_K
    n_groups = S5_LB // S5_K
    n_pairs = n_groups // 2

    def to_rows(s, _):
        xt_ref[s] = u_ref[s].T
        return 0
    lax.fori_loop(0, NPH, to_rows, 0, unroll=8)

    pars = []
    for g in range(n_groups):
        par, q_ref[g] = _s5_discretise(g, *raw_refs)
        pars.append(par)
    stages = [_s5_chunk_operators(pars[g], d_ref[g], g % 2) for g in range(n_groups)]
    for g, (a, bc, cc) in enumerate(_in_lockstep(stages)):
        a_ref[g], bc_ref[g], cc_ref[g] = a, bc, cc

    def pair(gp):
        xs, ys = [], []
        for h in range(2):
            g = 2 * gp + h
            rows = pl.ds(pl.multiple_of(g * S5_K, S5_K), S5_K)
            x = xt_ref[:, rows, :].reshape(tk, n)
            xs.append(x)
            ys.append(jnp.dot(a_ref[g], x, preferred_element_type=F32))
        bc = jnp.concatenate([bc_ref[2 * gp], bc_ref[2 * gp + 1]], axis=1)
        s = jnp.dot(bc, jnp.concatenate(xs, axis=0), preferred_element_type=F32)
        yield
        st = [s[i * p2:(i + 1) * p2].T for i in range(4)]
        lane = lax.broadcasted_iota(jnp.int32, (4 * 16, p2), 1)
        q = jnp.where(lane < S5_P, q_ref[2 * gp], q_ref[2 * gp + 1])
        yield
        hf_re, hf_im = _row_scan_exclusive(st[0], st[1], q[0:16], q[16:32], False)
        hb_re, hb_im = _row_scan_exclusive(st[2], st[3], q[32:48], q[48:64], True)
        hin = jnp.concatenate([hf_re, hf_im, hb_re, hb_im], axis=1).astype(BF16)
        yield
        for h in range(2):
            g = 2 * gp + h
            rows = pl.ds(pl.multiple_of(g * S5_K, S5_K), S5_K)
            y = ys[h] + lax.dot_general(cc_ref[g], hin, (((1,), (1,)), ((), ())), preferred_element_type=F32)
            yt_ref[:, rows, :] = _gelu(y).reshape(NPH, S5_K, n)

    def two_pairs(it, _):
        first = pl.program_id(0) * n_pairs + 2 * it
        stream = functools.partial(_cast_stream_step, n_steps=n_grid * n_pairs, srcs=w_src, dsts=w_dst,
                                   inbufs=w_in_buf, outbufs=w_out_buf, sems=w_sems)
        stream(first)
        _in_lockstep([pair(2 * it), pair(2 * it + 1)])
        stream(first + 1)
        return 0
    lax.fori_loop(0, n_pairs // 2, two_pairs, 0)

    def to_lanes(t, _):
        y_ref[t] = yt_ref[t].T
        return 0
    lax.fori_loop(0, NPH, to_lanes, 0, unroll=8)


def _s5(u3, lam_re, lam_im, log_step, b_re, b_im, c_re, c_im, d, weights):
    assert NPH == 16 and S5_K == 16 and 2 * lam_re.shape[-1] == 128
    _, n_chunks, d_s5 = u3.shape
    raw = [lam_re, lam_im, log_step[..., None], b_re, b_im, c_re, c_im]
    dvec = jnp.tile(d.astype(F32), (1, NPH))[:, None, :]
    gpb = S5_LB // S5_K
    tk = NPH * S5_K
    n_grid = d_s5 // S5_LB
    n_steps = n_grid * (gpb // 2)
    chunk = lambda w: (w.shape[0] // n_steps, w.shape[1])
    assert all(w.shape[0] % (16 * n_steps) == 0 for w in weights)
    per_block = lambda n, *tail: pl.BlockSpec((n,) + tail, lambda b: (b,) + (0,) * len(tail))
    per_dir_block = lambda v: pl.BlockSpec((2, gpb) + v.shape[2:], lambda b: (0, b) + (0,) * (v.ndim - 2))
    hbm = pl.BlockSpec(memory_space=pl.ANY)
    outs = pl.pallas_call(
        functools.partial(_s5_kernel, len(weights), n_grid),
        grid=(n_grid,),
        in_specs=[pl.BlockSpec((NPH, n_chunks, S5_LB), lambda b: (0, 0, b))]
        + [per_dir_block(v) for v in raw] + [per_block(gpb, 1, tk)] + [hbm] * len(weights),
        out_specs=[pl.BlockSpec((NPH, n_chunks, S5_LB), lambda b: (0, 0, b))] + [hbm] * len(weights),
        out_shape=[jax.ShapeDtypeStruct((NPH, n_chunks, d_s5), F32)]
        + [jax.ShapeDtypeStruct(w.shape, BF16) for w in weights],
        scratch_shapes=[
            pltpu.VMEM((NPH, S5_LB, n_chunks), BF16),
            pltpu.VMEM((NPH, S5_LB, n_chunks), F32),
            pltpu.VMEM((gpb, tk, tk), BF16),
            pltpu.VMEM((gpb, 8 * S5_P, tk), BF16),
            pltpu.VMEM((gpb, tk, 8 * S5_P), BF16),
            pltpu.VMEM((gpb, 4 * 16, 2 * S5_P), F32),
        ] + [pltpu.VMEM((2,) + chunk(w), F32) for w in weights]
        + [pltpu.VMEM((2,) + chunk(w), BF16) for w in weights]
        + [pltpu.SemaphoreType.DMA((len(weights), 2, 2))],
        compiler_params=_cparams(1),
        name="s5",
    )(u3, *raw, dvec, *weights)
    return outs[0], outs[1:]


def _s5_discretise(g, lam_re_ref, lam_im_ref, log_step_ref, b_re_ref, b_im_ref, c_re_ref, c_im_ref):
    def cmul(x, y):
        return x[0] * y[0] - x[1] * y[1], x[0] * y[1] + x[1] * y[0]

    def rows(zs):
        return jnp.concatenate([z[0] for z in zs], axis=0), jnp.concatenate([z[1] for z in zs], axis=0)

    def one_direction(d):
        lre, lim = jnp.minimum(lam_re_ref[d, g:g + 1, :], -1e-4), lam_im_ref[d, g:g + 1, :]
        step = jnp.exp(log_step_ref[d, g:g + 1, :])
        are, aim = lre * step, lim * step
        mag = jnp.exp(are)
        lam_bar = (mag * jnp.cos(aim), mag * jnp.sin(aim))
        powers = [(jnp.ones_like(are), jnp.zeros_like(are))]
        for _ in range(NPH):
            powers.append(cmul(powers[-1], lam_bar))
        squares = [powers[NPH]]
        for _ in range(15):
            squares.append(cmul(squares[-1], squares[-1]))
        nr, ni = lam_bar[0] - 1.0, lam_bar[1]
        den = lre * lre + lim * lim
        z = ((nr * lre + ni * lim) / den, (ni * lre - nr * lim) / den)
        bbar = cmul(z, (b_re_ref[d, g].T, b_im_ref[d, g].T))
        c = (c_re_ref[d, g], c_im_ref[d, g])
        return powers, squares, bbar, c

    pf, qf, bbar_f, c_f = one_direction(0)
    pb, qb, bbar_b, c_b = one_direction(1)
    par = jnp.concatenate(
        list(rows(pf[NPH - 1::-1])) + list(rows(pb[:NPH])) + list(bbar_f) + list(bbar_b)
        + list(rows(pf[1:])) + list(rows(pb[NPH:0:-1])) + list(c_f) + list(c_b), axis=0)
    q_tab = jnp.concatenate(list(rows(qf)) + list(rows(qb)), axis=0)
    both_halves = lambda v: jnp.concatenate([v, v], axis=1)
    return both_halves(par), both_halves(q_tab)


def _row_scan_carry(a, h, reverse):
    n, w = a.shape
    row = lax.broadcasted_iota(jnp.int32, (n, w), 0)

    def shift(v, k, fill):
        if k % 8 == 0:
            z = jnp.full((k, w), fill, v.dtype)
            if reverse:
                return jnp.concatenate([v[k:], z], axis=0)
            return jnp.concatenate([z, v[:n - k]], axis=0)
        if reverse:
            return jnp.where(row < n - k, pltpu.roll(v, n - k, axis=0), fill)
        return jnp.where(row >= k, pltpu.roll(v, k, axis=0), fill)

    k = 1
    while k < n:
        h = h + a * shift(h, k, 0.0)
        a = a * shift(a, k, 1.0)
        k *= 2
    return shift(h, 1, 0.0)


def _rglru_kernel(x_ref, cw_ref, cb_ref, wa_ref, wx_ref, ba_ref, bx_ref, lam_ref, y_ref, xe_ref,
                  alf_ref, hlf_ref, alb_ref, hlb_ref):
    n, w = x_ref.shape[1], x_ref.shape[2]

    def block_diag(heads):
        per, hd, _ = heads.shape
        zeros = jnp.zeros((hd, hd), heads.dtype)
        return jnp.concatenate(
            [jnp.concatenate([heads[p] if q == p else zeros for q in range(per)], axis=1) for p in range(per)],
            axis=0)
    row = lax.broadcasted_iota(jnp.int32, (n, w), 0)

    def from_prev_chunk(v):
        return jnp.where(row >= 1, pltpu.roll(v, 1, axis=0), 0.0)

    def from_next_chunk(v):
        return jnp.where(row < n - 1, pltpu.roll(v, n - 1, axis=0), 0.0)

    xe_ref[0] = from_prev_chunk(x_ref[NPH - 2])
    xe_ref[1] = from_prev_chunk(x_ref[NPH - 1])

    def copy_body(s, _):
        xe_ref[s + 2] = x_ref[s]
        return 0
    lax.fori_loop(0, NPH, copy_body, 0, unroll=4)
    xe_ref[NPH + 2] = from_next_chunk(x_ref[0])

    cw = cw_ref[...]
    cb = cb_ref[...]
    lam = lam_ref[...]
    neg = -lam
    softplus = jnp.maximum(neg, 0.0) + jnp.log(1.0 + jnp.exp(-jnp.abs(neg)))

    directions = []
    for d, (reverse, al_ref, hl_ref) in enumerate(((False, alf_ref, hlf_ref), (True, alb_ref, hlb_ref))):
        rate2 = (-RG_C * LOG2_E) * softplus[d:d + 1]
        wg = (-LOG2_E * jnp.concatenate([block_diag(wa_ref[d]), block_diag(wx_ref[d])], axis=1)).astype(BF16)
        bg = -LOG2_E * jnp.concatenate([ba_ref[d, 0], bx_ref[d, 0]], axis=1)
        init = NPH if reverse else 0
        al_ref[init] = jnp.ones((n, w), F32)
        hl_ref[init] = jnp.zeros((n, w), F32)
        directions.append((reverse, al_ref, hl_ref, rate2, wg, bg))

    def local_step(i, reverse, al_ref, hl_ref, rate2, wg, bg):
        s = (NPH - 1 - i) if reverse else i
        xc = (cw[0:1] * xe_ref[s] + cw[1:2] * xe_ref[s + 1]
              + cw[2:3] * xe_ref[s + 2] + cw[3:4] * xe_ref[s + 3] + cb)
        yield
        g = jnp.dot(xc.astype(BF16), wg, preferred_element_type=F32) + bg
        yield
        r = 1.0 / (1.0 + jnp.exp2(g[:, :w]))
        ig = 1.0 / (1.0 + jnp.exp2(g[:, w:]))
        a = jnp.exp2(rate2 * r)
        bt = jnp.sqrt(1.0 - a * a) * (ig * xc)
        yield
        src = (s + 1) if reverse else s
        dst = s if reverse else (s + 1)
        hl_ref[dst] = a * hl_ref[src] + bt
        al_ref[dst] = a * al_ref[src]

    def local_body(i, _):
        _in_lockstep([local_step(i, *direction) for direction in directions])
        return 0
    lax.fori_loop(0, NPH, local_body, 0, unroll=4)

    carry_f = _row_scan_carry(alf_ref[NPH], hlf_ref[NPH], False)
    carry_b = _row_scan_carry(alb_ref[0], hlb_ref[0], True)

    def fix_body(s, _):
        y_ref[s] = (hlf_ref[s + 1] + alf_ref[s + 1] * carry_f) + (hlb_ref[s] + alb_ref[s] * carry_b)
        return 0
    lax.fori_loop(0, NPH, fix_body, 0, unroll=4)


def _rglru(xrg3, conv_w, conv_b, wa, ba, wx, bx, lam):
    _, n_chunks, d_rg = xrg3.shape
    nb = d_rg // RG_CB
    n_dir, heads, hd, _ = wa.shape
    per = RG_CB // hd
    assert heads == nb * per
    gate_w = pl.BlockSpec((n_dir, per, hd, hd), lambda j: (0, j, 0, 0))
    gate_b = pl.BlockSpec((n_dir, 1, 1, RG_CB), lambda j: (0, j, 0, 0))
    bias4 = lambda b: b.reshape(n_dir, nb, 1, RG_CB)
    return pl.pallas_call(
        _rglru_kernel,
        grid=(nb,),
        in_specs=[
            pl.BlockSpec((NPH, n_chunks, RG_CB), lambda j: (0, 0, j)),
            pl.BlockSpec((CONV_W, RG_CB), lambda j: (0, j)),
            pl.BlockSpec((1, RG_CB), lambda j: (0, j)),
            gate_w, gate_w, gate_b, gate_b,
            pl.BlockSpec((2, RG_CB), lambda j: (0, j)),
        ],
        out_specs=pl.BlockSpec((NPH, n_chunks, RG_CB), lambda j: (0, 0, j)),
        out_shape=jax.ShapeDtypeStruct((NPH, n_chunks, d_rg), F32),
        scratch_shapes=[
            pltpu.VMEM((NPH + 3, n_chunks, RG_CB), F32),
        ] + [pltpu.VMEM((NPH + 1, n_chunks, RG_CB), F32)] * 4,
        compiler_params=_cparams(1),
        name="rglru",
    )(xrg3, conv_w, conv_b, wa, wx, bias4(ba), bias4(bx), lam)


def _rms_norm(x, g):
    return x * lax.rsqrt(jnp.mean(x * x, axis=-1, keepdims=True) + LN_EPS) * g


def _mixout_kernel(alpha, ys5_ref, yrg_ref, gz_ref, h_ref, gw_ref, gb_ref, gn5_ref, gnr_ref,
                   w_ref, b_ref, g_ref, be_ref, o_ref):
    rows = NPH * CT
    d_s5, d_model = ys5_ref.shape[-1], o_ref.shape[-1]
    pw = MXU_COLS_V7X
    pieces = lambda width: [slice(c, c + pw) for c in range(0, width, pw)]

    y = ys5_ref[...].reshape(rows, d_s5)
    yb = y.astype(BF16)
    glu, sq = [], 0.0
    for cols in pieces(d_s5):
        g = jnp.dot(yb, gw_ref[:, cols], preferred_element_type=F32) + gb_ref[:, cols]
        o = y[:, cols] * _sigmoid(g)
        sq = sq + jnp.sum(o * o, axis=-1, keepdims=True)
        glu.append(o)
    scale = lax.rsqrt(sq * (1.0 / d_s5) + LN_EPS)
    ys5n = jnp.concatenate([o * scale * gn5_ref[:, cols] for o, cols in zip(glu, pieces(d_s5))],
                           axis=1).astype(BF16)

    yrg = yrg_ref[...].reshape(rows, -1) * gz_ref[...].reshape(rows, -1)
    yrgn = _rms_norm(yrg, gnr_ref[...]).astype(BF16)

    ys5n, yrgn = _to_time_major(ys5n), _to_time_major(yrgn)
    s1 = 0.0
    for cols in pieces(d_model):
        v = (jnp.dot(ys5n, w_ref[:d_s5, cols], preferred_element_type=F32)
             + jnp.dot(yrgn, w_ref[d_s5:, cols], preferred_element_type=F32)
             + b_ref[:, cols] + alpha * h_ref[:, cols])
        s1 = s1 + jnp.sum(v, axis=-1, keepdims=True)
        o_ref[:, cols] = v
    mean = s1 * (1.0 / d_model)
    s2 = 0.0
    for cols in pieces(d_model):
        d = o_ref[:, cols] - mean
        s2 = s2 + jnp.sum(d * d, axis=-1, keepdims=True)
        o_ref[:, cols] = d
    rstd = lax.rsqrt(s2 * (1.0 / d_model) + LN_EPS)
    for cols in pieces(d_model):
        o_ref[:, cols] = o_ref[:, cols] * rstd * g_ref[:, cols] + be_ref[:, cols]


def _mixout(alpha, ys5, yrg, gz, h, glu_w, glu_b, gn_s5, gn_rg, w_out, b_out, ln_g, ln_b):
    seq, d_model = h.shape
    n_chunks = seq // NPH
    d_s5, d_rg = ys5.shape[-1], yrg.shape[-1]
    return pl.pallas_call(
        functools.partial(_mixout_kernel, alpha),
        grid=(n_chunks // CT,),
        in_specs=[
            _tile3(d_s5), _tile3(d_rg), _tile3(d_rg), _tile_rows(d_model),
            _resident((d_s5, d_s5)), _resident((1, d_s5)), _resident((1, d_s5)), _resident((1, d_rg)),
            _resident((d_s5 + d_rg, d_model)), _resident((1, d_model)),
            _resident((1, d_model)), _resident((1, d_model)),
        ],
        out_specs=_tile_rows(d_model),
        out_shape=jax.ShapeDtypeStruct((seq, d_model), F32),
        compiler_params=_cparams(1),
        name="mixout",
    )(ys5, yrg, gz, h, glu_w, glu_b, gn_s5, gn_rg, w_out, b_out, ln_g, ln_b)


def _ffn_kernel(alpha, h_ref, p_ref, w1_ref, b1_ref, w2_ref, b2_ref, pw_ref, gw_ref, gb_ref,
                g_ref, be_ref, o_ref, hb_ref, acc_ref):
    j = pl.program_id(1)

    @pl.when(j == 0)
    def _():
        h = h_ref[...]
        hb = h.astype(BF16)
        hb_ref[...] = hb
        gate = _sigmoid(jnp.dot(hb, gw_ref[...], preferred_element_type=F32) + gb_ref[...])
        pe = p_ref[...].astype(BF16)
        acc_ref[...] = alpha * h + gate * jnp.dot(pe, pw_ref[...], preferred_element_type=F32) + b2_ref[...]

    a = jnp.dot(hb_ref[...], w1_ref[...], preferred_element_type=F32) + b1_ref[...]
    a = jnp.maximum(a, 0.0)
    acc_ref[...] += jnp.dot((a * a).astype(BF16), w2_ref[...], preferred_element_type=F32)

    @pl.when(j == pl.num_programs(1) - 1)
    def _():
        o_ref[...] = _layer_norm(acc_ref[...], g_ref[...], be_ref[...])


def _ffn(alpha, h1, p2, w1, b1, w2, b2, ple_w, gate_w, gate_b, ln_g, ln_b, ff_tile):
    seq, d_model = h1.shape
    d_ff = w1.shape[1]
    ple_dim = ple_w.shape[0]
    rows = NPH * CT
    return pl.pallas_call(
        functools.partial(_ffn_kernel, alpha),
        grid=(seq // rows, d_ff // ff_tile),
        in_specs=[
            _tile_rows(d_model),
            _tile_rows(ple_dim),
            pl.BlockSpec((d_model, ff_tile), lambda i, j: (0, j)),
            pl.BlockSpec((1, ff_tile), lambda i, j: (0, j)),
            pl.BlockSpec((ff_tile, d_model), lambda i, j: (j, 0)),
            _resident((1, d_model)),
            _resident((ple_dim, d_model)), _resident((d_model, d_model)), _resident((1, d_model)),
            _resident((1, d_model)), _resident((1, d_model)),
        ],
        out_specs=_tile_rows(d_model),
        out_shape=jax.ShapeDtypeStruct((seq, d_model), F32),
        scratch_shapes=[pltpu.VMEM((rows, d_model), BF16), pltpu.VMEM((rows, d_model), F32)],
        compiler_params=_cparams(2),
        name="ffn",
    )(h1, p2, w1, b1, w2, b2, ple_w, gate_w, gate_b, ln_g, ln_b)


def kernel(x, p, ln_in_g, ln_in_b, w_in, b_in, s5_lambda_re, s5_lambda_im, s5_log_step, s5_b_re, s5_b_im, s5_c_re, s5_c_im, s5_d, s5_glu_w, s5_glu_b, rg_conv_w, rg_conv_b, rg_wa, rg_ba, rg_wx, rg_bx, rg_lambda, gn_s5, gn_rg, w_out, b_out, ln1_g, ln1_b, w_ff1, b_ff1, w_ff2, b_ff2, ple_w, ple_gate_w, ple_gate_b, ln2_g, ln2_b):
    batch, seq, d_model = x.shape
    depth = w_in.shape[0]
    assert batch == 1 and depth == 1 and seq % (NPH * CT) == 0
    n_chunks = seq // NPH
    d_s5 = s5_glu_w.shape[-1]
    d_rg = rg_conv_w.shape[-1]
    alpha = (2.0 * depth) ** 0.25
    row = lambda v: v.reshape(1, -1).astype(F32)

    h, u3, xrg, gz = _inproj(x.reshape(seq, d_model), row(ln_in_g), row(ln_in_b),
                              w_in[0].astype(BF16), row(b_in[0]), d_s5, d_rg)

    ys5, (w_out_b, gate_w_b, w_ff1_b, w_ff2_b) = _s5(
        u3, s5_lambda_re[0], s5_lambda_im[0], s5_log_step[0], s5_b_re[0], s5_b_im[0], s5_c_re[0], s5_c_im[0],
        s5_d[0], [w_out[0], ple_gate_w[0], w_ff1[0], w_ff2[0]])

    yrg = _rglru(xrg, rg_conv_w[0].astype(F32), row(rg_conv_b[0]), rg_wa[0], rg_ba[0], rg_wx[0], rg_bx[0],
                 rg_lambda[0].astype(F32))

    h1 = _mixout(alpha, ys5, yrg, gz, h, s5_glu_w[0].astype(BF16), row(s5_glu_b[0]), row(gn_s5[0]),
                 row(gn_rg[0]), w_out_b, row(b_out[0]), row(ln1_g[0]), row(ln1_b[0]))

    out = _ffn(alpha, h1, p.reshape(seq, p.shape[-1]), w_ff1_b, row(b_ff1[0]),
               w_ff2_b, row(b_ff2[0]), ple_w[0].astype(BF16), gate_w_b,
               row(ple_gate_b[0]), row(ln2_g[0]), row(ln2_b[0]), ff_tile=1024)
    return out.reshape(batch, seq, d_model).astype(x.dtype)
```

```python
import functools
import math

import jax
import jax.numpy as jnp
from jax import lax
from jax.experimental import pallas as pl
from jax.experimental.pallas import tpu as pltpu

F32 = jnp.float32
BF16 = jnp.bfloat16

NPH = 16
CT = 32
S5_K = 16
S5_P = 64
S5_LB = 128
RG_CB = 128
CONV_W = 4
RG_C = 8.0
LN_EPS = 1e-5
LOG2_E = math.log2(math.e)
VMEM_LIMIT_V7X = 56 * 1024 * 1024
MXU_COLS_V7X = 256


def _cparams(n_axes):
    return pltpu.CompilerParams(
        dimension_semantics=("arbitrary",) * n_axes,
        vmem_limit_bytes=VMEM_LIMIT_V7X)


def _resident(shape):
    return pl.BlockSpec(shape, lambda *_: (0,) * len(shape), pipeline_mode=pl.Buffered(1))


def _tile3(width):
    return pl.BlockSpec((NPH, CT, width), lambda i, *_: (0, i, 0))


def _tile_rows(width):
    return pl.BlockSpec((CT * NPH, width), lambda i, *_: (i, 0))


def _to_phase_major(x):
    w = x.shape[-1]
    return jnp.swapaxes(x.reshape(CT, NPH, w), 0, 1).reshape(NPH * CT, w)


def _to_time_major(x):
    w = x.shape[-1]
    return jnp.swapaxes(x.reshape(NPH, CT, w), 0, 1).reshape(CT * NPH, w)


def _layer_norm(x, g, b):
    mu = jnp.mean(x, axis=-1, keepdims=True)
    xc = x - mu
    var = jnp.mean(xc * xc, axis=-1, keepdims=True)
    return xc * lax.rsqrt(var + LN_EPS) * g + b


def _gelu(x):
    c = math.sqrt(2.0 / math.pi)
    return 0.5 * x * (1.0 + jnp.tanh(c * (x + 0.044715 * (x * x * x))))


def _sigmoid(x):
    return 1.0 / (1.0 + jnp.exp(-x))


W_IN_STAGE_ROWS = 256


def _inproj_kernel(x_ref, g_ref, b_ref, w_hbm, bw_ref, h_ref, u_ref, xrg_ref, gz_ref, w_ref, stage_ref, sem):
    d_s5, d_rg = u_ref.shape[-1], xrg_ref.shape[-1]

    @pl.when(pl.program_id(0) == 0)
    def _():
        n_slabs = w_ref.shape[0] // W_IN_STAGE_ROWS

        def copy(k):
            return pltpu.make_async_copy(w_hbm.at[pl.ds(k * W_IN_STAGE_ROWS, W_IN_STAGE_ROWS)],
                                         stage_ref.at[k % 2], sem.at[k % 2])

        copy(0).start()
        for k in range(n_slabs):
            if k + 1 < n_slabs:
                copy(k + 1).start()
            copy(k).wait()
            w_ref[k * W_IN_STAGE_ROWS:(k + 1) * W_IN_STAGE_ROWS, :] = stage_ref[k % 2].astype(BF16)

    h = _layer_norm(x_ref[...], g_ref[...], b_ref[...])
    h_ref[...] = h
    proj = jnp.dot(_to_phase_major(h.astype(BF16)), w_ref[...], preferred_element_type=F32) + bw_ref[...]
    u_ref[...] = proj[:, :d_s5].astype(BF16).reshape(u_ref.shape)
    xrg_ref[...] = proj[:, d_s5:d_s5 + d_rg].reshape(xrg_ref.shape)
    gz_ref[...] = _gelu(proj[:, d_s5 + d_rg:]).reshape(gz_ref.shape)


def _inproj(x2, ln_g, ln_b, w_in, b_in, d_s5, d_rg):
    seq, d_model = x2.shape
    n_chunks = seq // NPH
    return pl.pallas_call(
        _inproj_kernel,
        grid=(n_chunks // CT,),
        in_specs=[
            _tile_rows(d_model),
            _resident((1, d_model)), _resident((1, d_model)),
            pl.BlockSpec(memory_space=pl.ANY), _resident(b_in.shape),
        ],
        out_specs=[_tile_rows(d_model), _tile3(d_s5), _tile3(d_rg), _tile3(d_rg)],
        out_shape=[
            jax.ShapeDtypeStruct((seq, d_model), F32),
            jax.ShapeDtypeStruct((NPH, n_chunks, d_s5), BF16),
            jax.ShapeDtypeStruct((NPH, n_chunks, d_rg), F32),
            jax.ShapeDtypeStruct((NPH, n_chunks, d_rg), F32),
        ],
        scratch_shapes=[
            pltpu.VMEM(w_in.shape, BF16),
            pltpu.VMEM((2, W_IN_STAGE_ROWS, w_in.shape[1]), F32),
            pltpu.SemaphoreType.DMA((2,)),
        ],
        compiler_params=_cparams(1),
        name="inproj",
    )(x2, ln_g, ln_b, w_in, b_in)


def _row_scan_exclusive(sre, sim, qre, qim, reverse):
    n, w = sre.shape
    row = lax.broadcasted_iota(jnp.int32, (n, w), 0)

    def shift(v, k):
        if k % 8 == 0:
            z = jnp.zeros((k, w), v.dtype)
            return jnp.concatenate([v[k:], z] if reverse else [z, v[:n - k]], axis=0)
        if reverse:
            return jnp.where(row < n - k, pltpu.roll(v, n - k, axis=0), 0.0)
        return jnp.where(row >= k, pltpu.roll(v, k, axis=0), 0.0)

    xre, xim = shift(sre, 1), shift(sim, 1)
    k, i = 1, 0
    while k < n:
        pr, pi = qre[i:i + 1], qim[i:i + 1]
        if k % 8 == 0:
            keep = slice(n - k, n) if reverse else slice(0, k)
            dst = slice(0, n - k) if reverse else slice(k, n)
            src = slice(k, n) if reverse else slice(0, n - k)
            sr, si = xre[src], xim[src]
            nre = xre[dst] + pr * sr - pi * si
            nim = xim[dst] + pr * si + pi * sr
            order = (lambda new, old: [new, old]) if reverse else (lambda new, old: [old, new])
            xre = jnp.concatenate(order(nre, xre[keep]), axis=0)
            xim = jnp.concatenate(order(nim, xim[keep]), axis=0)
        else:
            sr, si = shift(xre, k), shift(xim, k)
            xre, xim = xre + pr * sr - pi * si, xim + pr * si + pi * sr
        k *= 2
        i += 1
    return xre, xim


def _dot(a, b):
    return jnp.dot(a.astype(BF16), b.astype(BF16), preferred_element_type=F32)


def _s5_chunk_operators(par, dvec, half):
    tk = NPH * S5_K
    lane = lax.broadcasted_iota(jnp.int32, par.shape, 1)
    par = jnp.where((lane >= S5_P) == bool(half), par, 0.0)
    pa = par[0:128].T
    pb = par[128:256]
    lane16 = lax.broadcasted_iota(jnp.int32, (16, tk), 1)
    row16 = lax.broadcasted_iota(jnp.int32, (16, tk), 0)
    rep = (lax.shift_right_logical(lane16, 4) == row16).astype(F32)
    til = ((lane16 & 15) == row16).astype(F32)

    def cmul(ar, ai, br, bi):
        return ar * br - ai * bi, ar * bi + ai * br

    def c_of(r0):
        c_re, c_im = pb[r0 + 64:r0 + 80], pb[r0 + 80:r0 + 96]
        blocks = [cmul(c_re, c_im, pb[r0 + t:r0 + t + 1], pb[r0 + 16 + t:r0 + 17 + t]) for t in range(NPH)]
        return (jnp.concatenate([b[0] for b in blocks], axis=0),
                -jnp.concatenate([b[1] for b in blocks], axis=0))

    cc = jnp.concatenate(list(c_of(0)) + list(c_of(32)), axis=1)
    yield

    spread = {c0: (_dot(pa[:, c0:c0 + 16], rep), _dot(pa[:, c0 + 16:c0 + 32], rep),
                   _dot(pa[:, c0 + 64:c0 + 80], til), _dot(pa[:, c0 + 80:c0 + 96], til)) for c0 in (0, 32)}
    yield

    wf_re, wf_im = cmul(*spread[0])
    wb_re, wb_im = cmul(*spread[32])
    bc = jnp.concatenate([wf_re, wf_im, wb_re, wb_im], axis=0)
    yield

    kf = _dot(pb[64:80], wf_re) - _dot(pb[80:96], wf_im)
    kb = _dot(pb[96:112], wb_re) - _dot(pb[112:128], wb_im)
    yield

    blocks = []
    for t in range(NPH):
        left = S5_K * (NPH - 1 - t)
        right = S5_K * t
        f = kf if left == 0 else jnp.where(lane16 < tk - left, pltpu.roll(kf, tk - left, axis=1), 0.0)
        b = kb if right == 0 else jnp.where(lane16 >= right, pltpu.roll(kb, right, axis=1), 0.0)
        blocks.append(f + b)
    a = jnp.concatenate(blocks, axis=0)
    ri = lax.broadcasted_iota(jnp.int32, (tk, tk), 0)
    ci = lax.broadcasted_iota(jnp.int32, (tk, tk), 1)
    a = a + jnp.where(ri == ci, dvec, 0.0)
    return a.astype(BF16), bc.astype(BF16), cc.astype(BF16)


def _in_lockstep(generators):
    results = [None] * len(generators)
    pending = list(enumerate(generators))
    while pending:
        still = []
        for i, gen in pending:
            try:
                next(gen)
                still.append((i, gen))
            except StopIteration as stop:
                results[i] = stop.value
        pending = still
    return results


def _cast_stream_step(k, n_steps, srcs, dsts, inbufs, outbufs, sems):
    assert n_steps >= 2
    slot = lax.rem(k, 2)
    n_streams = len(srcs)
    rows = [src.shape[0] // n_steps for src in srcs]

    def read(i, step, sl):
        return pltpu.make_async_copy(srcs[i].at[pl.ds(step * rows[i], rows[i]), :], inbufs[i].at[sl],
                                     sems.at[i, 0, sl])

    def write(i, step, sl):
        return pltpu.make_async_copy(outbufs[i].at[sl], dsts[i].at[pl.ds(step * rows[i], rows[i]), :],
                                     sems.at[i, 1, sl])

    @pl.when(k == 0)
    def _():
        for i in range(n_streams):
            read(i, 0, 0).start()
            read(i, 1, 1).start()

    @pl.when(k >= 2)
    def _():
        for i in range(n_streams):
            write(i, k - 2, slot).wait()

    for i in range(n_streams):
        read(i, k, slot).wait()
        outbufs[i][slot] = inbufs[i][slot].astype(BF16)

    @pl.when(k + 2 < n_steps)
    def _():
        for i in range(n_streams):
            read(i, k + 2, slot).start()

    for i in range(n_streams):
        write(i, k, slot).start()

    @pl.when(k == n_steps - 1)
    def _():
        for i in range(n_streams):
            write(i, k - 1, 1 - slot).wait()
            write(i, k, slot).wait()


def _s5_kernel(n_w, n_grid, u_ref, *rest):
    raw_refs, d_ref, rest = rest[:7], rest[7], rest[8:]
    w_src, rest = rest[:n_w], rest[n_w:]
    y_ref, rest = rest[0], rest[1:]
    w_dst, rest = rest[:n_w], rest[n_w:]
    xt_ref, yt_ref, a_ref, bc_ref, cc_ref, q_ref, rest = rest[:6] + (rest[6:],)
    w_in_buf, w_out_buf, w_sems = rest[:n_w], rest[n_w:2 * n_w], rest[2 * n_w]
    n = u_ref.shape[1]
    p2 = 2 * S5_P
    tk = NPH * S5_K
    n_groups = S5_LB // S5_K
    n_pairs = n_groups // 2

    def to_rows(s, _):
        xt_ref[s] = u_ref[s].T
        return 0
    lax.fori_loop(0, NPH, to_rows, 0, unroll=8)

    pars = []
    for g in range(n_groups):
        par, q_ref[g] = _s5_discretise(g, *raw_refs)
        pars.append(par)
    stages = [_s5_chunk_operators(pars[g], d_ref[g], g % 2) for g in range(n_groups)]
    for g, (a, bc, cc) in enumerate(_in_lockstep(stages)):
        a_ref[g], bc_ref[g], cc_ref[g] = a, bc, cc

    def pair(gp):
        xs, ys = [], []
        for h in range(2):
            g = 2 * gp + h
            rows = pl.ds(pl.multiple_of(g * S5_K, S5_K), S5_K)
            x = xt_ref[:, rows, :].reshape(tk, n)
            xs.append(x)
            ys.append(jnp.dot(a_ref[g], x, preferred_element_type=F32))
        bc = jnp.concatenate([bc_ref[2 * gp], bc_ref[2 * gp + 1]], axis=1)
        s = jnp.dot(bc, jnp.concatenate(xs, axis=0), preferred_element_type=F32)
        yield
        st = [s[i * p2:(i + 1) * p2].T for i in range(4)]
        lane = lax.broadcasted_iota(jnp.int32, (4 * 16, p2), 1)
        q = jnp.where(lane < S5_P, q_ref[2 * gp], q_ref[2 * gp + 1])
        yield
        hf_re, hf_im = _row_scan_exclusive(st[0], st[1], q[0:16], q[16:32], False)
        hb_re, hb_im = _row_scan_exclusive(st[2], st[3], q[32:48], q[48:64], True)
        hin = jnp.concatenate([hf_re, hf_im, hb_re, hb_im], axis=1).astype(BF16)
        yield
        for h in range(2):
            g = 2 * gp + h
            rows = pl.ds(pl.multiple_of(g * S5_K, S5_K), S5_K)
            y = ys[h] + lax.dot_general(cc_ref[g], hin, (((1,), (1,)), ((), ())), preferred_element_type=F32)
            yt_ref[:, rows, :] = _gelu(y).reshape(NPH, S5_K, n)

    def two_pairs(it, _):
        first = pl.program_id(0) * n_pairs + 2 * it
        stream = functools.partial(_cast_stream_step, n_steps=n_grid * n_pairs, srcs=w_src, dsts=w_dst,
                                   inbufs=w_in_buf, outbufs=w_out_buf, sems=w_sems)
        stream(first)
        _in_lockstep([pair(2 * it), pair(2 * it + 1)])
        stream(first + 1)
        return 0
    lax.fori_loop(0, n_pairs // 2, two_pairs, 0)

    def to_lanes(t, _):
        y_ref[t] = yt_ref[t].T
        return 0
    lax.fori_loop(0, NPH, to_lanes, 0, unroll=8)


def _s5(u3, lam_re, lam_im, log_step, b_re, b_im, c_re, c_im, d, weights):
    assert NPH == 16 and S5_K == 16 and 2 * lam_re.shape[-1] == 128
    _, n_chunks, d_s5 = u3.shape
    raw = [lam_re, lam_im, log_step[..., None], b_re, b_im, c_re, c_im]
    dvec = jnp.tile(d.astype(F32), (1, NPH))[:, None, :]
    gpb = S5_LB // S5_K
    tk = NPH * S5_K
    n_grid = d_s5 // S5_LB
    n_steps = n_grid * (gpb // 2)
    chunk = lambda w: (w.shape[0] // n_steps, w.shape[1])
    assert all(w.shape[0] % (16 * n_steps) == 0 for w in weights)
    per_block = lambda n, *tail: pl.BlockSpec((n,) + tail, lambda b: (b,) + (0,) * len(tail))
    per_dir_block = lambda v: pl.BlockSpec((2, gpb) + v.shape[2:], lambda b: (0, b) + (0,) * (v.ndim - 2))
    hbm = pl.BlockSpec(memory_space=pl.ANY)
    outs = pl.pallas_call(
        functools.partial(_s5_kernel, len(weights), n_grid),
        grid=(n_grid,),
        in_specs=[pl.BlockSpec((NPH, n_chunks, S5_LB), lambda b: (0, 0, b))]
        + [per_dir_block(v) for v in raw] + [per_block(gpb, 1, tk)] + [hbm] * len(weights),
        out_specs=[pl.BlockSpec((NPH, n_chunks, S5_LB), lambda b: (0, 0, b))] + [hbm] * len(weights),
        out_shape=[jax.ShapeDtypeStruct((NPH, n_chunks, d_s5), F32)]
        + [jax.ShapeDtypeStruct(w.shape, BF16) for w in weights],
        scratch_shapes=[
            pltpu.VMEM((NPH, S5_LB, n_chunks), BF16),
            pltpu.VMEM((NPH, S5_LB, n_chunks), F32),
            pltpu.VMEM((gpb, tk, tk), BF16),
            pltpu.VMEM((gpb, 8 * S5_P, tk), BF16),
            pltpu.VMEM((gpb, tk, 8 * S5_P), BF16),
            pltpu.VMEM((gpb, 4 * 16, 2 * S5_P), F32),
        ] + [pltpu.VMEM((2,) + chunk(w), F32) for w in weights]
        + [pltpu.VMEM((2,) + chunk(w), BF16) for w in weights]
        + [pltpu.SemaphoreType.DMA((len(weights), 2, 2))],
        compiler_params=_cparams(1),
        name="s5",
    )(u3, *raw, dvec, *weights)
    return outs[0], outs[1:]


def _s5_discretise(g, lam_re_ref, lam_im_ref, log_step_ref, b_re_ref, b_im_ref, c_re_ref, c_im_ref):
    def cmul(x, y):
        return x[0] * y[0] - x[1] * y[1], x[0] * y[1] + x[1] * y[0]

    def rows(zs):
        return jnp.concatenate([z[0] for z in zs], axis=0), jnp.concatenate([z[1] for z in zs], axis=0)

    def one_direction(d):
        lre, lim = jnp.minimum(lam_re_ref[d, g:g + 1, :], -1e-4), lam_im_ref[d, g:g + 1, :]
        step = jnp.exp(log_step_ref[d, g:g + 1, :])
        are, aim = lre * step, lim * step
        mag = jnp.exp(are)
        lam_bar = (mag * jnp.cos(aim), mag * jnp.sin(aim))
        powers = [(jnp.ones_like(are), jnp.zeros_like(are))]
        for _ in range(NPH):
            powers.append(cmul(powers[-1], lam_bar))
        squares = [powers[NPH]]
        for _ in range(15):
            squares.append(cmul(squares[-1], squares[-1]))
        nr, ni = lam_bar[0] - 1.0, lam_bar[1]
        den = lre * lre + lim * lim
        z = ((nr * lre + ni * lim) / den, (ni * lre - nr * lim) / den)
        bbar = cmul(z, (b_re_ref[d, g].T, b_im_ref[d, g].T))
        c = (c_re_ref[d, g], c_im_ref[d, g])
        return powers, squares, bbar, c

    pf, qf, bbar_f, c_f = one_direction(0)
    pb, qb, bbar_b, c_b = one_direction(1)
    par = jnp.concatenate(
        list(rows(pf[NPH - 1::-1])) + list(rows(pb[:NPH])) + list(bbar_f) + list(bbar_b)
        + list(rows(pf[1:])) + list(rows(pb[NPH:0:-1])) + list(c_f) + list(c_b), axis=0)
    q_tab = jnp.concatenate(list(rows(qf)) + list(rows(qb)), axis=0)
    both_halves = lambda v: jnp.concatenate([v, v], axis=1)
    return both_halves(par), both_halves(q_tab)


def _row_scan_carry(a, h, reverse):
    n, w = a.shape
    row = lax.broadcasted_iota(jnp.int32, (n, w), 0)

    def shift(v, k, fill):
        if k % 8 == 0:
            z = jnp.full((k, w), fill, v.dtype)
            if reverse:
                return jnp.concatenate([v[k:], z], axis=0)
            return jnp.concatenate([z, v[:n - k]], axis=0)
        if reverse:
            return jnp.where(row < n - k, pltpu.roll(v, n - k, axis=0), fill)
        return jnp.where(row >= k, pltpu.roll(v, k, axis=0), fill)

    k = 1
    while k < n:
        h = h + a * shift(h, k, 0.0)
        a = a * shift(a, k, 1.0)
        k *= 2
    return shift(h, 1, 0.0)


def _rglru_kernel(x_ref, cw_ref, cb_ref, wa_ref, wx_ref, ba_ref, bx_ref, lam_ref, y_ref, xe_ref,
                  alf_ref, hlf_ref, alb_ref, hlb_ref):
    n, w = x_ref.shape[1], x_ref.shape[2]

    def block_diag(heads):
        per, hd, _ = heads.shape
        zeros = jnp.zeros((hd, hd), heads.dtype)
        return jnp.concatenate(
            [jnp.concatenate([heads[p] if q == p else zeros for q in range(per)], axis=1) for p in range(per)],
            axis=0)
    row = lax.broadcasted_iota(jnp.int32, (n, w), 0)

    def from_prev_chunk(v):
        return jnp.where(row >= 1, pltpu.roll(v, 1, axis=0), 0.0)

    def from_next_chunk(v):
        return jnp.where(row < n - 1, pltpu.roll(v, n - 1, axis=0), 0.0)

    xe_ref[0] = from_prev_chunk(x_ref[NPH - 2])
    xe_ref[1] = from_prev_chunk(x_ref[NPH - 1])

    def copy_body(s, _):
        xe_ref[s + 2] = x_ref[s]
        return 0
    lax.fori_loop(0, NPH, copy_body, 0, unroll=4)
    xe_ref[NPH + 2] = from_next_chunk(x_ref[0])

    cw = cw_ref[...]
    cb = cb_ref[...]
    lam = lam_ref[...]
    neg = -lam
    softplus = jnp.maximum(neg, 0.0) + jnp.log(1.0 + jnp.exp(-jnp.abs(neg)))

    directions = []
    for d, (reverse, al_ref, hl_ref) in enumerate(((False, alf_ref, hlf_ref), (True, alb_ref, hlb_ref))):
        rate2 = (-RG_C * LOG2_E) * softplus[d:d + 1]
        wg = (-LOG2_E * jnp.concatenate([block_diag(wa_ref[d]), block_diag(wx_ref[d])], axis=1)).astype(BF16)
        bg = -LOG2_E * jnp.concatenate([ba_ref[d, 0], bx_ref[d, 0]], axis=1)
        init = NPH if reverse else 0
        al_ref[init] = jnp.ones((n, w), F32)
        hl_ref[init] = jnp.zeros((n, w), F32)
        directions.append((reverse, al_ref, hl_ref, rate2, wg, bg))

    def local_step(i, reverse, al_ref, hl_ref, rate2, wg, bg):
        s = (NPH - 1 - i) if reverse else i
        xc = (cw[0:1] * xe_ref[s] + cw[1:2] * xe_ref[s + 1]
              + cw[2:3] * xe_ref[s + 2] + cw[3:4] * xe_ref[s + 3] + cb)
        yield
        g = jnp.dot(xc.astype(BF16), wg, preferred_element_type=F32) + bg
        yield
        r = 1.0 / (1.0 + jnp.exp2(g[:, :w]))
        ig = 1.0 / (1.0 + jnp.exp2(g[:, w:]))
        a = jnp.exp2(rate2 * r)
        bt = jnp.sqrt(1.0 - a * a) * (ig * xc)
        yield
        src = (s + 1) if reverse else s
        dst = s if reverse else (s + 1)
        hl_ref[dst] = a * hl_ref[src] + bt
        al_ref[dst] = a * al_ref[src]

    def local_body(i, _):
        _in_lockstep([local_step(i, *direction) for direction in directions])
        return 0
    lax.fori_loop(0, NPH, local_body, 0, unroll=4)

    carry_f = _row_scan_carry(alf_ref[NPH], hlf_ref[NPH], False)
    carry_b = _row_scan_carry(alb_ref[0], hlb_ref[0], True)

    def fix_body(s, _):
        y_ref[s] = (hlf_ref[s + 1] + alf_ref[s + 1] * carry_f) + (hlb_ref[s] + alb_ref[s] * carry_b)
        return 0
    lax.fori_loop(0, NPH, fix_body, 0, unroll=4)


def _rglru(xrg3, conv_w, conv_b, wa, ba, wx, bx, lam):
    _, n_chunks, d_rg = xrg3.shape
    nb = d_rg // RG_CB
    n_dir, heads, hd, _ = wa.shape
    per = RG_CB // hd
    assert heads == nb * per
    gate_w = pl.BlockSpec((n_dir, per, hd, hd), lambda j: (0, j, 0, 0))
    gate_b = pl.BlockSpec((n_dir, 1, 1, RG_CB), lambda j: (0, j, 0, 0))
    bias4 = lambda b: b.reshape(n_dir, nb, 1, RG_CB)
    return pl.pallas_call(
        _rglru_kernel,
        grid=(nb,),
        in_specs=[
            pl.BlockSpec((NPH, n_chunks, RG_CB), lambda j: (0, 0, j)),
            pl.BlockSpec((CONV_W, RG_CB), lambda j: (0, j)),
            pl.BlockSpec((1, RG_CB), lambda j: (0, j)),
            gate_w, gate_w, gate_b, gate_b,
            pl.BlockSpec((2, RG_CB), lambda j: (0, j)),
        ],
        out_specs=pl.BlockSpec((NPH, n_chunks, RG_CB), lambda j: (0, 0, j)),
        out_shape=jax.ShapeDtypeStruct((NPH, n_chunks, d_rg), F32),
        scratch_shapes=[
            pltpu.VMEM((NPH + 3, n_chunks, RG_CB), F32),
        ] + [pltpu.VMEM((NPH + 1, n_chunks, RG_CB), F32)] * 4,
        compiler_params=_cparams(1),
        name="rglru",
    )(xrg3, conv_w, conv_b, wa, wx, bias4(ba), bias4(bx), lam)


def _rms_norm(x, g):
    return x * lax.rsqrt(jnp.mean(x * x, axis=-1, keepdims=True) + LN_EPS) * g


def _mixout_kernel(alpha, ys5_ref, yrg_ref, gz_ref, h_ref, gw_ref, gb_ref, gn5_ref, gnr_ref,
                   w_ref, b_ref, g_ref, be_ref, o_ref):
    rows = NPH * CT
    d_s5, d_model = ys5_ref.shape[-1], o_ref.shape[-1]
    pw = MXU_COLS_V7X
    pieces = lambda width: [slice(c, c + pw) for c in range(0, width, pw)]

    y = ys5_ref[...].reshape(rows, d_s5)
    yb = y.astype(BF16)
    glu, sq = [], 0.0
    for cols in pieces(d_s5):
        g = jnp.dot(yb, gw_ref[:, cols], preferred_element_type=F32) + gb_ref[:, cols]
        o = y[:, cols] * _sigmoid(g)
        sq = sq + jnp.sum(o * o, axis=-1, keepdims=True)
        glu.append(o)
    scale = lax.rsqrt(sq * (1.0 / d_s5) + LN_EPS)
    ys5n = jnp.concatenate([o * scale * gn5_ref[:, cols] for o, cols in zip(glu, pieces(d_s5))],
                           axis=1).astype(BF16)

    yrg = yrg_ref[...].reshape(rows, -1) * gz_ref[...].reshape(rows, -1)
    yrgn = _rms_norm(yrg, gnr_ref[...]).astype(BF16)

    ys5n, yrgn = _to_time_major(ys5n), _to_time_major(yrgn)
    s1 = 0.0
    for cols in pieces(d_model):
        v = (jnp.dot(ys5n, w_ref[:d_s5, cols], preferred_element_type=F32)
             + jnp.dot(yrgn, w_ref[d_s5:, cols], preferred_element_type=F32)
             + b_ref[:, cols] + alpha * h_ref[:, cols])
        s1 = s1 + jnp.sum(v, axis=-1, keepdims=True)
        o_ref[:, cols] = v
    mean = s1 * (1.0 / d_model)
    s2 = 0.0
    for cols in pieces(d_model):
        d = o_ref[:, cols] - mean
        s2 = s2 + jnp.sum(d * d, axis=-1, keepdims=True)
        o_ref[:, cols] = d
    rstd = lax.rsqrt(s2 * (1.0 / d_model) + LN_EPS)
    for cols in pieces(d_model):
        o_ref[:, cols] = o_ref[:, cols] * rstd * g_ref[:, cols] + be_ref[:, cols]


def _mixout(alpha, ys5, yrg, gz, h, glu_w, glu_b, gn_s5, gn_rg, w_out, b_out, ln_g, ln_b):
    seq, d_model = h.shape
    n_chunks = seq // NPH
    d_s5, d_rg = ys5.shape[-1], yrg.shape[-1]
    return pl.pallas_call(
        functools.partial(_mixout_kernel, alpha),
        grid=(n_chunks // CT,),
        in_specs=[
            _tile3(d_s5), _tile3(d_rg), _tile3(d_rg), _tile_rows(d_model),
            _resident((d_s5, d_s5)), _resident((1, d_s5)), _resident((1, d_s5)), _resident((1, d_rg)),
            _resident((d_s5 + d_rg, d_model)), _resident((1, d_model)),
            _resident((1, d_model)), _resident((1, d_model)),
        ],
        out_specs=_tile_rows(d_model),
        out_shape=jax.ShapeDtypeStruct((seq, d_model), F32),
        compiler_params=_cparams(1),
        name="mixout",
    )(ys5, yrg, gz, h, glu_w, glu_b, gn_s5, gn_rg, w_out, b_out, ln_g, ln_b)


def _ffn_kernel(alpha, h_ref, p_ref, w1_ref, b1_ref, w2_ref, b2_ref, pw_ref, gw_ref, gb_ref,
                g_ref, be_ref, o_ref, hb_ref, acc_ref):
    j = pl.program_id(1)

    @pl.when(j == 0)
    def _():
        h = h_ref[...]
        hb = h.astype(BF16)
        hb_ref[...] = hb
        gate = _sigmoid(jnp.dot(hb, gw_ref[...], preferred_element_type=F32) + gb_ref[...])
        pe = p_ref[...].astype(BF16)
        acc_ref[...] = alpha * h + gate * jnp.dot(pe, pw_ref[...], preferred_element_type=F32) + b2_ref[...]

    a = jnp.dot(hb_ref[...], w1_ref[...], preferred_element_type=F32) + b1_ref[...]
    a = jnp.maximum(a, 0.0)
    acc_ref[...] += jnp.dot((a * a).astype(BF16), w2_ref[...], preferred_element_type=F32)

    @pl.when(j == pl.num_programs(1) - 1)
    def _():
        o_ref[...] = _layer_norm(acc_ref[...], g_ref[...], be_ref[...])


def _ffn(alpha, h1, p2, w1, b1, w2, b2, ple_w, gate_w, gate_b, ln_g, ln_b, ff_tile):
    seq, d_model = h1.shape
    d_ff = w1.shape[1]
    ple_dim = ple_w.shape[0]
    rows = NPH * CT
    return pl.pallas_call(
        functools.partial(_ffn_kernel, alpha),
        grid=(seq // rows, d_ff // ff_tile),
        in_specs=[
            _tile_rows(d_model),
            _tile_rows(ple_dim),
            pl.BlockSpec((d_model, ff_tile), lambda i, j: (0, j)),
            pl.BlockSpec((1, ff_tile), lambda i, j: (0, j)),
            pl.BlockSpec((ff_tile, d_model), lambda i, j: (j, 0)),
            _resident((1, d_model)),
            _resident((ple_dim, d_model)), _resident((d_model, d_model)), _resident((1, d_model)),
            _resident((1, d_model)), _resident((1, d_model)),
        ],
        out_specs=_tile_rows(d_model),
        out_shape=jax.ShapeDtypeStruct((seq, d_model), F32),
        scratch_shapes=[pltpu.VMEM((rows, d_model), BF16), pltpu.VMEM((rows, d_model), F32)],
        compiler_params=_cparams(2),
        name="ffn",
    )(h1, p2, w1, b1, w2, b2, ple_w, gate_w, gate_b, ln_g, ln_b)


def kernel(x, p, ln_in_g, ln_in_b, w_in, b_in, s5_lambda_re, s5_lambda_im, s5_log_step, s5_b_re, s5_b_im, s5_c_re, s5_c_im, s5_d, s5_glu_w, s5_glu_b, rg_conv_w, rg_conv_b, rg_wa, rg_ba, rg_wx, rg_bx, rg_lambda, gn_s5, gn_rg, w_out, b_out, ln1_g, ln1_b, w_ff1, b_ff1, w_ff2, b_ff2, ple_w, ple_gate_w, ple_gate_b, ln2_g, ln2_b):
    batch, seq, d_model = x.shape
    depth = w_in.shape[0]
    assert batch == 1 and depth == 1 and seq % (NPH * CT) == 0
    n_chunks = seq // NPH
    d_s5 = s5_glu_w.shape[-1]
    d_rg = rg_conv_w.shape[-1]
    alpha = (2.0 * depth) ** 0.25
    row = lambda v: v.reshape(1, -1).astype(F32)

    h, u3, xrg, gz = _inproj(x.reshape(seq, d_model), row(ln_in_g), row(ln_in_b),
                              w_in.reshape(w_in.shape[1:]).astype(F32), row(b_in[0]), d_s5, d_rg)

    ys5, (w_out_b, gate_w_b, w_ff1_b, w_ff2_b) = _s5(
        u3, s5_lambda_re[0], s5_lambda_im[0], s5_log_step[0], s5_b_re[0], s5_b_im[0], s5_c_re[0], s5_c_im[0],
        s5_d[0], [w_out[0], ple_gate_w[0], w_ff1[0], w_ff2[0]])

    yrg = _rglru(xrg, rg_conv_w[0].astype(F32), row(rg_conv_b[0]), rg_wa[0], rg_ba[0], rg_wx[0], rg_bx[0],
                 rg_lambda[0].astype(F32))

    h1 = _mixout(alpha, ys5, yrg, gz, h, s5_glu_w[0].astype(BF16), row(s5_glu_b[0]), row(gn_s5[0]),
                 row(gn_rg[0]), w_out_b, row(b_out[0]), row(ln1_g[0]), row(ln1_b[0]))

    out = _ffn(alpha, h1, p.reshape(seq, p.shape[-1]), w_ff1_b, row(b_ff1[0]),
               w_ff2_b, row(b_ff2[0]), ple_w[0].astype(BF16), gate_w_b,
               row(ple_gate_b[0]), row(ln2_g[0]), row(ln2_b[0]), ff_tile=1024)
    return out.reshape(batch, seq, d_model).astype(x.dtype)
```
